```python
import math
import jax
import jax.numpy as jnp
from jax import lax
import numpy as np

D_MODEL = 4096
BATCH = 2
SEQ = 4096
DEPTH = 2

GRID_W = 64
CTX_LEN = 256
EPS = 1e-6
CONV_W = 3
D_FF = 11008
ADA_GAIN = 0.5

MLA_HEADS = 16
MLA_NOPE = 128
MLA_ROPE = 64
MLA_VDIM = 128
MLA_Q_RANK = 1024
MLA_KV_RANK = 512
MLA_SCALE = (MLA_NOPE + MLA_ROPE) ** -0.5
MLA_WIDTH = MLA_HEADS * MLA_VDIM
ROPE_THETA = 10000.0
Q_BLOCK = 128

SC_WIDTH = D_MODEL - MLA_WIDTH

GDN_QK_HEADS = 16
GDN_V_HEADS = 32
GDN_DK = 128
GDN_DV = 128
GDN_CHUNK = 64
GDN_KEY_W = GDN_QK_HEADS * GDN_DK
GDN_VAL_W = GDN_V_HEADS * GDN_DV
GDN_QKV_W = 2 * GDN_KEY_W + GDN_VAL_W

EVEN_SPLITS = [MLA_Q_RANK,
               MLA_Q_RANK + MLA_KV_RANK,
               MLA_Q_RANK + MLA_KV_RANK + MLA_ROPE,
               MLA_Q_RANK + MLA_KV_RANK + MLA_ROPE + SC_WIDTH,
               MLA_Q_RANK + MLA_KV_RANK + MLA_ROPE + 2 * SC_WIDTH]
EVEN_IN = MLA_Q_RANK + MLA_KV_RANK + MLA_ROPE + 3 * SC_WIDTH
EVEN_OUT = MLA_WIDTH + SC_WIDTH
ODD_SPLITS = [GDN_QKV_W, GDN_QKV_W + GDN_VAL_W, GDN_QKV_W + GDN_VAL_W + 2 * GDN_V_HEADS]
ODD_IN = GDN_QKV_W + GDN_VAL_W + 4 * GDN_V_HEADS

kernel_name = 'hybrid_mla_shortconv_gdn_dit'


def rmsnorm(x, g):
    xf = x.astype(jnp.float32)
    y = xf * lax.rsqrt(jnp.mean(xf * xf, axis=-1, keepdims=True) + EPS)
    return (y * g.astype(jnp.float32)).astype(x.dtype)


def l2norm(x):
    xf = x.astype(jnp.float32)
    return (xf * lax.rsqrt(jnp.sum(xf * xf, axis=-1, keepdims=True) + EPS)).astype(x.dtype)


def modulate(x, g, shift, scale):
    return rmsnorm(x, g) * (1 + scale[:, None]) + shift[:, None]


def dwconv3(x, w):
    xp = jnp.pad(x, ((0, 0), (1, 1), (0, 0)))
    return xp[:, :-2] * w[0] + xp[:, 1:-1] * w[1] + xp[:, 2:] * w[2]


def axial_rope_tables(rows):
    n_freq = MLA_ROPE // 4
    inv_freq = ROPE_THETA ** (-jnp.arange(n_freq, dtype=jnp.float32) / n_freq)
    row = jnp.broadcast_to(jnp.arange(rows, dtype=jnp.float32)[:, None], (rows, GRID_W)).reshape(-1)
    col = jnp.broadcast_to(jnp.arange(GRID_W, dtype=jnp.float32)[None, :], (rows, GRID_W)).reshape(-1)
    ang_r = row[:, None] * inv_freq
    ang_c = col[:, None] * inv_freq
    return jnp.cos(ang_r), jnp.sin(ang_r), jnp.cos(ang_c), jnp.sin(ang_c)


def rotate_pairs(x, cos, sin):
    n = cos.shape[-1]
    x1, x2 = x[..., :n], x[..., n:]
    cos, sin = cos[:, None, :], sin[:, None, :]
    return jnp.concatenate([x1 * cos - x2 * sin, x1 * sin + x2 * cos], axis=-1)


def apply_rope2d(t, tabs):
    cr, sr, cc, sc = tabs
    half = MLA_ROPE // 2
    r = t[..., MLA_NOPE:]
    rot = jnp.concatenate([rotate_pairs(r[..., :half], cr, sr),
                           rotate_pairs(r[..., half:], cc, sc)], axis=-1)
    return jnp.concatenate([t[..., :MLA_NOPE], rot.astype(t.dtype)], axis=-1)


def attend(q, k, v):
    s = jnp.einsum('bqhd,bkhd->bhqk', q, k).astype(jnp.float32) * MLA_SCALE
    p = jax.nn.softmax(s, axis=-1).astype(v.dtype)
    return jnp.einsum('bhqk,bkhd->bqhd', p, v)


def even_mixer(u_lat, u_ctx, params, tabs, need_ctx):
    w_in, q_a_g, kv_a_g, w_qb, w_kvb, q_g, k_g, sc_w, w_out = params

    def queries(cq, rotary):
        bn, t = cq.shape[:2]
        q = (rmsnorm(cq, q_a_g) @ w_qb).reshape(bn, t, MLA_HEADS, MLA_NOPE + MLA_ROPE)
        q = rmsnorm(q, q_g)
        return apply_rope2d(q, tabs) if rotary else q

    def keys_values(ckv, k_rope, rotary):
        bn, t = ckv.shape[:2]
        kv = (rmsnorm(ckv, kv_a_g) @ w_kvb).reshape(bn, t, MLA_HEADS, MLA_NOPE + MLA_VDIM)
        k_shared = jnp.broadcast_to(k_rope[:, :, None, :], (bn, t, MLA_HEADS, MLA_ROPE))
        k = rmsnorm(jnp.concatenate([kv[..., :MLA_NOPE], k_shared], axis=-1), k_g)
        k = apply_rope2d(k, tabs) if rotary else k
        return k, kv[..., MLA_NOPE:]

    def short_conv(gate_b, gate_c, xin):
        return gate_b * dwconv3(gate_c * xin, sc_w)

    def merge(attn_out, conv_out):
        bn, t = conv_out.shape[:2]
        return jnp.concatenate([attn_out.reshape(bn, t, MLA_WIDTH), conv_out], axis=-1) @ w_out

    cq_l, ckv_l, kr_l, b_l, c_l, x_l = jnp.split(u_lat @ w_in, EVEN_SPLITS, axis=-1)
    cq_c, ckv_c, kr_c, b_c, c_c, x_c = jnp.split(u_ctx @ w_in, EVEN_SPLITS, axis=-1)
    k_c, v_c = keys_values(ckv_c, kr_c, False)
    k_l, v_l = keys_values(ckv_l, kr_l, True)
    q_l = queries(cq_l, True)
    k_all = jnp.concatenate([k_c, k_l], axis=1)
    v_all = jnp.concatenate([v_c, v_l], axis=1)
    bn, t = u_lat.shape[:2]
    q_blocks = q_l.reshape(bn, t // Q_BLOCK, Q_BLOCK, MLA_HEADS, MLA_NOPE + MLA_ROPE).swapaxes(0, 1)
    o_l = lax.map(lambda qb: attend(qb, k_all, v_all), q_blocks)
    o_l = o_l.swapaxes(0, 1).reshape(bn, t, MLA_HEADS, MLA_VDIM)
    y_lat = merge(o_l, short_conv(b_l, c_l, x_l))
    y_ctx = merge(attend(queries(cq_c, False), k_c, v_c), short_conv(b_c, c_c, x_c)) if need_ctx else None
    return y_lat, y_ctx


def gated_delta_rule(q, k, v, g, beta, s0):
    f32 = jnp.float32
    bn, t, h, dk = q.shape
    dv = v.shape[-1]
    n = t // GDN_CHUNK

    def chunks(a):
        a = a.astype(f32).reshape(bn, n, GDN_CHUNK, h, *a.shape[3:])
        return jnp.moveaxis(a, 3, 1)

    q, k, v, g, beta = chunks(q), chunks(k), chunks(v), chunks(g), chunks(beta)
    g = jnp.cumsum(g, axis=-1)
    incl = jnp.tril(jnp.ones((GDN_CHUNK, GDN_CHUNK), bool))
    strict = jnp.tril(jnp.ones((GDN_CHUNK, GDN_CHUNK), bool), -1)
    decay = jnp.exp(jnp.where(incl, g[..., :, None] - g[..., None, :], -jnp.inf))
    kb = k * beta[..., None]
    low = jnp.where(strict, jnp.einsum('bhnid,bhnjd->bhnij', kb, k) * decay, 0.0)
    a_mat = low + jnp.eye(GDN_CHUNK, dtype=f32)
    rhs = jnp.concatenate([v * beta[..., None], kb * jnp.exp(g)[..., None]], axis=-1)
    sol = lax.linalg.triangular_solve(a_mat, rhs, left_side=True, lower=True, unit_diagonal=True)
    u, w = sol[..., :dv], sol[..., dv:]
    intra = jnp.where(incl, jnp.einsum('bhnid,bhnjd->bhnij', q, k) * decay, 0.0)
    q_dec = q * jnp.exp(g)[..., None]
    k_dec = k * jnp.exp(g[..., -1:] - g)[..., None]
    chunk_decay = jnp.exp(g[..., -1])
    xs = tuple(jnp.moveaxis(a, 2, 0) for a in (u, w, intra, q_dec, k_dec, chunk_decay))

    def step(s, inp):
        u_n, w_n, intra_n, qd_n, kd_n, cd_n = inp
        v_new = u_n - jnp.einsum('bhck,bhkv->bhcv', w_n, s)
        o_n = jnp.einsum('bhck,bhkv->bhcv', qd_n, s) + jnp.einsum('bhij,bhjv->bhiv', intra_n, v_new)
        s = s * cd_n[..., None, None] + jnp.einsum('bhck,bhcv->bhkv', kd_n, v_new)
        return s, o_n

    s_final, o = lax.scan(step, s0.astype(f32), xs)
    o = jnp.transpose(o, (1, 0, 3, 2, 4)).reshape(bn, t, h, dv)
    return o, s_final


def gdn_readout(o, z, o_g, w_out, dtype):
    bn, t = z.shape[:2]
    gate = jax.nn.silu(z).reshape(bn, t, GDN_V_HEADS, GDN_DV)
    y = (rmsnorm(o, o_g) * gate).astype(dtype).reshape(bn, t, GDN_VAL_W)
    return y @ w_out


def odd_mixer(u_lat, u_ctx, params, need_ctx):
    w_in, conv_w, a_log, dt_bias, o_g, w_out = params
    rep = GDN_V_HEADS // GDN_QK_HEADS

    def prepare(u):
        bn, t = u.shape[:2]
        qkv, z, b, a = jnp.split(u @ w_in, ODD_SPLITS, axis=-1)
        qkv = jax.nn.silu(dwconv3(qkv, conv_w))
        q, k, v = jnp.split(qkv, [GDN_KEY_W, 2 * GDN_KEY_W], axis=-1)
        q = jnp.repeat(l2norm(q.reshape(bn, t, GDN_QK_HEADS, GDN_DK)) * GDN_DK ** -0.5, rep, axis=2)
        k = jnp.repeat(l2norm(k.reshape(bn, t, GDN_QK_HEADS, GDN_DK)), rep, axis=2)
        v = v.reshape(bn, t, GDN_V_HEADS, GDN_DV)
        beta = jax.nn.sigmoid(b.reshape(bn, t, 2, GDN_V_HEADS))
        g = -jnp.exp(a_log) * jax.nn.softplus(a.reshape(bn, t, 2, GDN_V_HEADS) + dt_bias)
        return q, k, v, z, beta, g

    def flip(a):
        return jnp.flip(a, axis=1)

    q_c, k_c, v_c, z_c, beta_c, g_c = prepare(u_ctx)
    q_l, k_l, v_l, z_l, beta_l, g_l = prepare(u_lat)
    s0 = jnp.zeros((u_lat.shape[0], GDN_V_HEADS, GDN_DK, GDN_DV), jnp.float32)
    o_cf, s_cf = gated_delta_rule(q_c, k_c, v_c, g_c[:, :, 0], beta_c[:, :, 0], s0)
    o_lf, _ = gated_delta_rule(q_l, k_l, v_l, g_l[:, :, 0], beta_l[:, :, 0], s_cf)
    o_cb, s_cb = gated_delta_rule(flip(q_c), flip(k_c), flip(v_c), flip(g_c[:, :, 1]), flip(beta_c[:, :, 1]), s0)
    o_lb, _ = gated_delta_rule(flip(q_l), flip(k_l), flip(v_l), flip(g_l[:, :, 1]), flip(beta_l[:, :, 1]), s_cb)
    y_lat = gdn_readout(o_lf + flip(o_lb), z_l, o_g, w_out, u_lat.dtype)
    y_ctx = gdn_readout(o_cf + flip(o_cb), z_c, o_g, w_out, u_ctx.dtype) if need_ctx else None
    return y_lat, y_ctx


def conv_ffn(u, w_gate, w_up, conv_w, conv_b, w_down):
    a = dwconv3(u @ w_gate, conv_w) + conv_b
    return (jax.nn.silu(a) * (u @ w_up)) @ w_down


def setup_inputs(seed: int = 0) -> dict:
    key = jax.random.key(seed)
    ks = iter(jax.random.split(key, 32))
    f32 = jnp.float32
    n_even = (DEPTH + 1) // 2
    n_odd = DEPTH // 2

    def normal(shape, scale):
        return scale * jax.random.normal(next(ks), shape, f32)

    def dense(shape, fan_in, gain=1.0):
        return normal(shape, gain * fan_in ** -0.5)

    def norm_gain(shape):
        return 1.0 + normal(shape, 0.05)

    x = normal((BATCH, SEQ, D_MODEL), 1.0)
    c = normal((BATCH, D_MODEL), 1.0)
    ctx = normal((BATCH, CTX_LEN, D_MODEL), 1.0)
    c_ctx = normal((D_MODEL,), 1.0)
    ada_w = dense((DEPTH, D_MODEL, 6 * D_MODEL), D_MODEL, ADA_GAIN)
    ada_b = normal((DEPTH, 6 * D_MODEL), 0.01)
    norm_mix = norm_gain((DEPTH, D_MODEL))
    norm_ffn = norm_gain((DEPTH, D_MODEL))
    ffn_w_gate = dense((DEPTH, D_MODEL, D_FF), D_MODEL)
    ffn_w_up = dense((DEPTH, D_MODEL, D_FF), D_MODEL)
    ffn_conv_w = dense((DEPTH, CONV_W, D_FF), CONV_W)
    ffn_conv_b = normal((DEPTH, D_FF), 0.01)
    ffn_w_down = dense((DEPTH, D_FF, D_MODEL), D_FF)
    a_w_in = dense((n_even, D_MODEL, EVEN_IN), D_MODEL)
    a_q_a_norm = norm_gain((n_even, MLA_Q_RANK))
    a_kv_a_norm = norm_gain((n_even, MLA_KV_RANK))
    a_w_qb = dense((n_even, MLA_Q_RANK, MLA_HEADS * (MLA_NOPE + MLA_ROPE)), MLA_Q_RANK)
    a_w_kvb = dense((n_even, MLA_KV_RANK, MLA_HEADS * (MLA_NOPE + MLA_VDIM)), MLA_KV_RANK)
    a_q_norm = norm_gain((n_even, MLA_NOPE + MLA_ROPE))
    a_k_norm = norm_gain((n_even, MLA_NOPE + MLA_ROPE))
    a_sc_conv = dense((n_even, CONV_W, SC_WIDTH), CONV_W)
    a_w_out = dense((n_even, EVEN_OUT, D_MODEL), EVEN_OUT)
    c_w_in = dense((n_odd, D_MODEL, ODD_IN), D_MODEL)
    c_conv_w = dense((n_odd, CONV_W, GDN_QKV_W), CONV_W)
    c_a_log = jnp.log(jax.random.uniform(next(ks), (n_odd, 2, GDN_V_HEADS), f32, 1.0, 16.0))
    dt = jnp.exp(jax.random.uniform(next(ks), (n_odd, 2, GDN_V_HEADS), f32, math.log(1e-3), math.log(1e-1)))
    c_dt_bias = dt + jnp.log(-jnp.expm1(-dt))
    c_o_norm = norm_gain((n_odd, GDN_DV))
    c_w_out = dense((n_odd, GDN_VAL_W, D_MODEL), GDN_VAL_W)
    return {'x': x, 'c': c, 'ctx': ctx, 'c_ctx': c_ctx,
            'ada_w': ada_w, 'ada_b': ada_b, 'norm_mix': norm_mix, 'norm_ffn': norm_ffn,
            'ffn_w_gate': ffn_w_gate, 'ffn_w_up': ffn_w_up, 'ffn_conv_w': ffn_conv_w,
            'ffn_conv_b': ffn_conv_b, 'ffn_w_down': ffn_w_down,
            'a_w_in': a_w_in, 'a_q_a_norm': a_q_a_norm, 'a_kv_a_norm': a_kv_a_norm,
            'a_w_qb': a_w_qb, 'a_w_kvb': a_w_kvb, 'a_q_norm': a_q_norm, 'a_k_norm': a_k_norm,
            'a_sc_conv': a_sc_conv, 'a_w_out': a_w_out,
            'c_w_in': c_w_in, 'c_conv_w': c_conv_w, 'c_a_log': c_a_log, 'c_dt_bias': c_dt_bias,
            'c_o_norm': c_o_norm, 'c_w_out': c_w_out}


def reference(x, c, ctx, c_ctx, ada_w, ada_b, norm_mix, norm_ffn,
              ffn_w_gate, ffn_w_up, ffn_conv_w, ffn_conv_b, ffn_w_down,
              a_w_in, a_q_a_norm, a_kv_a_norm, a_w_qb, a_w_kvb, a_q_norm, a_k_norm, a_sc_conv, a_w_out,
              c_w_in, c_conv_w, c_a_log, c_dt_bias, c_o_norm, c_w_out):
    n_lat = x.shape[1]
    rows = n_lat // GRID_W
    tabs = axial_rope_tables(rows)
    cond = jnp.concatenate([c, c_ctx[None]], axis=0)
    h_lat, h_ctx = x, ctx
    for l in range(DEPTH):
        last = l == DEPTH - 1
        mod = jax.nn.silu(cond) @ ada_w[l] + ada_b[l]
        sh1, sc1, g1, sh2, sc2, g2 = jnp.split(mod[:-1], 6, axis=-1)
        csh1, csc1, cg1, csh2, csc2, cg2 = jnp.split(mod[-1:], 6, axis=-1)
        u_lat = modulate(h_lat, norm_mix[l], sh1, sc1)
        u_ctx = modulate(h_ctx, norm_mix[l], csh1, csc1)
        if l % 2 == 0:
            i = l // 2
            y_lat, y_ctx = even_mixer(u_lat, u_ctx,
                                      (a_w_in[i], a_q_a_norm[i], a_kv_a_norm[i], a_w_qb[i], a_w_kvb[i],
                                       a_q_norm[i], a_k_norm[i], a_sc_conv[i], a_w_out[i]),
                                      tabs, not last)
        else:
            i = l // 2
            y_lat, y_ctx = odd_mixer(u_lat, u_ctx,
                                     (c_w_in[i], c_conv_w[i], c_a_log[i], c_dt_bias[i], c_o_norm[i], c_w_out[i]),
                                     not last)
        ffn = (ffn_w_gate[l], ffn_w_up[l], ffn_conv_w[l], ffn_conv_b[l], ffn_w_down[l])
        h_lat = h_lat + g1[:, None] * y_lat
        h_lat = h_lat + g2[:, None] * conv_ffn(modulate(h_lat, norm_ffn[l], sh2, sc2), *ffn)
        if not last:
            h_ctx = h_ctx + cg1[:, None] * y_ctx
            h_ctx = h_ctx + cg2[:, None] * conv_ffn(modulate(h_ctx, norm_ffn[l], csh2, csc2), *ffn)
    return h_lat
```

```python
import functools

import jax
import jax.numpy as jnp
from jax import lax
from jax.experimental import pallas as pl
from jax.experimental.pallas import tpu as pltpu

F32 = jnp.float32
BF16 = jnp.bfloat16

BATCH = 2
SEQ = 4096
GRID_W = 64
CTX_LEN = 256
EPS = 1e-6
D_FF = 11008

MLA_HEADS = 16
MLA_NOPE = 128
MLA_ROPE = 64
MLA_VDIM = 128
MLA_Q_RANK = 1024
MLA_KV_RANK = 512
ROPE_THETA = 10000.0

GDN_QK_HEADS = 16
GDN_V_HEADS = 32
GDN_DK = 128
GDN_DV = 128

LANE = 128
SUBLANE = 8
QK_PAD = 256
ROW_BLK = 256
GDN_CHUNK = 64
GDN_BLK = 256
VMEM_LIMIT = 56 * 2**20


def _cp(*sem, vmem=VMEM_LIMIT):
    return pltpu.CompilerParams(dimension_semantics=sem, vmem_limit_bytes=vmem)


def _pick(n, prefs):
    for p in prefs:
        if n % p == 0:
            return p
    raise ValueError(f"no tile for {n} in {prefs}")


def _round_up(n, m):
    return (n + m - 1) // m * m


def _sigmoid(x):
    return 1.0 / (1.0 + jnp.exp(-x))


def _silu(x):
    return x * _sigmoid(x)


def _dot(a, b):
    return jnp.dot(a, b, preferred_element_type=F32)


def _dot_nt(a, b):
    return lax.dot_general(a, b, (((1,), (1,)), ((), ())), preferred_element_type=F32)


def _dot_tn(a, b):
    return lax.dot_general(a, b, (((0,), (0,)), ((), ())), preferred_element_type=F32)


def _n_lat():
    return BATCH * SEQ


def _seq_edges(row0, tm):
    assert SEQ & (SEQ - 1) == 0 and CTX_LEN & (CTX_LEN - 1) == 0 and SEQ % CTX_LEN == 0
    r = row0 + lax.broadcasted_iota(jnp.int32, (tm, 1), 0)
    is_ctx = r >= _n_lat()
    first = (jnp.bitwise_and(r, CTX_LEN - 1) == 0) & ((jnp.bitwise_and(r, SEQ - 1) == 0) | is_ctx)
    r1 = r + 1
    last = (jnp.bitwise_and(r1, CTX_LEN - 1) == 0) & ((jnp.bitwise_and(r1, SEQ - 1) == 0) | is_ctx)
    return first, last


def _shift_rows(x, prev_row, next_row, row0):
    tm = x.shape[0]
    ridx = lax.broadcasted_iota(jnp.int32, (tm, 1), 0)
    first, last = _seq_edges(row0, tm)
    dn = jnp.where(ridx == 0, prev_row, pltpu.roll(x, 1, 0))
    dn = jnp.where(first, 0.0, dn)
    up = jnp.where(ridx == tm - 1, next_row, pltpu.roll(x, tm - 1, 0))
    up = jnp.where(last, 0.0, up)
    return dn, up


def _row_select(row0, tm, table):
    r = row0 + lax.broadcasted_iota(jnp.int32, (tm, 1), 0)
    out = table[BATCH:BATCH + 1]
    for b in reversed(range(BATCH)):
        out = jnp.where(r < (b + 1) * SEQ, table[b:b + 1], out)
    return out


def _halo_specs(tm, tn, n_rows, col_blk0):
    tmb, last = tm // SUBLANE, n_rows // SUBLANE - 1
    prev = pl.BlockSpec((SUBLANE, tn), lambda i, j: (jnp.maximum(i * tmb - 1, 0), col_blk0 + j))
    nxt = pl.BlockSpec((SUBLANE, tn), lambda i, j: (jnp.minimum((i + 1) * tmb, last), col_blk0 + j))
    return prev, nxt


def _ada_kernel(cond_ref, w_ref, b_ref, o_ref):
    a = _silu(cond_ref[...]).astype(BF16)
    o_ref[...] = _dot(a, w_ref[...].astype(BF16)) + b_ref[...]


def _ada(cond8, ada_w, ada_b):
    n_layer, d, n = ada_w.shape
    tn = _pick(n, (512, 256, 128))
    return pl.pallas_call(
        _ada_kernel,
        grid=(n_layer, n // tn),
        in_specs=[pl.BlockSpec((SUBLANE, d), lambda l, j: (0, 0)),
                  pl.BlockSpec((None, d, tn), lambda l, j: (l, 0, j)),
                  pl.BlockSpec((None, 1, tn), lambda l, j: (l, 0, j))],
        out_specs=pl.BlockSpec((None, SUBLANE, tn), lambda l, j: (l, 0, j)),
        out_shape=jax.ShapeDtypeStruct((n_layer, SUBLANE, n), F32),
        compiler_params=_cp("parallel", "parallel"),
        name="ada",
    )(cond8, ada_w, ada_b.reshape(n_layer, 1, n))


def _modulate_kernel(h_ref, g_ref, sh_ref, sc_ref, o_ref):
    x = h_ref[...]
    y = x * lax.rsqrt(jnp.mean(x * x, axis=-1, keepdims=True) + EPS)
    o_ref[...] = ((y * g_ref[...]) * (1.0 + sc_ref[...]) + sh_ref[...]).astype(o_ref.dtype)


def _modulate(h, n_rows, gain, mod_rows, layer, k_shift, k_scale):
    d = h.shape[1]
    blk_per_seq = SEQ // ROW_BLK

    def mod_spec(k):
        return pl.BlockSpec(
            (None, 1, d),
            lambda i: ((layer * SUBLANE + jnp.minimum(i // blk_per_seq, BATCH)) * 6 + k, 0, 0))

    return pl.pallas_call(
        _modulate_kernel,
        grid=(n_rows // ROW_BLK,),
        in_specs=[pl.BlockSpec((ROW_BLK, d), lambda i: (i, 0)),
                  pl.BlockSpec((1, d), lambda i: (0, 0)),
                  mod_spec(k_shift), mod_spec(k_scale)],
        out_specs=pl.BlockSpec((ROW_BLK, d), lambda i: (i, 0)),
        out_shape=jax.ShapeDtypeStruct((n_rows, d), BF16),
        compiler_params=_cp("parallel"),
        name="modulate",
    )(h, gain.reshape(1, d), mod_rows, mod_rows)


def _mm_kernel(*refs, n_pairs, gated, tm, row_axis):
    a_refs, w_refs = refs[:n_pairs], refs[n_pairs:2 * n_pairs]
    acc = _dot(a_refs[0][...], w_refs[0][...])
    for a_ref, w_ref in zip(a_refs[1:], w_refs[1:]):
        acc += _dot(a_ref[...], w_ref[...])
    if gated:
        res_ref, gate_ref, o_ref = refs[2 * n_pairs:]
        gate = _row_select(pl.program_id(row_axis) * tm, tm, gate_ref[...])
        acc = res_ref[...] + gate * acc
    else:
        o_ref = refs[2 * n_pairs]
    o_ref[...] = acc.astype(o_ref.dtype)


def _mm(a_list, w_list, n_rows, *, out_dtype=F32, res=None, gate=None, tm=None, tn=None,
        a_blk_fn=None, w_stationary=False, w_col_blk0=0, n_cols=None, name="mm"):
    n_pairs = len(a_list)
    n = n_cols if n_cols is not None else w_list[0].shape[1]
    tm = tm or _pick(n_rows, (1088, 1024, 512, 256, 8))
    tn = tn or _pick(n, (1024, 512, 256, 128))
    gm, gn = n_rows // tm, n // tn
    a_blk_fn = a_blk_fn or (lambda i: i)
    if w_stationary:
        grid, row_axis = (gn, gm), 1
        ij = lambda f: (lambda j, i: f(i, j))
    else:
        grid, row_axis = (gm, gn), 0
        ij = lambda f: f
    in_specs = [pl.BlockSpec((tm, a.shape[1]), ij(lambda i, j: (a_blk_fn(i), 0))) for a in a_list]
    in_specs += [pl.BlockSpec((w.shape[0], tn), ij(lambda i, j: (0, w_col_blk0 + j))) for w in w_list]
    args = list(a_list) + list(w_list)
    if res is not None:
        table, layer, chunk = gate
        d = table.shape[2] // 6
        in_specs += [pl.BlockSpec((tm, tn), ij(lambda i, j: (i, j))),
                     pl.BlockSpec((None, SUBLANE, tn), ij(lambda i, j: (layer, 0, chunk * (d // tn) + j)))]
        args += [res, table]
    return pl.pallas_call(
        functools.partial(_mm_kernel, n_pairs=n_pairs, gated=res is not None, tm=tm, row_axis=row_axis),
        grid=grid,
        in_specs=in_specs,
        out_specs=pl.BlockSpec((tm, tn), ij(lambda i, j: (i, j))),
        out_shape=jax.ShapeDtypeStruct((n_rows, n), out_dtype),
        compiler_params=_cp("parallel", "parallel"),
        name=name,
    )(*args)


def _mmk_kernel(a_ref, w_ref, res_ref, gate_ref, o_ref, acc_ref, *, tm):
    k = pl.program_id(2)

    @pl.when(k == 0)
    def _():
        acc_ref[...] = jnp.zeros_like(acc_ref)

    acc_ref[...] += _dot(a_ref[...], w_ref[...])

    @pl.when(k == pl.num_programs(2) - 1)
    def _():
        gate = _row_select(pl.program_id(0) * tm, tm, gate_ref[...])
        o_ref[...] = res_ref[...] + gate * acc_ref[...]


def _mm_ktiled(a, w, n_rows, res, gate):
    kdim, n = w.shape
    table, layer, chunk = gate
    d = table.shape[2] // 6
    tm = _pick(n_rows, (1088, 1024, 512, 256))
    tn = _pick(n, (1024, 512, 256, 128))
    tk = _pick(kdim, (2816, 2048, 1024, 512, 256))
    return pl.pallas_call(
        functools.partial(_mmk_kernel, tm=tm),
        grid=(n_rows // tm, n // tn, kdim // tk),
        in_specs=[pl.BlockSpec((tm, tk), lambda i, j, k: (i, k)),
                  pl.BlockSpec((tk, tn), lambda i, j, k: (k, j)),
                  pl.BlockSpec((tm, tn), lambda i, j, k: (i, j)),
                  pl.BlockSpec((None, SUBLANE, tn), lambda i, j, k: (layer, 0, chunk * (d // tn) + j))],
        out_specs=pl.BlockSpec((tm, tn), lambda i, j, k: (i, j)),
        out_shape=jax.ShapeDtypeStruct((n_rows, n), F32),
        scratch_shapes=[pltpu.VMEM((tm, tn), F32)],
        compiler_params=_cp("parallel", "parallel", "arbitrary"),
        name="ffn_down",
    )(a, w, res, table)


def _ffn_up_kernel(u_ref, wg_ref, wu_ref, cw_ref, cb_ref, halo_ref, o_ref, *, tm):
    u = u_ref[...]
    g = _dot(u, wg_ref[...])
    dn, up = _shift_rows(g, halo_ref[SUBLANE - 1:SUBLANE, :], halo_ref[SUBLANE:SUBLANE + 1, :],
                         pl.program_id(0) * tm)
    a = dn * cw_ref[0:1, :] + g * cw_ref[1:2, :] + up * cw_ref[2:3, :] + cb_ref[...]
    o_ref[...] = (_silu(a) * _dot(u, wu_ref[...])).astype(o_ref.dtype)


def _ffn_up(u, n_rows, wg, wu, cw, cb):
    d, n = wg.shape
    tm = _pick(n_rows, (1088, 1024, 512, 256))
    tn = _pick(n, (512, 256, 128))
    gm = n_rows // tm
    tmb, last = tm // SUBLANE, n_rows // SUBLANE - 1

    def halo_blk(h):
        i = h // 2
        return jnp.where(h % 2 == 0, jnp.maximum(i * tmb - 1, 0), jnp.minimum((i + 1) * tmb, last))

    halo = _mm([u], [wg], 2 * gm * SUBLANE, tm=SUBLANE, tn=tn, a_blk_fn=halo_blk, w_stationary=True,
               name="ffn_halo")
    return pl.pallas_call(
        functools.partial(_ffn_up_kernel, tm=tm),
        grid=(gm, n // tn),
        in_specs=[pl.BlockSpec((tm, d), lambda i, j: (i, 0)),
                  pl.BlockSpec((d, tn), lambda i, j: (0, j)),
                  pl.BlockSpec((d, tn), lambda i, j: (0, j)),
                  pl.BlockSpec((3, tn), lambda i, j: (0, j)),
                  pl.BlockSpec((1, tn), lambda i, j: (0, j)),
                  pl.BlockSpec((2 * SUBLANE, tn), lambda i, j: (i, j))],
        out_specs=pl.BlockSpec((tm, tn), lambda i, j: (i, j)),
        out_shape=jax.ShapeDtypeStruct((n_rows, n), BF16),
        compiler_params=_cp("parallel", "parallel"),
        name="ffn_up",
    )(u, wg, wu, cw, cb, halo)


def _conv_ffn(h, n_rows, gain, mod_rows, mod, layer, wg, wu, cw, cb, wd):
    u = _modulate(h, n_rows, gain, mod_rows, layer, 3, 4)
    hid = _ffn_up(u, n_rows, wg, wu, cw, cb)
    return _mm_ktiled(hid, wd, n_rows, h, (mod, layer, 5))


def _rope(x, rope_ref):
    c, sa, sb = rope_ref[:, 0:LANE], rope_ref[:, LANE:2 * LANE], rope_ref[:, 2 * LANE:3 * LANE]
    quarter = MLA_ROPE // 4
    return x * c + pltpu.roll(x, LANE - quarter, 1) * sa + pltpu.roll(x, quarter, 1) * sb


def _rms(x, width):
    return lax.rsqrt(jnp.sum(x * x, axis=-1, keepdims=True) * (1.0 / width) + EPS)


def _qproj_kernel(cq_ref, ag_ref, w_ref, hg_ref, rope_ref, o_ref, *, heads):
    x = cq_ref[...]
    xn = ((x * _rms(x, x.shape[1])) * ag_ref[...]).astype(BF16)
    y = _dot(xn, w_ref[...])
    for hh in range(heads):
        yh = y[:, hh * QK_PAD:(hh + 1) * QK_PAD]
        yn = (yh * _rms(yh, MLA_NOPE + MLA_ROPE)) * hg_ref[...]
        o_ref[:, hh * QK_PAD:hh * QK_PAD + LANE] = yn[:, :LANE].astype(o_ref.dtype)
        o_ref[:, hh * QK_PAD + LANE:(hh + 1) * QK_PAD] = _rope(yn[:, LANE:], rope_ref).astype(o_ref.dtype)


def _qproj(y_in, n_rows, ag, w, hg, rope):
    heads = _pick(MLA_HEADS, (4, 2, 1))
    tm = _pick(n_rows, (544, 512, 256))
    tn = heads * QK_PAD
    return pl.pallas_call(
        functools.partial(_qproj_kernel, heads=heads),
        grid=(n_rows // tm, w.shape[1] // tn),
        in_specs=[pl.BlockSpec((tm, MLA_Q_RANK), lambda i, j: (i, 0)),
                  pl.BlockSpec((1, MLA_Q_RANK), lambda i, j: (0, 0)),
                  pl.BlockSpec((MLA_Q_RANK, tn), lambda i, j: (0, j)),
                  pl.BlockSpec((1, QK_PAD), lambda i, j: (0, 0)),
                  pl.BlockSpec((tm, 3 * LANE), lambda i, j: (i, 0))],
        out_specs=pl.BlockSpec((tm, tn), lambda i, j: (i, j)),
        out_shape=jax.ShapeDtypeStruct((n_rows, w.shape[1]), BF16),
        compiler_params=_cp("parallel", "parallel"),
        name="q_proj",
    )(y_in, ag, w, hg, rope)


def _kvproj_kernel(ckv_ref, kr_ref, ag_ref, w_ref, gn_ref, gr_ref, rope_ref, k_ref, v_ref, *, heads):
    x = ckv_ref[...]
    xn = ((x * _rms(x, x.shape[1])) * ag_ref[...]).astype(BF16)
    y = _dot(xn, w_ref[...])
    kr = kr_ref[...]
    kr_ss = jnp.sum(kr * kr, axis=-1, keepdims=True)
    width = MLA_NOPE + MLA_VDIM
    for hh in range(heads):
        kn = y[:, hh * width:hh * width + MLA_NOPE]
        r = lax.rsqrt((jnp.sum(kn * kn, axis=-1, keepdims=True) + kr_ss) * (1.0 / (MLA_NOPE + MLA_ROPE)) + EPS)
        k_ref[:, hh * QK_PAD:hh * QK_PAD + LANE] = ((kn * r) * gn_ref[...]).astype(k_ref.dtype)
        k_ref[:, hh * QK_PAD + LANE:(hh + 1) * QK_PAD] = _rope((kr * r) * gr_ref[...], rope_ref).astype(k_ref.dtype)
        v_ref[:, hh * MLA_VDIM:(hh + 1) * MLA_VDIM] = y[:, hh * width + MLA_NOPE:(hh + 1) * width].astype(v_ref.dtype)


def _kvproj(y_in, n_rows, ag, w, gn, gr, rope):
    heads = _pick(MLA_HEADS, (4, 2, 1))
    assert MLA_NOPE == LANE and MLA_VDIM == LANE and MLA_Q_RANK % MLA_KV_RANK == 0
    tm = _pick(n_rows, (544, 512, 256))
    tn = heads * (MLA_NOPE + MLA_VDIM)
    kr_blk = (MLA_Q_RANK + MLA_KV_RANK) // LANE
    return pl.pallas_call(
        functools.partial(_kvproj_kernel, heads=heads),
        grid=(n_rows // tm, w.shape[1] // tn),
        in_specs=[pl.BlockSpec((tm, MLA_KV_RANK), lambda i, j: (i, MLA_Q_RANK // MLA_KV_RANK)),
                  pl.BlockSpec((tm, LANE), lambda i, j: (i, kr_blk)),
                  pl.BlockSpec((1, MLA_KV_RANK), lambda i, j: (0, 0)),
                  pl.BlockSpec((MLA_KV_RANK, tn), lambda i, j: (0, j)),
                  pl.BlockSpec((1, LANE), lambda i, j: (0, 0)),
                  pl.BlockSpec((1, LANE), lambda i, j: (0, 0)),
                  pl.BlockSpec((tm, 3 * LANE), lambda i, j: (i, 0))],
        out_specs=[pl.BlockSpec((tm, heads * QK_PAD), lambda i, j: (i, j)),
                   pl.BlockSpec((tm, heads * MLA_VDIM), lambda i, j: (i, j))],
        out_shape=[jax.ShapeDtypeStruct((n_rows, MLA_HEADS * QK_PAD), BF16),
                   jax.ShapeDtypeStruct((n_rows, MLA_HEADS * MLA_VDIM), BF16)],
        compiler_params=_cp("parallel", "parallel"),
        name="kv_proj",
    )(y_in, y_in, ag, w, gn, gr, rope)


def _attn_kernel(*refs, n_seg, scale):
    q = refs[0][...]
    k_refs, v_refs, o_ref = refs[1:1 + n_seg], refs[1 + n_seg:1 + 2 * n_seg], refs[1 + 2 * n_seg]
    s = [_dot_nt(q, k_ref[...]) * scale for k_ref in k_refs]
    m = jnp.max(s[0], axis=-1, keepdims=True)
    for si in s[1:]:
        m = jnp.maximum(m, jnp.max(si, axis=-1, keepdims=True))
    p = [jnp.exp(si - m) for si in s]
    den = jnp.sum(p[0], axis=-1, keepdims=True)
    acc = _dot(p[0].astype(BF16), v_refs[0][...])
    for pi, v_ref in zip(p[1:], v_refs[1:]):
        den += jnp.sum(pi, axis=-1, keepdims=True)
        acc += _dot(pi.astype(BF16), v_ref[...])
    o_ref[...] = (acc / den).astype(o_ref.dtype)


def _attention(q, k, v, latent):
    scale = (MLA_NOPE + MLA_ROPE) ** -0.5
    ctx_blk0 = _n_lat() // CTX_LEN
    if latent:
        tq = _pick(SEQ, (512, 256))
        q_per_b = SEQ // tq
        q_spec = pl.BlockSpec((tq, QK_PAD), lambda b, h, i: (b * q_per_b + i, h))
        o_spec = pl.BlockSpec((tq, MLA_VDIM), lambda b, h, i: (b * q_per_b + i, h))
        k_specs = [pl.BlockSpec((CTX_LEN, QK_PAD), lambda b, h, i: (ctx_blk0 + b, h)),
                   pl.BlockSpec((SEQ, QK_PAD), lambda b, h, i: (b, h))]
        v_specs = [pl.BlockSpec((CTX_LEN, MLA_VDIM), lambda b, h, i: (ctx_blk0 + b, h)),
                   pl.BlockSpec((SEQ, MLA_VDIM), lambda b, h, i: (b, h))]
        n_out, grid = _n_lat(), (BATCH, MLA_HEADS, q_per_b)
    else:
        q_spec = pl.BlockSpec((CTX_LEN, QK_PAD), lambda b, h, i: (ctx_blk0 + b, h))
        o_spec = pl.BlockSpec((CTX_LEN, MLA_VDIM), lambda b, h, i: (b, h))
        k_specs = [pl.BlockSpec((CTX_LEN, QK_PAD), lambda b, h, i: (ctx_blk0 + b, h))]
        v_specs = [pl.BlockSpec((CTX_LEN, MLA_VDIM), lambda b, h, i: (ctx_blk0 + b, h))]
        n_out, grid = BATCH * CTX_LEN, (BATCH, MLA_HEADS, 1)
    n_seg = len(k_specs)
    return pl.pallas_call(
        functools.partial(_attn_kernel, n_seg=n_seg, scale=scale),
        grid=grid,
        in_specs=[q_spec] + k_specs + v_specs,
        out_specs=o_spec,
        out_shape=jax.ShapeDtypeStruct((n_out, MLA_HEADS * MLA_VDIM), BF16),
        compiler_params=_cp("parallel", "parallel", "arbitrary"),
        name="attn_lat" if latent else "attn_ctx",
    )(q, *([k] * n_seg), *([v] * n_seg))


def _sconv_kernel(b_ref, c_ref, x_ref, cp_ref, xp_ref, cn_ref, xn_ref, w_ref, o_ref, *, tm):
    p = c_ref[...] * x_ref[...]
    prev = cp_ref[SUBLANE - 1:SUBLANE, :] * xp_ref[SUBLANE - 1:SUBLANE, :]
    nxt = cn_ref[0:1, :] * xn_ref[0:1, :]
    dn, up = _shift_rows(p, prev, nxt, pl.program_id(0) * tm)
    conv = dn * w_ref[0:1, :] + p * w_ref[1:2, :] + up * w_ref[2:3, :]
    o_ref[...] = (b_ref[...] * conv).astype(o_ref.dtype)


def _sconv(y_in, n_rows, w, col0, width):
    tm = ROW_BLK
    tn = _pick(width, (1024, 512, 256, 128))
    blk0 = [(col0 + k * width) // tn for k in range(3)]
    main = [pl.BlockSpec((tm, tn), lambda i, j, o=o: (i, o + j)) for o in blk0]
    cp, cn = _halo_specs(tm, tn, n_rows, blk0[1])
    xp, xn = _halo_specs(tm, tn, n_rows, blk0[2])
    return pl.pallas_call(
        functools.partial(_sconv_kernel, tm=tm),
        grid=(n_rows // tm, width // tn),
        in_specs=main + [cp, xp, cn, xn, pl.BlockSpec((3, tn), lambda i, j: (0, j))],
        out_specs=pl.BlockSpec((tm, tn), lambda i, j: (i, j)),
        out_shape=jax.ShapeDtypeStruct((n_rows, width), BF16),
        compiler_params=_cp("parallel", "parallel"),
        name="short_conv",
    )(*([y_in] * 7), w)


def _gdn_conv_kernel(x_ref, xp_ref, xn_ref, w_ref, o_ref, *, tm, q_blks, qk_blks):
    j = pl.program_id(1)
    x = x_ref[...]
    dn, up = _shift_rows(x, xp_ref[SUBLANE - 1:SUBLANE, :], xn_ref[0:1, :], pl.program_id(0) * tm)
    y = _silu(dn * w_ref[0:1, :] + x * w_ref[1:2, :] + up * w_ref[2:3, :])
    q_scale = jnp.where(j < q_blks, GDN_DK ** -0.5, 1.0)
    for hh in range(x.shape[1] // GDN_DK):
        yh = y[:, hh * GDN_DK:(hh + 1) * GDN_DK]
        inv = lax.rsqrt(jnp.sum(yh * yh, axis=-1, keepdims=True) + EPS)
        inv = jnp.where(j < qk_blks, inv, 1.0)
        o_ref[:, hh * GDN_DK:(hh + 1) * GDN_DK] = (yh * inv) * q_scale


def _gdn_conv(y_in, n_rows, w):
    width = w.shape[1]
    key_w = GDN_QK_HEADS * GDN_DK
    tm = ROW_BLK
    tn = _pick(key_w, (1024, 512, 256, 128))
    xp, xn = _halo_specs(tm, tn, n_rows, 0)
    return pl.pallas_call(
        functools.partial(_gdn_conv_kernel, tm=tm, q_blks=key_w // tn, qk_blks=2 * key_w // tn),
        grid=(n_rows // tm, width // tn),
        in_specs=[pl.BlockSpec((tm, tn), lambda i, j: (i, j)), xp, xn,
                  pl.BlockSpec((3, tn), lambda i, j: (0, j))],
        out_specs=pl.BlockSpec((tm, tn), lambda i, j: (i, j)),
        out_shape=jax.ShapeDtypeStruct((n_rows, width), F32),
        compiler_params=_cp("parallel", "parallel"),
        name="gdn_conv",
    )(y_in, y_in, y_in, w)


def _gdn_gate_kernel(ba_ref, alog_ref, dtb_ref, o_ref):
    x = ba_ref[...]
    lane = lax.broadcasted_iota(jnp.int32, x.shape, 1)
    z = x + dtb_ref[...]
    softplus = jnp.maximum(z, 0.0) + jnp.log(1.0 + jnp.exp(-jnp.abs(z)))
    o_ref[...] = jnp.where(lane < x.shape[1] // 2, _sigmoid(x), -jnp.exp(alog_ref[...]) * softplus)


def _gdn_gate(ba, n_rows, alog, dtb):
    w = ba.shape[1]
    return pl.pallas_call(
        _gdn_gate_kernel,
        grid=(n_rows // ROW_BLK,),
        in_specs=[pl.BlockSpec((ROW_BLK, w), lambda i: (i, 0)),
                  pl.BlockSpec((1, w), lambda i: (0, 0)),
                  pl.BlockSpec((1, w), lambda i: (0, 0))],
        out_specs=pl.BlockSpec((ROW_BLK, w), lambda i: (i, 0)),
        out_shape=jax.ShapeDtypeStruct((n_rows, w), F32),
        compiler_params=_cp("parallel"),
        name="gdn_gate",
    )(ba, alog, dtb)


def _blk_mask(ii, jj, size):
    shift = size.bit_length() - 1
    return jnp.right_shift(ii, shift) == jnp.right_shift(jj, shift)


def _unit_tri_inverse(l_mat, ii, jj):
    eye = (ii == jj).astype(F32)
    x = jnp.where(_blk_mask(ii, jj, 16), -l_mat, 0.0)
    t = eye + x
    for _ in range(3):
        xb = x.astype(BF16)
        x = _dot(xb, xb)
        t = t + _dot(t.astype(BF16), x.astype(BF16))
    for size in (16, 32):
        off = jnp.where(_blk_mask(ii, jj, 2 * size) & jnp.logical_not(_blk_mask(ii, jj, size)), l_mat, 0.0)
        tb = t.astype(BF16)
        t = t - _dot(tb, _dot(off.astype(BF16), tb).astype(BF16))
    return t


def _gdn_prep_kernel(q_ref, k_ref, v_ref, bf_ref, bb_ref, gf_ref, gb_ref,
                     u_ref, w_ref, qd_ref, kd_ref, ic_ref, cd_ref):
    n = GDN_BLK
    ii = lax.broadcasted_iota(jnp.int32, (n, n), 0)
    jj = lax.broadcasted_iota(jnp.int32, (n, n), 1)
    same = _blk_mask(ii, jj, GDN_CHUNK)
    eye = ii == jj
    lower, upper = same & (jj <= ii), same & (jj >= ii)
    q, k, v = q_ref[...], k_ref[...], v_ref[...]
    qb, kb16 = q.astype(BF16), k.astype(BF16)
    qk = _dot_nt(qb, kb16)
    for d, (b_ref, g_ref) in enumerate(((bf_ref, gf_ref), (bb_ref, gb_ref))):
        incl, incl_t = (lower, upper) if d == 0 else (upper, lower)
        strict = incl & jnp.logical_not(eye)
        g_row, b_row = g_ref[...], b_ref[...]
        g_col = jnp.sum(jnp.where(eye, g_row, 0.0), axis=1, keepdims=True)
        b_col = jnp.sum(jnp.where(eye, b_row, 0.0), axis=1, keepdims=True)
        gc_col = jnp.sum(jnp.where(incl, g_row, 0.0), axis=1, keepdims=True)
        gc_row = jnp.sum(jnp.where(incl_t, g_col, 0.0), axis=0, keepdims=True)
        gtot_col = jnp.sum(jnp.where(same, g_row, 0.0), axis=1, keepdims=True)
        decay = jnp.where(incl, jnp.exp(jnp.where(incl, gc_col - gc_row, 0.0)), 0.0)
        kbeta = k * b_col
        l_mat = jnp.where(strict, _dot_nt(kbeta.astype(BF16), kb16) * decay, 0.0)
        intra = jnp.where(incl, qk * decay, 0.0)
        e_col = jnp.exp(gc_col)
        rhs = jnp.concatenate([v * b_col, kbeta * e_col], axis=1).astype(BF16)
        sol = _dot(_unit_tri_inverse(l_mat, ii, jj).astype(BF16), rhs)
        u_ref[d] = sol[:, :GDN_DV]
        w_ref[d] = sol[:, GDN_DV:].astype(w_ref.dtype)
        qd_ref[d] = (q * e_col).astype(qd_ref.dtype)
        kd_ref[d] = (k * jnp.exp(gtot_col - gc_col)).astype(kd_ref.dtype)
        ic_ref[d] = (intra[:, :LANE] + intra[:, LANE:]).astype(ic_ref.dtype)
        cd = jnp.broadcast_to(jnp.exp(gtot_col), (n, LANE))
        cd_ref[d] = jnp.concatenate([cd[c * GDN_CHUNK:c * GDN_CHUNK + SUBLANE] for c in range(n // GDN_CHUNK)], axis=0)


def _gdn_prep(qkv, bg_t, n_rows):
    assert GDN_DK == LANE and GDN_DV == LANE and GDN_BLK == 2 * LANE
    nb, hv, rep = n_rows // GDN_BLK, GDN_V_HEADS, GDN_V_HEADS // GDN_QK_HEADS
    half = bg_t.shape[0] // 2
    cd_rows = GDN_BLK // GDN_CHUNK * SUBLANE
    row = lambda base: pl.BlockSpec((None, 1, GDN_BLK), lambda i, h: (base + h, 0, i))
    big = lambda dt: jax.ShapeDtypeStruct((2, n_rows, hv * GDN_DV), dt)
    big_spec = pl.BlockSpec((2, GDN_BLK, GDN_DV), lambda i, h: (0, i, h))
    return pl.pallas_call(
        _gdn_prep_kernel,
        grid=(nb, hv),
        in_specs=[pl.BlockSpec((GDN_BLK, GDN_DK), lambda i, h: (i, h // rep)),
                  pl.BlockSpec((GDN_BLK, GDN_DK), lambda i, h: (i, GDN_QK_HEADS + h // rep)),
                  pl.BlockSpec((GDN_BLK, GDN_DV), lambda i, h: (i, 2 * GDN_QK_HEADS + h)),
                  row(0), row(hv), row(half), row(half + hv)],
        out_specs=[big_spec] * 5 + [pl.BlockSpec((2, None, cd_rows, LANE), lambda i, h: (0, h, i, 0))],
        out_shape=[big(F32), big(BF16), big(BF16), big(BF16), big(BF16),
                   jax.ShapeDtypeStruct((2, hv, nb * cd_rows, LANE), F32)],
        compiler_params=_cp("parallel", "parallel"),
        name="gdn_prep",
    )(qkv, qkv, qkv, bg_t, bg_t, bg_t, bg_t)


def _gdn_scan_kernel(*refs, heads):
    ins, (of_ref, ob_ref, s_ref) = refs[:12], refs[12:]
    n_chunk = GDN_BLK // GDN_CHUNK

    @pl.when(pl.program_id(2) == 0)
    def _():
        s_ref[...] = jnp.zeros_like(s_ref)

    for d, o_ref in enumerate((of_ref, ob_ref)):
        u_ref, w_ref, qd_ref, kd_ref, ic_ref, cd_ref = ins[6 * d:6 * d + 6]
        order = list(range(n_chunk)) if d == 0 else list(reversed(range(n_chunk)))
        for hh in range(heads):
            cols = slice(hh * GDN_DV, (hh + 1) * GDN_DV)
            s = s_ref[d, hh]
            for c in order:
                rows = slice(c * GDN_CHUNK, (c + 1) * GDN_CHUNK)
                sb = s.astype(BF16)
                v_new = u_ref[rows, cols] - _dot(w_ref[rows, cols], sb)
                vb = v_new.astype(BF16)
                o_ref[rows, cols] = (_dot(qd_ref[rows, cols], sb)
                                     + _dot(ic_ref[rows, cols], jnp.concatenate([vb, vb], axis=0)))
                s = s * cd_ref[hh, c * SUBLANE:c * SUBLANE + 1, :] + _dot_tn(kd_ref[rows, cols], vb)
            s_ref[d, hh] = s


def _gdn_scan(prep, n_rows):
    heads = 2
    hv = GDN_V_HEADS
    lat_blk, ctx_blk = SEQ // GDN_BLK, CTX_LEN // GDN_BLK
    steps = ctx_blk + lat_blk
    ctx0 = _n_lat() // GDN_BLK
    cd_rows = GDN_BLK // GDN_CHUNK * SUBLANE

    def blk_f(b, n):
        return jnp.where(n < ctx_blk, ctx0 + b * ctx_blk + n, b * lat_blk + n - ctx_blk)

    def blk_b(b, n):
        return jnp.where(n < ctx_blk, ctx0 + b * ctx_blk + ctx_blk - 1 - n, b * lat_blk + steps - 1 - n)

    in_specs = []
    for d, blk in enumerate((blk_f, blk_b)):
        big = pl.BlockSpec((None, GDN_BLK, heads * GDN_DV), lambda b, h, n, d=d, blk=blk: (d, blk(b, n), h))
        in_specs += [big] * 5 + [pl.BlockSpec((None, heads, cd_rows, LANE),
                                              lambda b, h, n, d=d, blk=blk: (d, h, blk(b, n), 0))]
    u, w, qd, kd, ic, cd = prep
    out = jax.ShapeDtypeStruct((n_rows, hv * GDN_DV), F32)
    return pl.pallas_call(
        functools.partial(_gdn_scan_kernel, heads=heads),
        grid=(BATCH, hv // heads, steps),
        in_specs=in_specs,
        out_specs=[pl.BlockSpec((GDN_BLK, heads * GDN_DV), lambda b, h, n: (blk_f(b, n), h)),
                   pl.BlockSpec((GDN_BLK, heads * GDN_DV), lambda b, h, n: (blk_b(b, n), h))],
        out_shape=[out, out],
        scratch_shapes=[pltpu.VMEM((2, heads, GDN_DK, GDN_DV), F32)],
        compiler_params=_cp("parallel", "parallel", "arbitrary"),
        name="gdn_scan",
    )(u, w, qd, kd, ic, cd, u, w, qd, kd, ic, cd)


def _gdn_readout_kernel(of_ref, ob_ref, z_ref, g_ref, o_ref):
    o = of_ref[...] + ob_ref[...]
    z = z_ref[...]
    for hh in range(o.shape[1] // GDN_DV):
        cols = slice(hh * GDN_DV, (hh + 1) * GDN_DV)
        oh = o[:, cols]
        y = (oh * _rms(oh, GDN_DV)) * g_ref[...]
        o_ref[:, cols] = (y * _silu(z[:, cols])).astype(o_ref.dtype)


def _gdn_readout(o_f, o_b, y_in, z_col0, gain, n_rows):
    width = GDN_V_HEADS * GDN_DV
    tn = _pick(width, (1024, 512, 256, 128))
    spec = pl.BlockSpec((ROW_BLK, tn), lambda i, j: (i, j))
    return pl.pallas_call(
        _gdn_readout_kernel,
        grid=(n_rows // ROW_BLK, width // tn),
        in_specs=[spec, spec, pl.BlockSpec((ROW_BLK, tn), lambda i, j: (i, z_col0 // tn + j)),
                  pl.BlockSpec((1, GDN_DV), lambda i, j: (0, 0))],
        out_specs=spec,
        out_shape=jax.ShapeDtypeStruct((n_rows, width), BF16),
        compiler_params=_cp("parallel", "parallel"),
        name="gdn_readout",
    )(o_f, o_b, y_in, gain.reshape(1, GDN_DV))


def _rope_table(n_rows):
    n_freq = MLA_ROPE // 4
    inv_freq = ROPE_THETA ** (-jnp.arange(n_freq, dtype=F32) / n_freq)
    t = jnp.arange(SEQ)
    ang = jnp.concatenate([(t // GRID_W).astype(F32)[:, None] * inv_freq,
                           (t % GRID_W).astype(F32)[:, None] * inv_freq], axis=1)
    cos, sin = jnp.cos(ang), jnp.sin(ang)
    zero = jnp.zeros_like(sin[:, :n_freq])
    pad = jnp.zeros((SEQ, LANE - MLA_ROPE), F32)
    c = jnp.concatenate([cos[:, :n_freq], cos[:, :n_freq], cos[:, n_freq:], cos[:, n_freq:], pad + 1.0], axis=1)
    sa = jnp.concatenate([-sin[:, :n_freq], zero, -sin[:, n_freq:], zero, pad], axis=1)
    sb = jnp.concatenate([zero, sin[:, :n_freq], zero, sin[:, n_freq:], pad], axis=1)
    lat = jnp.tile(jnp.concatenate([c, sa, sb], axis=1), (BATCH, 1))
    n_ctx = n_rows - _n_lat()
    ident = jnp.concatenate([jnp.ones((n_ctx, LANE), F32), jnp.zeros((n_ctx, 2 * LANE), F32)], axis=1)
    return jnp.concatenate([lat, ident], axis=0)


def _pad_cols(w, n):
    return jnp.pad(w, ((0, 0), (0, n - w.shape[1])))


def _even_layer(h, n_rows, need_ctx, mod, mod_rows, layer, gain, p):
    w_in, q_a_g, kv_a_g, w_qb, w_kvb, q_g, k_g, sc_w, w_out = p
    d = h.shape[1]
    sc_width = d - MLA_HEADS * MLA_VDIM
    head_w = MLA_NOPE + MLA_ROPE
    c0 = MLA_Q_RANK + MLA_KV_RANK
    conv0 = _round_up(c0 + LANE, _pick(sc_width, (1024, 512, 256, 128)))
    n_in = _round_up(conv0 + 3 * sc_width, 1024)
    w_in_p = jnp.concatenate([_pad_cols(w_in[:, :c0 + MLA_ROPE], conv0), w_in[:, c0 + MLA_ROPE:]], axis=1)
    w_in_p = _pad_cols(w_in_p, n_in).astype(BF16)
    w_q = w_qb.reshape(MLA_Q_RANK, MLA_HEADS, head_w)
    w_q = jnp.pad(w_q, ((0, 0), (0, 0), (0, QK_PAD - head_w))).reshape(MLA_Q_RANK, MLA_HEADS * QK_PAD).astype(BF16)
    rope = _rope_table(n_rows)

    u = _modulate(h, n_rows, gain, mod_rows, layer, 0, 1)
    y = _mm([u], [w_in_p], n_rows, name="even_in")
    q = _qproj(y, n_rows, q_a_g.reshape(1, -1), w_q, jnp.pad(q_g, (0, QK_PAD - head_w)).reshape(1, QK_PAD), rope)
    k, v = _kvproj(y, n_rows, kv_a_g.reshape(1, -1), w_kvb.astype(BF16), k_g[:MLA_NOPE].reshape(1, LANE),
                   jnp.pad(k_g[MLA_NOPE:], (0, LANE - MLA_ROPE)).reshape(1, LANE), rope)
    o = _attention(q, k, v, True)
    if need_ctx:
        o = jnp.concatenate([o, _attention(q, k, v, False)], axis=0)
    n_out = n_rows if need_ctx else _n_lat()
    conv = _sconv(y, n_out, sc_w, conv0, sc_width)
    w_o = w_out.astype(BF16)
    return _mm([o, conv], [w_o[:MLA_HEADS * MLA_VDIM], w_o[MLA_HEADS * MLA_VDIM:]], n_out,
               res=h, gate=(mod, layer, 2), tn=_pick(d, (512, 256, 128)), name="even_out")


def _odd_layer(h, n_rows, need_ctx, mod, mod_rows, layer, gain, p):
    w_in, conv_w, a_log, dt_bias, o_g, w_out = p
    qkv_w = 2 * GDN_QK_HEADS * GDN_DK + GDN_V_HEADS * GDN_DV
    val_w = GDN_V_HEADS * GDN_DV
    main_w = qkv_w + val_w
    w_main = w_in[:, :main_w].astype(BF16)
    gate_w = _round_up(4 * GDN_V_HEADS, LANE)
    half = gate_w // 2
    w_ba = jnp.concatenate([_pad_cols(w_in[:, main_w:main_w + 2 * GDN_V_HEADS], half),
                            _pad_cols(w_in[:, main_w + 2 * GDN_V_HEADS:], half)], axis=1).astype(BF16)
    zeros = jnp.zeros((1, half), F32)
    alog = jnp.concatenate([zeros, _pad_cols(a_log.reshape(1, -1), half)], axis=1)
    dtb = jnp.concatenate([zeros, _pad_cols(dt_bias.reshape(1, -1), half)], axis=1)

    u = _modulate(h, n_rows, gain, mod_rows, layer, 0, 1)
    y = _mm([u], [w_main], n_rows, name="odd_in")
    ba = _mm([u], [w_ba], n_rows, name="odd_in_gates")
    qkv = _gdn_conv(y, n_rows, conv_w)
    bg = _gdn_gate(ba, n_rows, alog, dtb)
    bg_t = bg.T.reshape(gate_w, 1, n_rows)
    o_f, o_b = _gdn_scan(_gdn_prep(qkv, bg_t, n_rows), n_rows)
    n_out = n_rows if need_ctx else _n_lat()
    yo = _gdn_readout(o_f, o_b, y, qkv_w, o_g, n_out)
    return _mm([yo], [w_out.astype(BF16)], n_out, res=h, gate=(mod, layer, 2),
               tn=_pick(h.shape[1], (512, 256, 128)), name="odd_out")


def kernel(x, c, ctx, c_ctx, ada_w, ada_b, norm_mix, norm_ffn, ffn_w_gate, ffn_w_up, ffn_conv_w, ffn_conv_b,
           ffn_w_down, a_w_in, a_q_a_norm, a_kv_a_norm, a_w_qb, a_w_kvb, a_q_norm, a_k_norm, a_sc_conv, a_w_out,
           c_w_in, c_conv_w, c_a_log, c_dt_bias, c_o_norm, c_w_out):
    bn, t, d = x.shape
    depth = ada_w.shape[0]
    assert (bn, t, ctx.shape[1], ffn_w_gate.shape[2]) == (BATCH, SEQ, CTX_LEN, D_FF)
    assert CTX_LEN % GDN_BLK == 0 and SEQ % GDN_BLK == 0
    h = jnp.concatenate([x.reshape(bn * t, d), ctx.reshape(bn * CTX_LEN, d)], axis=0)
    n_all = h.shape[0]
    cond8 = jnp.concatenate([c, c_ctx[None], jnp.zeros((SUBLANE - bn - 1, d), F32)], axis=0)
    mod = _ada(cond8, ada_w, ada_b)
    mod_rows = mod.reshape(depth * SUBLANE * 6, 1, d)
    d_ff_p = _round_up(D_FF, 512)
    for l in range(depth):
        last = l == depth - 1
        n_rows = h.shape[0]
        if l % 2 == 0:
            i = l // 2
            h = _even_layer(h, n_rows, not last, mod, mod_rows, l, norm_mix[l],
                            (a_w_in[i], a_q_a_norm[i], a_kv_a_norm[i], a_w_qb[i], a_w_kvb[i], a_q_norm[i],
                             a_k_norm[i], a_sc_conv[i], a_w_out[i]))
        else:
            i = l // 2
            h = _odd_layer(h, n_rows, not last, mod, mod_rows, l, norm_mix[l],
                           (c_w_in[i], c_conv_w[i], c_a_log[i], c_dt_bias[i], c_o_norm[i], c_w_out[i]))
        h = _conv_ffn(h, h.shape[0], norm_ffn[l], mod_rows, mod, l,
                      _pad_cols(ffn_w_gate[l], d_ff_p).astype(BF16), _pad_cols(ffn_w_up[l], d_ff_p).astype(BF16),
                      _pad_cols(ffn_conv_w[l], d_ff_p), _pad_cols(ffn_conv_b[l].reshape(1, -1), d_ff_p),
                      jnp.pad(ffn_w_down[l], ((0, d_ff_p - D_FF), (0, 0))).astype(BF16))
    return h[:bn * t].reshape(bn, t, d)
```

```python
import functools
import math

import jax
import jax.numpy as jnp
from jax import lax
from jax.experimental import pallas as pl
from jax.experimental.pallas import tpu as pltpu

F32 = jnp.float32
BF16 = jnp.bfloat16

BATCH = 2
SEQ = 4096
GRID_W = 64
CTX_LEN = 256
EPS = 1e-6
D_FF = 11008

MLA_HEADS = 16
MLA_NOPE = 128
MLA_ROPE = 64
MLA_VDIM = 128
MLA_Q_RANK = 1024
MLA_KV_RANK = 512
ROPE_THETA = 10000.0

GDN_QK_HEADS = 16
GDN_V_HEADS = 32
GDN_DK = 128
GDN_DV = 128

LANE = 128
SUBLANE = 8
QK_PAD = 256
V_PAD = 256
ROW_BLK = 256
GDN_BLK = 256
VMEM_LIMIT = 56 * 2**20


def _cp(*sem, vmem=VMEM_LIMIT):
    return pltpu.CompilerParams(dimension_semantics=sem, vmem_limit_bytes=vmem)


def _pick(n, prefs):
    for p in prefs:
        if n % p == 0:
            return p
    raise ValueError(f"no tile for {n} in {prefs}")


def _round_up(n, m):
    return (n + m - 1) // m * m


def _sigmoid(x):
    return 1.0 / (1.0 + jnp.exp(-x))


def _silu(x):
    return x * _sigmoid(x)


def _dot(a, b):
    return jnp.dot(a, b, preferred_element_type=F32)


def _dot_nt(a, b):
    return lax.dot_general(a, b, (((1,), (1,)), ((), ())), preferred_element_type=F32)


def _dot_tn(a, b):
    return lax.dot_general(a, b, (((0,), (0,)), ((), ())), preferred_element_type=F32)


def _n_lat():
    return BATCH * SEQ


def _seq_edges(row0, tm):
    assert SEQ & (SEQ - 1) == 0 and CTX_LEN & (CTX_LEN - 1) == 0 and SEQ % CTX_LEN == 0
    r = row0 + lax.broadcasted_iota(jnp.int32, (tm, 1), 0)
    is_ctx = r >= _n_lat()
    first = (jnp.bitwise_and(r, CTX_LEN - 1) == 0) & ((jnp.bitwise_and(r, SEQ - 1) == 0) | is_ctx)
    r1 = r + 1
    last = (jnp.bitwise_and(r1, CTX_LEN - 1) == 0) & ((jnp.bitwise_and(r1, SEQ - 1) == 0) | is_ctx)
    return first, last


def _shift_rows(x, prev_row, next_row, row0):
    tm = x.shape[0]
    ridx = lax.broadcasted_iota(jnp.int32, (tm, 1), 0)
    first, last = _seq_edges(row0, tm)
    dn = jnp.where(ridx == 0, prev_row, pltpu.roll(x, 1, 0))
    dn = jnp.where(first, 0.0, dn)
    up = jnp.where(ridx == tm - 1, next_row, pltpu.roll(x, tm - 1, 0))
    up = jnp.where(last, 0.0, up)
    return dn, up


def _row_select(row0, tm, table):
    r = row0 + lax.broadcasted_iota(jnp.int32, (tm, 1), 0)
    out = table[BATCH:BATCH + 1]
    for b in reversed(range(BATCH)):
        out = jnp.where(r < (b + 1) * SEQ, table[b:b + 1], out)
    return out


def _halo_specs(tm, tn, n_rows, col_blk0):
    tmb, last = tm // SUBLANE, n_rows // SUBLANE - 1
    prev = pl.BlockSpec((SUBLANE, tn), lambda i, j: (jnp.maximum(i * tmb - 1, 0), col_blk0 + j))
    nxt = pl.BlockSpec((SUBLANE, tn), lambda i, j: (jnp.minimum((i + 1) * tmb, last), col_blk0 + j))
    return prev, nxt


def _ada_kernel(cond_ref, w_ref, b_ref, o_ref):
    a = _silu(cond_ref[...]).astype(BF16)
    o_ref[...] = _dot(a, w_ref[...].astype(BF16)) + b_ref[...]


def _ada(cond8, ada_w, ada_b):
    n_layer, d, n = ada_w.shape
    tn = _pick(n, (512, 256, 128))
    return pl.pallas_call(
        _ada_kernel,
        grid=(n_layer, n // tn),
        in_specs=[pl.BlockSpec((SUBLANE, d), lambda l, j: (0, 0)),
                  pl.BlockSpec((None, d, tn), lambda l, j: (l, 0, j)),
                  pl.BlockSpec((None, 1, tn), lambda l, j: (l, 0, j))],
        out_specs=pl.BlockSpec((None, SUBLANE, tn), lambda l, j: (l, 0, j)),
        out_shape=jax.ShapeDtypeStruct((n_layer, SUBLANE, n), F32),
        compiler_params=_cp("parallel", "parallel"),
        name="ada",
    )(cond8, ada_w, ada_b.reshape(n_layer, 1, n))


def _modulate_kernel(h_ref, g_ref, sh_ref, sc_ref, o_ref):
    x = h_ref[...]
    y = x * lax.rsqrt(jnp.mean(x * x, axis=-1, keepdims=True) + EPS)
    o_ref[...] = ((y * g_ref[...]) * (1.0 + sc_ref[...]) + sh_ref[...]).astype(o_ref.dtype)


def _modulate(h, n_rows, gain, mod_rows, layer, k_shift, k_scale):
    d = h.shape[1]
    blk_per_seq = SEQ // ROW_BLK

    def mod_spec(k):
        return pl.BlockSpec(
            (None, 1, d),
            lambda i: ((layer * SUBLANE + jnp.minimum(i // blk_per_seq, BATCH)) * 6 + k, 0, 0))

    return pl.pallas_call(
        _modulate_kernel,
        grid=(n_rows // ROW_BLK,),
        in_specs=[pl.BlockSpec((ROW_BLK, d), lambda i: (i, 0)),
                  pl.BlockSpec((1, d), lambda i: (0, 0)),
                  mod_spec(k_shift), mod_spec(k_scale)],
        out_specs=pl.BlockSpec((ROW_BLK, d), lambda i: (i, 0)),
        out_shape=jax.ShapeDtypeStruct((n_rows, d), BF16),
        compiler_params=_cp("parallel"),
        name="modulate",
    )(h, gain.reshape(1, d), mod_rows, mod_rows)


def _mm_kernel(*refs, n_pairs, gated, tm):
    a_refs, w_refs = refs[:n_pairs], refs[n_pairs:2 * n_pairs]
    acc = _dot(a_refs[0][...], w_refs[0][...])
    for a_ref, w_ref in zip(a_refs[1:], w_refs[1:]):
        acc += _dot(a_ref[...], w_ref[...])
    if gated:
        res_ref, gate_ref, o_ref = refs[2 * n_pairs:]
        gate = _row_select(pl.program_id(0) * tm, tm, gate_ref[...])
        acc = res_ref[...] + gate * acc
    else:
        o_ref = refs[2 * n_pairs]
    o_ref[...] = acc.astype(o_ref.dtype)


def _mm(a_list, w, n_rows, *, res=None, gate=None, tn=None, name="mm"):
    n_pairs = len(a_list)
    n = w.shape[1]
    tm = _pick(n_rows, (1088, 1024, 512, 256))
    tn = tn or _pick(n, (1024, 512, 256, 128))
    in_specs = [pl.BlockSpec((tm, a.shape[1]), lambda i, j: (i, 0)) for a in a_list]
    in_specs += [pl.BlockSpec((a.shape[1], tn), lambda i, j, p=p: (p, j)) for p, a in enumerate(a_list)]
    assert all(a.shape[1] == a_list[0].shape[1] for a in a_list) and w.shape[0] == n_pairs * a_list[0].shape[1]
    args = list(a_list) + [w] * n_pairs
    if res is not None:
        table, layer, chunk = gate
        d = table.shape[2] // 6
        in_specs += [pl.BlockSpec((tm, tn), lambda i, j: (i, j)),
                     pl.BlockSpec((None, SUBLANE, tn), lambda i, j: (layer, 0, chunk * (d // tn) + j))]
        args += [res, table]
    return pl.pallas_call(
        functools.partial(_mm_kernel, n_pairs=n_pairs, gated=res is not None, tm=tm),
        grid=(n_rows // tm, n // tn),
        in_specs=in_specs,
        out_specs=pl.BlockSpec((tm, tn), lambda i, j: (i, j)),
        out_shape=jax.ShapeDtypeStruct((n_rows, n), F32),
        compiler_params=_cp("parallel", "parallel"),
        name=name,
    )(*args)


def _mm_wcast_kernel(a_ref, w_ref, o_ref, wb_ref):
    @pl.when(pl.program_id(1) == 0)
    def _():
        wb_ref[...] = w_ref[...].astype(BF16)

    o_ref[...] = _dot(a_ref[...], wb_ref[...])


def _mm_wcast(a, w_stack, layer, n_cols, n_rows, name):
    kdim = a.shape[1]
    tm = _pick(n_rows, (1088, 1024, 512, 256))
    tn = _pick(n_cols, (512, 256, 128))
    return pl.pallas_call(
        _mm_wcast_kernel,
        grid=(n_cols // tn, n_rows // tm),
        in_specs=[pl.BlockSpec((tm, kdim), lambda j, i: (i, 0)),
                  pl.BlockSpec((None, kdim, tn), lambda j, i: (layer, 0, j))],
        out_specs=pl.BlockSpec((tm, tn), lambda j, i: (i, j)),
        out_shape=jax.ShapeDtypeStruct((n_rows, n_cols), F32),
        scratch_shapes=[pltpu.VMEM((kdim, tn), BF16)],
        compiler_params=_cp("parallel", "arbitrary"),
        name=name,
    )(a, w_stack)


def _mmk_kernel(a_ref, w_ref, res_ref, gate_ref, o_ref, acc_ref, *, tm):
    k = pl.program_id(2)

    @pl.when(k == 0)
    def _():
        acc_ref[...] = jnp.zeros_like(acc_ref)

    acc_ref[...] += _dot(a_ref[...], w_ref[...])

    @pl.when(k == pl.num_programs(2) - 1)
    def _():
        gate = _row_select(pl.program_id(0) * tm, tm, gate_ref[...])
        o_ref[...] = res_ref[...] + gate * acc_ref[...]


def _mm_ktiled(a, w_stack, layer, n_rows, res, gate):
    _, kdim, n = w_stack.shape
    table, gate_layer, chunk = gate
    d = table.shape[2] // 6
    tm = _pick(n_rows, (1088, 1024, 512, 256))
    tn = _pick(n, (1024, 512, 256, 128))
    tk = _pick(kdim, (2816, 2048, 1024, 512, 256))
    return pl.pallas_call(
        functools.partial(_mmk_kernel, tm=tm),
        grid=(n_rows // tm, n // tn, kdim // tk),
        in_specs=[pl.BlockSpec((tm, tk), lambda i, j, k: (i, k)),
                  pl.BlockSpec((None, tk, tn), lambda i, j, k: (layer, k, j)),
                  pl.BlockSpec((tm, tn), lambda i, j, k: (i, j)),
                  pl.BlockSpec((None, SUBLANE, tn), lambda i, j, k: (gate_layer, 0, chunk * (d // tn) + j))],
        out_specs=pl.BlockSpec((tm, tn), lambda i, j, k: (i, j)),
        out_shape=jax.ShapeDtypeStruct((n_rows, n), F32),
        scratch_shapes=[pltpu.VMEM((tm, tn), F32)],
        compiler_params=_cp("parallel", "parallel", "arbitrary"),
        name="ffn_down",
    )(a, w_stack, res, table)


def _ffn_up_kernel(u_ref, uh_ref, wg_ref, wu_ref, cw_ref, cb_ref, o_ref, wgb_ref, wub_ref, halo_ref, *,
                   tm, n_tiles):
    j, i = pl.program_id(0), pl.program_id(1)

    @pl.when(j < n_tiles)
    def _():
        @pl.when(i == 0)
        def _():
            wgb_ref[...] = wg_ref[...].astype(BF16)
            wub_ref[...] = wu_ref[...].astype(BF16)
            halo_ref[...] = _dot(uh_ref[...], wgb_ref[...])

        u = u_ref[...]
        g = _dot(u, wgb_ref[...])
        dn, up = _shift_rows(g, halo_ref[pl.ds(2 * i, 1), :], halo_ref[pl.ds(2 * i + 1, 1), :], i * tm)
        a = dn * cw_ref[0:1, :] + g * cw_ref[1:2, :] + up * cw_ref[2:3, :] + cb_ref[...]
        o_ref[...] = (_silu(a) * _dot(u, wub_ref[...])).astype(o_ref.dtype)

    @pl.when(j >= n_tiles)
    def _():
        o_ref[...] = jnp.zeros_like(o_ref)


def _ffn_up(u, n_rows, layer, wg_stack, wu_stack, cw_stack, cb_stack, n_out):
    _, d, n = wg_stack.shape
    tm = _pick(n_rows, (1088, 1024, 512, 256))
    tn = _pick(n, (256, 128))
    gm, n_tiles = n_rows // tm, n // tn
    n_halo = _round_up(2 * gm, SUBLANE)
    rows = []
    for i in range(gm):
        rows += [max(i * tm - 1, 0), min((i + 1) * tm, n_rows - 1)]
    rows += [0] * (n_halo - len(rows))
    u_halo = jnp.concatenate([u[r:r + 1] for r in rows], axis=0)
    col = lambda j, i: (layer, 0, jnp.minimum(j, n_tiles - 1))
    return pl.pallas_call(
        functools.partial(_ffn_up_kernel, tm=tm, n_tiles=n_tiles),
        grid=(n_out // tn, gm),
        in_specs=[pl.BlockSpec((tm, d), lambda j, i: (i, 0)),
                  pl.BlockSpec((n_halo, d), lambda j, i: (0, 0)),
                  pl.BlockSpec((None, d, tn), col),
                  pl.BlockSpec((None, d, tn), col),
                  pl.BlockSpec((None, 3, tn), col),
                  pl.BlockSpec((None, 1, tn), col)],
        out_specs=pl.BlockSpec((tm, tn), lambda j, i: (i, j)),
        out_shape=jax.ShapeDtypeStruct((n_rows, n_out), BF16),
        scratch_shapes=[pltpu.VMEM((d, tn), BF16), pltpu.VMEM((d, tn), BF16), pltpu.VMEM((n_halo, tn), F32)],
        compiler_params=_cp("parallel", "arbitrary"),
        name="ffn_up",
    )(u, u_halo, wg_stack, wu_stack, cw_stack, cb_stack)


def _conv_ffn(h, n_rows, gain, mod_rows, mod, layer, wg, wu, cw, cb, wd):
    u = _modulate(h, n_rows, gain, mod_rows, layer, 3, 4)
    hid = _ffn_up(u, n_rows, layer, wg, wu, cw, cb, wd.shape[1])
    return _mm_ktiled(hid, wd, layer, n_rows, h, (mod, layer, 5))


def _rope(x, rope_ref):
    c, sa, sb = rope_ref[:, 0:LANE], rope_ref[:, LANE:2 * LANE], rope_ref[:, 2 * LANE:3 * LANE]
    quarter = MLA_ROPE // 4
    return x * c + pltpu.roll(x, LANE - quarter, 1) * sa + pltpu.roll(x, quarter, 1) * sb


def _rms(x, width):
    return lax.rsqrt(jnp.sum(x * x, axis=-1, keepdims=True) * (1.0 / width) + EPS)


def _qproj_kernel(cq_ref, ag_ref, w_ref, hg_ref, rope_ref, o_ref, *, heads):
    x = cq_ref[...]
    xn = ((x * _rms(x, x.shape[1])) * ag_ref[...]).astype(BF16)
    y = _dot(xn, w_ref[...])
    for hh in range(heads):
        yh = y[:, hh * QK_PAD:(hh + 1) * QK_PAD]
        yn = (yh * _rms(yh, MLA_NOPE + MLA_ROPE)) * hg_ref[...]
        o_ref[:, hh * QK_PAD:hh * QK_PAD + LANE] = yn[:, :LANE].astype(o_ref.dtype)
        o_ref[:, hh * QK_PAD + LANE:(hh + 1) * QK_PAD] = _rope(yn[:, LANE:], rope_ref).astype(o_ref.dtype)


def _qproj(y_in, n_rows, ag, w, hg, rope):
    heads = _pick(MLA_HEADS, (4, 2, 1))
    tm = _pick(n_rows, (544, 512, 256))
    tn = heads * QK_PAD
    return pl.pallas_call(
        functools.partial(_qproj_kernel, heads=heads),
        grid=(n_rows // tm, w.shape[1] // tn),
        in_specs=[pl.BlockSpec((tm, MLA_Q_RANK), lambda i, j: (i, 0)),
                  pl.BlockSpec((1, MLA_Q_RANK), lambda i, j: (0, 0)),
                  pl.BlockSpec((MLA_Q_RANK, tn), lambda i, j: (0, j)),
                  pl.BlockSpec((1, QK_PAD), lambda i, j: (0, 0)),
                  pl.BlockSpec((tm, 3 * LANE), lambda i, j: (i, 0))],
        out_specs=pl.BlockSpec((tm, tn), lambda i, j: (i, j)),
        out_shape=jax.ShapeDtypeStruct((n_rows, w.shape[1]), BF16),
        compiler_params=_cp("parallel", "parallel"),
        name="q_proj",
    )(y_in, ag, w, hg, rope)


def _kvproj_kernel(ckv_ref, kr_ref, ag_ref, w_ref, gn_ref, gr_ref, rope_ref, k_ref, v_ref, *, heads):
    x = ckv_ref[...]
    xn = ((x * _rms(x, x.shape[1])) * ag_ref[...]).astype(BF16)
    y = _dot(xn, w_ref[...])
    kr = kr_ref[...]
    kr_ss = jnp.sum(kr * kr, axis=-1, keepdims=True)
    width = MLA_NOPE + MLA_VDIM
    ones = jnp.ones((x.shape[0], V_PAD - MLA_VDIM), v_ref.dtype)
    for hh in range(heads):
        kn = y[:, hh * width:hh * width + MLA_NOPE]
        r = lax.rsqrt((jnp.sum(kn * kn, axis=-1, keepdims=True) + kr_ss) * (1.0 / (MLA_NOPE + MLA_ROPE)) + EPS)
        k_ref[:, hh * QK_PAD:hh * QK_PAD + LANE] = ((kn * r) * gn_ref[...]).astype(k_ref.dtype)
        k_ref[:, hh * QK_PAD + LANE:(hh + 1) * QK_PAD] = _rope((kr * r) * gr_ref[...], rope_ref).astype(k_ref.dtype)
        v_ref[:, hh * V_PAD:hh * V_PAD + MLA_VDIM] = y[:, hh * width + MLA_NOPE:(hh + 1) * width].astype(v_ref.dtype)
        v_ref[:, hh * V_PAD + MLA_VDIM:(hh + 1) * V_PAD] = ones


def _kvproj(y_in, n_rows, ag, w, gn, gr, rope):
    heads = _pick(MLA_HEADS, (4, 2, 1))
    assert MLA_NOPE == LANE and MLA_VDIM == LANE and MLA_Q_RANK % MLA_KV_RANK == 0
    tm = _pick(n_rows, (544, 512, 256))
    tn = heads * (MLA_NOPE + MLA_VDIM)
    kr_blk = (MLA_Q_RANK + MLA_KV_RANK) // LANE
    return pl.pallas_call(
        functools.partial(_kvproj_kernel, heads=heads),
        grid=(n_rows // tm, w.shape[1] // tn),
        in_specs=[pl.BlockSpec((tm, MLA_KV_RANK), lambda i, j: (i, MLA_Q_RANK // MLA_KV_RANK)),
                  pl.BlockSpec((tm, LANE), lambda i, j: (i, kr_blk)),
                  pl.BlockSpec((1, MLA_KV_RANK), lambda i, j: (0, 0)),
                  pl.BlockSpec((MLA_KV_RANK, tn), lambda i, j: (0, j)),
                  pl.BlockSpec((1, LANE), lambda i, j: (0, 0)),
                  pl.BlockSpec((1, LANE), lambda i, j: (0, 0)),
                  pl.BlockSpec((tm, 3 * LANE), lambda i, j: (i, 0))],
        out_specs=[pl.BlockSpec((tm, heads * QK_PAD), lambda i, j: (i, j)),
                   pl.BlockSpec((tm, heads * V_PAD), lambda i, j: (i, j))],
        out_shape=[jax.ShapeDtypeStruct((n_rows, MLA_HEADS * QK_PAD), BF16),
                   jax.ShapeDtypeStruct((n_rows, MLA_HEADS * V_PAD), BF16)],
        compiler_params=_cp("parallel", "parallel"),
        name="kv_proj",
    )(y_in, y_in, ag, w, gn, gr, rope)


def _attn_kernel(*refs, chunks, scale):
    q = refs[0][...]
    n_kv = (len(refs) - 2) // 2
    k_refs, v_refs, o_ref = refs[1:1 + n_kv], refs[1 + n_kv:1 + 2 * n_kv], refs[1 + 2 * n_kv]
    c = scale * math.log2(math.e)
    m = acc = None
    for idx, start, size in chunks:
        s = _dot_nt(q, k_refs[idx][start:start + size, :])
        m_blk = jnp.max(s, axis=-1, keepdims=True)
        m_new = m_blk if m is None else jnp.maximum(m, m_blk)
        pv = _dot(jnp.exp2((s - m_new) * c).astype(BF16), v_refs[idx][start:start + size, :])
        acc = pv if acc is None else jnp.exp2((m - m_new) * c) * acc + pv
        m = m_new
    o_ref[...] = (acc[:, :MLA_VDIM] / acc[:, MLA_VDIM:2 * MLA_VDIM]).astype(o_ref.dtype)


def _attention(q, k, v, latent):
    assert V_PAD == 2 * MLA_VDIM
    scale = (MLA_NOPE + MLA_ROPE) ** -0.5
    ctx_blk0 = _n_lat() // CTX_LEN
    ctx_k = pl.BlockSpec((CTX_LEN, QK_PAD), lambda b, h, i: (ctx_blk0 + b, h))
    ctx_v = pl.BlockSpec((CTX_LEN, V_PAD), lambda b, h, i: (ctx_blk0 + b, h))
    if latent:
        tq = _pick(SEQ, (1024, 512, 256))
        tk = _pick(SEQ, (1024, 512, 256))
        q_per_b = SEQ // tq
        q_spec = pl.BlockSpec((tq, QK_PAD), lambda b, h, i: (b * q_per_b + i, h))
        o_spec = pl.BlockSpec((tq, MLA_VDIM), lambda b, h, i: (b * q_per_b + i, h))
        k_specs = [ctx_k, pl.BlockSpec((SEQ, QK_PAD), lambda b, h, i: (b, h))]
        v_specs = [ctx_v, pl.BlockSpec((SEQ, V_PAD), lambda b, h, i: (b, h))]
        chunks = [(0, 0, CTX_LEN)] + [(1, s, tk) for s in range(0, SEQ, tk)]
        n_out, grid = _n_lat(), (BATCH, MLA_HEADS, q_per_b)
    else:
        q_spec = pl.BlockSpec((CTX_LEN, QK_PAD), lambda b, h, i: (ctx_blk0 + b, h))
        o_spec = pl.BlockSpec((CTX_LEN, MLA_VDIM), lambda b, h, i: (b, h))
        k_specs, v_specs, chunks = [ctx_k], [ctx_v], [(0, 0, CTX_LEN)]
        n_out, grid = BATCH * CTX_LEN, (BATCH, MLA_HEADS, 1)
    n_kv = len(k_specs)
    return pl.pallas_call(
        functools.partial(_attn_kernel, chunks=tuple(chunks), scale=scale),
        grid=grid,
        in_specs=[q_spec] + k_specs + v_specs,
        out_specs=o_spec,
        out_shape=jax.ShapeDtypeStruct((n_out, MLA_HEADS * MLA_VDIM), BF16),
        compiler_params=_cp("parallel", "parallel", "arbitrary"),
        name="attn_lat" if latent else "attn_ctx",
    )(q, *([k] * n_kv), *([v] * n_kv))


def _sconv_kernel(b_ref, c_ref, x_ref, cp_ref, xp_ref, cn_ref, xn_ref, w_ref, o_ref, *, tm):
    p = c_ref[...] * x_ref[...]
    prev = cp_ref[SUBLANE - 1:SUBLANE, :] * xp_ref[SUBLANE - 1:SUBLANE, :]
    nxt = cn_ref[0:1, :] * xn_ref[0:1, :]
    dn, up = _shift_rows(p, prev, nxt, pl.program_id(0) * tm)
    conv = dn * w_ref[0:1, :] + p * w_ref[1:2, :] + up * w_ref[2:3, :]
    o_ref[...] = (b_ref[...] * conv).astype(o_ref.dtype)


def _sconv(y_in, n_rows, w, col0, width):
    tm = ROW_BLK
    tn = _pick(width, (1024, 512, 256, 128))
    blk0 = [(col0 + k * width) // tn for k in range(3)]
    main = [pl.BlockSpec((tm, tn), lambda i, j, o=o: (i, o + j)) for o in blk0]
    cp, cn = _halo_specs(tm, tn, n_rows, blk0[1])
    xp, xn = _halo_specs(tm, tn, n_rows, blk0[2])
    return pl.pallas_call(
        functools.partial(_sconv_kernel, tm=tm),
        grid=(n_rows // tm, width // tn),
        in_specs=main + [cp, xp, cn, xn, pl.BlockSpec((3, tn), lambda i, j: (0, j))],
        out_specs=pl.BlockSpec((tm, tn), lambda i, j: (i, j)),
        out_shape=jax.ShapeDtypeStruct((n_rows, width), BF16),
        compiler_params=_cp("parallel", "parallel"),
        name="short_conv",
    )(*([y_in] * 7), w)


def _gdn_conv_kernel(x_ref, xp_ref, xn_ref, w_ref, o_ref, *, tm, q_blks, qk_blks):
    j = pl.program_id(1)
    x = x_ref[...]
    dn, up = _shift_rows(x, xp_ref[SUBLANE - 1:SUBLANE, :], xn_ref[0:1, :], pl.program_id(0) * tm)
    y = _silu(dn * w_ref[0:1, :] + x * w_ref[1:2, :] + up * w_ref[2:3, :])
    q_scale = jnp.where(j < q_blks, GDN_DK ** -0.5, 1.0)
    for hh in range(x.shape[1] // GDN_DK):
        yh = y[:, hh * GDN_DK:(hh + 1) * GDN_DK]
        inv = lax.rsqrt(jnp.sum(yh * yh, axis=-1, keepdims=True) + EPS)
        inv = jnp.where(j < qk_blks, inv, 1.0)
        o_ref[:, hh * GDN_DK:(hh + 1) * GDN_DK] = (yh * inv) * q_scale


def _gdn_conv(y_in, n_rows, w):
    width = w.shape[1]
    key_w = GDN_QK_HEADS * GDN_DK
    tm = ROW_BLK
    tn = _pick(key_w, (1024, 512, 256, 128))
    xp, xn = _halo_specs(tm, tn, n_rows, 0)
    return pl.pallas_call(
        functools.partial(_gdn_conv_kernel, tm=tm, q_blks=key_w // tn, qk_blks=2 * key_w // tn),
        grid=(n_rows // tm, width // tn),
        in_specs=[pl.BlockSpec((tm, tn), lambda i, j: (i, j)), xp, xn,
                  pl.BlockSpec((3, tn), lambda i, j: (0, j))],
        out_specs=pl.BlockSpec((tm, tn), lambda i, j: (i, j)),
        out_shape=jax.ShapeDtypeStruct((n_rows, width), F32),
        compiler_params=_cp("parallel", "parallel"),
        name="gdn_conv",
    )(y_in, y_in, y_in, w)


def _gdn_gate_kernel(ba_ref, alog_ref, dtb_ref, o_ref):
    x = ba_ref[...]
    lane = lax.broadcasted_iota(jnp.int32, x.shape, 1)
    z = x + dtb_ref[...]
    softplus = jnp.maximum(z, 0.0) + jnp.log(1.0 + jnp.exp(-jnp.abs(z)))
    o_ref[...] = jnp.where(lane < x.shape[1] // 2, _sigmoid(x), -jnp.exp(alog_ref[...]) * softplus)


def _gdn_gate(ba, n_rows, alog, dtb):
    w = ba.shape[1]
    return pl.pallas_call(
        _gdn_gate_kernel,
        grid=(n_rows // ROW_BLK,),
        in_specs=[pl.BlockSpec((ROW_BLK, w), lambda i: (i, 0)),
                  pl.BlockSpec((1, w), lambda i: (0, 0)),
                  pl.BlockSpec((1, w), lambda i: (0, 0))],
        out_specs=pl.BlockSpec((ROW_BLK, w), lambda i: (i, 0)),
        out_shape=jax.ShapeDtypeStruct((n_rows, w), F32),
        compiler_params=_cp("parallel"),
        name="gdn_gate",
    )(ba, alog, dtb)


def _blk_mask(ii, jj, size):
    shift = size.bit_length() - 1
    return jnp.right_shift(ii, shift) == jnp.right_shift(jj, shift)


def _unit_tri_inverses(l_mats, ii, jj):
    eye = (ii == jj).astype(F32)
    blk16 = _blk_mask(ii, jj, 16)
    xs = [jnp.where(blk16, -l, 0.0) for l in l_mats]
    ts = [eye + x for x in xs]
    for _ in range(3):
        xs = [_dot(x.astype(BF16), x.astype(BF16)) for x in xs]
        ts = [t + _dot(t.astype(BF16), x.astype(BF16)) for t, x in zip(ts, xs)]
    size = 16
    while size < GDN_BLK:
        ring = _blk_mask(ii, jj, 2 * size) & jnp.logical_not(_blk_mask(ii, jj, size))
        tbs = [t.astype(BF16) for t in ts]
        mids = [_dot(jnp.where(ring, l, 0.0).astype(BF16), tb).astype(BF16) for l, tb in zip(l_mats, tbs)]
        ts = [t - _dot(tb, mid) for t, tb, mid in zip(ts, tbs, mids)]
        size *= 2
    return ts


def _gdn_prep_kernel(q_ref, k_ref, v_ref, bg_ref, u_ref, wq_ref, kd_ref, in_ref, cd_ref, *, rep):
    n = GDN_BLK
    ii = lax.broadcasted_iota(jnp.int32, (n, n), 0)
    jj = lax.broadcasted_iota(jnp.int32, (n, n), 1)
    eye = ii == jj
    lower, upper = jj <= ii, jj >= ii
    q, k = q_ref[...], k_ref[...]
    q16, k16 = q.astype(BF16), k.astype(BF16)
    qk = _dot_nt(q16, k16)
    kk = _dot_nt(k16, k16)
    chains = [(e, d) for e in range(rep) for d in range(2)]
    l_mats, rhss = [], []
    for e, d in chains:
        cols = slice(e * GDN_DV, (e + 1) * GDN_DV)
        incl, incl_t = (lower, upper) if d == 0 else (upper, lower)
        strict = incl & jnp.logical_not(eye)
        b_row, g_row = bg_ref[0, d, e:e + 1, :], bg_ref[1, d, e:e + 1, :]
        g_col = jnp.sum(jnp.where(eye, g_row, 0.0), axis=1, keepdims=True)
        b_col = jnp.sum(jnp.where(eye, b_row, 0.0), axis=1, keepdims=True)
        gc_col = jnp.sum(jnp.where(incl, g_row, 0.0), axis=1, keepdims=True)
        gc_row = jnp.sum(jnp.where(incl_t, g_col, 0.0), axis=0, keepdims=True)
        g_tot = jnp.sum(g_row, axis=1, keepdims=True)
        decay = jnp.where(incl, jnp.exp(jnp.where(incl, gc_col - gc_row, 0.0)), 0.0)
        e_col = jnp.exp(gc_col)
        l_mats.append(jnp.where(strict, (kk * b_col) * decay, 0.0))
        rhss.append(jnp.concatenate([v_ref[:, cols] * b_col, (k * b_col) * e_col], axis=1).astype(BF16))
        wq_ref[d, n:, cols] = (q * e_col).astype(wq_ref.dtype)
        kd_ref[d, :, cols] = (k * jnp.exp(g_tot - gc_col)).astype(kd_ref.dtype)
        in_ref[d, e] = jnp.where(incl, qk * decay, 0.0).astype(in_ref.dtype)
        cd_ref[d, e] = jnp.broadcast_to(jnp.exp(g_tot), (SUBLANE, LANE))
    sols = [_dot(t.astype(BF16), rhs) for t, rhs in zip(_unit_tri_inverses(l_mats, ii, jj), rhss)]
    for (e, d), sol in zip(chains, sols):
        cols = slice(e * GDN_DV, (e + 1) * GDN_DV)
        u_ref[d, :, cols] = sol[:, :GDN_DV]
        wq_ref[d, :n, cols] = sol[:, GDN_DV:].astype(wq_ref.dtype)


def _gdn_prep(qkv, bg_t, n_rows):
    assert GDN_DK == LANE and GDN_DV == LANE and GDN_BLK == 2 * LANE
    nb, hv, rep = n_rows // GDN_BLK, GDN_V_HEADS, GDN_V_HEADS // GDN_QK_HEADS
    v_blk0 = 2 * GDN_QK_HEADS * GDN_DK // (rep * GDN_DV)
    wide = rep * GDN_DV
    return pl.pallas_call(
        functools.partial(_gdn_prep_kernel, rep=rep),
        grid=(nb, GDN_QK_HEADS),
        in_specs=[pl.BlockSpec((GDN_BLK, GDN_DK), lambda i, h: (i, h)),
                  pl.BlockSpec((GDN_BLK, GDN_DK), lambda i, h: (i, GDN_QK_HEADS + h)),
                  pl.BlockSpec((GDN_BLK, wide), lambda i, h: (i, v_blk0 + h)),
                  pl.BlockSpec((2, 2, None, rep, GDN_BLK), lambda i, h: (0, 0, h, 0, i))],
        out_specs=[pl.BlockSpec((2, GDN_BLK, wide), lambda i, h: (0, i, h)),
                   pl.BlockSpec((2, None, 2 * GDN_BLK, wide), lambda i, h: (0, i, 0, h)),
                   pl.BlockSpec((2, GDN_BLK, wide), lambda i, h: (0, i, h)),
                   pl.BlockSpec((2, rep, GDN_BLK, GDN_BLK), lambda i, h: (0, h, i, 0)),
                   pl.BlockSpec((2, rep, SUBLANE, LANE), lambda i, h: (0, h, i, 0))],
        out_shape=[jax.ShapeDtypeStruct((2, n_rows, hv * GDN_DV), F32),
                   jax.ShapeDtypeStruct((2, nb, 2 * GDN_BLK, hv * GDN_DV), BF16),
                   jax.ShapeDtypeStruct((2, n_rows, hv * GDN_DV), BF16),
                   jax.ShapeDtypeStruct((2, hv, n_rows, GDN_BLK), BF16),
                   jax.ShapeDtypeStruct((2, hv, nb * SUBLANE, LANE), F32)],
        compiler_params=_cp("parallel", "parallel"),
        name="gdn_prep",
    )(qkv, qkv, qkv, bg_t)


def _gdn_scan_kernel(*refs, heads):
    ins, (of_ref, ob_ref, s_ref) = refs[:10], refs[10:]
    n = GDN_BLK

    @pl.when(pl.program_id(2) == 0)
    def _():
        s_ref[...] = jnp.zeros_like(s_ref)

    o_refs = (of_ref, ob_ref)
    chains = [(d, hh, slice(hh * GDN_DV, (hh + 1) * GDN_DV)) for d in range(2) for hh in range(heads)]
    u_refs, wq_refs, kd_refs, in_refs, cd_refs = (ins[0::5], ins[1::5], ins[2::5], ins[3::5], ins[4::5])
    states = [s_ref[d, hh] for d, hh, _ in chains]
    wqs = [_dot(wq_refs[d][:, cols], s.astype(BF16)) for (d, hh, cols), s in zip(chains, states)]
    vbs = [(u_refs[d][:, cols] - wq[:n]).astype(BF16) for (d, hh, cols), wq in zip(chains, wqs)]
    for (d, hh, cols), s, wq, vb in zip(chains, states, wqs, vbs):
        o_refs[d][:, cols] = wq[n:] + _dot(in_refs[d][hh], vb)
        s_ref[d, hh] = s * cd_refs[d][hh, 0:1, :] + _dot_tn(kd_refs[d][:, cols], vb)


def _gdn_scan(prep, n_rows):
    hv = GDN_V_HEADS
    heads = _pick(hv, (4, 2, 1))
    lat_blk, ctx_blk = SEQ // GDN_BLK, CTX_LEN // GDN_BLK
    steps = ctx_blk + lat_blk
    ctx0 = _n_lat() // GDN_BLK
    wide = heads * GDN_DV

    def blk_f(b, n):
        return jnp.where(n < ctx_blk, ctx0 + b * ctx_blk + n, b * lat_blk + n - ctx_blk)

    def blk_b(b, n):
        return jnp.where(n < ctx_blk, ctx0 + b * ctx_blk + ctx_blk - 1 - n, b * lat_blk + steps - 1 - n)

    in_specs = []
    for d, blk in enumerate((blk_f, blk_b)):
        rows = pl.BlockSpec((None, GDN_BLK, wide), lambda b, h, n, d=d, blk=blk: (d, blk(b, n), h))
        in_specs += [rows,
                     pl.BlockSpec((None, None, 2 * GDN_BLK, wide), lambda b, h, n, d=d, blk=blk: (d, blk(b, n), 0, h)),
                     rows,
                     pl.BlockSpec((None, heads, GDN_BLK, GDN_BLK), lambda b, h, n, d=d, blk=blk: (d, h, blk(b, n), 0)),
                     pl.BlockSpec((None, heads, SUBLANE, LANE), lambda b, h, n, d=d, blk=blk: (d, h, blk(b, n), 0))]
    out = jax.ShapeDtypeStruct((n_rows, hv * GDN_DV), F32)
    return pl.pallas_call(
        functools.partial(_gdn_scan_kernel, heads=heads),
        grid=(BATCH, hv // heads, steps),
        in_specs=in_specs,
        out_specs=[pl.BlockSpec((GDN_BLK, wide), lambda b, h, n: (blk_f(b, n), h)),
                   pl.BlockSpec((GDN_BLK, wide), lambda b, h, n: (blk_b(b, n), h))],
        out_shape=[out, out],
        scratch_shapes=[pltpu.VMEM((2, heads, GDN_DK, GDN_DV), F32)],
        compiler_params=_cp("parallel", "parallel", "arbitrary"),
        name="gdn_scan",
    )(*prep, *prep)


def _gdn_readout_kernel(of_ref, ob_ref, z_ref, g_ref, o_ref):
    o = of_ref[...] + ob_ref[...]
    z = z_ref[...]
    for hh in range(o.shape[1] // GDN_DV):
        cols = slice(hh * GDN_DV, (hh + 1) * GDN_DV)
        oh = o[:, cols]
        y = (oh * _rms(oh, GDN_DV)) * g_ref[...]
        o_ref[:, cols] = (y * _silu(z[:, cols])).astype(o_ref.dtype)


def _gdn_readout(o_f, o_b, y_in, z_col0, gain, n_rows):
    width = GDN_V_HEADS * GDN_DV
    tn = _pick(width, (1024, 512, 256, 128))
    spec = pl.BlockSpec((ROW_BLK, tn), lambda i, j: (i, j))
    return pl.pallas_call(
        _gdn_readout_kernel,
        grid=(n_rows // ROW_BLK, width // tn),
        in_specs=[spec, spec, pl.BlockSpec((ROW_BLK, tn), lambda i, j: (i, z_col0 // tn + j)),
                  pl.BlockSpec((1, GDN_DV), lambda i, j: (0, 0))],
        out_specs=spec,
        out_shape=jax.ShapeDtypeStruct((n_rows, width), BF16),
        compiler_params=_cp("parallel", "parallel"),
        name="gdn_readout",
    )(o_f, o_b, y_in, gain.reshape(1, GDN_DV))


def _rope_table(n_rows):
    n_freq = MLA_ROPE // 4
    inv_freq = ROPE_THETA ** (-jnp.arange(n_freq, dtype=F32) / n_freq)
    t = jnp.arange(SEQ)
    ang = jnp.concatenate([(t // GRID_W).astype(F32)[:, None] * inv_freq,
                           (t % GRID_W).astype(F32)[:, None] * inv_freq], axis=1)
    cos, sin = jnp.cos(ang), jnp.sin(ang)
    zero = jnp.zeros_like(sin[:, :n_freq])
    pad = jnp.zeros((SEQ, LANE - MLA_ROPE), F32)
    c = jnp.concatenate([cos[:, :n_freq], cos[:, :n_freq], cos[:, n_freq:], cos[:, n_freq:], pad + 1.0], axis=1)
    sa = jnp.concatenate([-sin[:, :n_freq], zero, -sin[:, n_freq:], zero, pad], axis=1)
    sb = jnp.concatenate([zero, sin[:, :n_freq], zero, sin[:, n_freq:], pad], axis=1)
    lat = jnp.tile(jnp.concatenate([c, sa, sb], axis=1), (BATCH, 1))
    n_ctx = n_rows - _n_lat()
    ident = jnp.concatenate([jnp.ones((n_ctx, LANE), F32), jnp.zeros((n_ctx, 2 * LANE), F32)], axis=1)
    return jnp.concatenate([lat, ident], axis=0)


def _pad_cols(w, n):
    return jnp.pad(w, ((0, 0), (0, n - w.shape[1])))


def _even_layer(h, n_rows, need_ctx, mod, mod_rows, layer, gain, p):
    w_in, q_a_g, kv_a_g, w_qb, w_kvb, q_g, k_g, sc_w, w_out = p
    d = h.shape[1]
    sc_width = d - MLA_HEADS * MLA_VDIM
    assert sc_width == MLA_HEADS * MLA_VDIM
    head_w = MLA_NOPE + MLA_ROPE
    c0 = MLA_Q_RANK + MLA_KV_RANK
    conv0 = _round_up(c0 + LANE, _pick(sc_width, (1024, 512, 256, 128)))
    n_in = _round_up(conv0 + 3 * sc_width, 1024)
    w_in_p = jnp.concatenate([_pad_cols(w_in[:, :c0 + MLA_ROPE], conv0), w_in[:, c0 + MLA_ROPE:]], axis=1)
    w_in_p = _pad_cols(w_in_p, n_in).astype(BF16)
    w_q = w_qb.reshape(MLA_Q_RANK, MLA_HEADS, head_w)
    w_q = jnp.pad(w_q, ((0, 0), (0, 0), (0, QK_PAD - head_w))).reshape(MLA_Q_RANK, MLA_HEADS * QK_PAD).astype(BF16)
    rope = _rope_table(n_rows)

    u = _modulate(h, n_rows, gain, mod_rows, layer, 0, 1)
    y = _mm([u], w_in_p, n_rows, name="even_in")
    q = _qproj(y, n_rows, q_a_g.reshape(1, -1), w_q, jnp.pad(q_g, (0, QK_PAD - head_w)).reshape(1, QK_PAD), rope)
    k, v = _kvproj(y, n_rows, kv_a_g.reshape(1, -1), w_kvb.astype(BF16), k_g[:MLA_NOPE].reshape(1, LANE),
                   jnp.pad(k_g[MLA_NOPE:], (0, LANE - MLA_ROPE)).reshape(1, LANE), rope)
    o = _attention(q, k, v, True)
    if need_ctx:
        o = jnp.concatenate([o, _attention(q, k, v, False)], axis=0)
    n_out = n_rows if need_ctx else _n_lat()
    conv = _sconv(y, n_out, sc_w, conv0, sc_width)
    return _mm([o, conv], w_out.astype(BF16), n_out, res=h, gate=(mod, layer, 2),
               tn=_pick(d, (512, 256, 128)), name="even_out")


def _odd_layer(h, n_rows, need_ctx, mod, mod_rows, layer, gain, p, idx):
    w_in_stack, conv_w, a_log, dt_bias, o_g, w_out = p
    hv, hq = GDN_V_HEADS, GDN_QK_HEADS
    qkv_w = 2 * hq * GDN_DK + hv * GDN_DV
    main_w = qkv_w + hv * GDN_DV
    gate_w = _round_up(4 * hv, LANE)
    half = gate_w // 2
    w_in = w_in_stack[idx]
    w_ba = jnp.concatenate([_pad_cols(w_in[:, main_w:main_w + 2 * hv], half),
                            _pad_cols(w_in[:, main_w + 2 * hv:], half)], axis=1).astype(BF16)
    zeros = jnp.zeros((1, half), F32)
    alog = jnp.concatenate([zeros, _pad_cols(a_log.reshape(1, -1), half)], axis=1)
    dtb = jnp.concatenate([zeros, _pad_cols(dt_bias.reshape(1, -1), half)], axis=1)

    u = _modulate(h, n_rows, gain, mod_rows, layer, 0, 1)
    y = _mm_wcast(u, w_in_stack, idx, main_w, n_rows, "odd_in")
    ba = _mm([u], w_ba, n_rows, name="odd_in_gates")
    qkv = _gdn_conv(y, n_rows, conv_w)
    bg = _gdn_gate(ba, n_rows, alog, dtb)
    bg_t = jnp.stack([bg[:, :2 * hv], bg[:, half:half + 2 * hv]], axis=0)
    bg_t = bg_t.reshape(2, n_rows, 2, hq, hv // hq).transpose(0, 2, 3, 4, 1)
    o_f, o_b = _gdn_scan(_gdn_prep(qkv, bg_t, n_rows), n_rows)
    n_out = n_rows if need_ctx else _n_lat()
    yo = _gdn_readout(o_f, o_b, y, qkv_w, o_g, n_out)
    return _mm([yo], w_out.astype(BF16), n_out, res=h, gate=(mod, layer, 2),
               tn=_pick(h.shape[1], (512, 256, 128)), name="odd_out")


def kernel(x, c, ctx, c_ctx, ada_w, ada_b, norm_mix, norm_ffn, ffn_w_gate, ffn_w_up, ffn_conv_w, ffn_conv_b,
           ffn_w_down, a_w_in, a_q_a_norm, a_kv_a_norm, a_w_qb, a_w_kvb, a_q_norm, a_k_norm, a_sc_conv, a_w_out,
           c_w_in, c_conv_w, c_a_log, c_dt_bias, c_o_norm, c_w_out):
    bn, t, d = x.shape
    depth = ada_w.shape[0]
    assert (bn, t, ctx.shape[1], ffn_w_gate.shape[2]) == (BATCH, SEQ, CTX_LEN, D_FF)
    assert CTX_LEN % GDN_BLK == 0 and SEQ % GDN_BLK == 0
    h = jnp.concatenate([x.reshape(bn * t, d), ctx.reshape(bn * CTX_LEN, d)], axis=0)
    cond8 = jnp.concatenate([c, c_ctx[None], jnp.zeros((SUBLANE - bn - 1, d), F32)], axis=0)
    mod = _ada(cond8, ada_w, ada_b)
    mod_rows = mod.reshape(depth * SUBLANE * 6, 1, d)
    d_ff_p = _round_up(D_FF, 512)
    w_down = jnp.pad(ffn_w_down.astype(BF16), ((0, 0), (0, d_ff_p - D_FF), (0, 0)))
    conv_b = ffn_conv_b.reshape(depth, 1, D_FF)
    for l in range(depth):
        last = l == depth - 1
        n_rows = h.shape[0]
        i = l // 2
        if l % 2 == 0:
            h = _even_layer(h, n_rows, not last, mod, mod_rows, l, norm_mix[l],
                            (a_w_in[i], a_q_a_norm[i], a_kv_a_norm[i], a_w_qb[i], a_w_kvb[i], a_q_norm[i],
                             a_k_norm[i], a_sc_conv[i], a_w_out[i]))
        else:
            h = _odd_layer(h, n_rows, not last, mod, mod_rows, l, norm_mix[l],
                           (c_w_in, c_conv_w[i], c_a_log[i], c_dt_bias[i], c_o_norm[i], c_w_out[i]), i)
        h = _conv_ffn(h, h.shape[0], norm_ffn[l], mod_rows, mod, l,
                      ffn_w_gate, ffn_w_up, ffn_conv_w, conv_b, w_down)
    return h[:bn * t].reshape(bn, t, d)
```

```python
import functools
import math

import jax
import jax.numpy as jnp
from jax import lax
from jax.experimental import pallas as pl
from jax.experimental.pallas import tpu as pltpu

F32 = jnp.float32
BF16 = jnp.bfloat16

BATCH = 2
SEQ = 4096
GRID_W = 64
CTX_LEN = 256
EPS = 1e-6
D_FF = 11008

MLA_HEADS = 16
MLA_NOPE = 128
MLA_ROPE = 64
MLA_VDIM = 128
MLA_Q_RANK = 1024
MLA_KV_RANK = 512
ROPE_THETA = 10000.0

GDN_QK_HEADS = 16
GDN_V_HEADS = 32
GDN_DK = 128
GDN_DV = 128

LANE = 128
SUBLANE = 8
QK_PAD = 256
V_PAD = 256
ROW_BLK = 256
GDN_BLK = 256
VMEM_LIMIT = 56 * 2**20


def _cp(*sem, vmem=VMEM_LIMIT):
    return pltpu.CompilerParams(dimension_semantics=sem, vmem_limit_bytes=vmem)


def _pick(n, prefs):
    for p in prefs:
        if n % p == 0:
            return p
    raise ValueError(f"no tile for {n} in {prefs}")


def _round_up(n, m):
    return (n + m - 1) // m * m


def _sigmoid(x):
    return 1.0 / (1.0 + jnp.exp(-x))


def _silu(x):
    return x * _sigmoid(x)


def _dot(a, b):
    return jnp.dot(a, b, preferred_element_type=F32)


def _dot_nt(a, b):
    return lax.dot_general(a, b, (((1,), (1,)), ((), ())), preferred_element_type=F32)


def _dot_tn(a, b):
    return lax.dot_general(a, b, (((0,), (0,)), ((), ())), preferred_element_type=F32)


def _n_lat():
    return BATCH * SEQ


def _seq_edges(row0, tm):
    assert SEQ & (SEQ - 1) == 0 and CTX_LEN & (CTX_LEN - 1) == 0 and SEQ % CTX_LEN == 0
    r = row0 + lax.broadcasted_iota(jnp.int32, (tm, 1), 0)
    is_ctx = r >= _n_lat()
    first = (jnp.bitwise_and(r, CTX_LEN - 1) == 0) & ((jnp.bitwise_and(r, SEQ - 1) == 0) | is_ctx)
    r1 = r + 1
    last = (jnp.bitwise_and(r1, CTX_LEN - 1) == 0) & ((jnp.bitwise_and(r1, SEQ - 1) == 0) | is_ctx)
    return first, last


def _shift_rows(x, prev_row, next_row, row0):
    tm = x.shape[0]
    ridx = lax.broadcasted_iota(jnp.int32, (tm, 1), 0)
    first, last = _seq_edges(row0, tm)
    dn = jnp.where(ridx == 0, prev_row, pltpu.roll(x, 1, 0))
    dn = jnp.where(first, 0.0, dn)
    up = jnp.where(ridx == tm - 1, next_row, pltpu.roll(x, tm - 1, 0))
    up = jnp.where(last, 0.0, up)
    return dn, up


def _row_select(row0, tm, table):
    r = row0 + lax.broadcasted_iota(jnp.int32, (tm, 1), 0)
    out = table[BATCH:BATCH + 1]
    for b in reversed(range(BATCH)):
        out = jnp.where(r < (b + 1) * SEQ, table[b:b + 1], out)
    return out


def _halo_specs(tm, tn, n_rows, col_blk0):
    tmb, last = tm // SUBLANE, n_rows // SUBLANE - 1
    prev = pl.BlockSpec((SUBLANE, tn), lambda i, j: (jnp.maximum(i * tmb - 1, 0), col_blk0 + j))
    nxt = pl.BlockSpec((SUBLANE, tn), lambda i, j: (jnp.minimum((i + 1) * tmb, last), col_blk0 + j))
    return prev, nxt


def _ada_kernel(cond_ref, w_ref, b_ref, o_ref):
    a = _silu(cond_ref[...]).astype(BF16)
    o_ref[...] = _dot(a, w_ref[...].astype(BF16)) + b_ref[...]


def _ada(cond8, ada_w, ada_b):
    n_layer, d, n = ada_w.shape
    tn = _pick(n, (512, 256, 128))
    return pl.pallas_call(
        _ada_kernel,
        grid=(n_layer, n // tn),
        in_specs=[pl.BlockSpec((SUBLANE, d), lambda l, j: (0, 0)),
                  pl.BlockSpec((None, d, tn), lambda l, j: (l, 0, j)),
                  pl.BlockSpec((None, 1, tn), lambda l, j: (l, 0, j))],
        out_specs=pl.BlockSpec((None, SUBLANE, tn), lambda l, j: (l, 0, j)),
        out_shape=jax.ShapeDtypeStruct((n_layer, SUBLANE, n), F32),
        compiler_params=_cp("parallel", "parallel"),
        name="ada",
    )(cond8, ada_w, ada_b.reshape(n_layer, 1, n))


def _modulate_kernel(h_ref, g_ref, sh_ref, sc_ref, o_ref):
    x = h_ref[...]
    y = x * lax.rsqrt(jnp.mean(x * x, axis=-1, keepdims=True) + EPS)
    o_ref[...] = ((y * g_ref[...]) * (1.0 + sc_ref[...]) + sh_ref[...]).astype(o_ref.dtype)


def _modulate(h, n_rows, gain, mod_rows, layer, k_shift, k_scale):
    d = h.shape[1]
    blk_per_seq = SEQ // ROW_BLK

    def mod_spec(k):
        return pl.BlockSpec(
            (None, 1, d),
            lambda i: ((layer * SUBLANE + jnp.minimum(i // blk_per_seq, BATCH)) * 6 + k, 0, 0))

    return pl.pallas_call(
        _modulate_kernel,
        grid=(n_rows // ROW_BLK,),
        in_specs=[pl.BlockSpec((ROW_BLK, d), lambda i: (i, 0)),
                  pl.BlockSpec((1, d), lambda i: (0, 0)),
                  mod_spec(k_shift), mod_spec(k_scale)],
        out_specs=pl.BlockSpec((ROW_BLK, d), lambda i: (i, 0)),
        out_shape=jax.ShapeDtypeStruct((n_rows, d), BF16),
        compiler_params=_cp("parallel"),
        name="modulate",
    )(h, gain.reshape(1, d), mod_rows, mod_rows)


def _mm_kernel(*refs, n_pairs, gated, tm):
    a_refs, w_refs = refs[:n_pairs], refs[n_pairs:2 * n_pairs]
    acc = _dot(a_refs[0][...], w_refs[0][...])
    for a_ref, w_ref in zip(a_refs[1:], w_refs[1:]):
        acc += _dot(a_ref[...], w_ref[...])
    if gated:
        res_ref, gate_ref, o_ref = refs[2 * n_pairs:]
        gate = _row_select(pl.program_id(0) * tm, tm, gate_ref[...])
        acc = res_ref[...] + gate * acc
    else:
        o_ref = refs[2 * n_pairs]
    o_ref[...] = acc.astype(o_ref.dtype)


def _mm(a_list, w, n_rows, *, res=None, gate=None, tn=None, name="mm"):
    n_pairs = len(a_list)
    n = w.shape[1]
    tm = _pick(n_rows, (1088, 1024, 512, 256))
    tn = tn or _pick(n, (1024, 512, 256, 128))
    in_specs = [pl.BlockSpec((tm, a.shape[1]), lambda i, j: (i, 0)) for a in a_list]
    in_specs += [pl.BlockSpec((a.shape[1], tn), lambda i, j, p=p: (p, j)) for p, a in enumerate(a_list)]
    assert all(a.shape[1] == a_list[0].shape[1] for a in a_list) and w.shape[0] == n_pairs * a_list[0].shape[1]
    args = list(a_list) + [w] * n_pairs
    if res is not None:
        table, layer, chunk = gate
        d = table.shape[2] // 6
        in_specs += [pl.BlockSpec((tm, tn), lambda i, j: (i, j)),
                     pl.BlockSpec((None, SUBLANE, tn), lambda i, j: (layer, 0, chunk * (d // tn) + j))]
        args += [res, table]
    return pl.pallas_call(
        functools.partial(_mm_kernel, n_pairs=n_pairs, gated=res is not None, tm=tm),
        grid=(n_rows // tm, n // tn),
        in_specs=in_specs,
        out_specs=pl.BlockSpec((tm, tn), lambda i, j: (i, j)),
        out_shape=jax.ShapeDtypeStruct((n_rows, n), F32),
        compiler_params=_cp("parallel", "parallel"),
        name=name,
    )(*args)


def _mm_wcast_kernel(a_ref, w_ref, o_ref, wb_ref):
    @pl.when(pl.program_id(1) == 0)
    def _():
        wb_ref[...] = w_ref[...].astype(BF16)

    o_ref[...] = _dot(a_ref[...], wb_ref[...])


def _mm_wcast(a, w_stack, layer, n_cols, n_rows, name):
    kdim = a.shape[1]
    tm = _pick(n_rows, (1088, 1024, 512, 256))
    tn = _pick(n_cols, (512, 256, 128))
    return pl.pallas_call(
        _mm_wcast_kernel,
        grid=(n_cols // tn, n_rows // tm),
        in_specs=[pl.BlockSpec((tm, kdim), lambda j, i: (i, 0)),
                  pl.BlockSpec((None, kdim, tn), lambda j, i: (layer, 0, j))],
        out_specs=pl.BlockSpec((tm, tn), lambda j, i: (i, j)),
        out_shape=jax.ShapeDtypeStruct((n_rows, n_cols), F32),
        scratch_shapes=[pltpu.VMEM((kdim, tn), BF16)],
        compiler_params=_cp("parallel", "arbitrary"),
        name=name,
    )(a, w_stack)


def _mmk_kernel(a_ref, w_ref, res_ref, gate_ref, o_ref, acc_ref, *, tm):
    k = pl.program_id(2)

    @pl.when(k == 0)
    def _():
        acc_ref[...] = jnp.zeros_like(acc_ref)

    acc_ref[...] += _dot(a_ref[...], w_ref[...])

    @pl.when(k == pl.num_programs(2) - 1)
    def _():
        gate = _row_select(pl.program_id(0) * tm, tm, gate_ref[...])
        o_ref[...] = res_ref[...] + gate * acc_ref[...]


def _mm_ktiled(a, w_stack, layer, n_rows, res, gate):
    _, kdim, n = w_stack.shape
    table, gate_layer, chunk = gate
    d = table.shape[2] // 6
    tm = _pick(n_rows, (1088, 1024, 512, 256))
    tn = _pick(n, (1024, 512, 256, 128))
    tk = _pick(kdim, (2816, 2048, 1024, 512, 256))
    return pl.pallas_call(
        functools.partial(_mmk_kernel, tm=tm),
        grid=(n_rows // tm, n // tn, kdim // tk),
        in_specs=[pl.BlockSpec((tm, tk), lambda i, j, k: (i, k)),
                  pl.BlockSpec((None, tk, tn), lambda i, j, k: (layer, k, j)),
                  pl.BlockSpec((tm, tn), lambda i, j, k: (i, j)),
                  pl.BlockSpec((None, SUBLANE, tn), lambda i, j, k: (gate_layer, 0, chunk * (d // tn) + j))],
        out_specs=pl.BlockSpec((tm, tn), lambda i, j, k: (i, j)),
        out_shape=jax.ShapeDtypeStruct((n_rows, n), F32),
        scratch_shapes=[pltpu.VMEM((tm, tn), F32)],
        compiler_params=_cp("parallel", "parallel", "arbitrary"),
        name="ffn_down",
    )(a, w_stack, res, table)


def _ffn_up_kernel(u_ref, uh_ref, wg_ref, wu_ref, cw_ref, cb_ref, o_ref, halo_ref, *, tm):
    i = pl.program_id(1)

    @pl.when(i == 0)
    def _():
        halo_ref[...] = _dot(uh_ref[...], wg_ref[...])

    u = u_ref[...]
    g = _dot(u, wg_ref[...])
    dn, up = _shift_rows(g, halo_ref[pl.ds(2 * i, 1), :], halo_ref[pl.ds(2 * i + 1, 1), :], i * tm)
    a = dn * cw_ref[0:1, :] + g * cw_ref[1:2, :] + up * cw_ref[2:3, :] + cb_ref[...]
    o_ref[...] = (_silu(a) * _dot(u, wu_ref[...])).astype(o_ref.dtype)


def _ffn_up(u, n_rows, layer, wg_stack, wu_stack, cw_stack, cb_stack):
    _, d, n = wg_stack.shape
    tm = _pick(n_rows, (1088, 1024, 512, 256))
    tn = _pick(n, (512, 256, 128))
    gm = n_rows // tm
    n_halo = _round_up(2 * gm, 2 * SUBLANE)
    rows = []
    for i in range(gm):
        rows += [max(i * tm - 1, 0), min((i + 1) * tm, n_rows - 1)]
    rows += [0] * (n_halo - len(rows))
    u_halo = jnp.concatenate([u[r:r + 1] for r in rows], axis=0)
    col = lambda j, i: (layer, 0, j)
    return pl.pallas_call(
        functools.partial(_ffn_up_kernel, tm=tm),
        grid=(n // tn, gm),
        in_specs=[pl.BlockSpec((tm, d), lambda j, i: (i, 0)),
                  pl.BlockSpec((n_halo, d), lambda j, i: (0, 0)),
                  pl.BlockSpec((None, d, tn), col),
                  pl.BlockSpec((None, d, tn), col),
                  pl.BlockSpec((None, 3, tn), col),
                  pl.BlockSpec((None, 1, tn), col)],
        out_specs=pl.BlockSpec((tm, tn), lambda j, i: (i, j)),
        out_shape=jax.ShapeDtypeStruct((n_rows, n), BF16),
        scratch_shapes=[pltpu.VMEM((n_halo, tn), F32)],
        compiler_params=_cp("parallel", "arbitrary"),
        name="ffn_up",
    )(u, u_halo, wg_stack, wu_stack, cw_stack, cb_stack)


def _conv_ffn(h, n_rows, gain, mod_rows, mod, layer, wg, wu, cw, cb, wd):
    u = _modulate(h, n_rows, gain, mod_rows, layer, 3, 4)
    hid = _ffn_up(u, n_rows, layer, wg, wu, cw, cb)
    return _mm_ktiled(hid, wd, layer, n_rows, h, (mod, layer, 5))


def _rope(x, rope_ref):
    c, sa, sb = rope_ref[:, 0:LANE], rope_ref[:, LANE:2 * LANE], rope_ref[:, 2 * LANE:3 * LANE]
    quarter = MLA_ROPE // 4
    return x * c + pltpu.roll(x, LANE - quarter, 1) * sa + pltpu.roll(x, quarter, 1) * sb


def _rms(x, width):
    return lax.rsqrt(jnp.sum(x * x, axis=-1, keepdims=True) * (1.0 / width) + EPS)


def _qproj_kernel(cq_ref, ag_ref, w_ref, hg_ref, rope_ref, o_ref, *, heads):
    x = cq_ref[...]
    xn = ((x * _rms(x, x.shape[1])) * ag_ref[...]).astype(BF16)
    y = _dot(xn, w_ref[...])
    for hh in range(heads):
        yh = y[:, hh * QK_PAD:(hh + 1) * QK_PAD]
        yn = (yh * _rms(yh, MLA_NOPE + MLA_ROPE)) * hg_ref[...]
        o_ref[:, hh * QK_PAD:hh * QK_PAD + LANE] = yn[:, :LANE].astype(o_ref.dtype)
        o_ref[:, hh * QK_PAD + LANE:(hh + 1) * QK_PAD] = _rope(yn[:, LANE:], rope_ref).astype(o_ref.dtype)


def _qproj(y_in, n_rows, ag, w, hg, rope):
    heads = _pick(MLA_HEADS, (4, 2, 1))
    tm = _proj_rows()
    tn = heads * QK_PAD
    return pl.pallas_call(
        functools.partial(_qproj_kernel, heads=heads),
        grid=(n_rows // tm, w.shape[1] // tn),
        in_specs=[pl.BlockSpec((tm, MLA_Q_RANK), lambda i, j: (i, 0)),
                  pl.BlockSpec((1, MLA_Q_RANK), lambda i, j: (0, 0)),
                  pl.BlockSpec((MLA_Q_RANK, tn), lambda i, j: (0, j)),
                  pl.BlockSpec((1, QK_PAD), lambda i, j: (0, 0)),
                  _rope_spec()],
        out_specs=pl.BlockSpec((tm, tn), lambda i, j: (i, j)),
        out_shape=jax.ShapeDtypeStruct((n_rows, w.shape[1]), BF16),
        compiler_params=_cp("parallel", "parallel"),
        name="q_proj",
    )(y_in, ag, w, hg, rope)


def _kvproj_kernel(ckv_ref, kr_ref, ag_ref, w_ref, gn_ref, gr_ref, rope_ref, k_ref, v_ref, *, heads):
    x = ckv_ref[...]
    xn = ((x * _rms(x, x.shape[1])) * ag_ref[...]).astype(BF16)
    y = _dot(xn, w_ref[...])
    kr = kr_ref[...]
    kr_ss = jnp.sum(kr * kr, axis=-1, keepdims=True)
    width = MLA_NOPE + MLA_VDIM
    ones = jnp.ones((x.shape[0], V_PAD - MLA_VDIM), v_ref.dtype)
    for hh in range(heads):
        kn = y[:, hh * width:hh * width + MLA_NOPE]
        r = lax.rsqrt((jnp.sum(kn * kn, axis=-1, keepdims=True) + kr_ss) * (1.0 / (MLA_NOPE + MLA_ROPE)) + EPS)
        k_ref[:, hh * QK_PAD:hh * QK_PAD + LANE] = ((kn * r) * gn_ref[...]).astype(k_ref.dtype)
        k_ref[:, hh * QK_PAD + LANE:(hh + 1) * QK_PAD] = _rope((kr * r) * gr_ref[...], rope_ref).astype(k_ref.dtype)
        v_ref[:, hh * V_PAD:hh * V_PAD + MLA_VDIM] = y[:, hh * width + MLA_NOPE:(hh + 1) * width].astype(v_ref.dtype)
        v_ref[:, hh * V_PAD + MLA_VDIM:(hh + 1) * V_PAD] = ones


def _kvproj(y_in, n_rows, ag, w, gn, gr, rope):
    heads = _pick(MLA_HEADS, (4, 2, 1))
    assert MLA_NOPE == LANE and MLA_VDIM == LANE and MLA_Q_RANK % MLA_KV_RANK == 0
    tm = _proj_rows()
    tn = heads * (MLA_NOPE + MLA_VDIM)
    kr_blk = (MLA_Q_RANK + MLA_KV_RANK) // LANE
    return pl.pallas_call(
        functools.partial(_kvproj_kernel, heads=heads),
        grid=(n_rows // tm, w.shape[1] // tn),
        in_specs=[pl.BlockSpec((tm, MLA_KV_RANK), lambda i, j: (i, MLA_Q_RANK // MLA_KV_RANK)),
                  pl.BlockSpec((tm, LANE), lambda i, j: (i, kr_blk)),
                  pl.BlockSpec((1, MLA_KV_RANK), lambda i, j: (0, 0)),
                  pl.BlockSpec((MLA_KV_RANK, tn), lambda i, j: (0, j)),
                  pl.BlockSpec((1, LANE), lambda i, j: (0, 0)),
                  pl.BlockSpec((1, LANE), lambda i, j: (0, 0)),
                  _rope_spec()],
        out_specs=[pl.BlockSpec((tm, heads * QK_PAD), lambda i, j: (i, j)),
                   pl.BlockSpec((tm, heads * V_PAD), lambda i, j: (i, j))],
        out_shape=[jax.ShapeDtypeStruct((n_rows, MLA_HEADS * QK_PAD), BF16),
                   jax.ShapeDtypeStruct((n_rows, MLA_HEADS * V_PAD), BF16)],
        compiler_params=_cp("parallel", "parallel"),
        name="kv_proj",
    )(y_in, y_in, ag, w, gn, gr, rope)


def _attn_kernel(*refs, chunks, scale):
    q = refs[0][...]
    n_kv = (len(refs) - 2) // 2
    k_refs, v_refs, o_ref = refs[1:1 + n_kv], refs[1 + n_kv:1 + 2 * n_kv], refs[1 + 2 * n_kv]
    c = scale * math.log2(math.e)
    m = acc = None
    for idx, start, size in chunks:
        s = _dot_nt(q, k_refs[idx][start:start + size, :])
        m_blk = jnp.max(s, axis=-1, keepdims=True)
        m_new = m_blk if m is None else jnp.maximum(m, m_blk)
        pv = _dot(jnp.exp2((s - m_new) * c).astype(BF16), v_refs[idx][start:start + size, :])
        acc = pv if acc is None else jnp.exp2((m - m_new) * c) * acc + pv
        m = m_new
    o_ref[...] = (acc[:, :MLA_VDIM] / acc[:, MLA_VDIM:2 * MLA_VDIM]).astype(o_ref.dtype)


def _attention(q, k, v, latent):
    assert V_PAD == 2 * MLA_VDIM
    scale = (MLA_NOPE + MLA_ROPE) ** -0.5
    ctx_blk0 = _n_lat() // CTX_LEN
    ctx_k = pl.BlockSpec((CTX_LEN, QK_PAD), lambda b, h, i: (ctx_blk0 + b, h))
    ctx_v = pl.BlockSpec((CTX_LEN, V_PAD), lambda b, h, i: (ctx_blk0 + b, h))
    if latent:
        tq = _pick(SEQ, (1024, 512, 256))
        tk = _pick(SEQ, (1024, 512, 256))
        q_per_b = SEQ // tq
        q_spec = pl.BlockSpec((tq, QK_PAD), lambda b, h, i: (b * q_per_b + i, h))
        o_spec = pl.BlockSpec((tq, MLA_VDIM), lambda b, h, i: (b * q_per_b + i, h))
        k_specs = [ctx_k, pl.BlockSpec((SEQ, QK_PAD), lambda b, h, i: (b, h))]
        v_specs = [ctx_v, pl.BlockSpec((SEQ, V_PAD), lambda b, h, i: (b, h))]
        chunks = [(0, 0, CTX_LEN)] + [(1, s, tk) for s in range(0, SEQ, tk)]
        n_out, grid = _n_lat(), (BATCH, MLA_HEADS, q_per_b)
    else:
        q_spec = pl.BlockSpec((CTX_LEN, QK_PAD), lambda b, h, i: (ctx_blk0 + b, h))
        o_spec = pl.BlockSpec((CTX_LEN, MLA_VDIM), lambda b, h, i: (b, h))
        k_specs, v_specs, chunks = [ctx_k], [ctx_v], [(0, 0, CTX_LEN)]
        n_out, grid = BATCH * CTX_LEN, (BATCH, MLA_HEADS, 1)
    n_kv = len(k_specs)
    return pl.pallas_call(
        functools.partial(_attn_kernel, chunks=tuple(chunks), scale=scale),
        grid=grid,
        in_specs=[q_spec] + k_specs + v_specs,
        out_specs=o_spec,
        out_shape=jax.ShapeDtypeStruct((n_out, MLA_HEADS * MLA_VDIM), BF16),
        compiler_params=_cp("parallel", "parallel", "arbitrary"),
        name="attn_lat" if latent else "attn_ctx",
    )(q, *([k] * n_kv), *([v] * n_kv))


def _sconv_kernel(b_ref, c_ref, x_ref, cp_ref, xp_ref, cn_ref, xn_ref, w_ref, o_ref, *, tm):
    p = c_ref[...] * x_ref[...]
    prev = cp_ref[SUBLANE - 1:SUBLANE, :] * xp_ref[SUBLANE - 1:SUBLANE, :]
    nxt = cn_ref[0:1, :] * xn_ref[0:1, :]
    dn, up = _shift_rows(p, prev, nxt, pl.program_id(0) * tm)
    conv = dn * w_ref[0:1, :] + p * w_ref[1:2, :] + up * w_ref[2:3, :]
    o_ref[...] = (b_ref[...] * conv).astype(o_ref.dtype)


def _sconv(y_in, n_rows, w, col0, width):
    tm = ROW_BLK
    tn = _pick(width, (1024, 512, 256, 128))
    blk0 = [(col0 + k * width) // tn for k in range(3)]
    main = [pl.BlockSpec((tm, tn), lambda i, j, o=o: (i, o + j)) for o in blk0]
    cp, cn = _halo_specs(tm, tn, n_rows, blk0[1])
    xp, xn = _halo_specs(tm, tn, n_rows, blk0[2])
    return pl.pallas_call(
        functools.partial(_sconv_kernel, tm=tm),
        grid=(n_rows // tm, width // tn),
        in_specs=main + [cp, xp, cn, xn, pl.BlockSpec((3, tn), lambda i, j: (0, j))],
        out_specs=pl.BlockSpec((tm, tn), lambda i, j: (i, j)),
        out_shape=jax.ShapeDtypeStruct((n_rows, width), BF16),
        compiler_params=_cp("parallel", "parallel"),
        name="short_conv",
    )(*([y_in] * 7), w)


def _conv_silu(x_ref, xp_ref, xn_ref, w_ref, row0):
    x = x_ref[...]
    dn, up = _shift_rows(x, xp_ref[SUBLANE - 1:SUBLANE, :], xn_ref[0:1, :], row0)
    return _silu(dn * w_ref[0:1, :] + x * w_ref[1:2, :] + up * w_ref[2:3, :])


def _l2norm(x):
    return x * lax.rsqrt(jnp.sum(x * x, axis=-1, keepdims=True) + EPS)


def _gdn_gate_kernel(ba_ref, alog_ref, dtb_ref, o_ref):
    x = ba_ref[...]
    lane = lax.broadcasted_iota(jnp.int32, x.shape, 1)
    z = x + dtb_ref[...]
    softplus = jnp.maximum(z, 0.0) + jnp.log(1.0 + jnp.exp(-jnp.abs(z)))
    o_ref[...] = jnp.where(lane < x.shape[1] // 2, _sigmoid(x), -jnp.exp(alog_ref[...]) * softplus)


def _gdn_gate(ba, n_rows, alog, dtb):
    w = ba.shape[1]
    return pl.pallas_call(
        _gdn_gate_kernel,
        grid=(n_rows // ROW_BLK,),
        in_specs=[pl.BlockSpec((ROW_BLK, w), lambda i: (i, 0)),
                  pl.BlockSpec((1, w), lambda i: (0, 0)),
                  pl.BlockSpec((1, w), lambda i: (0, 0))],
        out_specs=pl.BlockSpec((ROW_BLK, w), lambda i: (i, 0)),
        out_shape=jax.ShapeDtypeStruct((n_rows, w), F32),
        compiler_params=_cp("parallel"),
        name="gdn_gate",
    )(ba, alog, dtb)


def _blk_mask(ii, jj, size):
    shift = size.bit_length() - 1
    return jnp.right_shift(ii, shift) == jnp.right_shift(jj, shift)


def _active_rows(x, size, odd):
    n = x.shape[0]
    return jnp.concatenate([x[(2 * b + odd) * size:(2 * b + odd + 1) * size] for b in range(n // (2 * size))], axis=0)


def _weave_rows(rest, active, size, odd):
    pieces = []
    for b in range(active.shape[0] // size):
        if rest is None:
            keep = jnp.zeros((size, active.shape[1]), active.dtype)
        else:
            keep = rest[(2 * b + 1 - odd) * size:(2 * b + 2 - odd) * size]
        act = active[b * size:(b + 1) * size]
        pieces += [keep, act] if odd else [act, keep]
    return jnp.concatenate(pieces, axis=0)


def _unit_tri_inverses(l_mats, dirs, ii, jj):
    eye = (ii == jj).astype(F32)
    size = SUBLANE
    base = _blk_mask(ii, jj, size)
    xs = [jnp.where(base, -l, 0.0) for l in l_mats]
    ts = [eye + x for x in xs]
    for _ in range(size.bit_length() - 2):
        xs = [_dot(x.astype(BF16), x.astype(BF16)) for x in xs]
        ts = [t + _dot(t.astype(BF16), x.astype(BF16)) for t, x in zip(ts, xs)]
    while size < GDN_BLK:
        rings = {}
        shift = size.bit_length() - 1
        ra = lax.broadcasted_iota(jnp.int32, (GDN_BLK // 2, GDN_BLK), 0)
        ja = lax.broadcasted_iota(jnp.int32, (GDN_BLK // 2, GDN_BLK), 1)
        for d in set(dirs):
            ia = jnp.left_shift(jnp.right_shift(ra, shift), shift + 1) + (1 - d) * size + jnp.bitwise_and(ra, size - 1)
            rings[d] = _blk_mask(ia, ja, 2 * size) & jnp.logical_not(_blk_mask(ia, ja, size))
        tbs = [t.astype(BF16) for t in ts]
        offs = [jnp.where(rings[d], _active_rows(l, size, 1 - d), 0.0).astype(BF16) for l, d in zip(l_mats, dirs)]
        mids = [_weave_rows(None,_dot(off, tb), size, 1 - d).astype(BF16) for off, tb, d in zip(offs, tbs, dirs)]
        acts = [_active_rows(t, size, 1 - d) for t, d in zip(ts, dirs)]
        news = [act - _dot(act.astype(BF16), mid) for act, mid in zip(acts, mids)]
        ts = [_weave_rows(t, new, size, 1 - d) for t, new, d in zip(ts, news, dirs)]
        size *= 2
    return ts


def _gdn_prep_kernel(q_ref, qp_ref, qn_ref, k_ref, kp_ref, kn_ref, v_ref, vp_ref, vn_ref, cwq_ref, cwk_ref, cwv_ref,
                     bg_ref, u_ref, wq_ref, kd_ref, in_ref, cd_ref, *, rep):
    n = GDN_BLK
    row0 = pl.program_id(0) * n
    q = _l2norm(_conv_silu(q_ref, qp_ref, qn_ref, cwq_ref, row0)) * GDN_DK ** -0.5
    k = _l2norm(_conv_silu(k_ref, kp_ref, kn_ref, cwk_ref, row0))
    v_all = _conv_silu(v_ref, vp_ref, vn_ref, cwv_ref, row0)
    ii = lax.broadcasted_iota(jnp.int32, (n, n), 0)
    jj = lax.broadcasted_iota(jnp.int32, (n, n), 1)
    eye = ii == jj
    lower, upper = jj <= ii, jj >= ii
    q16, k16 = q.astype(BF16), k.astype(BF16)
    qk = _dot_nt(q16, k16)
    kk = _dot_nt(k16, k16)
    chains = [(e, d) for e in range(rep) for d in range(2)]
    l_mats, rhss = [], []
    for e, d in chains:
        cols = slice(e * GDN_DV, (e + 1) * GDN_DV)
        incl, incl_t = (lower, upper) if d == 0 else (upper, lower)
        strict = incl & jnp.logical_not(eye)
        b_row, g_row = bg_ref[0, d, e:e + 1, :], bg_ref[1, d, e:e + 1, :]
        g_col = jnp.sum(jnp.where(eye, g_row, 0.0), axis=1, keepdims=True)
        b_col = jnp.sum(jnp.where(eye, b_row, 0.0), axis=1, keepdims=True)
        gc_col = jnp.sum(jnp.where(incl, g_row, 0.0), axis=1, keepdims=True)
        gc_row = jnp.sum(jnp.where(incl_t, g_col, 0.0), axis=0, keepdims=True)
        g_tot = jnp.sum(g_row, axis=1, keepdims=True)
        decay = jnp.where(incl, jnp.exp(jnp.where(incl, gc_col - gc_row, 0.0)), 0.0)
        e_col = jnp.exp(gc_col)
        l_mats.append(jnp.where(strict, (kk * b_col) * decay, 0.0))
        rhss.append(jnp.concatenate([v_all[:, cols] * b_col, (k * b_col) * e_col], axis=1).astype(BF16))
        wq_ref[d, n:, cols] = (q * e_col).astype(wq_ref.dtype)
        kd_ref[d, :, cols] = (k * jnp.exp(g_tot - gc_col)).astype(kd_ref.dtype)
        in_ref[d, e] = jnp.where(incl, qk * decay, 0.0).astype(in_ref.dtype)
        cd_ref[d, e] = jnp.broadcast_to(jnp.exp(g_tot), (SUBLANE, LANE))
    invs = _unit_tri_inverses(l_mats, [d for _, d in chains], ii, jj)
    sols = [_dot(t.astype(BF16), rhs) for t, rhs in zip(invs, rhss)]
    for (e, d), sol in zip(chains, sols):
        cols = slice(e * GDN_DV, (e + 1) * GDN_DV)
        u_ref[d, :, cols] = sol[:, :GDN_DV]
        wq_ref[d, :n, cols] = sol[:, GDN_DV:].astype(wq_ref.dtype)


def _gdn_prep(y_in, conv_w, bg_t, n_rows):
    assert GDN_DK == LANE and GDN_DV == LANE and GDN_BLK == 2 * LANE
    nb, hv, hq, rep = n_rows // GDN_BLK, GDN_V_HEADS, GDN_QK_HEADS, GDN_V_HEADS // GDN_QK_HEADS
    wide = rep * GDN_DV
    parts = ((GDN_DK, 0), (GDN_DK, hq), (wide, 2 * hq * GDN_DK // wide))
    in_specs = []
    for tn, blk0 in parts:
        in_specs += [pl.BlockSpec((GDN_BLK, tn), lambda i, h, blk0=blk0: (i, blk0 + h))]
        in_specs += _halo_specs(GDN_BLK, tn, n_rows, blk0)
    in_specs += [pl.BlockSpec((3, tn), lambda i, h, blk0=blk0: (0, blk0 + h)) for tn, blk0 in parts]
    in_specs += [pl.BlockSpec((2, 2, None, rep, GDN_BLK), lambda i, h: (0, 0, h, 0, i))]
    return pl.pallas_call(
        functools.partial(_gdn_prep_kernel, rep=rep),
        grid=(nb, hq),
        in_specs=in_specs,
        out_specs=[pl.BlockSpec((2, GDN_BLK, wide), lambda i, h: (0, i, h)),
                   pl.BlockSpec((2, None, 2 * GDN_BLK, wide), lambda i, h: (0, i, 0, h)),
                   pl.BlockSpec((2, GDN_BLK, wide), lambda i, h: (0, i, h)),
                   pl.BlockSpec((2, rep, GDN_BLK, GDN_BLK), lambda i, h: (0, h, i, 0)),
                   pl.BlockSpec((2, rep, SUBLANE, LANE), lambda i, h: (0, h, i, 0))],
        out_shape=[jax.ShapeDtypeStruct((2, n_rows, hv * GDN_DV), F32),
                   jax.ShapeDtypeStruct((2, nb, 2 * GDN_BLK, hv * GDN_DV), BF16),
                   jax.ShapeDtypeStruct((2, n_rows, hv * GDN_DV), BF16),
                   jax.ShapeDtypeStruct((2, hv, n_rows, GDN_BLK), BF16),
                   jax.ShapeDtypeStruct((2, hv, nb * SUBLANE, LANE), F32)],
        compiler_params=_cp("parallel", "parallel"),
        name="gdn_prep",
    )(*([y_in] * 9), conv_w, conv_w, conv_w, bg_t)


def _gdn_scan_kernel(*refs, heads):
    ins, (of_ref, ob_ref, s_ref) = refs[:10], refs[10:]
    n = GDN_BLK

    @pl.when(pl.program_id(2) == 0)
    def _():
        s_ref[...] = jnp.zeros_like(s_ref)

    o_refs = (of_ref, ob_ref)
    chains = [(d, hh, slice(hh * GDN_DV, (hh + 1) * GDN_DV)) for d in range(2) for hh in range(heads)]
    u_refs, wq_refs, kd_refs, in_refs, cd_refs = (ins[0::5], ins[1::5], ins[2::5], ins[3::5], ins[4::5])
    states = [s_ref[d, hh] for d, hh, _ in chains]
    wqs = [_dot(wq_refs[d][:, cols], s.astype(BF16)) for (d, hh, cols), s in zip(chains, states)]
    vbs = [(u_refs[d][:, cols] - wq[:n]).astype(BF16) for (d, hh, cols), wq in zip(chains, wqs)]
    for (d, hh, cols), s, wq, vb in zip(chains, states, wqs, vbs):
        o_refs[d][:, cols] = wq[n:] + _dot(in_refs[d][hh], vb)
        s_ref[d, hh] = s * cd_refs[d][hh, 0:1, :] + _dot_tn(kd_refs[d][:, cols], vb)


def _gdn_scan(prep, n_rows):
    hv = GDN_V_HEADS
    heads = _pick(hv, (4, 2, 1))
    lat_blk, ctx_blk = SEQ // GDN_BLK, CTX_LEN // GDN_BLK
    steps = ctx_blk + lat_blk
    ctx0 = _n_lat() // GDN_BLK
    wide = heads * GDN_DV

    def blk_f(b, n):
        return jnp.where(n < ctx_blk, ctx0 + b * ctx_blk + n, b * lat_blk + n - ctx_blk)

    def blk_b(b, n):
        return jnp.where(n < ctx_blk, ctx0 + b * ctx_blk + ctx_blk - 1 - n, b * lat_blk + steps - 1 - n)

    in_specs = []
    for d, blk in enumerate((blk_f, blk_b)):
        rows = pl.BlockSpec((None, GDN_BLK, wide), lambda b, h, n, d=d, blk=blk: (d, blk(b, n), h))
        in_specs += [rows,
                     pl.BlockSpec((None, None, 2 * GDN_BLK, wide), lambda b, h, n, d=d, blk=blk: (d, blk(b, n), 0, h)),
                     rows,
                     pl.BlockSpec((None, heads, GDN_BLK, GDN_BLK), lambda b, h, n, d=d, blk=blk: (d, h, blk(b, n), 0)),
                     pl.BlockSpec((None, heads, SUBLANE, LANE), lambda b, h, n, d=d, blk=blk: (d, h, blk(b, n), 0))]
    out = jax.ShapeDtypeStruct((n_rows, hv * GDN_DV), F32)
    return pl.pallas_call(
        functools.partial(_gdn_scan_kernel, heads=heads),
        grid=(BATCH, hv // heads, steps),
        in_specs=in_specs,
        out_specs=[pl.BlockSpec((GDN_BLK, wide), lambda b, h, n: (blk_f(b, n), h)),
                   pl.BlockSpec((GDN_BLK, wide), lambda b, h, n: (blk_b(b, n), h))],
        out_shape=[out, out],
        scratch_shapes=[pltpu.VMEM((2, heads, GDN_DK, GDN_DV), F32)],
        compiler_params=_cp("parallel", "parallel", "arbitrary"),
        name="gdn_scan",
    )(*prep, *prep)


def _gdn_readout_kernel(of_ref, ob_ref, z_ref, g_ref, o_ref):
    o = of_ref[...] + ob_ref[...]
    z = z_ref[...]
    for hh in range(o.shape[1] // GDN_DV):
        cols = slice(hh * GDN_DV, (hh + 1) * GDN_DV)
        oh = o[:, cols]
        y = (oh * _rms(oh, GDN_DV)) * g_ref[...]
        o_ref[:, cols] = (y * _silu(z[:, cols])).astype(o_ref.dtype)


def _gdn_readout(o_f, o_b, y_in, z_col0, gain, n_rows):
    width = GDN_V_HEADS * GDN_DV
    tn = _pick(width, (1024, 512, 256, 128))
    spec = pl.BlockSpec((ROW_BLK, tn), lambda i, j: (i, j))
    return pl.pallas_call(
        _gdn_readout_kernel,
        grid=(n_rows // ROW_BLK, width // tn),
        in_specs=[spec, spec, pl.BlockSpec((ROW_BLK, tn), lambda i, j: (i, z_col0 // tn + j)),
                  pl.BlockSpec((1, GDN_DV), lambda i, j: (0, 0))],
        out_specs=spec,
        out_shape=jax.ShapeDtypeStruct((n_rows, width), BF16),
        compiler_params=_cp("parallel", "parallel"),
        name="gdn_readout",
    )(o_f, o_b, y_in, gain.reshape(1, GDN_DV))


def _proj_rows():
    return _pick(math.gcd(SEQ, BATCH * CTX_LEN), (512, 256))


def _rope_spec():
    tm = _proj_rows()
    return pl.BlockSpec((tm, 3 * LANE), lambda i, j: (jnp.where(i < _n_lat() // tm, i % (SEQ // tm), SEQ // tm), 0))


def _rope_table(n_ident):
    n_freq = MLA_ROPE // 4
    inv_freq = ROPE_THETA ** (-jnp.arange(n_freq, dtype=F32) / n_freq)
    t = jnp.arange(SEQ)
    ang = jnp.concatenate([(t // GRID_W).astype(F32)[:, None] * inv_freq,
                           (t % GRID_W).astype(F32)[:, None] * inv_freq], axis=1)
    cos, sin = jnp.cos(ang), jnp.sin(ang)
    zero = jnp.zeros_like(sin[:, :n_freq])
    pad = jnp.zeros((SEQ, LANE - MLA_ROPE), F32)
    c = jnp.concatenate([cos[:, :n_freq], cos[:, :n_freq], cos[:, n_freq:], cos[:, n_freq:], pad + 1.0], axis=1)
    sa = jnp.concatenate([-sin[:, :n_freq], zero, -sin[:, n_freq:], zero, pad], axis=1)
    sb = jnp.concatenate([zero, sin[:, :n_freq], zero, sin[:, n_freq:], pad], axis=1)
    ident = jnp.concatenate([jnp.ones((n_ident, LANE), F32), jnp.zeros((n_ident, 2 * LANE), F32)], axis=1)
    return jnp.concatenate([jnp.concatenate([c, sa, sb], axis=1), ident], axis=0)


def _pad_cols(w, n):
    return jnp.pad(w, ((0, 0), (0, n - w.shape[1])))


def _even_layer(h, n_rows, need_ctx, mod, mod_rows, layer, gain, p):
    w_in, q_a_g, kv_a_g, w_qb, w_kvb, q_g, k_g, sc_w, w_out = p
    d = h.shape[1]
    sc_width = d - MLA_HEADS * MLA_VDIM
    assert sc_width == MLA_HEADS * MLA_VDIM
    head_w = MLA_NOPE + MLA_ROPE
    c0 = MLA_Q_RANK + MLA_KV_RANK
    conv0 = _round_up(c0 + LANE, _pick(sc_width, (1024, 512, 256, 128)))
    n_in = _round_up(conv0 + 3 * sc_width, 1024)
    w_in_p = jnp.concatenate([_pad_cols(w_in[:, :c0 + MLA_ROPE], conv0), w_in[:, c0 + MLA_ROPE:]], axis=1)
    w_in_p = _pad_cols(w_in_p, n_in).astype(BF16)
    w_q = w_qb.reshape(MLA_Q_RANK, MLA_HEADS, head_w)
    w_q = jnp.pad(w_q, ((0, 0), (0, 0), (0, QK_PAD - head_w))).reshape(MLA_Q_RANK, MLA_HEADS * QK_PAD).astype(BF16)
    rope = _rope_table(_proj_rows())

    u = _modulate(h, n_rows, gain, mod_rows, layer, 0, 1)
    y = _mm([u], w_in_p, n_rows, name="even_in")
    q = _qproj(y, n_rows, q_a_g.reshape(1, -1), w_q, jnp.pad(q_g, (0, QK_PAD - head_w)).reshape(1, QK_PAD), rope)
    k, v = _kvproj(y, n_rows, kv_a_g.reshape(1, -1), w_kvb.astype(BF16), k_g[:MLA_NOPE].reshape(1, LANE),
                   jnp.pad(k_g[MLA_NOPE:], (0, LANE - MLA_ROPE)).reshape(1, LANE), rope)
    o = _attention(q, k, v, True)
    if need_ctx:
        o = jnp.concatenate([o, _attention(q, k, v, False)], axis=0)
    n_out = n_rows if need_ctx else _n_lat()
    conv = _sconv(y, n_out, sc_w, conv0, sc_width)
    return _mm([o, conv], w_out.astype(BF16), n_out, res=h, gate=(mod, layer, 2),
               tn=_pick(d, (512, 256, 128)), name="even_out")


def _odd_layer(h, n_rows, need_ctx, mod, mod_rows, layer, gain, p, idx):
    w_in_stack, conv_w, a_log, dt_bias, o_g, w_out = p
    hv, hq = GDN_V_HEADS, GDN_QK_HEADS
    qkv_w = 2 * hq * GDN_DK + hv * GDN_DV
    main_w = qkv_w + hv * GDN_DV
    gate_w = _round_up(4 * hv, LANE)
    half = gate_w // 2
    w_in = w_in_stack[idx]
    w_ba = jnp.concatenate([_pad_cols(w_in[:, main_w:main_w + 2 * hv], half),
                            _pad_cols(w_in[:, main_w + 2 * hv:], half)], axis=1).astype(BF16)
    zeros = jnp.zeros((1, half), F32)
    alog = jnp.concatenate([zeros, _pad_cols(a_log.reshape(1, -1), half)], axis=1)
    dtb = jnp.concatenate([zeros, _pad_cols(dt_bias.reshape(1, -1), half)], axis=1)

    u = _modulate(h, n_rows, gain, mod_rows, layer, 0, 1)
    y = _mm_wcast(u, w_in_stack, idx, main_w, n_rows, "odd_in")
    ba = _mm([u], w_ba, n_rows, name="odd_in_gates")
    bg = _gdn_gate(ba, n_rows, alog, dtb)
    bg_t = jnp.stack([bg[:, :2 * hv], bg[:, half:half + 2 * hv]], axis=0)
    bg_t = bg_t.reshape(2, n_rows, 2, hq, hv // hq).transpose(0, 2, 3, 4, 1)
    o_f, o_b = _gdn_scan(_gdn_prep(y, conv_w, bg_t, n_rows), n_rows)
    n_out = n_rows if need_ctx else _n_lat()
    yo = _gdn_readout(o_f, o_b, y, qkv_w, o_g, n_out)
    return _mm([yo], w_out.astype(BF16), n_out, res=h, gate=(mod, layer, 2),
               tn=_pick(h.shape[1], (512, 256, 128)), name="odd_out")


def kernel(x, c, ctx, c_ctx, ada_w, ada_b, norm_mix, norm_ffn, ffn_w_gate, ffn_w_up, ffn_conv_w, ffn_conv_b,
           ffn_w_down, a_w_in, a_q_a_norm, a_kv_a_norm, a_w_qb, a_w_kvb, a_q_norm, a_k_norm, a_sc_conv, a_w_out,
           c_w_in, c_conv_w, c_a_log, c_dt_bias, c_o_norm, c_w_out):
    bn, t, d = x.shape
    depth = ada_w.shape[0]
    assert (bn, t, ctx.shape[1], ffn_w_gate.shape[2]) == (BATCH, SEQ, CTX_LEN, D_FF)
    assert CTX_LEN % GDN_BLK == 0 and SEQ % GDN_BLK == 0
    h = jnp.concatenate([x.reshape(bn * t, d), ctx.reshape(bn * CTX_LEN, d)], axis=0)
    cond8 = jnp.concatenate([c, c_ctx[None], jnp.zeros((SUBLANE - bn - 1, d), F32)], axis=0)
    mod = _ada(cond8, ada_w, ada_b)
    mod_rows = mod.reshape(depth * SUBLANE * 6, 1, d)
    ff_pad = ((0, 0), (0, 0), (0, _round_up(D_FF, 512) - D_FF))
    w_gate = jnp.pad(ffn_w_gate, ff_pad).astype(BF16)
    w_up = jnp.pad(ffn_w_up, ff_pad).astype(BF16)
    w_down = jnp.pad(ffn_w_down, (ff_pad[0], ff_pad[2], ff_pad[1])).astype(BF16)
    conv_w = jnp.pad(ffn_conv_w, ff_pad)
    conv_b = jnp.pad(ffn_conv_b.reshape(depth, 1, D_FF), ff_pad)
    for l in range(depth):
        last = l == depth - 1
        n_rows = h.shape[0]
        i = l // 2
        if l % 2 == 0:
            h = _even_layer(h, n_rows, not last, mod, mod_rows, l, norm_mix[l],
                            (a_w_in[i], a_q_a_norm[i], a_kv_a_norm[i], a_w_qb[i], a_w_kvb[i], a_q_norm[i],
                             a_k_norm[i], a_sc_conv[i], a_w_out[i]))
        else:
            h = _odd_layer(h, n_rows, not last, mod, mod_rows, l, norm_mix[l],
                           (c_w_in, c_conv_w[i], c_a_log[i], c_dt_bias[i], c_o_norm[i], c_w_out[i]), i)
        h = _conv_ffn(h, h.shape[0], norm_ffn[l], mod_rows, mod, l,
                      w_gate, w_up, conv_w, conv_b, w_down)
    return h[:bn * t].reshape(bn, t, d)
```

```python
import functools
import math

import jax
import jax.numpy as jnp
import numpy as np
from jax import lax
from jax.experimental import pallas as pl
from jax.experimental.pallas import tpu as pltpu

F32 = jnp.float32
BF16 = jnp.bfloat16

BATCH = 2
SEQ = 4096
GRID_W = 64
CTX_LEN = 256
EPS = 1e-6
D_FF = 11008

MLA_HEADS = 16
MLA_NOPE = 128
MLA_ROPE = 64
MLA_VDIM = 128
MLA_Q_RANK = 1024
MLA_KV_RANK = 512
ROPE_THETA = 10000.0

GDN_QK_HEADS = 16
GDN_V_HEADS = 32
GDN_DK = 128
GDN_DV = 128

LANE = 128
SUBLANE = 8
QK_PAD = 256
V_PAD = 256
ROW_BLK = 256
GDN_BLK = 256
VMEM_LIMIT = 56 * 2**20


def _cp(*sem, vmem=VMEM_LIMIT):
    return pltpu.CompilerParams(dimension_semantics=sem, vmem_limit_bytes=vmem)


def _pick(n, prefs):
    for p in prefs:
        if n % p == 0:
            return p
    raise ValueError(f"no tile for {n} in {prefs}")


def _round_up(n, m):
    return (n + m - 1) // m * m


def _sigmoid(x):
    return 1.0 / (1.0 + jnp.exp(-x))


def _silu(x):
    return x * _sigmoid(x)


def _dot(a, b):
    return jnp.dot(a, b, preferred_element_type=F32)


def _dot_nt(a, b):
    return lax.dot_general(a, b, (((1,), (1,)), ((), ())), preferred_element_type=F32)


def _dot_tn(a, b):
    return lax.dot_general(a, b, (((0,), (0,)), ((), ())), preferred_element_type=F32)


def _n_lat():
    return BATCH * SEQ


def _seq_edges(row0, tm):
    assert SEQ & (SEQ - 1) == 0 and CTX_LEN & (CTX_LEN - 1) == 0 and SEQ % CTX_LEN == 0
    r = row0 + lax.broadcasted_iota(jnp.int32, (tm, 1), 0)
    is_ctx = r >= _n_lat()
    first = (jnp.bitwise_and(r, CTX_LEN - 1) == 0) & ((jnp.bitwise_and(r, SEQ - 1) == 0) | is_ctx)
    r1 = r + 1
    last = (jnp.bitwise_and(r1, CTX_LEN - 1) == 0) & ((jnp.bitwise_and(r1, SEQ - 1) == 0) | is_ctx)
    return first, last


def _shift_rows(x, prev_row, next_row, row0):
    tm = x.shape[0]
    ridx = lax.broadcasted_iota(jnp.int32, (tm, 1), 0)
    first, last = _seq_edges(row0, tm)
    dn = jnp.where(ridx == 0, prev_row, pltpu.roll(x, 1, 0))
    dn = jnp.where(first, 0.0, dn)
    up = jnp.where(ridx == tm - 1, next_row, pltpu.roll(x, tm - 1, 0))
    up = jnp.where(last, 0.0, up)
    return dn, up


def _row_select(row0, tm, table):
    r = row0 + lax.broadcasted_iota(jnp.int32, (tm, 1), 0)
    out = table[BATCH:BATCH + 1]
    for b in reversed(range(BATCH)):
        out = jnp.where(r < (b + 1) * SEQ, table[b:b + 1], out)
    return out


def _halo_specs(tm, tn, n_rows, col_blk0):
    tmb, last = tm // SUBLANE, n_rows // SUBLANE - 1
    prev = pl.BlockSpec((SUBLANE, tn), lambda i, j: (jnp.maximum(i * tmb - 1, 0), col_blk0 + j))
    nxt = pl.BlockSpec((SUBLANE, tn), lambda i, j: (jnp.minimum((i + 1) * tmb, last), col_blk0 + j))
    return prev, nxt


def _ada_kernel(cond_ref, w_ref, b_ref, o_ref):
    a = _silu(cond_ref[...]).astype(BF16)
    o_ref[...] = _dot(a, w_ref[...].astype(BF16)) + b_ref[...]


def _ada(cond8, ada_w, ada_b):
    n_layer, d, n = ada_w.shape
    tn = _pick(n, (512, 256, 128))
    return pl.pallas_call(
        _ada_kernel,
        grid=(n_layer, n // tn),
        in_specs=[pl.BlockSpec((SUBLANE, d), lambda l, j: (0, 0)),
                  pl.BlockSpec((None, d, tn), lambda l, j: (l, 0, j)),
                  pl.BlockSpec((None, 1, tn), lambda l, j: (l, 0, j))],
        out_specs=pl.BlockSpec((None, SUBLANE, tn), lambda l, j: (l, 0, j)),
        out_shape=jax.ShapeDtypeStruct((n_layer, SUBLANE, n), F32),
        compiler_params=_cp("parallel", "parallel"),
        name="ada",
    )(cond8, ada_w, ada_b.reshape(n_layer, 1, n))


def _modulate_kernel(h_ref, g_ref, sh_ref, sc_ref, o_ref):
    x = h_ref[...]
    y = x * lax.rsqrt(jnp.mean(x * x, axis=-1, keepdims=True) + EPS)
    o_ref[...] = ((y * g_ref[...]) * (1.0 + sc_ref[...]) + sh_ref[...]).astype(o_ref.dtype)


def _modulate(h, n_rows, gain, mod_rows, layer, k_shift, k_scale):
    d = h.shape[1]
    blk_per_seq = SEQ // ROW_BLK

    def mod_spec(k):
        return pl.BlockSpec(
            (None, 1, d),
            lambda i: ((layer * SUBLANE + jnp.minimum(i // blk_per_seq, BATCH)) * 6 + k, 0, 0))

    return pl.pallas_call(
        _modulate_kernel,
        grid=(n_rows // ROW_BLK,),
        in_specs=[pl.BlockSpec((ROW_BLK, d), lambda i: (i, 0)),
                  pl.BlockSpec((1, d), lambda i: (0, 0)),
                  mod_spec(k_shift), mod_spec(k_scale)],
        out_specs=pl.BlockSpec((ROW_BLK, d), lambda i: (i, 0)),
        out_shape=jax.ShapeDtypeStruct((n_rows, d), BF16),
        compiler_params=_cp("parallel"),
        name="modulate",
    )(h, gain.reshape(1, d), mod_rows, mod_rows)


def _cast_pad_kernel(w_ref, o_ref, *, n_in_blocks):
    i = pl.program_id(1)
    n_in = w_ref.shape[1]

    @pl.when(i < n_in_blocks)
    def _():
        o_ref[:, :n_in] = w_ref[...].astype(o_ref.dtype)
        if o_ref.shape[1] > n_in:
            o_ref[:, n_in:] = jnp.zeros((o_ref.shape[0], o_ref.shape[1] - n_in), o_ref.dtype)

    @pl.when(i >= n_in_blocks)
    def _():
        o_ref[...] = jnp.zeros_like(o_ref)


def _cast_pad(w_stack, n_rows_out, n_cols_out):
    n_layer, k, n = w_stack.shape
    tr = ROW_BLK
    assert k % tr == 0 and n_rows_out % tr == 0 and n % LANE == 0 and n_cols_out % LANE == 0
    n_in_blocks = k // tr
    return pl.pallas_call(
        functools.partial(_cast_pad_kernel, n_in_blocks=n_in_blocks),
        grid=(n_layer, n_rows_out // tr),
        in_specs=[pl.BlockSpec((None, tr, n), lambda l, i: (l, jnp.minimum(i, n_in_blocks - 1), 0))],
        out_specs=pl.BlockSpec((None, tr, n_cols_out), lambda l, i: (l, i, 0)),
        out_shape=jax.ShapeDtypeStruct((n_layer, n_rows_out, n_cols_out), BF16),
        compiler_params=_cp("parallel", "parallel"),
        name="cast_pad",
    )(w_stack)


def _mm_kernel(*refs, n_pairs, gated, tm):
    a_refs, w_refs = refs[:n_pairs], refs[n_pairs:2 * n_pairs]
    acc = _dot(a_refs[0][...], w_refs[0][...])
    for a_ref, w_ref in zip(a_refs[1:], w_refs[1:]):
        acc += _dot(a_ref[...], w_ref[...])
    if gated:
        res_ref, gate_ref, o_ref = refs[2 * n_pairs:]
        gate = _row_select(pl.program_id(0) * tm, tm, gate_ref[...])
        acc = res_ref[...] + gate * acc
    else:
        o_ref = refs[2 * n_pairs]
    o_ref[...] = acc.astype(o_ref.dtype)


def _mm(a_list, w, n_rows, *, res=None, gate=None, tn=None, name="mm"):
    n_pairs = len(a_list)
    n = w.shape[1]
    tm = _pick(n_rows, (1088, 1024, 512, 256))
    tn = tn or _pick(n, (1024, 512, 256, 128))
    in_specs = [pl.BlockSpec((tm, a.shape[1]), lambda i, j: (i, 0)) for a in a_list]
    in_specs += [pl.BlockSpec((a.shape[1], tn), lambda i, j, p=p: (p, j)) for p, a in enumerate(a_list)]
    assert all(a.shape[1] == a_list[0].shape[1] for a in a_list) and w.shape[0] == n_pairs * a_list[0].shape[1]
    args = list(a_list) + [w] * n_pairs
    if res is not None:
        table, layer, chunk = gate
        d = table.shape[2] // 6
        in_specs += [pl.BlockSpec((tm, tn), lambda i, j: (i, j)),
                     pl.BlockSpec((None, SUBLANE, tn), lambda i, j: (layer, 0, chunk * (d // tn) + j))]
        args += [res, table]
    return pl.pallas_call(
        functools.partial(_mm_kernel, n_pairs=n_pairs, gated=res is not None, tm=tm),
        grid=(n_rows // tm, n // tn),
        in_specs=in_specs,
        out_specs=pl.BlockSpec((tm, tn), lambda i, j: (i, j)),
        out_shape=jax.ShapeDtypeStruct((n_rows, n), F32),
        compiler_params=_cp("parallel", "parallel"),
        name=name,
    )(*args)


def _mm_wcast_kernel(a_ref, w_ref, o_ref, wb_ref):
    @pl.when(pl.program_id(1) == 0)
    def _():
        wb_ref[...] = w_ref[...].astype(BF16)

    o_ref[...] = _dot(a_ref[...], wb_ref[...])


def _mm_wcast(a, w_stack, layer, n_cols, n_rows, name):
    kdim = a.shape[1]
    tm = _pick(n_rows, (1088, 1024, 512, 256))
    tn = _pick(n_cols, (512, 256, 128))
    return pl.pallas_call(
        _mm_wcast_kernel,
        grid=(n_cols // tn, n_rows // tm),
        in_specs=[pl.BlockSpec((tm, kdim), lambda j, i: (i, 0)),
                  pl.BlockSpec((None, kdim, tn), lambda j, i: (layer, 0, j))],
        out_specs=pl.BlockSpec((tm, tn), lambda j, i: (i, j)),
        out_shape=jax.ShapeDtypeStruct((n_rows, n_cols), F32),
        scratch_shapes=[pltpu.VMEM((kdim, tn), BF16)],
        compiler_params=_cp("parallel", "arbitrary"),
        name=name,
    )(a, w_stack)


def _mmk_kernel(a_ref, w_ref, res_ref, gate_ref, o_ref, acc_ref, *, tm):
    k = pl.program_id(2)

    @pl.when(k == 0)
    def _():
        acc_ref[...] = jnp.zeros_like(acc_ref)

    acc_ref[...] += _dot(a_ref[...], w_ref[...])

    @pl.when(k == pl.num_programs(2) - 1)
    def _():
        gate = _row_select(pl.program_id(0) * tm, tm, gate_ref[...])
        o_ref[...] = res_ref[...] + gate * acc_ref[...]


def _mm_ktiled(a, w_stack, layer, n_rows, res, gate):
    _, kdim, n = w_stack.shape
    table, gate_layer, chunk = gate
    d = table.shape[2] // 6
    tm = _pick(n_rows, (1088, 1024, 512, 256))
    tn = _pick(n, (1024, 512, 256, 128))
    tk = _pick(kdim, (2816, 2048, 1024, 512, 256))
    return pl.pallas_call(
        functools.partial(_mmk_kernel, tm=tm),
        grid=(n_rows // tm, n // tn, kdim // tk),
        in_specs=[pl.BlockSpec((tm, tk), lambda i, j, k: (i, k)),
                  pl.BlockSpec((None, tk, tn), lambda i, j, k: (layer, k, j)),
                  pl.BlockSpec((tm, tn), lambda i, j, k: (i, j)),
                  pl.BlockSpec((None, SUBLANE, tn), lambda i, j, k: (gate_layer, 0, chunk * (d // tn) + j))],
        out_specs=pl.BlockSpec((tm, tn), lambda i, j, k: (i, j)),
        out_shape=jax.ShapeDtypeStruct((n_rows, n), F32),
        scratch_shapes=[pltpu.VMEM((tm, tn), F32)],
        compiler_params=_cp("parallel", "parallel", "arbitrary"),
        name="ffn_down",
    )(a, w_stack, res, table)


def _ffn_up_kernel(u_ref, uh_ref, wg_ref, wu_ref, cw_ref, cb_ref, o_ref, halo_ref, *, tm):
    i = pl.program_id(1)

    @pl.when(i == 0)
    def _():
        halo_ref[...] = _dot(uh_ref[...], wg_ref[...])

    u = u_ref[...]
    g = _dot(u, wg_ref[...])
    dn, up = _shift_rows(g, halo_ref[pl.ds(2 * i, 1), :], halo_ref[pl.ds(2 * i + 1, 1), :], i * tm)
    a = dn * cw_ref[0:1, :] + g * cw_ref[1:2, :] + up * cw_ref[2:3, :] + cb_ref[...]
    o_ref[...] = (_silu(a) * _dot(u, wu_ref[...])).astype(o_ref.dtype)


def _ffn_up(u, n_rows, layer, wg_stack, wu_stack, cw_stack, cb_stack):
    _, d, n = wg_stack.shape
    tm = _pick(n_rows, (1088, 1024, 512, 256))
    tn = _pick(n, (512, 256, 128))
    gm = n_rows // tm
    n_halo = _round_up(2 * gm, 2 * SUBLANE)
    rows = []
    for i in range(gm):
        rows += [max(i * tm - 1, 0), min((i + 1) * tm, n_rows - 1)]
    rows += [0] * (n_halo - len(rows))
    u_halo = jnp.concatenate([u[r:r + 1] for r in rows], axis=0)
    col = lambda j, i: (layer, 0, j)
    return pl.pallas_call(
        functools.partial(_ffn_up_kernel, tm=tm),
        grid=(n // tn, gm),
        in_specs=[pl.BlockSpec((tm, d), lambda j, i: (i, 0)),
                  pl.BlockSpec((n_halo, d), lambda j, i: (0, 0)),
                  pl.BlockSpec((None, d, tn), col),
                  pl.BlockSpec((None, d, tn), col),
                  pl.BlockSpec((None, 3, tn), col),
                  pl.BlockSpec((None, 1, tn), col)],
        out_specs=pl.BlockSpec((tm, tn), lambda j, i: (i, j)),
        out_shape=jax.ShapeDtypeStruct((n_rows, n), BF16),
        scratch_shapes=[pltpu.VMEM((n_halo, tn), F32)],
        compiler_params=_cp("parallel", "arbitrary"),
        name="ffn_up",
    )(u, u_halo, wg_stack, wu_stack, cw_stack, cb_stack)


def _conv_ffn(h, n_rows, gain, mod_rows, mod, layer, wg, wu, cw, cb, wd):
    u = _modulate(h, n_rows, gain, mod_rows, layer, 3, 4)
    hid = _ffn_up(u, n_rows, layer, wg, wu, cw, cb)
    return _mm_ktiled(hid, wd, layer, n_rows, h, (mod, layer, 5))


def _rope(x, rope_ref):
    c, sa, sb = rope_ref[:, 0:LANE], rope_ref[:, LANE:2 * LANE], rope_ref[:, 2 * LANE:3 * LANE]
    quarter = MLA_ROPE // 4
    return x * c + pltpu.roll(x, LANE - quarter, 1) * sa + pltpu.roll(x, quarter, 1) * sb


def _rms(x, width):
    return lax.rsqrt(jnp.sum(x * x, axis=-1, keepdims=True) * (1.0 / width) + EPS)


def _qproj_kernel(cq_ref, ag_ref, w_ref, hg_ref, rope_ref, o_ref, *, heads):
    x = cq_ref[...]
    xn = ((x * _rms(x, x.shape[1])) * ag_ref[...]).astype(BF16)
    y = _dot(xn, w_ref[...])
    for hh in range(heads):
        yh = y[:, hh * QK_PAD:(hh + 1) * QK_PAD]
        yn = (yh * _rms(yh, MLA_NOPE + MLA_ROPE)) * hg_ref[...]
        o_ref[:, hh * QK_PAD:hh * QK_PAD + LANE] = yn[:, :LANE].astype(o_ref.dtype)
        o_ref[:, hh * QK_PAD + LANE:(hh + 1) * QK_PAD] = _rope(yn[:, LANE:], rope_ref).astype(o_ref.dtype)


def _qproj(y_in, n_rows, ag, w, hg, rope):
    heads = _pick(MLA_HEADS, (4, 2, 1))
    tm = _proj_rows()
    tn = heads * QK_PAD
    return pl.pallas_call(
        functools.partial(_qproj_kernel, heads=heads),
        grid=(n_rows // tm, w.shape[1] // tn),
        in_specs=[pl.BlockSpec((tm, MLA_Q_RANK), lambda i, j: (i, 0)),
                  pl.BlockSpec((1, MLA_Q_RANK), lambda i, j: (0, 0)),
                  pl.BlockSpec((MLA_Q_RANK, tn), lambda i, j: (0, j)),
                  pl.BlockSpec((1, QK_PAD), lambda i, j: (0, 0)),
                  _rope_spec()],
        out_specs=pl.BlockSpec((tm, tn), lambda i, j: (i, j)),
        out_shape=jax.ShapeDtypeStruct((n_rows, w.shape[1]), BF16),
        compiler_params=_cp("parallel", "parallel"),
        name="q_proj",
    )(y_in, ag, w, hg, rope)


def _kvproj_kernel(ckv_ref, kr_ref, ag_ref, w_ref, gn_ref, gr_ref, rope_ref, k_ref, v_ref, *, heads):
    x = ckv_ref[...]
    xn = ((x * _rms(x, x.shape[1])) * ag_ref[...]).astype(BF16)
    y = _dot(xn, w_ref[...])
    kr = kr_ref[...]
    kr_ss = jnp.sum(kr * kr, axis=-1, keepdims=True)
    width = MLA_NOPE + MLA_VDIM
    ones = jnp.ones((x.shape[0], V_PAD - MLA_VDIM), v_ref.dtype)
    for hh in range(heads):
        kn = y[:, hh * width:hh * width + MLA_NOPE]
        r = lax.rsqrt((jnp.sum(kn * kn, axis=-1, keepdims=True) + kr_ss) * (1.0 / (MLA_NOPE + MLA_ROPE)) + EPS)
        k_ref[:, hh * QK_PAD:hh * QK_PAD + LANE] = ((kn * r) * gn_ref[...]).astype(k_ref.dtype)
        k_ref[:, hh * QK_PAD + LANE:(hh + 1) * QK_PAD] = _rope((kr * r) * gr_ref[...], rope_ref).astype(k_ref.dtype)
        v_ref[:, hh * V_PAD:hh * V_PAD + MLA_VDIM] = y[:, hh * width + MLA_NOPE:(hh + 1) * width].astype(v_ref.dtype)
        v_ref[:, hh * V_PAD + MLA_VDIM:(hh + 1) * V_PAD] = ones


def _kvproj(y_in, n_rows, ag, w, gn, gr, rope):
    heads = _pick(MLA_HEADS, (4, 2, 1))
    assert MLA_NOPE == LANE and MLA_VDIM == LANE and MLA_Q_RANK % MLA_KV_RANK == 0
    tm = _proj_rows()
    tn = heads * (MLA_NOPE + MLA_VDIM)
    kr_blk = (MLA_Q_RANK + MLA_KV_RANK) // LANE
    return pl.pallas_call(
        functools.partial(_kvproj_kernel, heads=heads),
        grid=(n_rows // tm, w.shape[1] // tn),
        in_specs=[pl.BlockSpec((tm, MLA_KV_RANK), lambda i, j: (i, MLA_Q_RANK // MLA_KV_RANK)),
                  pl.BlockSpec((tm, LANE), lambda i, j: (i, kr_blk)),
                  pl.BlockSpec((1, MLA_KV_RANK), lambda i, j: (0, 0)),
                  pl.BlockSpec((MLA_KV_RANK, tn), lambda i, j: (0, j)),
                  pl.BlockSpec((1, LANE), lambda i, j: (0, 0)),
                  pl.BlockSpec((1, LANE), lambda i, j: (0, 0)),
                  _rope_spec()],
        out_specs=[pl.BlockSpec((tm, heads * QK_PAD), lambda i, j: (i, j)),
                   pl.BlockSpec((tm, heads * V_PAD), lambda i, j: (i, j))],
        out_shape=[jax.ShapeDtypeStruct((n_rows, MLA_HEADS * QK_PAD), BF16),
                   jax.ShapeDtypeStruct((n_rows, MLA_HEADS * V_PAD), BF16)],
        compiler_params=_cp("parallel", "parallel"),
        name="kv_proj",
    )(y_in, y_in, ag, w, gn, gr, rope)


def _attn_kernel(*refs, chunks, scale):
    q = refs[0][...]
    n_kv = (len(refs) - 2) // 2
    k_refs, v_refs, o_ref = refs[1:1 + n_kv], refs[1 + n_kv:1 + 2 * n_kv], refs[1 + 2 * n_kv]
    c = scale * math.log2(math.e)
    m = acc = None
    for idx, start, size in chunks:
        s = _dot_nt(q, k_refs[idx][start:start + size, :])
        m_blk = jnp.max(s, axis=-1, keepdims=True)
        m_new = m_blk if m is None else jnp.maximum(m, m_blk)
        pv = _dot(jnp.exp2((s - m_new) * c).astype(BF16), v_refs[idx][start:start + size, :])
        acc = pv if acc is None else jnp.exp2((m - m_new) * c) * acc + pv
        m = m_new
    o_ref[...] = (acc[:, :MLA_VDIM] / acc[:, MLA_VDIM:2 * MLA_VDIM]).astype(o_ref.dtype)


def _attention(q, k, v, latent):
    assert V_PAD == 2 * MLA_VDIM
    scale = (MLA_NOPE + MLA_ROPE) ** -0.5
    ctx_blk0 = _n_lat() // CTX_LEN
    ctx_k = pl.BlockSpec((CTX_LEN, QK_PAD), lambda b, h, i: (ctx_blk0 + b, h))
    ctx_v = pl.BlockSpec((CTX_LEN, V_PAD), lambda b, h, i: (ctx_blk0 + b, h))
    if latent:
        tq = _pick(SEQ, (1024, 512, 256))
        tk = _pick(SEQ, (1024, 512, 256))
        q_per_b = SEQ // tq
        q_spec = pl.BlockSpec((tq, QK_PAD), lambda b, h, i: (b * q_per_b + i, h))
        o_spec = pl.BlockSpec((tq, MLA_VDIM), lambda b, h, i: (b * q_per_b + i, h))
        k_specs = [ctx_k, pl.BlockSpec((SEQ, QK_PAD), lambda b, h, i: (b, h))]
        v_specs = [ctx_v, pl.BlockSpec((SEQ, V_PAD), lambda b, h, i: (b, h))]
        chunks = [(0, 0, CTX_LEN)] + [(1, s, tk) for s in range(0, SEQ, tk)]
        n_out, grid = _n_lat(), (BATCH, MLA_HEADS, q_per_b)
    else:
        q_spec = pl.BlockSpec((CTX_LEN, QK_PAD), lambda b, h, i: (ctx_blk0 + b, h))
        o_spec = pl.BlockSpec((CTX_LEN, MLA_VDIM), lambda b, h, i: (b, h))
        k_specs, v_specs, chunks = [ctx_k], [ctx_v], [(0, 0, CTX_LEN)]
        n_out, grid = BATCH * CTX_LEN, (BATCH, MLA_HEADS, 1)
    n_kv = len(k_specs)
    return pl.pallas_call(
        functools.partial(_attn_kernel, chunks=tuple(chunks), scale=scale),
        grid=grid,
        in_specs=[q_spec] + k_specs + v_specs,
        out_specs=o_spec,
        out_shape=jax.ShapeDtypeStruct((n_out, MLA_HEADS * MLA_VDIM), BF16),
        compiler_params=_cp("parallel", "parallel", "arbitrary"),
        name="attn_lat" if latent else "attn_ctx",
    )(q, *([k] * n_kv), *([v] * n_kv))


def _sconv_kernel(b_ref, c_ref, x_ref, cp_ref, xp_ref, cn_ref, xn_ref, w_ref, o_ref, *, tm):
    p = c_ref[...] * x_ref[...]
    prev = cp_ref[SUBLANE - 1:SUBLANE, :] * xp_ref[SUBLANE - 1:SUBLANE, :]
    nxt = cn_ref[0:1, :] * xn_ref[0:1, :]
    dn, up = _shift_rows(p, prev, nxt, pl.program_id(0) * tm)
    conv = dn * w_ref[0:1, :] + p * w_ref[1:2, :] + up * w_ref[2:3, :]
    o_ref[...] = (b_ref[...] * conv).astype(o_ref.dtype)


def _sconv(y_in, n_rows, w, col0, width):
    tm = ROW_BLK
    tn = _pick(width, (1024, 512, 256, 128))
    blk0 = [(col0 + k * width) // tn for k in range(3)]
    main = [pl.BlockSpec((tm, tn), lambda i, j, o=o: (i, o + j)) for o in blk0]
    cp, cn = _halo_specs(tm, tn, n_rows, blk0[1])
    xp, xn = _halo_specs(tm, tn, n_rows, blk0[2])
    return pl.pallas_call(
        functools.partial(_sconv_kernel, tm=tm),
        grid=(n_rows // tm, width // tn),
        in_specs=main + [cp, xp, cn, xn, pl.BlockSpec((3, tn), lambda i, j: (0, j))],
        out_specs=pl.BlockSpec((tm, tn), lambda i, j: (i, j)),
        out_shape=jax.ShapeDtypeStruct((n_rows, width), BF16),
        compiler_params=_cp("parallel", "parallel"),
        name="short_conv",
    )(*([y_in] * 7), w)


def _conv_silu(x_ref, xp_ref, xn_ref, w_ref, row0):
    x = x_ref[...]
    dn, up = _shift_rows(x, xp_ref[SUBLANE - 1:SUBLANE, :], xn_ref[0:1, :], row0)
    return _silu(dn * w_ref[0:1, :] + x * w_ref[1:2, :] + up * w_ref[2:3, :])


def _l2norm(x):
    return x * lax.rsqrt(jnp.sum(x * x, axis=-1, keepdims=True) + EPS)


def _gdn_gate_kernel(ba_ref, alog_ref, dtb_ref, o_ref):
    x = ba_ref[...]
    lane = lax.broadcasted_iota(jnp.int32, x.shape, 1)
    z = x + dtb_ref[...]
    softplus = jnp.maximum(z, 0.0) + jnp.log(1.0 + jnp.exp(-jnp.abs(z)))
    o_ref[...] = jnp.where(lane < x.shape[1] // 2, _sigmoid(x), -jnp.exp(alog_ref[...]) * softplus)


def _gdn_gate(ba, n_rows, alog, dtb):
    w = ba.shape[1]
    return pl.pallas_call(
        _gdn_gate_kernel,
        grid=(n_rows // ROW_BLK,),
        in_specs=[pl.BlockSpec((ROW_BLK, w), lambda i: (i, 0)),
                  pl.BlockSpec((1, w), lambda i: (0, 0)),
                  pl.BlockSpec((1, w), lambda i: (0, 0))],
        out_specs=pl.BlockSpec((ROW_BLK, w), lambda i: (i, 0)),
        out_shape=jax.ShapeDtypeStruct((n_rows, w), F32),
        compiler_params=_cp("parallel"),
        name="gdn_gate",
    )(ba, alog, dtb)


def _blk_mask(ii, jj, size):
    shift = size.bit_length() - 1
    return jnp.right_shift(ii, shift) == jnp.right_shift(jj, shift)


def _active_rows(x, size, odd):
    n = x.shape[0]
    return jnp.concatenate([x[(2 * b + odd) * size:(2 * b + odd + 1) * size] for b in range(n // (2 * size))], axis=0)


def _weave_rows(rest, active, size, odd):
    pieces = []
    for b in range(active.shape[0] // size):
        if rest is None:
            keep = jnp.zeros((size, active.shape[1]), active.dtype)
        else:
            keep = rest[(2 * b + 1 - odd) * size:(2 * b + 2 - odd) * size]
        act = active[b * size:(b + 1) * size]
        pieces += [keep, act] if odd else [act, keep]
    return jnp.concatenate(pieces, axis=0)


def _unit_tri_inverses(l_mats, dirs, eye, base, ring_ref):
    size = SUBLANE
    xs = [-(l * base) for l in l_mats]
    ts = [eye + x for x in xs]
    for _ in range(size.bit_length() - 2):
        xs = [_dot(x.astype(BF16), x.astype(BF16)) for x in xs]
        ts = [t + _dot(t.astype(BF16), x.astype(BF16)) for t, x in zip(ts, xs)]
    level = 0
    while size < GDN_BLK:
        tbs = [t.astype(BF16) for t in ts]
        offs = [(_active_rows(l, size, 1 - d) * ring_ref[level, d]).astype(BF16) for l, d in zip(l_mats, dirs)]
        mids = [_weave_rows(None, _dot(off, tb), size, 1 - d).astype(BF16) for off, tb, d in zip(offs, tbs, dirs)]
        acts = [_active_rows(t, size, 1 - d) for t, d in zip(ts, dirs)]
        news = [act - _dot(act.astype(BF16), mid) for act, mid in zip(acts, mids)]
        ts = [_weave_rows(t, new, size, 1 - d) for t, new, d in zip(ts, news, dirs)]
        size *= 2
        level += 1
    return ts


def _gdn_masks():
    n, size = GDN_BLK, SUBLANE
    i, j = np.arange(n)[:, None], np.arange(n)[None, :]
    tri = np.stack([j <= i, j >= i, j == i, i // size == j // size]).astype(np.float32)
    rings = []
    while size < n:
        ra = np.arange(n // 2)[:, None]
        per_dir = []
        for d in range(2):
            ia = (ra // size) * 2 * size + (1 - d) * size + ra % size
            per_dir.append((ia // (2 * size) == j // (2 * size)) & (ia // size != j // size))
        rings.append(np.stack(per_dir))
        size *= 2
    return jnp.asarray(tri), jnp.asarray(np.stack(rings).astype(np.float32))


def _gdn_prep_kernel(q_ref, qp_ref, qn_ref, k_ref, kp_ref, kn_ref, v_ref, vp_ref, vn_ref, cwq_ref, cwk_ref, cwv_ref,
                     bg_ref, tri_ref, ring_ref, u_ref, wq_ref, kd_ref, in_ref, cd_ref, *, rep, hpb):
    n = GDN_BLK
    row0 = pl.program_id(0) * n
    q_all = _conv_silu(q_ref, qp_ref, qn_ref, cwq_ref, row0)
    k_all = _conv_silu(k_ref, kp_ref, kn_ref, cwk_ref, row0)
    v_all = _conv_silu(v_ref, vp_ref, vn_ref, cwv_ref, row0)
    eye, base = tri_ref[2], tri_ref[3]
    not_eye = 1.0 - eye
    chains, l_mats, rhss = [], [], []
    for hh in range(hpb):
        q = _l2norm(q_all[:, hh * GDN_DK:(hh + 1) * GDN_DK]) * GDN_DK ** -0.5
        k = _l2norm(k_all[:, hh * GDN_DK:(hh + 1) * GDN_DK])
        q16, k16 = q.astype(BF16), k.astype(BF16)
        qk = _dot_nt(q16, k16)
        kk = _dot_nt(k16, k16) * not_eye
        bg = bg_ref[hh]
        bg_cols = jnp.concatenate([bg, jnp.zeros((LANE - bg.shape[0], n), F32)], axis=0).T
        for e in range(rep):
            for d in range(2):
                vh = hh * rep + e
                cols = slice(vh * GDN_DV, (vh + 1) * GDN_DV)
                incl, incl_t = tri_ref[d], tri_ref[1 - d]
                b_idx, g_idx = d * rep + e, (2 + d) * rep + e
                g_row = bg[g_idx:g_idx + 1, :]
                b_col, g_col = bg_cols[:, b_idx:b_idx + 1], bg_cols[:, g_idx:g_idx + 1]
                gc_col = jnp.sum(incl * g_row, axis=1, keepdims=True)
                gc_row = jnp.sum(incl_t * g_col, axis=0, keepdims=True)
                g_tot = jnp.sum(g_row, axis=1, keepdims=True)
                decay = jnp.exp(jnp.minimum(gc_col - gc_row, 0.0)) * incl
                e_col = jnp.exp(gc_col)
                chains.append((vh, d, cols))
                l_mats.append((kk * b_col) * decay)
                rhss.append(jnp.concatenate([v_all[:, cols] * b_col, (k * b_col) * e_col], axis=1).astype(BF16))
                wq_ref[d, n:, cols] = (q * e_col).astype(wq_ref.dtype)
                kd_ref[d, :, cols] = (k * jnp.exp(g_tot - gc_col)).astype(kd_ref.dtype)
                in_ref[d, vh] = (qk * decay).astype(in_ref.dtype)
                cd_ref[d, vh] = jnp.broadcast_to(jnp.exp(g_tot), (SUBLANE, LANE))
    invs = _unit_tri_inverses(l_mats, [d for _, d, _ in chains], eye, base, ring_ref)
    sols = [_dot(t.astype(BF16), rhs) for t, rhs in zip(invs, rhss)]
    for (vh, d, cols), sol in zip(chains, sols):
        u_ref[d, :, cols] = sol[:, :GDN_DV]
        wq_ref[d, :n, cols] = sol[:, GDN_DV:].astype(wq_ref.dtype)


def _gdn_prep(y_in, conv_w, bg_t, n_rows):
    assert GDN_DK == LANE and GDN_DV == LANE and GDN_BLK == 2 * LANE
    nb, hv, hq, rep = n_rows // GDN_BLK, GDN_V_HEADS, GDN_QK_HEADS, GDN_V_HEADS // GDN_QK_HEADS
    hpb = _pick(hq, (2, 1))
    key_w, wide = hpb * GDN_DK, hpb * rep * GDN_DV
    parts = ((key_w, 0), (key_w, hq // hpb), (wide, 2 * hq * GDN_DK // wide))
    in_specs = []
    for tn, blk0 in parts:
        in_specs += [pl.BlockSpec((GDN_BLK, tn), lambda i, h, blk0=blk0: (i, blk0 + h))]
        in_specs += _halo_specs(GDN_BLK, tn, n_rows, blk0)
    in_specs += [pl.BlockSpec((3, tn), lambda i, h, blk0=blk0: (0, blk0 + h)) for tn, blk0 in parts]
    tri, rings = _gdn_masks()
    in_specs += [pl.BlockSpec((hpb, bg_t.shape[1], GDN_BLK), lambda i, h: (h, 0, i)),
                 pl.BlockSpec(tri.shape, lambda i, h: (0, 0, 0)),
                 pl.BlockSpec(rings.shape, lambda i, h: (0, 0, 0, 0))]
    return pl.pallas_call(
        functools.partial(_gdn_prep_kernel, rep=rep, hpb=hpb),
        grid=(nb, hq // hpb),
        in_specs=in_specs,
        out_specs=[pl.BlockSpec((2, GDN_BLK, wide), lambda i, h: (0, i, h)),
                   pl.BlockSpec((2, None, 2 * GDN_BLK, wide), lambda i, h: (0, i, 0, h)),
                   pl.BlockSpec((2, GDN_BLK, wide), lambda i, h: (0, i, h)),
                   pl.BlockSpec((2, hpb * rep, GDN_BLK, GDN_BLK), lambda i, h: (0, h, i, 0)),
                   pl.BlockSpec((2, hpb * rep, SUBLANE, LANE), lambda i, h: (0, h, i, 0))],
        out_shape=[jax.ShapeDtypeStruct((2, n_rows, hv * GDN_DV), F32),
                   jax.ShapeDtypeStruct((2, nb, 2 * GDN_BLK, hv * GDN_DV), BF16),
                   jax.ShapeDtypeStruct((2, n_rows, hv * GDN_DV), BF16),
                   jax.ShapeDtypeStruct((2, hv, n_rows, GDN_BLK), BF16),
                   jax.ShapeDtypeStruct((2, hv, nb * SUBLANE, LANE), F32)],
        compiler_params=_cp("parallel", "parallel"),
        name="gdn_prep",
    )(*([y_in] * 9), conv_w, conv_w, conv_w, bg_t, tri, rings)


def _gdn_scan_kernel(*refs, heads):
    ins, (of_ref, ob_ref, s_ref) = refs[:10], refs[10:]
    n = GDN_BLK

    @pl.when(pl.program_id(2) == 0)
    def _():
        s_ref[...] = jnp.zeros_like(s_ref)

    o_refs = (of_ref, ob_ref)
    chains = [(d, hh, slice(hh * GDN_DV, (hh + 1) * GDN_DV)) for d in range(2) for hh in range(heads)]
    u_refs, wq_refs, kd_refs, in_refs, cd_refs = (ins[0::5], ins[1::5], ins[2::5], ins[3::5], ins[4::5])
    states = [s_ref[d, hh] for d, hh, _ in chains]
    wqs = [_dot(wq_refs[d][:, cols], s.astype(BF16)) for (d, hh, cols), s in zip(chains, states)]
    vbs = [(u_refs[d][:, cols] - wq[:n]).astype(BF16) for (d, hh, cols), wq in zip(chains, wqs)]
    for (d, hh, cols), s, wq, vb in zip(chains, states, wqs, vbs):
        o_refs[d][:, cols] = wq[n:] + _dot(in_refs[d][hh], vb)
        s_ref[d, hh] = s * cd_refs[d][hh, 0:1, :] + _dot_tn(kd_refs[d][:, cols], vb)


def _gdn_scan(prep, n_rows):
    hv = GDN_V_HEADS
    heads = _pick(hv, (4, 2, 1))
    lat_blk, ctx_blk = SEQ // GDN_BLK, CTX_LEN // GDN_BLK
    steps = ctx_blk + lat_blk
    ctx0 = _n_lat() // GDN_BLK
    wide = heads * GDN_DV

    def blk_f(b, n):
        return jnp.where(n < ctx_blk, ctx0 + b * ctx_blk + n, b * lat_blk + n - ctx_blk)

    def blk_b(b, n):
        return jnp.where(n < ctx_blk, ctx0 + b * ctx_blk + ctx_blk - 1 - n, b * lat_blk + steps - 1 - n)

    in_specs = []
    for d, blk in enumerate((blk_f, blk_b)):
        rows = pl.BlockSpec((None, GDN_BLK, wide), lambda b, h, n, d=d, blk=blk: (d, blk(b, n), h))
        in_specs += [rows,
                     pl.BlockSpec((None, None, 2 * GDN_BLK, wide), lambda b, h, n, d=d, blk=blk: (d, blk(b, n), 0, h)),
                     rows,
                     pl.BlockSpec((None, heads, GDN_BLK, GDN_BLK), lambda b, h, n, d=d, blk=blk: (d, h, blk(b, n), 0)),
                     pl.BlockSpec((None, heads, SUBLANE, LANE), lambda b, h, n, d=d, blk=blk: (d, h, blk(b, n), 0))]
    out = jax.ShapeDtypeStruct((n_rows, hv * GDN_DV), F32)
    return pl.pallas_call(
        functools.partial(_gdn_scan_kernel, heads=heads),
        grid=(BATCH, hv // heads, steps),
        in_specs=in_specs,
        out_specs=[pl.BlockSpec((GDN_BLK, wide), lambda b, h, n: (blk_f(b, n), h)),
                   pl.BlockSpec((GDN_BLK, wide), lambda b, h, n: (blk_b(b, n), h))],
        out_shape=[out, out],
        scratch_shapes=[pltpu.VMEM((2, heads, GDN_DK, GDN_DV), F32)],
        compiler_params=_cp("parallel", "parallel", "arbitrary"),
        name="gdn_scan",
    )(*prep, *prep)


def _gdn_readout_kernel(of_ref, ob_ref, z_ref, g_ref, o_ref):
    o = of_ref[...] + ob_ref[...]
    z = z_ref[...]
    for hh in range(o.shape[1] // GDN_DV):
        cols = slice(hh * GDN_DV, (hh + 1) * GDN_DV)
        oh = o[:, cols]
        y = (oh * _rms(oh, GDN_DV)) * g_ref[...]
        o_ref[:, cols] = (y * _silu(z[:, cols])).astype(o_ref.dtype)


def _gdn_readout(o_f, o_b, y_in, z_col0, gain, n_rows):
    width = GDN_V_HEADS * GDN_DV
    tn = _pick(width, (1024, 512, 256, 128))
    spec = pl.BlockSpec((ROW_BLK, tn), lambda i, j: (i, j))
    return pl.pallas_call(
        _gdn_readout_kernel,
        grid=(n_rows // ROW_BLK, width // tn),
        in_specs=[spec, spec, pl.BlockSpec((ROW_BLK, tn), lambda i, j: (i, z_col0 // tn + j)),
                  pl.BlockSpec((1, GDN_DV), lambda i, j: (0, 0))],
        out_specs=spec,
        out_shape=jax.ShapeDtypeStruct((n_rows, width), BF16),
        compiler_params=_cp("parallel", "parallel"),
        name="gdn_readout",
    )(o_f, o_b, y_in, gain.reshape(1, GDN_DV))


def _proj_rows():
    return _pick(math.gcd(SEQ, BATCH * CTX_LEN), (512, 256))


def _rope_spec():
    tm = _proj_rows()
    return pl.BlockSpec((tm, 3 * LANE), lambda i, j: (jnp.where(i < _n_lat() // tm, i % (SEQ // tm), SEQ // tm), 0))


def _rope_table(n_ident):
    n_freq = MLA_ROPE // 4
    inv_freq = ROPE_THETA ** (-jnp.arange(n_freq, dtype=F32) / n_freq)
    t = jnp.arange(SEQ)
    ang = jnp.concatenate([(t // GRID_W).astype(F32)[:, None] * inv_freq,
                           (t % GRID_W).astype(F32)[:, None] * inv_freq], axis=1)
    cos, sin = jnp.cos(ang), jnp.sin(ang)
    zero = jnp.zeros_like(sin[:, :n_freq])
    pad = jnp.zeros((SEQ, LANE - MLA_ROPE), F32)
    c = jnp.concatenate([cos[:, :n_freq], cos[:, :n_freq], cos[:, n_freq:], cos[:, n_freq:], pad + 1.0], axis=1)
    sa = jnp.concatenate([-sin[:, :n_freq], zero, -sin[:, n_freq:], zero, pad], axis=1)
    sb = jnp.concatenate([zero, sin[:, :n_freq], zero, sin[:, n_freq:], pad], axis=1)
    ident = jnp.concatenate([jnp.ones((n_ident, LANE), F32), jnp.zeros((n_ident, 2 * LANE), F32)], axis=1)
    return jnp.concatenate([jnp.concatenate([c, sa, sb], axis=1), ident], axis=0)


def _pad_cols(w, n):
    return jnp.pad(w, ((0, 0), (0, n - w.shape[1])))


def _even_layer(h, n_rows, need_ctx, mod, mod_rows, layer, gain, p):
    w_in, q_a_g, kv_a_g, w_qb, w_kvb, q_g, k_g, sc_w, w_out = p
    d = h.shape[1]
    sc_width = d - MLA_HEADS * MLA_VDIM
    assert sc_width == MLA_HEADS * MLA_VDIM
    head_w = MLA_NOPE + MLA_ROPE
    c0 = MLA_Q_RANK + MLA_KV_RANK
    conv0 = _round_up(c0 + LANE, _pick(sc_width, (1024, 512, 256, 128)))
    n_in = _round_up(conv0 + 3 * sc_width, 1024)
    w_in_p = jnp.concatenate([_pad_cols(w_in[:, :c0 + MLA_ROPE], conv0), w_in[:, c0 + MLA_ROPE:]], axis=1)
    w_in_p = _pad_cols(w_in_p, n_in).astype(BF16)
    w_q = w_qb.reshape(MLA_Q_RANK, MLA_HEADS, head_w)
    w_q = jnp.pad(w_q, ((0, 0), (0, 0), (0, QK_PAD - head_w))).reshape(MLA_Q_RANK, MLA_HEADS * QK_PAD).astype(BF16)
    rope = _rope_table(_proj_rows())

    u = _modulate(h, n_rows, gain, mod_rows, layer, 0, 1)
    y = _mm([u], w_in_p, n_rows, name="even_in")
    q = _qproj(y, n_rows, q_a_g.reshape(1, -1), w_q, jnp.pad(q_g, (0, QK_PAD - head_w)).reshape(1, QK_PAD), rope)
    k, v = _kvproj(y, n_rows, kv_a_g.reshape(1, -1), w_kvb.astype(BF16), k_g[:MLA_NOPE].reshape(1, LANE),
                   jnp.pad(k_g[MLA_NOPE:], (0, LANE - MLA_ROPE)).reshape(1, LANE), rope)
    o = _attention(q, k, v, True)
    if need_ctx:
        o = jnp.concatenate([o, _attention(q, k, v, False)], axis=0)
    n_out = n_rows if need_ctx else _n_lat()
    conv = _sconv(y, n_out, sc_w, conv0, sc_width)
    return _mm([o, conv], w_out.astype(BF16), n_out, res=h, gate=(mod, layer, 2),
               tn=_pick(d, (512, 256, 128)), name="even_out")


def _odd_layer(h, n_rows, need_ctx, mod, mod_rows, layer, gain, p, idx):
    w_in_stack, conv_w, a_log, dt_bias, o_g, w_out = p
    hv, hq = GDN_V_HEADS, GDN_QK_HEADS
    qkv_w = 2 * hq * GDN_DK + hv * GDN_DV
    main_w = qkv_w + hv * GDN_DV
    gate_w = _round_up(4 * hv, LANE)
    half = gate_w // 2
    w_in = w_in_stack[idx]
    w_ba = jnp.concatenate([_pad_cols(w_in[:, main_w:main_w + 2 * hv], half),
                            _pad_cols(w_in[:, main_w + 2 * hv:], half)], axis=1).astype(BF16)
    zeros = jnp.zeros((1, half), F32)
    alog = jnp.concatenate([zeros, _pad_cols(a_log.reshape(1, -1), half)], axis=1)
    dtb = jnp.concatenate([zeros, _pad_cols(dt_bias.reshape(1, -1), half)], axis=1)

    u = _modulate(h, n_rows, gain, mod_rows, layer, 0, 1)
    y = _mm_wcast(u, w_in_stack, idx, main_w, n_rows, "odd_in")
    ba = _mm([u], w_ba, n_rows, name="odd_in_gates")
    bg = _gdn_gate(ba, n_rows, alog, dtb)
    rep = hv // hq
    bg_t = jnp.stack([bg[:, :2 * hv], bg[:, half:half + 2 * hv]], axis=0)
    bg_t = bg_t.reshape(2, n_rows, 2, hq, rep).transpose(3, 0, 2, 4, 1).reshape(hq, 4 * rep, n_rows)
    bg_t = jnp.pad(bg_t, ((0, 0), (0, _round_up(4 * rep, SUBLANE) - 4 * rep), (0, 0)))
    o_f, o_b = _gdn_scan(_gdn_prep(y, conv_w, bg_t, n_rows), n_rows)
    n_out = n_rows if need_ctx else _n_lat()
    yo = _gdn_readout(o_f, o_b, y, qkv_w, o_g, n_out)
    return _mm([yo], w_out.astype(BF16), n_out, res=h, gate=(mod, layer, 2),
               tn=_pick(h.shape[1], (512, 256, 128)), name="odd_out")


def kernel(x, c, ctx, c_ctx, ada_w, ada_b, norm_mix, norm_ffn, ffn_w_gate, ffn_w_up, ffn_conv_w, ffn_conv_b,
           ffn_w_down, a_w_in, a_q_a_norm, a_kv_a_norm, a_w_qb, a_w_kvb, a_q_norm, a_k_norm, a_sc_conv, a_w_out,
           c_w_in, c_conv_w, c_a_log, c_dt_bias, c_o_norm, c_w_out):
    bn, t, d = x.shape
    depth = ada_w.shape[0]
    assert (bn, t, ctx.shape[1], ffn_w_gate.shape[2]) == (BATCH, SEQ, CTX_LEN, D_FF)
    assert CTX_LEN % GDN_BLK == 0 and SEQ % GDN_BLK == 0
    h = jnp.concatenate([x.reshape(bn * t, d), ctx.reshape(bn * CTX_LEN, d)], axis=0)
    cond8 = jnp.concatenate([c, c_ctx[None], jnp.zeros((SUBLANE - bn - 1, d), F32)], axis=0)
    mod = _ada(cond8, ada_w, ada_b)
    mod_rows = mod.reshape(depth * SUBLANE * 6, 1, d)
    d_ff_p = _round_up(D_FF, 512)
    ff_pad = ((0, 0), (0, 0), (0, d_ff_p - D_FF))
    w_gate = _cast_pad(ffn_w_gate, d, d_ff_p)
    w_up = _cast_pad(ffn_w_up, d, d_ff_p)
    w_down = _cast_pad(ffn_w_down, d_ff_p, d)
    conv_w = jnp.pad(ffn_conv_w, ff_pad)
    conv_b = jnp.pad(ffn_conv_b.reshape(depth, 1, D_FF), ff_pad)
    for l in range(depth):
        last = l == depth - 1
        n_rows = h.shape[0]
        i = l // 2
        if l % 2 == 0:
            h = _even_layer(h, n_rows, not last, mod, mod_rows, l, norm_mix[l],
                            (a_w_in[i], a_q_a_norm[i], a_kv_a_norm[i], a_w_qb[i], a_w_kvb[i], a_q_norm[i],
                             a_k_norm[i], a_sc_conv[i], a_w_out[i]))
        else:
            h = _odd_layer(h, n_rows, not last, mod, mod_rows, l, norm_mix[l],
                           (c_w_in, c_conv_w[i], c_a_log[i], c_dt_bias[i], c_o_norm[i], c_w_out[i]), i)
        h = _conv_ffn(h, h.shape[0], norm_ffn[l], mod_rows, mod, l,
                      w_gate, w_up, conv_w, conv_b, w_down)
    return h[:bn * t].reshape(bn, t, d)
```

```python
import functools
import math
from typing import NamedTuple

import jax
import jax.numpy as jnp
import numpy as np
from jax import lax
from jax.experimental import pallas as pl
from jax.experimental.pallas import tpu as pltpu

F32 = jnp.float32
BF16 = jnp.bfloat16

BATCH = 2
SEQ = 4096
GRID_W = 64
CTX_LEN = 256
EPS = 1e-6
D_FF = 11008

MLA_HEADS = 16
MLA_NOPE = 128
MLA_ROPE = 64
MLA_VDIM = 128
MLA_Q_RANK = 1024
MLA_KV_RANK = 512
ROPE_THETA = 10000.0

GDN_QK_HEADS = 16
GDN_V_HEADS = 32
GDN_DK = 128
GDN_DV = 128

LANE = 128
SUBLANE = 8
QK_PAD = 256
V_PAD = 256
ROW_BLK = 256
GDN_BLK = 256
VMEM_LIMIT = 56 * 2**20


def _cp(*sem, vmem=VMEM_LIMIT):
    return pltpu.CompilerParams(dimension_semantics=sem, vmem_limit_bytes=vmem)


def _pick(n, prefs):
    for p in prefs:
        if n % p == 0:
            return p
    raise ValueError(f"no tile for {n} in {prefs}")


def _round_up(n, m):
    return (n + m - 1) // m * m


def _sigmoid(x):
    return 1.0 / (1.0 + jnp.exp(-x))


def _silu(x):
    return x * _sigmoid(x)


def _dot(a, b):
    return jnp.dot(a, b, preferred_element_type=F32)


def _dot_nt(a, b):
    return lax.dot_general(a, b, (((1,), (1,)), ((), ())), preferred_element_type=F32)


def _dot_tn(a, b):
    return lax.dot_general(a, b, (((0,), (0,)), ((), ())), preferred_element_type=F32)


def _n_lat():
    return BATCH * SEQ


def _seq_edges(row0, tm):
    assert SEQ & (SEQ - 1) == 0 and CTX_LEN & (CTX_LEN - 1) == 0 and SEQ % CTX_LEN == 0
    r = row0 + lax.broadcasted_iota(jnp.int32, (tm, 1), 0)
    is_ctx = r >= _n_lat()
    first = (jnp.bitwise_and(r, CTX_LEN - 1) == 0) & ((jnp.bitwise_and(r, SEQ - 1) == 0) | is_ctx)
    r1 = r + 1
    last = (jnp.bitwise_and(r1, CTX_LEN - 1) == 0) & ((jnp.bitwise_and(r1, SEQ - 1) == 0) | is_ctx)
    return first, last


def _shift_rows(x, prev_row, next_row, row0):
    tm = x.shape[0]
    ridx = lax.broadcasted_iota(jnp.int32, (tm, 1), 0)
    first, last = _seq_edges(row0, tm)
    dn = jnp.where(ridx == 0, prev_row, pltpu.roll(x, 1, 0))
    dn = jnp.where(first, 0.0, dn)
    up = jnp.where(ridx == tm - 1, next_row, pltpu.roll(x, tm - 1, 0))
    up = jnp.where(last, 0.0, up)
    return dn, up


def _row_select(row0, tm, table):
    r = row0 + lax.broadcasted_iota(jnp.int32, (tm, 1), 0)
    out = table[BATCH:BATCH + 1]
    for b in reversed(range(BATCH)):
        out = jnp.where(r < (b + 1) * SEQ, table[b:b + 1], out)
    return out


def _halo_specs(tm, tn, n_rows, col_blk0):
    tmb, last = tm // SUBLANE, n_rows // SUBLANE - 1
    prev = pl.BlockSpec((SUBLANE, tn), lambda i, j: (jnp.maximum(i * tmb - 1, 0), col_blk0 + j))
    nxt = pl.BlockSpec((SUBLANE, tn), lambda i, j: (jnp.minimum((i + 1) * tmb, last), col_blk0 + j))
    return prev, nxt


def _ada_kernel(cond_ref, w_ref, b_ref, o_ref):
    a = _silu(cond_ref[...]).astype(BF16)
    o_ref[...] = _dot(a, w_ref[...].astype(BF16)) + b_ref[...]


def _ada(cond8, ada_w, ada_b):
    n_layer, d, n = ada_w.shape
    tn = _pick(n, (512, 256, 128))
    return pl.pallas_call(
        _ada_kernel,
        grid=(n_layer, n // tn),
        in_specs=[pl.BlockSpec((SUBLANE, d), lambda l, j: (0, 0)),
                  pl.BlockSpec((None, d, tn), lambda l, j: (l, 0, j)),
                  pl.BlockSpec((None, 1, tn), lambda l, j: (l, 0, j))],
        out_specs=pl.BlockSpec((None, SUBLANE, tn), lambda l, j: (l, 0, j)),
        out_shape=jax.ShapeDtypeStruct((n_layer, SUBLANE, n), F32),
        compiler_params=_cp("parallel", "parallel"),
        name="ada",
    )(cond8, ada_w, ada_b.reshape(n_layer, 1, n))


def _modulate_kernel(h_ref, g_ref, sh_ref, sc_ref, o_ref):
    x = h_ref[...]
    y = x * lax.rsqrt(jnp.mean(x * x, axis=-1, keepdims=True) + EPS)
    o_ref[...] = ((y * g_ref[...]) * (1.0 + sc_ref[...]) + sh_ref[...]).astype(o_ref.dtype)


def _modulate(h, n_rows, gain, mod_rows, layer, k_shift, k_scale):
    d = h.shape[1]
    blk_per_seq = SEQ // ROW_BLK

    def mod_spec(k):
        return pl.BlockSpec(
            (None, 1, d),
            lambda i: ((layer * SUBLANE + jnp.minimum(i // blk_per_seq, BATCH)) * 6 + k, 0, 0))

    return pl.pallas_call(
        _modulate_kernel,
        grid=(n_rows // ROW_BLK,),
        in_specs=[pl.BlockSpec((ROW_BLK, d), lambda i: (i, 0)),
                  pl.BlockSpec((1, d), lambda i: (0, 0)),
                  mod_spec(k_shift), mod_spec(k_scale)],
        out_specs=pl.BlockSpec((ROW_BLK, d), lambda i: (i, 0)),
        out_shape=jax.ShapeDtypeStruct((n_rows, d), BF16),
        compiler_params=_cp("parallel"),
        name="modulate",
    )(h, gain.reshape(1, d), mod_rows, mod_rows)


class _CastJob(NamedTuple):
    src: jax.Array
    layer: int
    rows_out: int
    cols_out: int
    start: int
    steps: int

    @property
    def tr(self):
        for tr in (16, 32, 64, 128, 256, 512, 1024):
            if self.rows_out % tr == 0 and self.src.shape[1] % tr == 0 and self.rows_out // tr <= self.steps:
                return tr
        raise ValueError("host kernel has too few steps for this cast")


def _hosted_call(kernel_fn, jobs, *, grid, in_specs, out_specs, out_shape, args, scratch_shapes=(), sem, name):
    out_specs, out_shape = list(out_specs), list(out_shape)
    n_in, n_out, n_jobs = len(in_specs), len(out_specs), len(jobs)
    strides = [math.prod(grid[a + 1:]) for a in range(len(grid))]

    def step_of(ids):
        return sum(i * s for i, s in zip(ids, strides))

    cast_in, cast_out, cast_shape = [], [], []
    for job in jobs:
        tr, n_blk, n_src_blk = job.tr, job.rows_out // job.tr, job.src.shape[1] // job.tr

        def blk(*ids, job=job, n_blk=n_blk):
            return jnp.clip(step_of(ids) - job.start, 0, n_blk - 1)

        cast_in.append(pl.BlockSpec((None, tr, job.src.shape[2]),
                                    lambda *ids, job=job, blk=blk, last=n_src_blk - 1:
                                    (job.layer, jnp.minimum(blk(*ids), last), 0)))
        cast_out.append(pl.BlockSpec((tr, job.cols_out), lambda *ids, blk=blk: (blk(*ids), 0)))
        cast_shape.append(jax.ShapeDtypeStruct((job.rows_out, job.cols_out), BF16))

    def kernel(*refs):
        host_in, src_refs = refs[:n_in], refs[n_in:n_in + n_jobs]
        host_out = refs[n_in + n_jobs:n_in + n_jobs + n_out]
        dst_refs = refs[n_in + n_jobs + n_out:n_in + 2 * n_jobs + n_out]
        kernel_fn(*host_in, *host_out, *refs[n_in + 2 * n_jobs + n_out:])
        step = step_of([pl.program_id(a) for a in range(len(grid))])
        for job, src_ref, dst_ref in zip(jobs, src_refs, dst_refs):
            rel, n_src = step - job.start, src_ref.shape[1]
            n_src_blk = job.src.shape[1] // job.tr

            @pl.when((rel >= 0) & (rel < n_src_blk))
            def _(src_ref=src_ref, dst_ref=dst_ref, n_src=n_src):
                dst_ref[:, :n_src] = src_ref[...].astype(dst_ref.dtype)
                if dst_ref.shape[1] > n_src:
                    dst_ref[:, n_src:] = jnp.zeros((dst_ref.shape[0], dst_ref.shape[1] - n_src), dst_ref.dtype)

            @pl.when((rel >= n_src_blk) & (rel < job.rows_out // job.tr))
            def _(dst_ref=dst_ref):
                dst_ref[...] = jnp.zeros_like(dst_ref)

    outs = pl.pallas_call(
        kernel,
        grid=grid,
        in_specs=list(in_specs) + cast_in,
        out_specs=out_specs + cast_out,
        out_shape=out_shape + cast_shape,
        scratch_shapes=list(scratch_shapes),
        compiler_params=_cp(*(["arbitrary"] * len(grid) if jobs else sem)),
        name=name,
    )(*args, *[job.src for job in jobs])
    return list(outs[:n_out]), list(outs[n_out:])


def _mm_kernel(*refs, n_pairs, gated, tm):
    a_refs, w_refs = refs[:n_pairs], refs[n_pairs:2 * n_pairs]
    acc = _dot(a_refs[0][...], w_refs[0][...])
    for a_ref, w_ref in zip(a_refs[1:], w_refs[1:]):
        acc += _dot(a_ref[...], w_ref[...])
    if gated:
        res_ref, gate_ref, o_ref = refs[2 * n_pairs:]
        gate = _row_select(pl.program_id(0) * tm, tm, gate_ref[...])
        acc = res_ref[...] + gate * acc
    else:
        o_ref = refs[2 * n_pairs]
    o_ref[...] = acc.astype(o_ref.dtype)


def _mm_steps(n_rows, n, tn=None):
    return (n_rows // _pick(n_rows, (1088, 1024, 512, 256))) * (n // (tn or _pick(n, (1024, 512, 256, 128))))


def _mm(a_list, w, n_rows, *, res=None, gate=None, tn=None, jobs=(), name="mm"):
    n_pairs = len(a_list)
    n = w.shape[1]
    tm = _pick(n_rows, (1088, 1024, 512, 256))
    tn = tn or _pick(n, (1024, 512, 256, 128))
    in_specs = [pl.BlockSpec((tm, a.shape[1]), lambda i, j: (i, 0)) for a in a_list]
    in_specs += [pl.BlockSpec((a.shape[1], tn), lambda i, j, p=p: (p, j)) for p, a in enumerate(a_list)]
    assert all(a.shape[1] == a_list[0].shape[1] for a in a_list) and w.shape[0] == n_pairs * a_list[0].shape[1]
    args = list(a_list) + [w] * n_pairs
    if res is not None:
        table, layer, chunk = gate
        d = table.shape[2] // 6
        in_specs += [pl.BlockSpec((tm, tn), lambda i, j: (i, j)),
                     pl.BlockSpec((None, SUBLANE, tn), lambda i, j: (layer, 0, chunk * (d // tn) + j))]
        args += [res, table]
    (out,), casts = _hosted_call(
        functools.partial(_mm_kernel, n_pairs=n_pairs, gated=res is not None, tm=tm), jobs,
        grid=(n_rows // tm, n // tn),
        in_specs=in_specs,
        out_specs=[pl.BlockSpec((tm, tn), lambda i, j: (i, j))],
        out_shape=[jax.ShapeDtypeStruct((n_rows, n), F32)],
        args=args, sem=("parallel", "parallel"), name=name)
    return (out, casts) if jobs else out


def _mm_wcast_kernel(a_ref, w_ref, o_ref, wb_ref):
    @pl.when(pl.program_id(1) == 0)
    def _():
        wb_ref[...] = w_ref[...].astype(BF16)

    o_ref[...] = _dot(a_ref[...], wb_ref[...])


def _mm_wcast_steps(n_cols, n_rows):
    return (n_cols // _pick(n_cols, (512, 256, 128))) * (n_rows // _pick(n_rows, (1088, 1024, 512, 256)))


def _mm_wcast(a, w_stack, layer, n_cols, n_rows, jobs, name):
    kdim = a.shape[1]
    tm = _pick(n_rows, (1088, 1024, 512, 256))
    tn = _pick(n_cols, (512, 256, 128))
    (out,), casts = _hosted_call(
        _mm_wcast_kernel, jobs,
        grid=(n_cols // tn, n_rows // tm),
        in_specs=[pl.BlockSpec((tm, kdim), lambda j, i: (i, 0)),
                  pl.BlockSpec((None, kdim, tn), lambda j, i: (layer, 0, j))],
        out_specs=[pl.BlockSpec((tm, tn), lambda j, i: (i, j))],
        out_shape=[jax.ShapeDtypeStruct((n_rows, n_cols), F32)],
        scratch_shapes=[pltpu.VMEM((kdim, tn), BF16)],
        args=(a, w_stack), sem=("parallel", "arbitrary"), name=name)
    return out, casts


def _mmk_kernel(a_ref, w_ref, res_ref, gate_ref, o_ref, acc_ref, *, tm):
    k = pl.program_id(2)

    @pl.when(k == 0)
    def _():
        acc_ref[...] = jnp.zeros_like(acc_ref)

    acc_ref[...] += _dot(a_ref[...], w_ref[...])

    @pl.when(k == pl.num_programs(2) - 1)
    def _():
        gate = _row_select(pl.program_id(0) * tm, tm, gate_ref[...])
        o_ref[...] = res_ref[...] + gate * acc_ref[...]


def _mm_ktiled(a, w, n_rows, res, gate):
    kdim, n = w.shape
    table, gate_layer, chunk = gate
    d = table.shape[2] // 6
    tm = _pick(n_rows, (1088, 1024, 512, 256))
    tn = _pick(n, (1024, 512, 256, 128))
    tk = _pick(kdim, (2816, 2048, 1024, 512, 256))
    return pl.pallas_call(
        functools.partial(_mmk_kernel, tm=tm),
        grid=(n_rows // tm, n // tn, kdim // tk),
        in_specs=[pl.BlockSpec((tm, tk), lambda i, j, k: (i, k)),
                  pl.BlockSpec((tk, tn), lambda i, j, k: (k, j)),
                  pl.BlockSpec((tm, tn), lambda i, j, k: (i, j)),
                  pl.BlockSpec((None, SUBLANE, tn), lambda i, j, k: (gate_layer, 0, chunk * (d // tn) + j))],
        out_specs=pl.BlockSpec((tm, tn), lambda i, j, k: (i, j)),
        out_shape=jax.ShapeDtypeStruct((n_rows, n), F32),
        scratch_shapes=[pltpu.VMEM((tm, tn), F32)],
        compiler_params=_cp("parallel", "parallel", "arbitrary"),
        name="ffn_down",
    )(a, w, res, table)


def _ffn_up_kernel(u_ref, uh_ref, wg_ref, wu_ref, cw_ref, cb_ref, o_ref, halo_ref, *, tm):
    i = pl.program_id(1)

    @pl.when(i == 0)
    def _():
        halo_ref[...] = _dot(uh_ref[...], wg_ref[...])

    u = u_ref[...]
    g = _dot(u, wg_ref[...])
    dn, up = _shift_rows(g, halo_ref[pl.ds(2 * i, 1), :], halo_ref[pl.ds(2 * i + 1, 1), :], i * tm)
    a = dn * cw_ref[0:1, :] + g * cw_ref[1:2, :] + up * cw_ref[2:3, :] + cb_ref[...]
    o_ref[...] = (_silu(a) * _dot(u, wu_ref[...])).astype(o_ref.dtype)


def _ffn_up_steps(n_rows, n):
    return (n // _pick(n, (512, 256, 128))) * (n_rows // _pick(n_rows, (1088, 1024, 512, 256)))


def _ffn_up(u, n_rows, layer, wg, wu, cw_stack, cb_stack, jobs):
    d, n = wg.shape
    tm = _pick(n_rows, (1088, 1024, 512, 256))
    tn = _pick(n, (512, 256, 128))
    gm = n_rows // tm
    n_halo = _round_up(2 * gm, 2 * SUBLANE)
    rows = []
    for i in range(gm):
        rows += [max(i * tm - 1, 0), min((i + 1) * tm, n_rows - 1)]
    rows += [0] * (n_halo - len(rows))
    u_halo = jnp.concatenate([u[r:r + 1] for r in rows], axis=0)
    col = lambda j, i: (layer, 0, j)
    (hid,), casts = _hosted_call(
        functools.partial(_ffn_up_kernel, tm=tm), jobs,
        grid=(n // tn, gm),
        in_specs=[pl.BlockSpec((tm, d), lambda j, i: (i, 0)),
                  pl.BlockSpec((n_halo, d), lambda j, i: (0, 0)),
                  pl.BlockSpec((d, tn), lambda j, i: (0, j)),
                  pl.BlockSpec((d, tn), lambda j, i: (0, j)),
                  pl.BlockSpec((None, 3, tn), col),
                  pl.BlockSpec((None, 1, tn), col)],
        out_specs=[pl.BlockSpec((tm, tn), lambda j, i: (i, j))],
        out_shape=[jax.ShapeDtypeStruct((n_rows, n), BF16)],
        scratch_shapes=[pltpu.VMEM((n_halo, tn), F32)],
        args=(u, u_halo, wg, wu, cw_stack, cb_stack), sem=("parallel", "arbitrary"), name="ffn_up")
    return hid, casts


def _conv_ffn(h, n_rows, gain, mod_rows, mod, layer, wg, wu, cw, cb, wd_stack):
    d_ff_p = wg.shape[1]
    u = _modulate(h, n_rows, gain, mod_rows, layer, 3, 4)
    job = _CastJob(wd_stack, layer, d_ff_p, wd_stack.shape[2], 0, _ffn_up_steps(n_rows, d_ff_p))
    hid, (wd,) = _ffn_up(u, n_rows, layer, wg, wu, cw, cb, [job])
    return _mm_ktiled(hid, wd, n_rows, h, (mod, layer, 5))


def _rope(x, rope_ref):
    c, sa, sb = rope_ref[:, 0:LANE], rope_ref[:, LANE:2 * LANE], rope_ref[:, 2 * LANE:3 * LANE]
    quarter = MLA_ROPE // 4
    return x * c + pltpu.roll(x, LANE - quarter, 1) * sa + pltpu.roll(x, quarter, 1) * sb


def _rms(x, width):
    return lax.rsqrt(jnp.sum(x * x, axis=-1, keepdims=True) * (1.0 / width) + EPS)


def _qproj_kernel(cq_ref, ag_ref, w_ref, hg_ref, rope_ref, o_ref, *, heads):
    x = cq_ref[...]
    xn = ((x * _rms(x, x.shape[1])) * ag_ref[...]).astype(BF16)
    y = _dot(xn, w_ref[...])
    for hh in range(heads):
        yh = y[:, hh * QK_PAD:(hh + 1) * QK_PAD]
        yn = (yh * _rms(yh, MLA_NOPE + MLA_ROPE)) * hg_ref[...]
        o_ref[:, hh * QK_PAD:hh * QK_PAD + LANE] = yn[:, :LANE].astype(o_ref.dtype)
        o_ref[:, hh * QK_PAD + LANE:(hh + 1) * QK_PAD] = _rope(yn[:, LANE:], rope_ref).astype(o_ref.dtype)


def _qproj(y_in, n_rows, ag, w, hg, rope):
    heads = _pick(MLA_HEADS, (4, 2, 1))
    tm = _proj_rows()
    tn = heads * QK_PAD
    return pl.pallas_call(
        functools.partial(_qproj_kernel, heads=heads),
        grid=(n_rows // tm, w.shape[1] // tn),
        in_specs=[pl.BlockSpec((tm, MLA_Q_RANK), lambda i, j: (i, 0)),
                  pl.BlockSpec((1, MLA_Q_RANK), lambda i, j: (0, 0)),
                  pl.BlockSpec((MLA_Q_RANK, tn), lambda i, j: (0, j)),
                  pl.BlockSpec((1, QK_PAD), lambda i, j: (0, 0)),
                  _rope_spec()],
        out_specs=pl.BlockSpec((tm, tn), lambda i, j: (i, j)),
        out_shape=jax.ShapeDtypeStruct((n_rows, w.shape[1]), BF16),
        compiler_params=_cp("parallel", "parallel"),
        name="q_proj",
    )(y_in, ag, w, hg, rope)


def _kvproj_kernel(ckv_ref, kr_ref, ag_ref, w_ref, gn_ref, gr_ref, rope_ref, k_ref, v_ref, *, heads):
    x = ckv_ref[...]
    xn = ((x * _rms(x, x.shape[1])) * ag_ref[...]).astype(BF16)
    y = _dot(xn, w_ref[...])
    kr = kr_ref[...]
    kr_ss = jnp.sum(kr * kr, axis=-1, keepdims=True)
    width = MLA_NOPE + MLA_VDIM
    ones = jnp.ones((x.shape[0], V_PAD - MLA_VDIM), v_ref.dtype)
    for hh in range(heads):
        kn = y[:, hh * width:hh * width + MLA_NOPE]
        r = lax.rsqrt((jnp.sum(kn * kn, axis=-1, keepdims=True) + kr_ss) * (1.0 / (MLA_NOPE + MLA_ROPE)) + EPS)
        k_ref[:, hh * QK_PAD:hh * QK_PAD + LANE] = ((kn * r) * gn_ref[...]).astype(k_ref.dtype)
        k_ref[:, hh * QK_PAD + LANE:(hh + 1) * QK_PAD] = _rope((kr * r) * gr_ref[...], rope_ref).astype(k_ref.dtype)
        v_ref[:, hh * V_PAD:hh * V_PAD + MLA_VDIM] = y[:, hh * width + MLA_NOPE:(hh + 1) * width].astype(v_ref.dtype)
        v_ref[:, hh * V_PAD + MLA_VDIM:(hh + 1) * V_PAD] = ones


def _kvproj(y_in, n_rows, ag, w, gn, gr, rope):
    heads = _pick(MLA_HEADS, (4, 2, 1))
    assert MLA_NOPE == LANE and MLA_VDIM == LANE and MLA_Q_RANK % MLA_KV_RANK == 0
    tm = _proj_rows()
    tn = heads * (MLA_NOPE + MLA_VDIM)
    kr_blk = (MLA_Q_RANK + MLA_KV_RANK) // LANE
    return pl.pallas_call(
        functools.partial(_kvproj_kernel, heads=heads),
        grid=(n_rows // tm, w.shape[1] // tn),
        in_specs=[pl.BlockSpec((tm, MLA_KV_RANK), lambda i, j: (i, MLA_Q_RANK // MLA_KV_RANK)),
                  pl.BlockSpec((tm, LANE), lambda i, j: (i, kr_blk)),
                  pl.BlockSpec((1, MLA_KV_RANK), lambda i, j: (0, 0)),
                  pl.BlockSpec((MLA_KV_RANK, tn), lambda i, j: (0, j)),
                  pl.BlockSpec((1, LANE), lambda i, j: (0, 0)),
                  pl.BlockSpec((1, LANE), lambda i, j: (0, 0)),
                  _rope_spec()],
        out_specs=[pl.BlockSpec((tm, heads * QK_PAD), lambda i, j: (i, j)),
                   pl.BlockSpec((tm, heads * V_PAD), lambda i, j: (i, j))],
        out_shape=[jax.ShapeDtypeStruct((n_rows, MLA_HEADS * QK_PAD), BF16),
                   jax.ShapeDtypeStruct((n_rows, MLA_HEADS * V_PAD), BF16)],
        compiler_params=_cp("parallel", "parallel"),
        name="kv_proj",
    )(y_in, y_in, ag, w, gn, gr, rope)


def _attn_kernel(*refs, chunks, scale):
    q = refs[0][...]
    n_kv = (len(refs) - 2) // 2
    k_refs, v_refs, o_ref = refs[1:1 + n_kv], refs[1 + n_kv:1 + 2 * n_kv], refs[1 + 2 * n_kv]
    c = scale * math.log2(math.e)
    m = acc = None
    for idx, start, size in chunks:
        s = _dot_nt(q, k_refs[idx][start:start + size, :])
        m_blk = jnp.max(s, axis=-1, keepdims=True)
        m_new = m_blk if m is None else jnp.maximum(m, m_blk)
        pv = _dot(jnp.exp2((s - m_new) * c).astype(BF16), v_refs[idx][start:start + size, :])
        acc = pv if acc is None else jnp.exp2((m - m_new) * c) * acc + pv
        m = m_new
    o_ref[...] = (acc[:, :MLA_VDIM] / acc[:, MLA_VDIM:2 * MLA_VDIM]).astype(o_ref.dtype)


def _attn_lat_steps():
    return BATCH * MLA_HEADS * (SEQ // _pick(SEQ, (1024, 512, 256)))


def _attention(q, k, v, latent, jobs=()):
    assert V_PAD == 2 * MLA_VDIM
    scale = (MLA_NOPE + MLA_ROPE) ** -0.5
    ctx_blk0 = _n_lat() // CTX_LEN
    ctx_k = pl.BlockSpec((CTX_LEN, QK_PAD), lambda b, h, i: (ctx_blk0 + b, h))
    ctx_v = pl.BlockSpec((CTX_LEN, V_PAD), lambda b, h, i: (ctx_blk0 + b, h))
    if latent:
        tq = _pick(SEQ, (1024, 512, 256))
        tk = _pick(SEQ, (1024, 512, 256))
        q_per_b = SEQ // tq
        q_spec = pl.BlockSpec((tq, QK_PAD), lambda b, h, i: (b * q_per_b + i, h))
        o_spec = pl.BlockSpec((tq, MLA_VDIM), lambda b, h, i: (b * q_per_b + i, h))
        k_specs = [ctx_k, pl.BlockSpec((SEQ, QK_PAD), lambda b, h, i: (b, h))]
        v_specs = [ctx_v, pl.BlockSpec((SEQ, V_PAD), lambda b, h, i: (b, h))]
        chunks = [(0, 0, CTX_LEN)] + [(1, s, tk) for s in range(0, SEQ, tk)]
        n_out, grid = _n_lat(), (BATCH, MLA_HEADS, q_per_b)
    else:
        q_spec = pl.BlockSpec((CTX_LEN, QK_PAD), lambda b, h, i: (ctx_blk0 + b, h))
        o_spec = pl.BlockSpec((CTX_LEN, MLA_VDIM), lambda b, h, i: (b, h))
        k_specs, v_specs, chunks = [ctx_k], [ctx_v], [(0, 0, CTX_LEN)]
        n_out, grid = BATCH * CTX_LEN, (BATCH, MLA_HEADS, 1)
    n_kv = len(k_specs)
    (out,), casts = _hosted_call(
        functools.partial(_attn_kernel, chunks=tuple(chunks), scale=scale), jobs,
        grid=grid,
        in_specs=[q_spec] + k_specs + v_specs,
        out_specs=[o_spec],
        out_shape=[jax.ShapeDtypeStruct((n_out, MLA_HEADS * MLA_VDIM), BF16)],
        args=(q, *([k] * n_kv), *([v] * n_kv)), sem=("parallel", "parallel", "arbitrary"),
        name="attn_lat" if latent else "attn_ctx")
    return (out, casts) if jobs else out


def _sconv_kernel(b_ref, c_ref, x_ref, cp_ref, xp_ref, cn_ref, xn_ref, w_ref, o_ref, *, tm):
    p = c_ref[...] * x_ref[...]
    prev = cp_ref[SUBLANE - 1:SUBLANE, :] * xp_ref[SUBLANE - 1:SUBLANE, :]
    nxt = cn_ref[0:1, :] * xn_ref[0:1, :]
    dn, up = _shift_rows(p, prev, nxt, pl.program_id(0) * tm)
    conv = dn * w_ref[0:1, :] + p * w_ref[1:2, :] + up * w_ref[2:3, :]
    o_ref[...] = (b_ref[...] * conv).astype(o_ref.dtype)


def _sconv(y_in, n_rows, w, col0, width):
    tm = ROW_BLK
    tn = _pick(width, (1024, 512, 256, 128))
    blk0 = [(col0 + k * width) // tn for k in range(3)]
    main = [pl.BlockSpec((tm, tn), lambda i, j, o=o: (i, o + j)) for o in blk0]
    cp, cn = _halo_specs(tm, tn, n_rows, blk0[1])
    xp, xn = _halo_specs(tm, tn, n_rows, blk0[2])
    return pl.pallas_call(
        functools.partial(_sconv_kernel, tm=tm),
        grid=(n_rows // tm, width // tn),
        in_specs=main + [cp, xp, cn, xn, pl.BlockSpec((3, tn), lambda i, j: (0, j))],
        out_specs=pl.BlockSpec((tm, tn), lambda i, j: (i, j)),
        out_shape=jax.ShapeDtypeStruct((n_rows, width), BF16),
        compiler_params=_cp("parallel", "parallel"),
        name="short_conv",
    )(*([y_in] * 7), w)


def _conv_silu(x_ref, xp_ref, xn_ref, w_ref, row0):
    x = x_ref[...]
    dn, up = _shift_rows(x, xp_ref[SUBLANE - 1:SUBLANE, :], xn_ref[0:1, :], row0)
    return _silu(dn * w_ref[0:1, :] + x * w_ref[1:2, :] + up * w_ref[2:3, :])


def _l2norm(x):
    return x * lax.rsqrt(jnp.sum(x * x, axis=-1, keepdims=True) + EPS)


def _gdn_gate_kernel(ba_ref, alog_ref, dtb_ref, o_ref):
    x = ba_ref[...]
    lane = lax.broadcasted_iota(jnp.int32, x.shape, 1)
    z = x + dtb_ref[...]
    softplus = jnp.maximum(z, 0.0) + jnp.log(1.0 + jnp.exp(-jnp.abs(z)))
    o_ref[...] = jnp.where(lane < x.shape[1] // 2, _sigmoid(x), -jnp.exp(alog_ref[...]) * softplus)


def _gdn_gate(ba, n_rows, alog, dtb):
    w = ba.shape[1]
    return pl.pallas_call(
        _gdn_gate_kernel,
        grid=(n_rows // ROW_BLK,),
        in_specs=[pl.BlockSpec((ROW_BLK, w), lambda i: (i, 0)),
                  pl.BlockSpec((1, w), lambda i: (0, 0)),
                  pl.BlockSpec((1, w), lambda i: (0, 0))],
        out_specs=pl.BlockSpec((ROW_BLK, w), lambda i: (i, 0)),
        out_shape=jax.ShapeDtypeStruct((n_rows, w), F32),
        compiler_params=_cp("parallel"),
        name="gdn_gate",
    )(ba, alog, dtb)


def _blk_mask(ii, jj, size):
    shift = size.bit_length() - 1
    return jnp.right_shift(ii, shift) == jnp.right_shift(jj, shift)


def _active_rows(x, size, odd):
    n = x.shape[0]
    return jnp.concatenate([x[(2 * b + odd) * size:(2 * b + odd + 1) * size] for b in range(n // (2 * size))], axis=0)


def _weave_rows(rest, active, size, odd):
    pieces = []
    for b in range(active.shape[0] // size):
        if rest is None:
            keep = jnp.zeros((size, active.shape[1]), active.dtype)
        else:
            keep = rest[(2 * b + 1 - odd) * size:(2 * b + 2 - odd) * size]
        act = active[b * size:(b + 1) * size]
        pieces += [keep, act] if odd else [act, keep]
    return jnp.concatenate(pieces, axis=0)


def _unit_tri_inverses(l_mats, dirs, eye, base, ring_ref):
    size = SUBLANE
    xs = [-(l * base) for l in l_mats]
    ts = [eye + x for x in xs]
    for _ in range(size.bit_length() - 2):
        xs = [_dot(x.astype(BF16), x.astype(BF16)) for x in xs]
        ts = [t + _dot(t.astype(BF16), x.astype(BF16)) for t, x in zip(ts, xs)]
    level = 0
    while size < GDN_BLK:
        tbs = [t.astype(BF16) for t in ts]
        offs = [(_active_rows(l, size, 1 - d) * ring_ref[level, d]).astype(BF16) for l, d in zip(l_mats, dirs)]
        mids = [_weave_rows(None, _dot(off, tb), size, 1 - d).astype(BF16) for off, tb, d in zip(offs, tbs, dirs)]
        acts = [_active_rows(t, size, 1 - d) for t, d in zip(ts, dirs)]
        news = [act - _dot(act.astype(BF16), mid) for act, mid in zip(acts, mids)]
        ts = [_weave_rows(t, new, size, 1 - d) for t, new, d in zip(ts, news, dirs)]
        size *= 2
        level += 1
    return ts


def _gdn_masks():
    n, size = GDN_BLK, SUBLANE
    i, j = np.arange(n)[:, None], np.arange(n)[None, :]
    tri = np.stack([j <= i, j >= i, j == i, i // size == j // size]).astype(np.float32)
    rings = []
    while size < n:
        ra = np.arange(n // 2)[:, None]
        per_dir = []
        for d in range(2):
            ia = (ra // size) * 2 * size + (1 - d) * size + ra % size
            per_dir.append((ia // (2 * size) == j // (2 * size)) & (ia // size != j // size))
        rings.append(np.stack(per_dir))
        size *= 2
    return jnp.asarray(tri), jnp.asarray(np.stack(rings).astype(np.float32))


def _gdn_prep_kernel(q_ref, qp_ref, qn_ref, k_ref, kp_ref, kn_ref, v_ref, vp_ref, vn_ref, cwq_ref, cwk_ref, cwv_ref,
                     bg_ref, tri_ref, ring_ref, u_ref, wq_ref, kd_ref, in_ref, cd_ref, *, rep, hpb):
    n = GDN_BLK
    row0 = pl.program_id(0) * n
    q_all = _conv_silu(q_ref, qp_ref, qn_ref, cwq_ref, row0)
    k_all = _conv_silu(k_ref, kp_ref, kn_ref, cwk_ref, row0)
    v_all = _conv_silu(v_ref, vp_ref, vn_ref, cwv_ref, row0)
    eye, base = tri_ref[2], tri_ref[3]
    not_eye = 1.0 - eye
    chains, l_mats, rhss = [], [], []
    for hh in range(hpb):
        q = _l2norm(q_all[:, hh * GDN_DK:(hh + 1) * GDN_DK]) * GDN_DK ** -0.5
        k = _l2norm(k_all[:, hh * GDN_DK:(hh + 1) * GDN_DK])
        q16, k16 = q.astype(BF16), k.astype(BF16)
        qk = _dot_nt(q16, k16)
        kk = _dot_nt(k16, k16) * not_eye
        bg = bg_ref[hh]
        bg_cols = jnp.concatenate([bg, jnp.zeros((LANE - bg.shape[0], n), F32)], axis=0).T
        for e in range(rep):
            for d in range(2):
                vh = hh * rep + e
                cols = slice(vh * GDN_DV, (vh + 1) * GDN_DV)
                incl, incl_t = tri_ref[d], tri_ref[1 - d]
                b_idx, g_idx = d * rep + e, (2 + d) * rep + e
                g_row = bg[g_idx:g_idx + 1, :]
                b_col, g_col = bg_cols[:, b_idx:b_idx + 1], bg_cols[:, g_idx:g_idx + 1]
                gc_col = jnp.sum(incl * g_row, axis=1, keepdims=True)
                gc_row = jnp.sum(incl_t * g_col, axis=0, keepdims=True)
                g_tot = jnp.sum(g_row, axis=1, keepdims=True)
                decay = jnp.exp(jnp.minimum(gc_col - gc_row, 0.0)) * incl
                e_col = jnp.exp(gc_col)
                chains.append((vh, d, cols))
                l_mats.append((kk * b_col) * decay)
                rhss.append(jnp.concatenate([v_all[:, cols] * b_col, (k * b_col) * e_col], axis=1).astype(BF16))
                wq_ref[d, n:, cols] = (q * e_col).astype(wq_ref.dtype)
                kd_ref[d, :, cols] = (k * jnp.exp(g_tot - gc_col)).astype(kd_ref.dtype)
                in_ref[d, vh] = (qk * decay).astype(in_ref.dtype)
                cd_ref[d, vh] = jnp.broadcast_to(jnp.exp(g_tot), (SUBLANE, LANE))
    invs = _unit_tri_inverses(l_mats, [d for _, d, _ in chains], eye, base, ring_ref)
    sols = [_dot(t.astype(BF16), rhs) for t, rhs in zip(invs, rhss)]
    for (vh, d, cols), sol in zip(chains, sols):
        u_ref[d, :, cols] = sol[:, :GDN_DV]
        wq_ref[d, :n, cols] = sol[:, GDN_DV:].astype(wq_ref.dtype)


def _gdn_prep(y_in, conv_w, bg_t, n_rows):
    assert GDN_DK == LANE and GDN_DV == LANE and GDN_BLK == 2 * LANE
    nb, hv, hq, rep = n_rows // GDN_BLK, GDN_V_HEADS, GDN_QK_HEADS, GDN_V_HEADS // GDN_QK_HEADS
    hpb = _pick(hq, (2, 1))
    key_w, wide = hpb * GDN_DK, hpb * rep * GDN_DV
    parts = ((key_w, 0), (key_w, hq // hpb), (wide, 2 * hq * GDN_DK // wide))
    in_specs = []
    for tn, blk0 in parts:
        in_specs += [pl.BlockSpec((GDN_BLK, tn), lambda i, h, blk0=blk0: (i, blk0 + h))]
        in_specs += _halo_specs(GDN_BLK, tn, n_rows, blk0)
    in_specs += [pl.BlockSpec((3, tn), lambda i, h, blk0=blk0: (0, blk0 + h)) for tn, blk0 in parts]
    tri, rings = _gdn_masks()
    in_specs += [pl.BlockSpec((hpb, bg_t.shape[1], GDN_BLK), lambda i, h: (h, 0, i)),
                 pl.BlockSpec(tri.shape, lambda i, h: (0, 0, 0)),
                 pl.BlockSpec(rings.shape, lambda i, h: (0, 0, 0, 0))]
    return pl.pallas_call(
        functools.partial(_gdn_prep_kernel, rep=rep, hpb=hpb),
        grid=(nb, hq // hpb),
        in_specs=in_specs,
        out_specs=[pl.BlockSpec((2, GDN_BLK, wide), lambda i, h: (0, i, h)),
                   pl.BlockSpec((2, None, 2 * GDN_BLK, wide), lambda i, h: (0, i, 0, h)),
                   pl.BlockSpec((2, GDN_BLK, wide), lambda i, h: (0, i, h)),
                   pl.BlockSpec((2, hpb * rep, GDN_BLK, GDN_BLK), lambda i, h: (0, h, i, 0)),
                   pl.BlockSpec((2, hpb * rep, SUBLANE, LANE), lambda i, h: (0, h, i, 0))],
        out_shape=[jax.ShapeDtypeStruct((2, n_rows, hv * GDN_DV), F32),
                   jax.ShapeDtypeStruct((2, nb, 2 * GDN_BLK, hv * GDN_DV), BF16),
                   jax.ShapeDtypeStruct((2, n_rows, hv * GDN_DV), BF16),
                   jax.ShapeDtypeStruct((2, hv, n_rows, GDN_BLK), BF16),
                   jax.ShapeDtypeStruct((2, hv, nb * SUBLANE, LANE), F32)],
        compiler_params=_cp("parallel", "parallel"),
        name="gdn_prep",
    )(*([y_in] * 9), conv_w, conv_w, conv_w, bg_t, tri, rings)


def _gdn_scan_kernel(*refs, heads):
    ins, (of_ref, ob_ref, s_ref) = refs[:10], refs[10:]
    n = GDN_BLK

    @pl.when(pl.program_id(2) == 0)
    def _():
        s_ref[...] = jnp.zeros_like(s_ref)

    o_refs = (of_ref, ob_ref)
    chains = [(d, hh, slice(hh * GDN_DV, (hh + 1) * GDN_DV)) for d in range(2) for hh in range(heads)]
    u_refs, wq_refs, kd_refs, in_refs, cd_refs = (ins[0::5], ins[1::5], ins[2::5], ins[3::5], ins[4::5])
    states = [s_ref[d, hh] for d, hh, _ in chains]
    wqs = [_dot(wq_refs[d][:, cols], s.astype(BF16)) for (d, hh, cols), s in zip(chains, states)]
    vbs = [(u_refs[d][:, cols] - wq[:n]).astype(BF16) for (d, hh, cols), wq in zip(chains, wqs)]
    for (d, hh, cols), s, wq, vb in zip(chains, states, wqs, vbs):
        o_refs[d][:, cols] = wq[n:] + _dot(in_refs[d][hh], vb)
        s_ref[d, hh] = s * cd_refs[d][hh, 0:1, :] + _dot_tn(kd_refs[d][:, cols], vb)


def _gdn_scan(prep, n_rows):
    hv = GDN_V_HEADS
    heads = _pick(hv, (4, 2, 1))
    lat_blk, ctx_blk = SEQ // GDN_BLK, CTX_LEN // GDN_BLK
    steps = ctx_blk + lat_blk
    ctx0 = _n_lat() // GDN_BLK
    wide = heads * GDN_DV

    def blk_f(b, n):
        return jnp.where(n < ctx_blk, ctx0 + b * ctx_blk + n, b * lat_blk + n - ctx_blk)

    def blk_b(b, n):
        return jnp.where(n < ctx_blk, ctx0 + b * ctx_blk + ctx_blk - 1 - n, b * lat_blk + steps - 1 - n)

    in_specs = []
    for d, blk in enumerate((blk_f, blk_b)):
        rows = pl.BlockSpec((None, GDN_BLK, wide), lambda b, h, n, d=d, blk=blk: (d, blk(b, n), h))
        in_specs += [rows,
                     pl.BlockSpec((None, None, 2 * GDN_BLK, wide), lambda b, h, n, d=d, blk=blk: (d, blk(b, n), 0, h)),
                     rows,
                     pl.BlockSpec((None, heads, GDN_BLK, GDN_BLK), lambda b, h, n, d=d, blk=blk: (d, h, blk(b, n), 0)),
                     pl.BlockSpec((None, heads, SUBLANE, LANE), lambda b, h, n, d=d, blk=blk: (d, h, blk(b, n), 0))]
    out = jax.ShapeDtypeStruct((n_rows, hv * GDN_DV), F32)
    return pl.pallas_call(
        functools.partial(_gdn_scan_kernel, heads=heads),
        grid=(BATCH, hv // heads, steps),
        in_specs=in_specs,
        out_specs=[pl.BlockSpec((GDN_BLK, wide), lambda b, h, n: (blk_f(b, n), h)),
                   pl.BlockSpec((GDN_BLK, wide), lambda b, h, n: (blk_b(b, n), h))],
        out_shape=[out, out],
        scratch_shapes=[pltpu.VMEM((2, heads, GDN_DK, GDN_DV), F32)],
        compiler_params=_cp("parallel", "parallel", "arbitrary"),
        name="gdn_scan",
    )(*prep, *prep)


def _gdn_readout_kernel(of_ref, ob_ref, z_ref, g_ref, o_ref):
    o = of_ref[...] + ob_ref[...]
    z = z_ref[...]
    for hh in range(o.shape[1] // GDN_DV):
        cols = slice(hh * GDN_DV, (hh + 1) * GDN_DV)
        oh = o[:, cols]
        y = (oh * _rms(oh, GDN_DV)) * g_ref[...]
        o_ref[:, cols] = (y * _silu(z[:, cols])).astype(o_ref.dtype)


def _gdn_readout(o_f, o_b, y_in, z_col0, gain, n_rows):
    width = GDN_V_HEADS * GDN_DV
    tn = _pick(width, (1024, 512, 256, 128))
    spec = pl.BlockSpec((ROW_BLK, tn), lambda i, j: (i, j))
    return pl.pallas_call(
        _gdn_readout_kernel,
        grid=(n_rows // ROW_BLK, width // tn),
        in_specs=[spec, spec, pl.BlockSpec((ROW_BLK, tn), lambda i, j: (i, z_col0 // tn + j)),
                  pl.BlockSpec((1, GDN_DV), lambda i, j: (0, 0))],
        out_specs=spec,
        out_shape=jax.ShapeDtypeStruct((n_rows, width), BF16),
        compiler_params=_cp("parallel", "parallel"),
        name="gdn_readout",
    )(o_f, o_b, y_in, gain.reshape(1, GDN_DV))


def _proj_rows():
    return _pick(math.gcd(SEQ, BATCH * CTX_LEN), (512, 256))


def _rope_spec():
    tm = _proj_rows()
    return pl.BlockSpec((tm, 3 * LANE), lambda i, j: (jnp.where(i < _n_lat() // tm, i % (SEQ // tm), SEQ // tm), 0))


def _rope_table(n_ident):
    n_freq = MLA_ROPE // 4
    inv_freq = ROPE_THETA ** (-jnp.arange(n_freq, dtype=F32) / n_freq)
    t = jnp.arange(SEQ)
    ang = jnp.concatenate([(t // GRID_W).astype(F32)[:, None] * inv_freq,
                           (t % GRID_W).astype(F32)[:, None] * inv_freq], axis=1)
    cos, sin = jnp.cos(ang), jnp.sin(ang)
    zero = jnp.zeros_like(sin[:, :n_freq])
    pad = jnp.zeros((SEQ, LANE - MLA_ROPE), F32)
    c = jnp.concatenate([cos[:, :n_freq], cos[:, :n_freq], cos[:, n_freq:], cos[:, n_freq:], pad + 1.0], axis=1)
    sa = jnp.concatenate([-sin[:, :n_freq], zero, -sin[:, n_freq:], zero, pad], axis=1)
    sb = jnp.concatenate([zero, sin[:, :n_freq], zero, sin[:, n_freq:], pad], axis=1)
    ident = jnp.concatenate([jnp.ones((n_ident, LANE), F32), jnp.zeros((n_ident, 2 * LANE), F32)], axis=1)
    return jnp.concatenate([jnp.concatenate([c, sa, sb], axis=1), ident], axis=0)


def _pad_cols(w, n):
    return jnp.pad(w, ((0, 0), (0, n - w.shape[1])))


def _even_layer(h, n_rows, need_ctx, mod, mod_rows, layer, gain, p, ffn):
    w_in, q_a_g, kv_a_g, w_qb, w_kvb, q_g, k_g, sc_w, w_out = p
    ffn_w_gate, ffn_w_up, d_ff_p = ffn
    d = h.shape[1]
    sc_width = d - MLA_HEADS * MLA_VDIM
    assert sc_width == MLA_HEADS * MLA_VDIM
    head_w = MLA_NOPE + MLA_ROPE
    c0 = MLA_Q_RANK + MLA_KV_RANK
    conv0 = _round_up(c0 + LANE, _pick(sc_width, (1024, 512, 256, 128)))
    n_in = _round_up(conv0 + 3 * sc_width, 1024)
    w_in_p = jnp.concatenate([_pad_cols(w_in[:, :c0 + MLA_ROPE], conv0), w_in[:, c0 + MLA_ROPE:]], axis=1)
    w_in_p = _pad_cols(w_in_p, n_in).astype(BF16)
    w_q = w_qb.reshape(MLA_Q_RANK, MLA_HEADS, head_w)
    w_q = jnp.pad(w_q, ((0, 0), (0, 0), (0, QK_PAD - head_w))).reshape(MLA_Q_RANK, MLA_HEADS * QK_PAD).astype(BF16)
    rope = _rope_table(_proj_rows())

    u = _modulate(h, n_rows, gain, mod_rows, layer, 0, 1)
    y = _mm([u], w_in_p, n_rows, name="even_in")
    q = _qproj(y, n_rows, q_a_g.reshape(1, -1), w_q, jnp.pad(q_g, (0, QK_PAD - head_w)).reshape(1, QK_PAD), rope)
    k, v = _kvproj(y, n_rows, kv_a_g.reshape(1, -1), w_kvb.astype(BF16), k_g[:MLA_NOPE].reshape(1, LANE),
                   jnp.pad(k_g[MLA_NOPE:], (0, LANE - MLA_ROPE)).reshape(1, LANE), rope)
    steps = _attn_lat_steps()
    jobs = [_CastJob(ffn_w_gate, layer, d, d_ff_p, 0, steps // 2),
            _CastJob(ffn_w_up, layer, d, d_ff_p, steps // 2, steps - steps // 2)]
    o, ffn_w = _attention(q, k, v, True, jobs)
    if need_ctx:
        o = jnp.concatenate([o, _attention(q, k, v, False)], axis=0)
    n_out = n_rows if need_ctx else _n_lat()
    conv = _sconv(y, n_out, sc_w, conv0, sc_width)
    h = _mm([o, conv], w_out.astype(BF16), n_out, res=h, gate=(mod, layer, 2),
            tn=_pick(d, (512, 256, 128)), name="even_out")
    return h, ffn_w


def _odd_layer(h, n_rows, need_ctx, mod, mod_rows, layer, gain, p, idx, ffn):
    w_in_stack, conv_w, a_log, dt_bias, o_g, w_out = p
    ffn_w_gate, ffn_w_up, d_ff_p = ffn
    d = h.shape[1]
    hv, hq = GDN_V_HEADS, GDN_QK_HEADS
    qkv_w = 2 * hq * GDN_DK + hv * GDN_DV
    main_w = qkv_w + hv * GDN_DV
    gate_w = _round_up(4 * hv, LANE)
    half = gate_w // 2
    w_in = w_in_stack[idx]
    w_ba = jnp.concatenate([_pad_cols(w_in[:, main_w:main_w + 2 * hv], half),
                            _pad_cols(w_in[:, main_w + 2 * hv:], half)], axis=1).astype(BF16)
    zeros = jnp.zeros((1, half), F32)
    alog = jnp.concatenate([zeros, _pad_cols(a_log.reshape(1, -1), half)], axis=1)
    dtb = jnp.concatenate([zeros, _pad_cols(dt_bias.reshape(1, -1), half)], axis=1)

    u = _modulate(h, n_rows, gain, mod_rows, layer, 0, 1)
    y, (wg,) = _mm_wcast(u, w_in_stack, idx, main_w, n_rows,
                         [_CastJob(ffn_w_gate, layer, d, d_ff_p, 0, _mm_wcast_steps(main_w, n_rows))], "odd_in")
    ba = _mm([u], w_ba, n_rows, name="odd_in_gates")
    bg = _gdn_gate(ba, n_rows, alog, dtb)
    rep = hv // hq
    bg_t = jnp.stack([bg[:, :2 * hv], bg[:, half:half + 2 * hv]], axis=0)
    bg_t = bg_t.reshape(2, n_rows, 2, hq, rep).transpose(3, 0, 2, 4, 1).reshape(hq, 4 * rep, n_rows)
    bg_t = jnp.pad(bg_t, ((0, 0), (0, _round_up(4 * rep, SUBLANE) - 4 * rep), (0, 0)))
    o_f, o_b = _gdn_scan(_gdn_prep(y, conv_w, bg_t, n_rows), n_rows)
    n_out = n_rows if need_ctx else _n_lat()
    yo = _gdn_readout(o_f, o_b, y, qkv_w, o_g, n_out)
    tn = _pick(d, (512, 256, 128))
    h, (wu,) = _mm([yo], w_out.astype(BF16), n_out, res=h, gate=(mod, layer, 2), tn=tn,
                   jobs=[_CastJob(ffn_w_up, layer, d, d_ff_p, 0, _mm_steps(n_out, d, tn))], name="odd_out")
    return h, (wg, wu)


def kernel(x, c, ctx, c_ctx, ada_w, ada_b, norm_mix, norm_ffn, ffn_w_gate, ffn_w_up, ffn_conv_w, ffn_conv_b,
           ffn_w_down, a_w_in, a_q_a_norm, a_kv_a_norm, a_w_qb, a_w_kvb, a_q_norm, a_k_norm, a_sc_conv, a_w_out,
           c_w_in, c_conv_w, c_a_log, c_dt_bias, c_o_norm, c_w_out):
    bn, t, d = x.shape
    depth = ada_w.shape[0]
    assert (bn, t, ctx.shape[1], ffn_w_gate.shape[2]) == (BATCH, SEQ, CTX_LEN, D_FF)
    assert CTX_LEN % GDN_BLK == 0 and SEQ % GDN_BLK == 0
    h = jnp.concatenate([x.reshape(bn * t, d), ctx.reshape(bn * CTX_LEN, d)], axis=0)
    cond8 = jnp.concatenate([c, c_ctx[None], jnp.zeros((SUBLANE - bn - 1, d), F32)], axis=0)
    mod = _ada(cond8, ada_w, ada_b)
    mod_rows = mod.reshape(depth * SUBLANE * 6, 1, d)
    d_ff_p = _round_up(D_FF, 512)
    ff_pad = ((0, 0), (0, 0), (0, d_ff_p - D_FF))
    conv_w = jnp.pad(ffn_conv_w, ff_pad)
    conv_b = jnp.pad(ffn_conv_b.reshape(depth, 1, D_FF), ff_pad)
    ffn = (ffn_w_gate, ffn_w_up, d_ff_p)
    for l in range(depth):
        last = l == depth - 1
        n_rows = h.shape[0]
        i = l // 2
        if l % 2 == 0:
            h, (w_gate, w_up) = _even_layer(
                h, n_rows, not last, mod, mod_rows, l, norm_mix[l],
                (a_w_in[i], a_q_a_norm[i], a_kv_a_norm[i], a_w_qb[i], a_w_kvb[i], a_q_norm[i], a_k_norm[i],
                 a_sc_conv[i], a_w_out[i]), ffn)
        else:
            h, (w_gate, w_up) = _odd_layer(
                h, n_rows, not last, mod, mod_rows, l, norm_mix[l],
                (c_w_in, c_conv_w[i], c_a_log[i], c_dt_bias[i], c_o_norm[i], c_w_out[i]), i, ffn)
        h = _conv_ffn(h, h.shape[0], norm_ffn[l], mod_rows, mod, l,
                      w_gate, w_up, conv_w, conv_b, ffn_w_down)
    return h[:bn * t].reshape(bn, t, d)
```

```python
import functools
import math
from typing import NamedTuple

import jax
import jax.numpy as jnp
import numpy as np
from jax import lax
from jax.experimental import pallas as pl
from jax.experimental.pallas import tpu as pltpu

F32 = jnp.float32
BF16 = jnp.bfloat16

BATCH = 2
SEQ = 4096
GRID_W = 64
CTX_LEN = 256
EPS = 1e-6
D_FF = 11008

MLA_HEADS = 16
MLA_NOPE = 128
MLA_ROPE = 64
MLA_VDIM = 128
MLA_Q_RANK = 1024
MLA_KV_RANK = 512
ROPE_THETA = 10000.0

GDN_QK_HEADS = 16
GDN_V_HEADS = 32
GDN_DK = 128
GDN_DV = 128

LANE = 128
SUBLANE = 8
QK_PAD = 256
V_PAD = 256
ROW_BLK = 256
GDN_BLK = 256
VMEM_LIMIT = 56 * 2**20


def _cp(*sem, vmem=VMEM_LIMIT):
    return pltpu.CompilerParams(dimension_semantics=sem, vmem_limit_bytes=vmem)


def _pick(n, prefs):
    for p in prefs:
        if n % p == 0:
            return p
    raise ValueError(f"no tile for {n} in {prefs}")


def _round_up(n, m):
    return (n + m - 1) // m * m


def _sigmoid(x):
    return 1.0 / (1.0 + jnp.exp(-x))


def _silu(x):
    return x * _sigmoid(x)


def _dot(a, b):
    return jnp.dot(a, b, preferred_element_type=F32)


def _dot_nt(a, b):
    return lax.dot_general(a, b, (((1,), (1,)), ((), ())), preferred_element_type=F32)


def _dot_tn(a, b):
    return lax.dot_general(a, b, (((0,), (0,)), ((), ())), preferred_element_type=F32)


def _n_lat():
    return BATCH * SEQ


def _seq_edges(row0, tm):
    assert SEQ & (SEQ - 1) == 0 and CTX_LEN & (CTX_LEN - 1) == 0 and SEQ % CTX_LEN == 0
    r = row0 + lax.broadcasted_iota(jnp.int32, (tm, 1), 0)
    is_ctx = r >= _n_lat()
    first = (jnp.bitwise_and(r, CTX_LEN - 1) == 0) & ((jnp.bitwise_and(r, SEQ - 1) == 0) | is_ctx)
    r1 = r + 1
    last = (jnp.bitwise_and(r1, CTX_LEN - 1) == 0) & ((jnp.bitwise_and(r1, SEQ - 1) == 0) | is_ctx)
    return first, last


def _shift_rows(x, prev_row, next_row, row0):
    tm = x.shape[0]
    ridx = lax.broadcasted_iota(jnp.int32, (tm, 1), 0)
    first, last = _seq_edges(row0, tm)
    dn = jnp.where(ridx == 0, prev_row, pltpu.roll(x, 1, 0))
    dn = jnp.where(first, 0.0, dn)
    up = jnp.where(ridx == tm - 1, next_row, pltpu.roll(x, tm - 1, 0))
    up = jnp.where(last, 0.0, up)
    return dn, up


def _row_select(row0, tm, table):
    r = row0 + lax.broadcasted_iota(jnp.int32, (tm, 1), 0)
    out = table[BATCH:BATCH + 1]
    for b in reversed(range(BATCH)):
        out = jnp.where(r < (b + 1) * SEQ, table[b:b + 1], out)
    return out


def _halo_specs(tm, tn, n_rows, col_blk0):
    tmb, last = tm // SUBLANE, n_rows // SUBLANE - 1
    prev = pl.BlockSpec((SUBLANE, tn), lambda i, j: (jnp.maximum(i * tmb - 1, 0), col_blk0 + j))
    nxt = pl.BlockSpec((SUBLANE, tn), lambda i, j: (jnp.minimum((i + 1) * tmb, last), col_blk0 + j))
    return prev, nxt


def _ada_kernel(cond_ref, w_ref, b_ref, o_ref):
    a = _silu(cond_ref[...]).astype(BF16)
    o_ref[...] = _dot(a, w_ref[...].astype(BF16)) + b_ref[...]


def _ada(cond8, ada_w, ada_b):
    n_layer, d, n = ada_w.shape
    tn = _pick(n, (512, 256, 128))
    return pl.pallas_call(
        _ada_kernel,
        grid=(n_layer, n // tn),
        in_specs=[pl.BlockSpec((SUBLANE, d), lambda l, j: (0, 0)),
                  pl.BlockSpec((None, d, tn), lambda l, j: (l, 0, j)),
                  pl.BlockSpec((None, 1, tn), lambda l, j: (l, 0, j))],
        out_specs=pl.BlockSpec((None, SUBLANE, tn), lambda l, j: (l, 0, j)),
        out_shape=jax.ShapeDtypeStruct((n_layer, SUBLANE, n), F32),
        compiler_params=_cp("parallel", "parallel"),
        name="ada",
    )(cond8, ada_w, ada_b.reshape(n_layer, 1, n))


def _modulate_kernel(h_ref, g_ref, sh_ref, sc_ref, o_ref):
    x = h_ref[...]
    y = x * lax.rsqrt(jnp.mean(x * x, axis=-1, keepdims=True) + EPS)
    o_ref[...] = ((y * g_ref[...]) * (1.0 + sc_ref[...]) + sh_ref[...]).astype(o_ref.dtype)


def _modulate(h, n_rows, gain, mod_rows, layer, k_shift, k_scale):
    d = h.shape[1]
    blk_per_seq = SEQ // ROW_BLK

    def mod_spec(k):
        return pl.BlockSpec(
            (None, 1, d),
            lambda i: ((layer * SUBLANE + jnp.minimum(i // blk_per_seq, BATCH)) * 6 + k, 0, 0))

    return pl.pallas_call(
        _modulate_kernel,
        grid=(n_rows // ROW_BLK,),
        in_specs=[pl.BlockSpec((ROW_BLK, d), lambda i: (i, 0)),
                  pl.BlockSpec((1, d), lambda i: (0, 0)),
                  mod_spec(k_shift), mod_spec(k_scale)],
        out_specs=pl.BlockSpec((ROW_BLK, d), lambda i: (i, 0)),
        out_shape=jax.ShapeDtypeStruct((n_rows, d), BF16),
        compiler_params=_cp("parallel"),
        name="modulate",
    )(h, gain.reshape(1, d), mod_rows, mod_rows)


class _CastJob(NamedTuple):
    src: jax.Array
    layer: int
    rows_out: int
    cols_out: int
    start: int
    steps: int

    @property
    def tr(self):
        for tr in (16, 32, 64, 128, 256, 512, 1024):
            if self.rows_out % tr == 0 and self.src.shape[1] % tr == 0 and self.rows_out // tr <= self.steps:
                return tr
        raise ValueError("host kernel has too few steps for this cast")


def _hosted_call(kernel_fn, jobs, *, grid, in_specs, out_specs, out_shape, args, scratch_shapes=(), sem, name):
    out_specs, out_shape = list(out_specs), list(out_shape)
    n_in, n_out, n_jobs = len(in_specs), len(out_specs), len(jobs)
    strides = [math.prod(grid[a + 1:]) for a in range(len(grid))]

    def step_of(ids):
        return sum(i * s for i, s in zip(ids, strides))

    cast_in, cast_out, cast_shape = [], [], []
    for job in jobs:
        tr, n_blk, n_src_blk = job.tr, job.rows_out // job.tr, job.src.shape[1] // job.tr

        def blk(*ids, job=job, n_blk=n_blk):
            return jnp.clip(step_of(ids) - job.start, 0, n_blk - 1)

        cast_in.append(pl.BlockSpec((None, tr, job.src.shape[2]),
                                    lambda *ids, job=job, blk=blk, last=n_src_blk - 1:
                                    (job.layer, jnp.minimum(blk(*ids), last), 0)))
        cast_out.append(pl.BlockSpec((tr, job.cols_out), lambda *ids, blk=blk: (blk(*ids), 0)))
        cast_shape.append(jax.ShapeDtypeStruct((job.rows_out, job.cols_out), BF16))

    def kernel(*refs):
        host_in, src_refs = refs[:n_in], refs[n_in:n_in + n_jobs]
        host_out = refs[n_in + n_jobs:n_in + n_jobs + n_out]
        dst_refs = refs[n_in + n_jobs + n_out:n_in + 2 * n_jobs + n_out]
        kernel_fn(*host_in, *host_out, *refs[n_in + 2 * n_jobs + n_out:])
        step = step_of([pl.program_id(a) for a in range(len(grid))])
        for job, src_ref, dst_ref in zip(jobs, src_refs, dst_refs):
            rel, n_src = step - job.start, src_ref.shape[1]
            n_src_blk = job.src.shape[1] // job.tr

            @pl.when((rel >= 0) & (rel < n_src_blk))
            def _(src_ref=src_ref, dst_ref=dst_ref, n_src=n_src):
                dst_ref[:, :n_src] = src_ref[...].astype(dst_ref.dtype)
                if dst_ref.shape[1] > n_src:
                    dst_ref[:, n_src:] = jnp.zeros((dst_ref.shape[0], dst_ref.shape[1] - n_src), dst_ref.dtype)

            @pl.when((rel >= n_src_blk) & (rel < job.rows_out // job.tr))
            def _(dst_ref=dst_ref):
                dst_ref[...] = jnp.zeros_like(dst_ref)

    outs = pl.pallas_call(
        kernel,
        grid=grid,
        in_specs=list(in_specs) + cast_in,
        out_specs=out_specs + cast_out,
        out_shape=out_shape + cast_shape,
        scratch_shapes=list(scratch_shapes),
        compiler_params=_cp(*(["arbitrary"] * len(grid) if jobs else sem)),
        name=name,
    )(*args, *[job.src for job in jobs])
    return list(outs[:n_out]), list(outs[n_out:])


def _mm_kernel(*refs, n_pairs, gated, tm):
    a_refs, w_refs = refs[:n_pairs], refs[n_pairs:2 * n_pairs]
    acc = _dot(a_refs[0][...], w_refs[0][...])
    for a_ref, w_ref in zip(a_refs[1:], w_refs[1:]):
        acc += _dot(a_ref[...], w_ref[...])
    if gated:
        res_ref, gate_ref, o_ref = refs[2 * n_pairs:]
        gate = _row_select(pl.program_id(0) * tm, tm, gate_ref[...])
        acc = res_ref[...] + gate * acc
    else:
        o_ref = refs[2 * n_pairs]
    o_ref[...] = acc.astype(o_ref.dtype)


def _mm_steps(n_rows, n, tn=None):
    return (n_rows // _pick(n_rows, (1088, 1024, 512, 256))) * (n // (tn or _pick(n, (1024, 512, 256, 128))))


def _mm(a_list, w, n_rows, *, n_cols=None, res=None, gate=None, tn=None, jobs=(), name="mm"):
    n_pairs = len(a_list)
    n = n_cols or w.shape[1]
    tm = _pick(n_rows, (1088, 1024, 512, 256))
    tn = tn or _pick(n, (1024, 512, 256, 128))
    in_specs = [pl.BlockSpec((tm, a.shape[1]), lambda i, j: (i, 0)) for a in a_list]
    in_specs += [pl.BlockSpec((a.shape[1], tn), lambda i, j, p=p: (p, j)) for p, a in enumerate(a_list)]
    assert all(a.shape[1] == a_list[0].shape[1] for a in a_list) and w.shape[0] == n_pairs * a_list[0].shape[1]
    args = list(a_list) + [w] * n_pairs
    if res is not None:
        table, layer, chunk = gate
        d = table.shape[2] // 6
        in_specs += [pl.BlockSpec((tm, tn), lambda i, j: (i, j)),
                     pl.BlockSpec((None, SUBLANE, tn), lambda i, j: (layer, 0, chunk * (d // tn) + j))]
        args += [res, table]
    (out,), casts = _hosted_call(
        functools.partial(_mm_kernel, n_pairs=n_pairs, gated=res is not None, tm=tm), jobs,
        grid=(n_rows // tm, n // tn),
        in_specs=in_specs,
        out_specs=[pl.BlockSpec((tm, tn), lambda i, j: (i, j))],
        out_shape=[jax.ShapeDtypeStruct((n_rows, n), F32)],
        args=args, sem=("parallel", "parallel"), name=name)
    return (out, casts) if jobs else out


def _mmk_kernel(a_ref, w_ref, res_ref, gate_ref, o_ref, acc_ref, *, tm):
    k = pl.program_id(2)

    @pl.when(k == 0)
    def _():
        acc_ref[...] = jnp.zeros_like(acc_ref)

    acc_ref[...] += _dot(a_ref[...], w_ref[...])

    @pl.when(k == pl.num_programs(2) - 1)
    def _():
        gate = _row_select(pl.program_id(0) * tm, tm, gate_ref[...])
        o_ref[...] = res_ref[...] + gate * acc_ref[...]


def _mm_ktiled_steps(kdim, n, n_rows):
    return ((n_rows // _pick(n_rows, (1088, 1024, 512, 256))) * (n // _pick(n, (1024, 512, 256, 128)))
            * (kdim // _pick(kdim, (2816, 2048, 1024, 512, 256))))


def _mm_ktiled(a, w, n_rows, res, gate, jobs=()):
    kdim, n = w.shape
    table, gate_layer, chunk = gate
    d = table.shape[2] // 6
    tm = _pick(n_rows, (1088, 1024, 512, 256))
    tn = _pick(n, (1024, 512, 256, 128))
    tk = _pick(kdim, (2816, 2048, 1024, 512, 256))
    (out,), casts = _hosted_call(
        functools.partial(_mmk_kernel, tm=tm), jobs,
        grid=(n_rows // tm, n // tn, kdim // tk),
        in_specs=[pl.BlockSpec((tm, tk), lambda i, j, k: (i, k)),
                  pl.BlockSpec((tk, tn), lambda i, j, k: (k, j)),
                  pl.BlockSpec((tm, tn), lambda i, j, k: (i, j)),
                  pl.BlockSpec((None, SUBLANE, tn), lambda i, j, k: (gate_layer, 0, chunk * (d // tn) + j))],
        out_specs=[pl.BlockSpec((tm, tn), lambda i, j, k: (i, j))],
        out_shape=[jax.ShapeDtypeStruct((n_rows, n), F32)],
        scratch_shapes=[pltpu.VMEM((tm, tn), F32)],
        args=(a, w, res, table), sem=("parallel", "parallel", "arbitrary"), name="ffn_down")
    return out, casts


def _ffn_up_kernel(u_ref, uh_ref, wg_ref, wu_ref, cw_ref, cb_ref, o_ref, halo_ref, *, tm):
    i = pl.program_id(1)

    @pl.when(i == 0)
    def _():
        halo_ref[...] = _dot(uh_ref[...], wg_ref[...])

    u = u_ref[...]
    g = _dot(u, wg_ref[...])
    dn, up = _shift_rows(g, halo_ref[pl.ds(2 * i, 1), :], halo_ref[pl.ds(2 * i + 1, 1), :], i * tm)
    a = dn * cw_ref[0:1, :] + g * cw_ref[1:2, :] + up * cw_ref[2:3, :] + cb_ref[...]
    o_ref[...] = (_silu(a) * _dot(u, wu_ref[...])).astype(o_ref.dtype)


def _ffn_up_steps(n_rows, n):
    return (n // _pick(n, (512, 256, 128))) * (n_rows // _pick(n_rows, (1088, 1024, 512, 256)))


def _ffn_up(u, n_rows, layer, wg, wu, cw_stack, cb_stack, jobs):
    d, n = wg.shape
    tm = _pick(n_rows, (1088, 1024, 512, 256))
    tn = _pick(n, (512, 256, 128))
    gm = n_rows // tm
    n_halo = _round_up(2 * gm, 2 * SUBLANE)
    rows = []
    for i in range(gm):
        rows += [max(i * tm - 1, 0), min((i + 1) * tm, n_rows - 1)]
    rows += [0] * (n_halo - len(rows))
    u_halo = jnp.concatenate([u[r:r + 1] for r in rows], axis=0)
    col = lambda j, i: (layer, 0, j)
    (hid,), casts = _hosted_call(
        functools.partial(_ffn_up_kernel, tm=tm), jobs,
        grid=(n // tn, gm),
        in_specs=[pl.BlockSpec((tm, d), lambda j, i: (i, 0)),
                  pl.BlockSpec((n_halo, d), lambda j, i: (0, 0)),
                  pl.BlockSpec((d, tn), lambda j, i: (0, j)),
                  pl.BlockSpec((d, tn), lambda j, i: (0, j)),
                  pl.BlockSpec((None, 3, tn), col),
                  pl.BlockSpec((None, 1, tn), col)],
        out_specs=[pl.BlockSpec((tm, tn), lambda j, i: (i, j))],
        out_shape=[jax.ShapeDtypeStruct((n_rows, n), BF16)],
        scratch_shapes=[pltpu.VMEM((n_halo, tn), F32)],
        args=(u, u_halo, wg, wu, cw_stack, cb_stack), sem=("parallel", "arbitrary"), name="ffn_up")
    return hid, casts


def _conv_ffn(h, n_rows, gain, mod_rows, mod, layer, wg, wu, cw, cb, wd_stack, next_w):
    d_ff_p = wg.shape[1]
    u = _modulate(h, n_rows, gain, mod_rows, layer, 3, 4)
    job = _CastJob(wd_stack, layer, d_ff_p, wd_stack.shape[2], 0, _ffn_up_steps(n_rows, d_ff_p))
    hid, (wd,) = _ffn_up(u, n_rows, layer, wg, wu, cw, cb, [job])
    jobs = []
    if next_w is not None:
        stack, idx = next_w
        jobs = [_CastJob(stack, idx, stack.shape[1], stack.shape[2], 0, _mm_ktiled_steps(d_ff_p, wd.shape[1], n_rows))]
    h, casts = _mm_ktiled(hid, wd, n_rows, h, (mod, layer, 5), jobs)
    return h, (casts[0] if casts else None)


def _rope(x, rope_ref):
    c, sa, sb = rope_ref[:, 0:LANE], rope_ref[:, LANE:2 * LANE], rope_ref[:, 2 * LANE:3 * LANE]
    quarter = MLA_ROPE // 4
    return x * c + pltpu.roll(x, LANE - quarter, 1) * sa + pltpu.roll(x, quarter, 1) * sb


def _rms(x, width):
    return lax.rsqrt(jnp.sum(x * x, axis=-1, keepdims=True) * (1.0 / width) + EPS)


def _qproj_kernel(cq_ref, ag_ref, w_ref, hg_ref, rope_ref, o_ref, *, heads):
    x = cq_ref[...]
    xn = ((x * _rms(x, x.shape[1])) * ag_ref[...]).astype(BF16)
    y = _dot(xn, w_ref[...])
    for hh in range(heads):
        yh = y[:, hh * QK_PAD:(hh + 1) * QK_PAD]
        yn = (yh * _rms(yh, MLA_NOPE + MLA_ROPE)) * hg_ref[...]
        o_ref[:, hh * QK_PAD:hh * QK_PAD + LANE] = yn[:, :LANE].astype(o_ref.dtype)
        o_ref[:, hh * QK_PAD + LANE:(hh + 1) * QK_PAD] = _rope(yn[:, LANE:], rope_ref).astype(o_ref.dtype)


def _qproj(y_in, n_rows, ag, w, hg, rope):
    heads = _pick(MLA_HEADS, (4, 2, 1))
    tm = _proj_rows()
    tn = heads * QK_PAD
    return pl.pallas_call(
        functools.partial(_qproj_kernel, heads=heads),
        grid=(n_rows // tm, w.shape[1] // tn),
        in_specs=[pl.BlockSpec((tm, MLA_Q_RANK), lambda i, j: (i, 0)),
                  pl.BlockSpec((1, MLA_Q_RANK), lambda i, j: (0, 0)),
                  pl.BlockSpec((MLA_Q_RANK, tn), lambda i, j: (0, j)),
                  pl.BlockSpec((1, QK_PAD), lambda i, j: (0, 0)),
                  _rope_spec()],
        out_specs=pl.BlockSpec((tm, tn), lambda i, j: (i, j)),
        out_shape=jax.ShapeDtypeStruct((n_rows, w.shape[1]), BF16),
        compiler_params=_cp("parallel", "parallel"),
        name="q_proj",
    )(y_in, ag, w, hg, rope)


def _kvproj_kernel(ckv_ref, kr_ref, ag_ref, w_ref, gn_ref, gr_ref, rope_ref, k_ref, v_ref, *, heads):
    x = ckv_ref[...]
    xn = ((x * _rms(x, x.shape[1])) * ag_ref[...]).astype(BF16)
    y = _dot(xn, w_ref[...])
    kr = kr_ref[...]
    kr_ss = jnp.sum(kr * kr, axis=-1, keepdims=True)
    width = MLA_NOPE + MLA_VDIM
    ones = jnp.ones((x.shape[0], V_PAD - MLA_VDIM), v_ref.dtype)
    for hh in range(heads):
        kn = y[:, hh * width:hh * width + MLA_NOPE]
        r = lax.rsqrt((jnp.sum(kn * kn, axis=-1, keepdims=True) + kr_ss) * (1.0 / (MLA_NOPE + MLA_ROPE)) + EPS)
        k_ref[:, hh * QK_PAD:hh * QK_PAD + LANE] = ((kn * r) * gn_ref[...]).astype(k_ref.dtype)
        k_ref[:, hh * QK_PAD + LANE:(hh + 1) * QK_PAD] = _rope((kr * r) * gr_ref[...], rope_ref).astype(k_ref.dtype)
        v_ref[:, hh * V_PAD:hh * V_PAD + MLA_VDIM] = y[:, hh * width + MLA_NOPE:(hh + 1) * width].astype(v_ref.dtype)
        v_ref[:, hh * V_PAD + MLA_VDIM:(hh + 1) * V_PAD] = ones


def _kvproj(y_in, n_rows, ag, w, gn, gr, rope):
    heads = _pick(MLA_HEADS, (4, 2, 1))
    assert MLA_NOPE == LANE and MLA_VDIM == LANE and MLA_Q_RANK % MLA_KV_RANK == 0
    tm = _proj_rows()
    tn = heads * (MLA_NOPE + MLA_VDIM)
    kr_blk = (MLA_Q_RANK + MLA_KV_RANK) // LANE
    return pl.pallas_call(
        functools.partial(_kvproj_kernel, heads=heads),
        grid=(n_rows // tm, w.shape[1] // tn),
        in_specs=[pl.BlockSpec((tm, MLA_KV_RANK), lambda i, j: (i, MLA_Q_RANK // MLA_KV_RANK)),
                  pl.BlockSpec((tm, LANE), lambda i, j: (i, kr_blk)),
                  pl.BlockSpec((1, MLA_KV_RANK), lambda i, j: (0, 0)),
                  pl.BlockSpec((MLA_KV_RANK, tn), lambda i, j: (0, j)),
                  pl.BlockSpec((1, LANE), lambda i, j: (0, 0)),
                  pl.BlockSpec((1, LANE), lambda i, j: (0, 0)),
                  _rope_spec()],
        out_specs=[pl.BlockSpec((tm, heads * QK_PAD), lambda i, j: (i, j)),
                   pl.BlockSpec((tm, heads * V_PAD), lambda i, j: (i, j))],
        out_shape=[jax.ShapeDtypeStruct((n_rows, MLA_HEADS * QK_PAD), BF16),
                   jax.ShapeDtypeStruct((n_rows, MLA_HEADS * V_PAD), BF16)],
        compiler_params=_cp("parallel", "parallel"),
        name="kv_proj",
    )(y_in, y_in, ag, w, gn, gr, rope)


def _attn_kernel(*refs, chunks, scale):
    q = refs[0][...]
    n_kv = (len(refs) - 2) // 2
    k_refs, v_refs, o_ref = refs[1:1 + n_kv], refs[1 + n_kv:1 + 2 * n_kv], refs[1 + 2 * n_kv]
    c = scale * math.log2(math.e)
    m = acc = None
    for idx, start, size in chunks:
        s = _dot_nt(q, k_refs[idx][start:start + size, :])
        m_blk = jnp.max(s, axis=-1, keepdims=True)
        m_new = m_blk if m is None else jnp.maximum(m, m_blk)
        pv = _dot(jnp.exp2((s - m_new) * c).astype(BF16), v_refs[idx][start:start + size, :])
        acc = pv if acc is None else jnp.exp2((m - m_new) * c) * acc + pv
        m = m_new
    o_ref[...] = (acc[:, :MLA_VDIM] / acc[:, MLA_VDIM:2 * MLA_VDIM]).astype(o_ref.dtype)


def _attn_lat_steps():
    return BATCH * MLA_HEADS * (SEQ // _pick(SEQ, (1024, 512, 256)))


def _attention(q, k, v, latent, n_out, jobs=(), o_prev=None):
    assert V_PAD == 2 * MLA_VDIM
    scale = (MLA_NOPE + MLA_ROPE) ** -0.5
    ctx_blk0 = _n_lat() // CTX_LEN
    ctx_k = pl.BlockSpec((CTX_LEN, QK_PAD), lambda b, h, i: (ctx_blk0 + b, h))
    ctx_v = pl.BlockSpec((CTX_LEN, V_PAD), lambda b, h, i: (ctx_blk0 + b, h))
    if latent:
        tq = _pick(SEQ, (1024, 512, 256))
        tk = _pick(SEQ, (1024, 512, 256))
        q_per_b = SEQ // tq
        q_spec = pl.BlockSpec((tq, QK_PAD), lambda b, h, i: (b * q_per_b + i, h))
        o_spec = pl.BlockSpec((tq, MLA_VDIM), lambda b, h, i: (b * q_per_b + i, h))
        k_specs = [ctx_k, pl.BlockSpec((SEQ, QK_PAD), lambda b, h, i: (b, h))]
        v_specs = [ctx_v, pl.BlockSpec((SEQ, V_PAD), lambda b, h, i: (b, h))]
        chunks = [(0, 0, CTX_LEN)] + [(1, s, tk) for s in range(0, SEQ, tk)]
        grid = (BATCH, MLA_HEADS, q_per_b)
    else:
        q_spec = pl.BlockSpec((CTX_LEN, QK_PAD), lambda b, h, i: (ctx_blk0 + b, h))
        o_spec = pl.BlockSpec((CTX_LEN, MLA_VDIM), lambda b, h, i: (ctx_blk0 + b, h))
        k_specs, v_specs, chunks = [ctx_k], [ctx_v], [(0, 0, CTX_LEN)]
        grid = (BATCH, MLA_HEADS, 1)
    n_kv = len(k_specs)
    in_specs = [q_spec] + k_specs + v_specs
    args = (q, *([k] * n_kv), *([v] * n_kv))
    out_shape = jax.ShapeDtypeStruct((n_out, MLA_HEADS * MLA_VDIM), BF16)
    body = functools.partial(_attn_kernel, chunks=tuple(chunks), scale=scale)
    if latent:
        (out,), casts = _hosted_call(body, jobs, grid=grid, in_specs=in_specs, out_specs=[o_spec],
                                     out_shape=[out_shape], args=args,
                                     sem=("parallel", "parallel", "arbitrary"), name="attn_lat")
        return out, casts
    return pl.pallas_call(
        lambda *refs: body(*refs[:len(in_specs)], refs[-1]),
        grid=grid,
        in_specs=in_specs + [pl.BlockSpec(memory_space=pl.ANY)],
        out_specs=o_spec,
        out_shape=out_shape,
        input_output_aliases={len(in_specs): 0},
        compiler_params=_cp("parallel", "parallel", "arbitrary"),
        name="attn_ctx",
    )(*args, o_prev)


def _sconv_kernel(b_ref, c_ref, x_ref, cp_ref, xp_ref, cn_ref, xn_ref, w_ref, o_ref, *, tm):
    p = c_ref[...] * x_ref[...]
    prev = cp_ref[SUBLANE - 1:SUBLANE, :] * xp_ref[SUBLANE - 1:SUBLANE, :]
    nxt = cn_ref[0:1, :] * xn_ref[0:1, :]
    dn, up = _shift_rows(p, prev, nxt, pl.program_id(0) * tm)
    conv = dn * w_ref[0:1, :] + p * w_ref[1:2, :] + up * w_ref[2:3, :]
    o_ref[...] = (b_ref[...] * conv).astype(o_ref.dtype)


def _sconv(y_in, n_rows, w, col0, width):
    tm = ROW_BLK
    tn = _pick(width, (1024, 512, 256, 128))
    blk0 = [(col0 + k * width) // tn for k in range(3)]
    main = [pl.BlockSpec((tm, tn), lambda i, j, o=o: (i, o + j)) for o in blk0]
    cp, cn = _halo_specs(tm, tn, n_rows, blk0[1])
    xp, xn = _halo_specs(tm, tn, n_rows, blk0[2])
    return pl.pallas_call(
        functools.partial(_sconv_kernel, tm=tm),
        grid=(n_rows // tm, width // tn),
        in_specs=main + [cp, xp, cn, xn, pl.BlockSpec((3, tn), lambda i, j: (0, j))],
        out_specs=pl.BlockSpec((tm, tn), lambda i, j: (i, j)),
        out_shape=jax.ShapeDtypeStruct((n_rows, width), BF16),
        compiler_params=_cp("parallel", "parallel"),
        name="short_conv",
    )(*([y_in] * 7), w)


def _conv_silu(x_ref, xp_ref, xn_ref, w_ref, row0):
    x = x_ref[...]
    dn, up = _shift_rows(x, xp_ref[SUBLANE - 1:SUBLANE, :], xn_ref[0:1, :], row0)
    return _silu(dn * w_ref[0:1, :] + x * w_ref[1:2, :] + up * w_ref[2:3, :])


def _l2norm(x):
    return x * lax.rsqrt(jnp.sum(x * x, axis=-1, keepdims=True) + EPS)


def _gdn_gate_kernel(ba_ref, alog_ref, dtb_ref, o_ref):
    x = ba_ref[...]
    lane = lax.broadcasted_iota(jnp.int32, x.shape, 1)
    z = x + dtb_ref[...]
    softplus = jnp.maximum(z, 0.0) + jnp.log(1.0 + jnp.exp(-jnp.abs(z)))
    o_ref[...] = jnp.where(lane < x.shape[1] // 2, _sigmoid(x), -jnp.exp(alog_ref[...]) * softplus)


def _gdn_gate(ba, n_rows, alog, dtb):
    w = ba.shape[1]
    return pl.pallas_call(
        _gdn_gate_kernel,
        grid=(n_rows // ROW_BLK,),
        in_specs=[pl.BlockSpec((ROW_BLK, w), lambda i: (i, 0)),
                  pl.BlockSpec((1, w), lambda i: (0, 0)),
                  pl.BlockSpec((1, w), lambda i: (0, 0))],
        out_specs=pl.BlockSpec((ROW_BLK, w), lambda i: (i, 0)),
        out_shape=jax.ShapeDtypeStruct((n_rows, w), F32),
        compiler_params=_cp("parallel"),
        name="gdn_gate",
    )(ba, alog, dtb)


def _active_rows(x, size, odd):
    n = x.shape[0]
    return jnp.concatenate([x[(2 * b + odd) * size:(2 * b + odd + 1) * size] for b in range(n // (2 * size))], axis=0)


def _weave_rows(rest, active, size, odd):
    pieces = []
    for b in range(active.shape[0] // size):
        if rest is None:
            keep = jnp.zeros((size, active.shape[1]), active.dtype)
        else:
            keep = rest[(2 * b + 1 - odd) * size:(2 * b + 2 - odd) * size]
        act = active[b * size:(b + 1) * size]
        pieces += [keep, act] if odd else [act, keep]
    return jnp.concatenate(pieces, axis=0)


def _unit_tri_inverses(l_mats, dirs, eye, base, ring_ref):
    size = SUBLANE
    xs = [-(l * base) for l in l_mats]
    ts = [eye + x for x in xs]
    for _ in range(size.bit_length() - 2):
        xs = [_dot(x.astype(BF16), x.astype(BF16)) for x in xs]
        ts = [t + _dot(t.astype(BF16), x.astype(BF16)) for t, x in zip(ts, xs)]
    level = 0
    while size < GDN_BLK:
        tbs = [t.astype(BF16) for t in ts]
        offs = [(_active_rows(l, size, 1 - d) * ring_ref[level, d]).astype(BF16) for l, d in zip(l_mats, dirs)]
        mids = [_weave_rows(None, _dot(off, tb), size, 1 - d).astype(BF16) for off, tb, d in zip(offs, tbs, dirs)]
        acts = [_active_rows(t, size, 1 - d) for t, d in zip(ts, dirs)]
        news = [act - _dot(act.astype(BF16), mid) for act, mid in zip(acts, mids)]
        ts = [_weave_rows(t, new, size, 1 - d) for t, new, d in zip(ts, news, dirs)]
        size *= 2
        level += 1
    return ts


def _gdn_masks():
    n, size = GDN_BLK, SUBLANE
    i, j = np.arange(n)[:, None], np.arange(n)[None, :]
    tri = np.stack([j <= i, j >= i, j == i, i // size == j // size]).astype(np.float32)
    rings = []
    while size < n:
        ra = np.arange(n // 2)[:, None]
        per_dir = []
        for d in range(2):
            ia = (ra // size) * 2 * size + (1 - d) * size + ra % size
            per_dir.append((ia // (2 * size) == j // (2 * size)) & (ia // size != j // size))
        rings.append(np.stack(per_dir))
        size *= 2
    return jnp.asarray(tri), jnp.asarray(np.stack(rings).astype(np.float32))


def _gdn_prep_kernel(q_ref, qp_ref, qn_ref, k_ref, kp_ref, kn_ref, v_ref, vp_ref, vn_ref, cwq_ref, cwk_ref, cwv_ref,
                     bg_ref, tri_ref, tri16_ref, ring_ref, u_ref, wq_ref, kd_ref, in_ref, cd_ref, *, rep, hpb):
    n = GDN_BLK
    row0 = pl.program_id(0) * n
    q_all = _conv_silu(q_ref, qp_ref, qn_ref, cwq_ref, row0)
    k_all = _conv_silu(k_ref, kp_ref, kn_ref, cwk_ref, row0)
    v_all = _conv_silu(v_ref, vp_ref, vn_ref, cwv_ref, row0)
    eye, base = tri_ref[2], tri_ref[3]
    not_eye = 1.0 - eye
    chains, l_mats, rhss = [], [], []
    for hh in range(hpb):
        q = _l2norm(q_all[:, hh * GDN_DK:(hh + 1) * GDN_DK]) * GDN_DK ** -0.5
        k = _l2norm(k_all[:, hh * GDN_DK:(hh + 1) * GDN_DK])
        q16, k16 = q.astype(BF16), k.astype(BF16)
        qk = _dot_nt(q16, k16)
        kk = _dot_nt(k16, k16) * not_eye
        bg = bg_ref[hh]
        bg_cols = jnp.concatenate([bg, jnp.zeros((LANE - bg.shape[0], n), F32)], axis=0).T
        hi = bg_cols.astype(BF16)
        rest = bg_cols - hi.astype(F32)
        mid = rest.astype(BF16)
        pieces = jnp.concatenate([hi, mid, (rest - mid.astype(F32)).astype(BF16)], axis=1)
        cums = [_dot(tri16_ref[d], pieces) for d in range(2)]
        gc_cols = [c[:, :LANE] + c[:, LANE:2 * LANE] + c[:, 2 * LANE:] for c in cums]
        gc_rows = [g.T for g in gc_cols]
        for e in range(rep):
            for d in range(2):
                vh = hh * rep + e
                cols = slice(vh * GDN_DV, (vh + 1) * GDN_DV)
                incl = tri_ref[d]
                b_idx, g_idx = d * rep + e, (2 + d) * rep + e
                b_col = bg_cols[:, b_idx:b_idx + 1]
                gc_col, gc_row = gc_cols[d][:, g_idx:g_idx + 1], gc_rows[d][g_idx:g_idx + 1, :]
                g_tot = jnp.sum(bg[g_idx:g_idx + 1, :], axis=1, keepdims=True)
                decay = jnp.exp(jnp.minimum(gc_col - gc_row, 0.0)) * incl
                e_col = jnp.exp(gc_col)
                chains.append((vh, d, cols))
                l_mats.append((kk * b_col) * decay)
                rhss.append(jnp.concatenate([v_all[:, cols] * b_col, (k * b_col) * e_col], axis=1).astype(BF16))
                wq_ref[d, n:, cols] = (q * e_col).astype(wq_ref.dtype)
                kd_ref[d, :, cols] = (k * jnp.exp(g_tot - gc_col)).astype(kd_ref.dtype)
                in_ref[d, vh] = (qk * decay).astype(in_ref.dtype)
                cd_ref[d, vh] = jnp.broadcast_to(jnp.exp(g_tot), (SUBLANE, LANE))
    invs = _unit_tri_inverses(l_mats, [d for _, d, _ in chains], eye, base, ring_ref)
    sols = [_dot(t.astype(BF16), rhs) for t, rhs in zip(invs, rhss)]
    for (vh, d, cols), sol in zip(chains, sols):
        u_ref[d, :, cols] = sol[:, :GDN_DV]
        wq_ref[d, :n, cols] = sol[:, GDN_DV:].astype(wq_ref.dtype)


def _gdn_prep(y_in, conv_w, bg_t, n_rows):
    assert GDN_DK == LANE and GDN_DV == LANE and GDN_BLK == 2 * LANE
    nb, hv, hq, rep = n_rows // GDN_BLK, GDN_V_HEADS, GDN_QK_HEADS, GDN_V_HEADS // GDN_QK_HEADS
    hpb = _pick(hq, (2, 1))
    key_w, wide = hpb * GDN_DK, hpb * rep * GDN_DV
    parts = ((key_w, 0), (key_w, hq // hpb), (wide, 2 * hq * GDN_DK // wide))
    in_specs = []
    for tn, blk0 in parts:
        in_specs += [pl.BlockSpec((GDN_BLK, tn), lambda i, h, blk0=blk0: (i, blk0 + h))]
        in_specs += _halo_specs(GDN_BLK, tn, n_rows, blk0)
    in_specs += [pl.BlockSpec((3, tn), lambda i, h, blk0=blk0: (0, blk0 + h)) for tn, blk0 in parts]
    tri, rings = _gdn_masks()
    in_specs += [pl.BlockSpec((hpb, bg_t.shape[1], GDN_BLK), lambda i, h: (h, 0, i)),
                 pl.BlockSpec(tri.shape, lambda i, h: (0, 0, 0)),
                 pl.BlockSpec((2,) + tri.shape[1:], lambda i, h: (0, 0, 0)),
                 pl.BlockSpec(rings.shape, lambda i, h: (0, 0, 0, 0))]
    return pl.pallas_call(
        functools.partial(_gdn_prep_kernel, rep=rep, hpb=hpb),
        grid=(nb, hq // hpb),
        in_specs=in_specs,
        out_specs=[pl.BlockSpec((2, GDN_BLK, wide), lambda i, h: (0, i, h)),
                   pl.BlockSpec((2, None, 2 * GDN_BLK, wide), lambda i, h: (0, i, 0, h)),
                   pl.BlockSpec((2, GDN_BLK, wide), lambda i, h: (0, i, h)),
                   pl.BlockSpec((2, hpb * rep, GDN_BLK, GDN_BLK), lambda i, h: (0, h, i, 0)),
                   pl.BlockSpec((2, hpb * rep, SUBLANE, LANE), lambda i, h: (0, h, i, 0))],
        out_shape=[jax.ShapeDtypeStruct((2, n_rows, hv * GDN_DV), F32),
                   jax.ShapeDtypeStruct((2, nb, 2 * GDN_BLK, hv * GDN_DV), BF16),
                   jax.ShapeDtypeStruct((2, n_rows, hv * GDN_DV), BF16),
                   jax.ShapeDtypeStruct((2, hv, n_rows, GDN_BLK), BF16),
                   jax.ShapeDtypeStruct((2, hv, nb * SUBLANE, LANE), F32)],
        compiler_params=_cp("parallel", "parallel"),
        name="gdn_prep",
    )(*([y_in] * 9), conv_w, conv_w, conv_w, bg_t, tri, tri[:2].astype(BF16), rings)


def _gdn_scan_kernel(*refs, heads):
    ins, (of_ref, ob_ref, s_ref) = refs[:10], refs[10:]
    n = GDN_BLK

    @pl.when(pl.program_id(2) == 0)
    def _():
        s_ref[...] = jnp.zeros_like(s_ref)

    o_refs = (of_ref, ob_ref)
    chains = [(d, hh, slice(hh * GDN_DV, (hh + 1) * GDN_DV)) for d in range(2) for hh in range(heads)]
    u_refs, wq_refs, kd_refs, in_refs, cd_refs = (ins[0::5], ins[1::5], ins[2::5], ins[3::5], ins[4::5])
    states = [s_ref[d, hh] for d, hh, _ in chains]
    wqs = [_dot(wq_refs[d][:, cols], s.astype(BF16)) for (d, hh, cols), s in zip(chains, states)]
    vbs = [(u_refs[d][:, cols] - wq[:n]).astype(BF16) for (d, hh, cols), wq in zip(chains, wqs)]
    for (d, hh, cols), s, wq, vb in zip(chains, states, wqs, vbs):
        o_refs[d][:, cols] = wq[n:] + _dot(in_refs[d][hh], vb)
        s_ref[d, hh] = s * cd_refs[d][hh, 0:1, :] + _dot_tn(kd_refs[d][:, cols], vb)


def _gdn_scan(prep, n_rows):
    hv = GDN_V_HEADS
    heads = _pick(hv, (4, 2, 1))
    lat_blk, ctx_blk = SEQ // GDN_BLK, CTX_LEN // GDN_BLK
    steps = ctx_blk + lat_blk
    ctx0 = _n_lat() // GDN_BLK
    wide = heads * GDN_DV

    def blk_f(b, n):
        return jnp.where(n < ctx_blk, ctx0 + b * ctx_blk + n, b * lat_blk + n - ctx_blk)

    def blk_b(b, n):
        return jnp.where(n < ctx_blk, ctx0 + b * ctx_blk + ctx_blk - 1 - n, b * lat_blk + steps - 1 - n)

    in_specs = []
    for d, blk in enumerate((blk_f, blk_b)):
        rows = pl.BlockSpec((None, GDN_BLK, wide), lambda b, h, n, d=d, blk=blk: (d, blk(b, n), h))
        in_specs += [rows,
                     pl.BlockSpec((None, None, 2 * GDN_BLK, wide), lambda b, h, n, d=d, blk=blk: (d, blk(b, n), 0, h)),
                     rows,
                     pl.BlockSpec((None, heads, GDN_BLK, GDN_BLK), lambda b, h, n, d=d, blk=blk: (d, h, blk(b, n), 0)),
                     pl.BlockSpec((None, heads, SUBLANE, LANE), lambda b, h, n, d=d, blk=blk: (d, h, blk(b, n), 0))]
    out = jax.ShapeDtypeStruct((n_rows, hv * GDN_DV), F32)
    return pl.pallas_call(
        functools.partial(_gdn_scan_kernel, heads=heads),
        grid=(BATCH, hv // heads, steps),
        in_specs=in_specs,
        out_specs=[pl.BlockSpec((GDN_BLK, wide), lambda b, h, n: (blk_f(b, n), h)),
                   pl.BlockSpec((GDN_BLK, wide), lambda b, h, n: (blk_b(b, n), h))],
        out_shape=[out, out],
        scratch_shapes=[pltpu.VMEM((2, heads, GDN_DK, GDN_DV), F32)],
        compiler_params=_cp("parallel", "parallel", "arbitrary"),
        name="gdn_scan",
    )(*prep, *prep)


def _gdn_readout_kernel(of_ref, ob_ref, z_ref, g_ref, o_ref):
    o = of_ref[...] + ob_ref[...]
    z = z_ref[...]
    for hh in range(o.shape[1] // GDN_DV):
        cols = slice(hh * GDN_DV, (hh + 1) * GDN_DV)
        oh = o[:, cols]
        y = (oh * _rms(oh, GDN_DV)) * g_ref[...]
        o_ref[:, cols] = (y * _silu(z[:, cols])).astype(o_ref.dtype)


def _gdn_readout(o_f, o_b, y_in, z_col0, gain, n_rows):
    width = GDN_V_HEADS * GDN_DV
    tn = _pick(width, (1024, 512, 256, 128))
    spec = pl.BlockSpec((ROW_BLK, tn), lambda i, j: (i, j))
    return pl.pallas_call(
        _gdn_readout_kernel,
        grid=(n_rows // ROW_BLK, width // tn),
        in_specs=[spec, spec, pl.BlockSpec((ROW_BLK, tn), lambda i, j: (i, z_col0 // tn + j)),
                  pl.BlockSpec((1, GDN_DV), lambda i, j: (0, 0))],
        out_specs=spec,
        out_shape=jax.ShapeDtypeStruct((n_rows, width), BF16),
        compiler_params=_cp("parallel", "parallel"),
        name="gdn_readout",
    )(o_f, o_b, y_in, gain.reshape(1, GDN_DV))


def _proj_rows():
    return _pick(math.gcd(SEQ, BATCH * CTX_LEN), (512, 256))


def _rope_spec():
    tm = _proj_rows()
    return pl.BlockSpec((tm, 3 * LANE), lambda i, j: (jnp.where(i < _n_lat() // tm, i % (SEQ // tm), SEQ // tm), 0))


def _rope_table(n_ident):
    n_freq = MLA_ROPE // 4
    inv_freq = ROPE_THETA ** (-jnp.arange(n_freq, dtype=F32) / n_freq)
    t = jnp.arange(SEQ)
    ang = jnp.concatenate([(t // GRID_W).astype(F32)[:, None] * inv_freq,
                           (t % GRID_W).astype(F32)[:, None] * inv_freq], axis=1)
    cos, sin = jnp.cos(ang), jnp.sin(ang)
    zero = jnp.zeros_like(sin[:, :n_freq])
    pad = jnp.zeros((SEQ, LANE - MLA_ROPE), F32)
    c = jnp.concatenate([cos[:, :n_freq], cos[:, :n_freq], cos[:, n_freq:], cos[:, n_freq:], pad + 1.0], axis=1)
    sa = jnp.concatenate([-sin[:, :n_freq], zero, -sin[:, n_freq:], zero, pad], axis=1)
    sb = jnp.concatenate([zero, sin[:, :n_freq], zero, sin[:, n_freq:], pad], axis=1)
    ident = jnp.concatenate([jnp.ones((n_ident, LANE), F32), jnp.zeros((n_ident, 2 * LANE), F32)], axis=1)
    return jnp.concatenate([jnp.concatenate([c, sa, sb], axis=1), ident], axis=0)


def _pad_cols(w, n):
    return jnp.pad(w, ((0, 0), (0, n - w.shape[1])))


def _even_layer(h, n_rows, need_ctx, mod, mod_rows, layer, gain, p, ffn):
    w_in, q_a_g, kv_a_g, w_qb, w_kvb, q_g, k_g, sc_w, w_out = p
    ffn_w_gate, ffn_w_up, d_ff_p = ffn
    d = h.shape[1]
    sc_width = d - MLA_HEADS * MLA_VDIM
    assert sc_width == MLA_HEADS * MLA_VDIM
    head_w = MLA_NOPE + MLA_ROPE
    c0 = MLA_Q_RANK + MLA_KV_RANK
    conv0 = _round_up(c0 + LANE, _pick(sc_width, (1024, 512, 256, 128)))
    n_in = _round_up(conv0 + 3 * sc_width, 1024)
    w_in_p = jnp.concatenate([_pad_cols(w_in[:, :c0 + MLA_ROPE], conv0), w_in[:, c0 + MLA_ROPE:]], axis=1)
    w_in_p = _pad_cols(w_in_p, n_in).astype(BF16)
    w_q = w_qb.reshape(MLA_Q_RANK, MLA_HEADS, head_w)
    w_q = jnp.pad(w_q, ((0, 0), (0, 0), (0, QK_PAD - head_w))).reshape(MLA_Q_RANK, MLA_HEADS * QK_PAD).astype(BF16)
    rope = _rope_table(_proj_rows())

    u = _modulate(h, n_rows, gain, mod_rows, layer, 0, 1)
    y = _mm([u], w_in_p, n_rows, name="even_in")
    q = _qproj(y, n_rows, q_a_g.reshape(1, -1), w_q, jnp.pad(q_g, (0, QK_PAD - head_w)).reshape(1, QK_PAD), rope)
    k, v = _kvproj(y, n_rows, kv_a_g.reshape(1, -1), w_kvb.astype(BF16), k_g[:MLA_NOPE].reshape(1, LANE),
                   jnp.pad(k_g[MLA_NOPE:], (0, LANE - MLA_ROPE)).reshape(1, LANE), rope)
    steps = _attn_lat_steps()
    jobs = [_CastJob(ffn_w_gate, layer, d, d_ff_p, 0, steps // 2),
            _CastJob(ffn_w_up, layer, d, d_ff_p, steps // 2, steps - steps // 2)]
    n_out = n_rows if need_ctx else _n_lat()
    o, ffn_w = _attention(q, k, v, True, n_out, jobs)
    if need_ctx:
        o = _attention(q, k, v, False, n_out, o_prev=o)
    conv = _sconv(y, n_out, sc_w, conv0, sc_width)
    h = _mm([o, conv], w_out.astype(BF16), n_out, res=h, gate=(mod, layer, 2),
            tn=_pick(d, (512, 256, 128)), name="even_out")
    return h, ffn_w


def _odd_layer(h, n_rows, need_ctx, mod, mod_rows, layer, gain, p, idx, ffn, w_in_b):
    w_in_stack, conv_w, a_log, dt_bias, o_g, w_out = p
    ffn_w_gate, ffn_w_up, d_ff_p = ffn
    d = h.shape[1]
    hv, hq = GDN_V_HEADS, GDN_QK_HEADS
    qkv_w = 2 * hq * GDN_DK + hv * GDN_DV
    main_w = qkv_w + hv * GDN_DV
    gate_w = _round_up(4 * hv, LANE)
    half = gate_w // 2
    w_in = w_in_stack[idx]
    w_ba = jnp.concatenate([_pad_cols(w_in[:, main_w:main_w + 2 * hv], half),
                            _pad_cols(w_in[:, main_w + 2 * hv:], half)], axis=1).astype(BF16)
    zeros = jnp.zeros((1, half), F32)
    alog = jnp.concatenate([zeros, _pad_cols(a_log.reshape(1, -1), half)], axis=1)
    dtb = jnp.concatenate([zeros, _pad_cols(dt_bias.reshape(1, -1), half)], axis=1)

    u = _modulate(h, n_rows, gain, mod_rows, layer, 0, 1)
    if w_in_b is None:
        w_in_b = w_in.astype(BF16)
    tn = _pick(main_w, (512, 256, 128))
    y, (wg,) = _mm([u], w_in_b, n_rows, n_cols=main_w, tn=tn,
                   jobs=[_CastJob(ffn_w_gate, layer, d, d_ff_p, 0, _mm_steps(n_rows, main_w, tn))], name="odd_in")
    ba = _mm([u], w_ba, n_rows, name="odd_in_gates")
    bg = _gdn_gate(ba, n_rows, alog, dtb)
    rep = hv // hq
    bg_t = jnp.stack([bg[:, :2 * hv], bg[:, half:half + 2 * hv]], axis=0)
    bg_t = bg_t.reshape(2, n_rows, 2, hq, rep).transpose(3, 0, 2, 4, 1).reshape(hq, 4 * rep, n_rows)
    bg_t = jnp.pad(bg_t, ((0, 0), (0, _round_up(4 * rep, SUBLANE) - 4 * rep), (0, 0)))
    o_f, o_b = _gdn_scan(_gdn_prep(y, conv_w, bg_t, n_rows), n_rows)
    n_out = n_rows if need_ctx else _n_lat()
    yo = _gdn_readout(o_f, o_b, y, qkv_w, o_g, n_out)
    tn = _pick(d, (512, 256, 128))
    h, (wu,) = _mm([yo], w_out.astype(BF16), n_out, res=h, gate=(mod, layer, 2), tn=tn,
                   jobs=[_CastJob(ffn_w_up, layer, d, d_ff_p, 0, _mm_steps(n_out, d, tn))], name="odd_out")
    return h, (wg, wu)


def kernel(x, c, ctx, c_ctx, ada_w, ada_b, norm_mix, norm_ffn, ffn_w_gate, ffn_w_up, ffn_conv_w, ffn_conv_b,
           ffn_w_down, a_w_in, a_q_a_norm, a_kv_a_norm, a_w_qb, a_w_kvb, a_q_norm, a_k_norm, a_sc_conv, a_w_out,
           c_w_in, c_conv_w, c_a_log, c_dt_bias, c_o_norm, c_w_out):
    bn, t, d = x.shape
    depth = ada_w.shape[0]
    assert (bn, t, ctx.shape[1], ffn_w_gate.shape[2]) == (BATCH, SEQ, CTX_LEN, D_FF)
    assert CTX_LEN % GDN_BLK == 0 and SEQ % GDN_BLK == 0
    h = jnp.concatenate([x.reshape(bn * t, d), ctx.reshape(bn * CTX_LEN, d)], axis=0)
    cond8 = jnp.concatenate([c, c_ctx[None], jnp.zeros((SUBLANE - bn - 1, d), F32)], axis=0)
    mod = _ada(cond8, ada_w, ada_b)
    mod_rows = mod.reshape(depth * SUBLANE * 6, 1, d)
    d_ff_p = _round_up(D_FF, 512)
    ff_pad = ((0, 0), (0, 0), (0, d_ff_p - D_FF))
    conv_w = jnp.pad(ffn_conv_w, ff_pad)
    conv_b = jnp.pad(ffn_conv_b.reshape(depth, 1, D_FF), ff_pad)
    ffn = (ffn_w_gate, ffn_w_up, d_ff_p)
    w_in_b = None
    for l in range(depth):
        last = l == depth - 1
        n_rows = h.shape[0]
        i = l // 2
        if l % 2 == 0:
            h, (w_gate, w_up) = _even_layer(
                h, n_rows, not last, mod, mod_rows, l, norm_mix[l],
                (a_w_in[i], a_q_a_norm[i], a_kv_a_norm[i], a_w_qb[i], a_w_kvb[i], a_q_norm[i], a_k_norm[i],
                 a_sc_conv[i], a_w_out[i]), ffn)
        else:
            h, (w_gate, w_up) = _odd_layer(
                h, n_rows, not last, mod, mod_rows, l, norm_mix[l],
                (c_w_in, c_conv_w[i], c_a_log[i], c_dt_bias[i], c_o_norm[i], c_w_out[i]), i, ffn, w_in_b)
        next_w = (c_w_in, (l + 1) // 2) if (l + 1 < depth and (l + 1) % 2 == 1) else None
        h, w_in_b = _conv_ffn(h, h.shape[0], norm_ffn[l], mod_rows, mod, l,
                              w_gate, w_up, conv_w, conv_b, ffn_w_down, next_w)
    return h[:bn * t].reshape(bn, t, d)
```

```python
import functools
import math
from typing import NamedTuple

import jax
import jax.numpy as jnp
import numpy as np
from jax import lax
from jax.experimental import pallas as pl
from jax.experimental.pallas import tpu as pltpu

F32 = jnp.float32
BF16 = jnp.bfloat16

BATCH = 2
SEQ = 4096
GRID_W = 64
CTX_LEN = 256
EPS = 1e-6
D_FF = 11008

MLA_HEADS = 16
MLA_NOPE = 128
MLA_ROPE = 64
MLA_VDIM = 128
MLA_Q_RANK = 1024
MLA_KV_RANK = 512
ROPE_THETA = 10000.0

GDN_QK_HEADS = 16
GDN_V_HEADS = 32
GDN_DK = 128
GDN_DV = 128

LANE = 128
SUBLANE = 8
QK_PAD = 256
V_PAD = 256
ROW_BLK = 256
GDN_BLK = 256
VMEM_LIMIT = 56 * 2**20


def _cp(*sem, vmem=VMEM_LIMIT):
    return pltpu.CompilerParams(dimension_semantics=sem, vmem_limit_bytes=vmem)


def _pick(n, prefs):
    for p in prefs:
        if n % p == 0:
            return p
    raise ValueError(f"no tile for {n} in {prefs}")


def _round_up(n, m):
    return (n + m - 1) // m * m


def _sigmoid(x):
    return 1.0 / (1.0 + jnp.exp(-x))


def _silu(x):
    return x * _sigmoid(x)


def _dot(a, b):
    return jnp.dot(a, b, preferred_element_type=F32)


def _dot_nt(a, b):
    return lax.dot_general(a, b, (((1,), (1,)), ((), ())), preferred_element_type=F32)


def _dot_tn(a, b):
    return lax.dot_general(a, b, (((0,), (0,)), ((), ())), preferred_element_type=F32)


def _n_lat():
    return BATCH * SEQ


def _seq_edges(row0, tm):
    assert SEQ & (SEQ - 1) == 0 and CTX_LEN & (CTX_LEN - 1) == 0 and SEQ % CTX_LEN == 0
    r = row0 + lax.broadcasted_iota(jnp.int32, (tm, 1), 0)
    is_ctx = r >= _n_lat()
    first = (jnp.bitwise_and(r, CTX_LEN - 1) == 0) & ((jnp.bitwise_and(r, SEQ - 1) == 0) | is_ctx)
    r1 = r + 1
    last = (jnp.bitwise_and(r1, CTX_LEN - 1) == 0) & ((jnp.bitwise_and(r1, SEQ - 1) == 0) | is_ctx)
    return first, last


def _shift_rows(x, prev_row, next_row, row0):
    tm = x.shape[0]
    ridx = lax.broadcasted_iota(jnp.int32, (tm, 1), 0)
    first, last = _seq_edges(row0, tm)
    dn = jnp.where(ridx == 0, prev_row, pltpu.roll(x, 1, 0))
    dn = jnp.where(first, 0.0, dn)
    up = jnp.where(ridx == tm - 1, next_row, pltpu.roll(x, tm - 1, 0))
    up = jnp.where(last, 0.0, up)
    return dn, up


def _row_select(row0, tm, table):
    r = row0 + lax.broadcasted_iota(jnp.int32, (tm, 1), 0)
    out = table[BATCH:BATCH + 1]
    for b in reversed(range(BATCH)):
        out = jnp.where(r < (b + 1) * SEQ, table[b:b + 1], out)
    return out


def _halo_specs(tm, tn, n_rows, col_blk0):
    tmb, last = tm // SUBLANE, n_rows // SUBLANE - 1
    prev = pl.BlockSpec((SUBLANE, tn), lambda i, j: (jnp.maximum(i * tmb - 1, 0), col_blk0 + j))
    nxt = pl.BlockSpec((SUBLANE, tn), lambda i, j: (jnp.minimum((i + 1) * tmb, last), col_blk0 + j))
    return prev, nxt


def _ada_kernel(cond_ref, w_ref, b_ref, o_ref):
    a = _silu(cond_ref[...]).astype(BF16)
    o_ref[...] = _dot(a, w_ref[...].astype(BF16)) + b_ref[...]


def _ada(cond8, ada_w, ada_b):
    n_layer, d, n = ada_w.shape
    tn = _pick(n, (512, 256, 128))
    return pl.pallas_call(
        _ada_kernel,
        grid=(n_layer, n // tn),
        in_specs=[pl.BlockSpec((SUBLANE, d), lambda l, j: (0, 0)),
                  pl.BlockSpec((None, d, tn), lambda l, j: (l, 0, j)),
                  pl.BlockSpec((None, 1, tn), lambda l, j: (l, 0, j))],
        out_specs=pl.BlockSpec((None, SUBLANE, tn), lambda l, j: (l, 0, j)),
        out_shape=jax.ShapeDtypeStruct((n_layer, SUBLANE, n), F32),
        compiler_params=_cp("parallel", "parallel"),
        name="ada",
    )(cond8, ada_w, ada_b.reshape(n_layer, 1, n))


def _modulate_kernel(h_ref, g_ref, sh_ref, sc_ref, o_ref):
    x = h_ref[...]
    y = x * lax.rsqrt(jnp.mean(x * x, axis=-1, keepdims=True) + EPS)
    o_ref[...] = ((y * g_ref[...]) * (1.0 + sc_ref[...]) + sh_ref[...]).astype(o_ref.dtype)


def _modulate(h, n_rows, gain, mod_rows, layer, k_shift, k_scale):
    d = h.shape[1]
    blk_per_seq = SEQ // ROW_BLK

    def mod_spec(k):
        return pl.BlockSpec(
            (None, 1, d),
            lambda i: ((layer * SUBLANE + jnp.minimum(i // blk_per_seq, BATCH)) * 6 + k, 0, 0))

    return pl.pallas_call(
        _modulate_kernel,
        grid=(n_rows // ROW_BLK,),
        in_specs=[pl.BlockSpec((ROW_BLK, d), lambda i: (i, 0)),
                  pl.BlockSpec((1, d), lambda i: (0, 0)),
                  mod_spec(k_shift), mod_spec(k_scale)],
        out_specs=pl.BlockSpec((ROW_BLK, d), lambda i: (i, 0)),
        out_shape=jax.ShapeDtypeStruct((n_rows, d), BF16),
        compiler_params=_cp("parallel"),
        name="modulate",
    )(h, gain.reshape(1, d), mod_rows, mod_rows)


class _CastJob(NamedTuple):
    src: jax.Array
    layer: int
    rows_out: int
    cols_out: int
    start: int
    steps: int

    @property
    def tr(self):
        for tr in (16, 32, 64, 128, 256, 512, 1024):
            if self.rows_out % tr == 0 and self.src.shape[1] % tr == 0 and self.rows_out // tr <= self.steps:
                return tr
        raise ValueError("host kernel has too few steps for this cast")


def _hosted_call(kernel_fn, jobs, *, grid, in_specs, out_specs, out_shape, args, scratch_shapes=(), sem, name):
    out_specs, out_shape = list(out_specs), list(out_shape)
    n_in, n_out, n_jobs = len(in_specs), len(out_specs), len(jobs)
    strides = [math.prod(grid[a + 1:]) for a in range(len(grid))]

    def step_of(ids):
        return sum(i * s for i, s in zip(ids, strides))

    cast_in, cast_out, cast_shape = [], [], []
    for job in jobs:
        tr, n_blk, n_src_blk = job.tr, job.rows_out // job.tr, job.src.shape[1] // job.tr

        def blk(*ids, job=job, n_blk=n_blk):
            return jnp.clip(step_of(ids) - job.start, 0, n_blk - 1)

        cast_in.append(pl.BlockSpec((None, tr, job.src.shape[2]),
                                    lambda *ids, job=job, blk=blk, last=n_src_blk - 1:
                                    (job.layer, jnp.minimum(blk(*ids), last), 0)))
        cast_out.append(pl.BlockSpec((tr, job.cols_out), lambda *ids, blk=blk: (blk(*ids), 0)))
        cast_shape.append(jax.ShapeDtypeStruct((job.rows_out, job.cols_out), BF16))

    def kernel(*refs):
        host_in, src_refs = refs[:n_in], refs[n_in:n_in + n_jobs]
        host_out = refs[n_in + n_jobs:n_in + n_jobs + n_out]
        dst_refs = refs[n_in + n_jobs + n_out:n_in + 2 * n_jobs + n_out]
        kernel_fn(*host_in, *host_out, *refs[n_in + 2 * n_jobs + n_out:])
        step = step_of([pl.program_id(a) for a in range(len(grid))])
        for job, src_ref, dst_ref in zip(jobs, src_refs, dst_refs):
            rel, n_src = step - job.start, src_ref.shape[1]
            n_src_blk = job.src.shape[1] // job.tr

            @pl.when((rel >= 0) & (rel < n_src_blk))
            def _(src_ref=src_ref, dst_ref=dst_ref, n_src=n_src):
                dst_ref[:, :n_src] = src_ref[...].astype(dst_ref.dtype)
                if dst_ref.shape[1] > n_src:
                    dst_ref[:, n_src:] = jnp.zeros((dst_ref.shape[0], dst_ref.shape[1] - n_src), dst_ref.dtype)

            @pl.when((rel >= n_src_blk) & (rel < job.rows_out // job.tr))
            def _(dst_ref=dst_ref):
                dst_ref[...] = jnp.zeros_like(dst_ref)

    outs = pl.pallas_call(
        kernel,
        grid=grid,
        in_specs=list(in_specs) + cast_in,
        out_specs=out_specs + cast_out,
        out_shape=out_shape + cast_shape,
        scratch_shapes=list(scratch_shapes),
        compiler_params=_cp(*(["arbitrary"] * len(grid) if jobs else sem)),
        name=name,
    )(*args, *[job.src for job in jobs])
    return list(outs[:n_out]), list(outs[n_out:])


def _split_cols_kernel(w_ref, o_ref, *, split, second):
    n_in = w_ref.shape[1]
    o_ref[:, :split] = w_ref[:, :split].astype(o_ref.dtype)
    o_ref[:, split:second] = jnp.zeros((o_ref.shape[0], second - split), o_ref.dtype)
    o_ref[:, second:second + n_in - split] = w_ref[:, split:].astype(o_ref.dtype)
    tail = o_ref.shape[1] - (second + n_in - split)
    if tail:
        o_ref[:, second + n_in - split:] = jnp.zeros((o_ref.shape[0], tail), o_ref.dtype)


def _split_cols(w, split, second, n_out):
    k, n = w.shape
    tr = _pick(k, (256, 128, 64, 32, 16))
    return pl.pallas_call(
        functools.partial(_split_cols_kernel, split=split, second=second),
        grid=(k // tr,),
        in_specs=[pl.BlockSpec((tr, n), lambda i: (i, 0))],
        out_specs=pl.BlockSpec((tr, n_out), lambda i: (i, 0)),
        out_shape=jax.ShapeDtypeStruct((k, n_out), BF16),
        compiler_params=_cp("parallel"),
        name="split_cols",
    )(w)


def _mm_kernel(*refs, n_pairs, gated, tm):
    a_refs, w_refs = refs[:n_pairs], refs[n_pairs:2 * n_pairs]
    acc = _dot(a_refs[0][...], w_refs[0][...])
    for a_ref, w_ref in zip(a_refs[1:], w_refs[1:]):
        acc += _dot(a_ref[...], w_ref[...])
    if gated:
        res_ref, gate_ref, o_ref = refs[2 * n_pairs:]
        gate = _row_select(pl.program_id(0) * tm, tm, gate_ref[...])
        acc = res_ref[...] + gate * acc
    else:
        o_ref = refs[2 * n_pairs]
    o_ref[...] = acc.astype(o_ref.dtype)


def _mm_steps(n_rows, n, tn=None):
    return (n_rows // _pick(n_rows, (1088, 1024, 512, 256))) * (n // (tn or _pick(n, (1024, 512, 256, 128))))


def _mm(a_list, w, n_rows, *, n_cols=None, res=None, gate=None, tn=None, jobs=(), name="mm"):
    n_pairs = len(a_list)
    n = n_cols or w.shape[1]
    tm = _pick(n_rows, (1088, 1024, 512, 256))
    tn = tn or _pick(n, (1024, 512, 256, 128))
    in_specs = [pl.BlockSpec((tm, a.shape[1]), lambda i, j: (i, 0)) for a in a_list]
    in_specs += [pl.BlockSpec((a.shape[1], tn), lambda i, j, p=p: (p, j)) for p, a in enumerate(a_list)]
    assert all(a.shape[1] == a_list[0].shape[1] for a in a_list) and w.shape[0] == n_pairs * a_list[0].shape[1]
    args = list(a_list) + [w] * n_pairs
    if res is not None:
        table, layer, chunk = gate
        d = table.shape[2] // 6
        in_specs += [pl.BlockSpec((tm, tn), lambda i, j: (i, j)),
                     pl.BlockSpec((None, SUBLANE, tn), lambda i, j: (layer, 0, chunk * (d // tn) + j))]
        args += [res, table]
    (out,), casts = _hosted_call(
        functools.partial(_mm_kernel, n_pairs=n_pairs, gated=res is not None, tm=tm), jobs,
        grid=(n_rows // tm, n // tn),
        in_specs=in_specs,
        out_specs=[pl.BlockSpec((tm, tn), lambda i, j: (i, j))],
        out_shape=[jax.ShapeDtypeStruct((n_rows, n), F32)],
        args=args, sem=("parallel", "parallel"), name=name)
    return (out, casts) if jobs else out


def _mmk_kernel(a_ref, w_ref, res_ref, gate_ref, o_ref, acc_ref, *, tm):
    k = pl.program_id(2)

    @pl.when(k == 0)
    def _():
        acc_ref[...] = jnp.zeros_like(acc_ref)

    acc_ref[...] += _dot(a_ref[...], w_ref[...])

    @pl.when(k == pl.num_programs(2) - 1)
    def _():
        gate = _row_select(pl.program_id(0) * tm, tm, gate_ref[...])
        o_ref[...] = res_ref[...] + gate * acc_ref[...]


def _mm_ktiled_steps(kdim, n, n_rows):
    return ((n_rows // _pick(n_rows, (1088, 1024, 512, 256))) * (n // _pick(n, (1024, 512, 256, 128)))
            * (kdim // _pick(kdim, (2816, 2048, 1024, 512, 256))))


def _mm_ktiled(a, w, n_rows, res, gate, jobs=()):
    kdim, n = w.shape
    table, gate_layer, chunk = gate
    d = table.shape[2] // 6
    tm = _pick(n_rows, (1088, 1024, 512, 256))
    tn = _pick(n, (1024, 512, 256, 128))
    tk = _pick(kdim, (2816, 2048, 1024, 512, 256))
    (out,), casts = _hosted_call(
        functools.partial(_mmk_kernel, tm=tm), jobs,
        grid=(n_rows // tm, n // tn, kdim // tk),
        in_specs=[pl.BlockSpec((tm, tk), lambda i, j, k: (i, k)),
                  pl.BlockSpec((tk, tn), lambda i, j, k: (k, j)),
                  pl.BlockSpec((tm, tn), lambda i, j, k: (i, j)),
                  pl.BlockSpec((None, SUBLANE, tn), lambda i, j, k: (gate_layer, 0, chunk * (d // tn) + j))],
        out_specs=[pl.BlockSpec((tm, tn), lambda i, j, k: (i, j))],
        out_shape=[jax.ShapeDtypeStruct((n_rows, n), F32)],
        scratch_shapes=[pltpu.VMEM((tm, tn), F32)],
        args=(a, w, res, table), sem=("parallel", "parallel", "arbitrary"), name="ffn_down")
    return out, casts


def _ffn_up_kernel(u_ref, uh_ref, wg_ref, wu_ref, cw_ref, cb_ref, o_ref, halo_ref, *, tm):
    i = pl.program_id(1)

    @pl.when(i == 0)
    def _():
        halo_ref[...] = _dot(uh_ref[...], wg_ref[...])

    u = u_ref[...]
    g = _dot(u, wg_ref[...])
    dn, up = _shift_rows(g, halo_ref[pl.ds(2 * i, 1), :], halo_ref[pl.ds(2 * i + 1, 1), :], i * tm)
    a = dn * cw_ref[0:1, :] + g * cw_ref[1:2, :] + up * cw_ref[2:3, :] + cb_ref[...]
    o_ref[...] = (_silu(a) * _dot(u, wu_ref[...])).astype(o_ref.dtype)


def _ffn_up_steps(n_rows, n):
    return (n // _pick(n, (512, 256, 128))) * (n_rows // _pick(n_rows, (1088, 1024, 512, 256)))


def _ffn_up(u, n_rows, layer, wg, wu, cw_stack, cb_stack, jobs):
    d, n = wg.shape
    tm = _pick(n_rows, (1088, 1024, 512, 256))
    tn = _pick(n, (512, 256, 128))
    gm = n_rows // tm
    n_halo = _round_up(2 * gm, 2 * SUBLANE)
    rows = []
    for i in range(gm):
        rows += [max(i * tm - 1, 0), min((i + 1) * tm, n_rows - 1)]
    rows += [0] * (n_halo - len(rows))
    u_halo = jnp.concatenate([u[r:r + 1] for r in rows], axis=0)
    col = lambda j, i: (layer, 0, j)
    (hid,), casts = _hosted_call(
        functools.partial(_ffn_up_kernel, tm=tm), jobs,
        grid=(n // tn, gm),
        in_specs=[pl.BlockSpec((tm, d), lambda j, i: (i, 0)),
                  pl.BlockSpec((n_halo, d), lambda j, i: (0, 0)),
                  pl.BlockSpec((d, tn), lambda j, i: (0, j)),
                  pl.BlockSpec((d, tn), lambda j, i: (0, j)),
                  pl.BlockSpec((None, 3, tn), col),
                  pl.BlockSpec((None, 1, tn), col)],
        out_specs=[pl.BlockSpec((tm, tn), lambda j, i: (i, j))],
        out_shape=[jax.ShapeDtypeStruct((n_rows, n), BF16)],
        scratch_shapes=[pltpu.VMEM((n_halo, tn), F32)],
        args=(u, u_halo, wg, wu, cw_stack, cb_stack), sem=("parallel", "arbitrary"), name="ffn_up")
    return hid, casts


def _conv_ffn(h, n_rows, gain, mod_rows, mod, layer, wg, wu, cw, cb, wd_stack, next_w):
    d_ff_p = wg.shape[1]
    u = _modulate(h, n_rows, gain, mod_rows, layer, 3, 4)
    job = _CastJob(wd_stack, layer, d_ff_p, wd_stack.shape[2], 0, _ffn_up_steps(n_rows, d_ff_p))
    hid, (wd,) = _ffn_up(u, n_rows, layer, wg, wu, cw, cb, [job])
    jobs = []
    if next_w is not None:
        stack, idx = next_w
        jobs = [_CastJob(stack, idx, stack.shape[1], stack.shape[2], 0, _mm_ktiled_steps(d_ff_p, wd.shape[1], n_rows))]
    h, casts = _mm_ktiled(hid, wd, n_rows, h, (mod, layer, 5), jobs)
    return h, (casts[0] if casts else None)


def _rope(x, rope_ref):
    c, sa, sb = rope_ref[:, 0:LANE], rope_ref[:, LANE:2 * LANE], rope_ref[:, 2 * LANE:3 * LANE]
    quarter = MLA_ROPE // 4
    return x * c + pltpu.roll(x, LANE - quarter, 1) * sa + pltpu.roll(x, quarter, 1) * sb


def _rms(x, width):
    return lax.rsqrt(jnp.sum(x * x, axis=-1, keepdims=True) * (1.0 / width) + EPS)


def _qproj_kernel(cq_ref, ag_ref, w_ref, hg_ref, rope_ref, o_ref, *, heads):
    x = cq_ref[...]
    xn = ((x * _rms(x, x.shape[1])) * ag_ref[...]).astype(BF16)
    y = _dot(xn, w_ref[...])
    for hh in range(heads):
        yh = y[:, hh * QK_PAD:(hh + 1) * QK_PAD]
        yn = (yh * _rms(yh, MLA_NOPE + MLA_ROPE)) * hg_ref[...]
        o_ref[:, hh * QK_PAD:hh * QK_PAD + LANE] = yn[:, :LANE].astype(o_ref.dtype)
        o_ref[:, hh * QK_PAD + LANE:(hh + 1) * QK_PAD] = _rope(yn[:, LANE:], rope_ref).astype(o_ref.dtype)


def _qproj(y_in, n_rows, ag, w, hg, rope):
    heads = _pick(MLA_HEADS, (4, 2, 1))
    tm = _proj_rows()
    tn = heads * QK_PAD
    return pl.pallas_call(
        functools.partial(_qproj_kernel, heads=heads),
        grid=(n_rows // tm, w.shape[1] // tn),
        in_specs=[pl.BlockSpec((tm, MLA_Q_RANK), lambda i, j: (i, 0)),
                  pl.BlockSpec((1, MLA_Q_RANK), lambda i, j: (0, 0)),
                  pl.BlockSpec((MLA_Q_RANK, tn), lambda i, j: (0, j)),
                  pl.BlockSpec((1, QK_PAD), lambda i, j: (0, 0)),
                  _rope_spec()],
        out_specs=pl.BlockSpec((tm, tn), lambda i, j: (i, j)),
        out_shape=jax.ShapeDtypeStruct((n_rows, w.shape[1]), BF16),
        compiler_params=_cp("parallel", "parallel"),
        name="q_proj",
    )(y_in, ag, w, hg, rope)


def _kvproj_kernel(ckv_ref, kr_ref, ag_ref, w_ref, gn_ref, gr_ref, rope_ref, k_ref, v_ref, *, heads):
    x = ckv_ref[...]
    xn = ((x * _rms(x, x.shape[1])) * ag_ref[...]).astype(BF16)
    y = _dot(xn, w_ref[...])
    kr = kr_ref[...]
    kr_ss = jnp.sum(kr * kr, axis=-1, keepdims=True)
    width = MLA_NOPE + MLA_VDIM
    ones = jnp.ones((x.shape[0], V_PAD - MLA_VDIM), v_ref.dtype)
    for hh in range(heads):
        kn = y[:, hh * width:hh * width + MLA_NOPE]
        r = lax.rsqrt((jnp.sum(kn * kn, axis=-1, keepdims=True) + kr_ss) * (1.0 / (MLA_NOPE + MLA_ROPE)) + EPS)
        k_ref[:, hh * QK_PAD:hh * QK_PAD + LANE] = ((kn * r) * gn_ref[...]).astype(k_ref.dtype)
        k_ref[:, hh * QK_PAD + LANE:(hh + 1) * QK_PAD] = _rope((kr * r) * gr_ref[...], rope_ref).astype(k_ref.dtype)
        v_ref[:, hh * V_PAD:hh * V_PAD + MLA_VDIM] = y[:, hh * width + MLA_NOPE:(hh + 1) * width].astype(v_ref.dtype)
        v_ref[:, hh * V_PAD + MLA_VDIM:(hh + 1) * V_PAD] = ones


def _kvproj(y_in, n_rows, ag, w, gn, gr, rope):
    heads = _pick(MLA_HEADS, (4, 2, 1))
    assert MLA_NOPE == LANE and MLA_VDIM == LANE and MLA_Q_RANK % MLA_KV_RANK == 0
    tm = _proj_rows()
    tn = heads * (MLA_NOPE + MLA_VDIM)
    kr_blk = (MLA_Q_RANK + MLA_KV_RANK) // LANE
    return pl.pallas_call(
        functools.partial(_kvproj_kernel, heads=heads),
        grid=(n_rows // tm, w.shape[1] // tn),
        in_specs=[pl.BlockSpec((tm, MLA_KV_RANK), lambda i, j: (i, MLA_Q_RANK // MLA_KV_RANK)),
                  pl.BlockSpec((tm, LANE), lambda i, j: (i, kr_blk)),
                  pl.BlockSpec((1, MLA_KV_RANK), lambda i, j: (0, 0)),
                  pl.BlockSpec((MLA_KV_RANK, tn), lambda i, j: (0, j)),
                  pl.BlockSpec((1, LANE), lambda i, j: (0, 0)),
                  pl.BlockSpec((1, LANE), lambda i, j: (0, 0)),
                  _rope_spec()],
        out_specs=[pl.BlockSpec((tm, heads * QK_PAD), lambda i, j: (i, j)),
                   pl.BlockSpec((tm, heads * V_PAD), lambda i, j: (i, j))],
        out_shape=[jax.ShapeDtypeStruct((n_rows, MLA_HEADS * QK_PAD), BF16),
                   jax.ShapeDtypeStruct((n_rows, MLA_HEADS * V_PAD), BF16)],
        compiler_params=_cp("parallel", "parallel"),
        name="kv_proj",
    )(y_in, y_in, ag, w, gn, gr, rope)


def _attn_kernel(*refs, chunks, scale):
    q = refs[0][...]
    n_kv = (len(refs) - 2) // 2
    k_refs, v_refs, o_ref = refs[1:1 + n_kv], refs[1 + n_kv:1 + 2 * n_kv], refs[1 + 2 * n_kv]
    c = scale * math.log2(math.e)
    m = acc = None
    for idx, start, size in chunks:
        s = _dot_nt(q, k_refs[idx][start:start + size, :])
        m_blk = jnp.max(s, axis=-1, keepdims=True)
        m_new = m_blk if m is None else jnp.maximum(m, m_blk)
        pv = _dot(jnp.exp2((s - m_new) * c).astype(BF16), v_refs[idx][start:start + size, :])
        acc = pv if acc is None else jnp.exp2((m - m_new) * c) * acc + pv
        m = m_new
    o_ref[...] = (acc[:, :MLA_VDIM] / acc[:, MLA_VDIM:2 * MLA_VDIM]).astype(o_ref.dtype)


def _attn_lat_steps():
    return BATCH * MLA_HEADS * (SEQ // _pick(SEQ, (1024, 512, 256)))


def _attention(q, k, v, latent, n_out, jobs=(), o_prev=None):
    assert V_PAD == 2 * MLA_VDIM
    scale = (MLA_NOPE + MLA_ROPE) ** -0.5
    ctx_blk0 = _n_lat() // CTX_LEN
    ctx_k = pl.BlockSpec((CTX_LEN, QK_PAD), lambda b, h, i: (ctx_blk0 + b, h))
    ctx_v = pl.BlockSpec((CTX_LEN, V_PAD), lambda b, h, i: (ctx_blk0 + b, h))
    if latent:
        tq = _pick(SEQ, (1024, 512, 256))
        tk = _pick(SEQ, (1024, 512, 256))
        q_per_b = SEQ // tq
        q_spec = pl.BlockSpec((tq, QK_PAD), lambda b, h, i: (b * q_per_b + i, h))
        o_spec = pl.BlockSpec((tq, MLA_VDIM), lambda b, h, i: (b * q_per_b + i, h))
        k_specs = [ctx_k, pl.BlockSpec((SEQ, QK_PAD), lambda b, h, i: (b, h))]
        v_specs = [ctx_v, pl.BlockSpec((SEQ, V_PAD), lambda b, h, i: (b, h))]
        chunks = [(0, 0, CTX_LEN)] + [(1, s, tk) for s in range(0, SEQ, tk)]
        grid = (BATCH, MLA_HEADS, q_per_b)
    else:
        q_spec = pl.BlockSpec((CTX_LEN, QK_PAD), lambda b, h, i: (ctx_blk0 + b, h))
        o_spec = pl.BlockSpec((CTX_LEN, MLA_VDIM), lambda b, h, i: (ctx_blk0 + b, h))
        k_specs, v_specs, chunks = [ctx_k], [ctx_v], [(0, 0, CTX_LEN)]
        grid = (BATCH, MLA_HEADS, 1)
    n_kv = len(k_specs)
    in_specs = [q_spec] + k_specs + v_specs
    args = (q, *([k] * n_kv), *([v] * n_kv))
    out_shape = jax.ShapeDtypeStruct((n_out, MLA_HEADS * MLA_VDIM), BF16)
    body = functools.partial(_attn_kernel, chunks=tuple(chunks), scale=scale)
    if latent:
        (out,), casts = _hosted_call(body, jobs, grid=grid, in_specs=in_specs, out_specs=[o_spec],
                                     out_shape=[out_shape], args=args,
                                     sem=("parallel", "parallel", "arbitrary"), name="attn_lat")
        return out, casts
    return pl.pallas_call(
        lambda *refs: body(*refs[:len(in_specs)], refs[-1]),
        grid=grid,
        in_specs=in_specs + [pl.BlockSpec(memory_space=pl.ANY)],
        out_specs=o_spec,
        out_shape=out_shape,
        input_output_aliases={len(in_specs): 0},
        compiler_params=_cp("parallel", "parallel", "arbitrary"),
        name="attn_ctx",
    )(*args, o_prev)


def _sconv_kernel(b_ref, c_ref, x_ref, cp_ref, xp_ref, cn_ref, xn_ref, w_ref, o_ref, *, tm):
    p = c_ref[...] * x_ref[...]
    prev = cp_ref[SUBLANE - 1:SUBLANE, :] * xp_ref[SUBLANE - 1:SUBLANE, :]
    nxt = cn_ref[0:1, :] * xn_ref[0:1, :]
    dn, up = _shift_rows(p, prev, nxt, pl.program_id(0) * tm)
    conv = dn * w_ref[0:1, :] + p * w_ref[1:2, :] + up * w_ref[2:3, :]
    o_ref[...] = (b_ref[...] * conv).astype(o_ref.dtype)


def _sconv(y_in, n_rows, w, col0, width):
    tm = ROW_BLK
    tn = _pick(width, (1024, 512, 256, 128))
    blk0 = [(col0 + k * width) // tn for k in range(3)]
    main = [pl.BlockSpec((tm, tn), lambda i, j, o=o: (i, o + j)) for o in blk0]
    cp, cn = _halo_specs(tm, tn, n_rows, blk0[1])
    xp, xn = _halo_specs(tm, tn, n_rows, blk0[2])
    return pl.pallas_call(
        functools.partial(_sconv_kernel, tm=tm),
        grid=(n_rows // tm, width // tn),
        in_specs=main + [cp, xp, cn, xn, pl.BlockSpec((3, tn), lambda i, j: (0, j))],
        out_specs=pl.BlockSpec((tm, tn), lambda i, j: (i, j)),
        out_shape=jax.ShapeDtypeStruct((n_rows, width), BF16),
        compiler_params=_cp("parallel", "parallel"),
        name="short_conv",
    )(*([y_in] * 7), w)


def _conv_silu(x_ref, xp_ref, xn_ref, w_ref, row0):
    x = x_ref[...]
    dn, up = _shift_rows(x, xp_ref[SUBLANE - 1:SUBLANE, :], xn_ref[0:1, :], row0)
    return _silu(dn * w_ref[0:1, :] + x * w_ref[1:2, :] + up * w_ref[2:3, :])


def _l2norm(x):
    return x * lax.rsqrt(jnp.sum(x * x, axis=-1, keepdims=True) + EPS)


def _gdn_gate_kernel(ba_ref, alog_ref, dtb_ref, o_ref):
    x = ba_ref[...]
    lane = lax.broadcasted_iota(jnp.int32, x.shape, 1)
    z = x + dtb_ref[...]
    softplus = jnp.maximum(z, 0.0) + jnp.log(1.0 + jnp.exp(-jnp.abs(z)))
    o_ref[...] = jnp.where(lane < x.shape[1] // 2, _sigmoid(x), -jnp.exp(alog_ref[...]) * softplus)


def _gdn_gate(ba, n_rows, alog, dtb):
    w = ba.shape[1]
    return pl.pallas_call(
        _gdn_gate_kernel,
        grid=(n_rows // ROW_BLK,),
        in_specs=[pl.BlockSpec((ROW_BLK, w), lambda i: (i, 0)),
                  pl.BlockSpec((1, w), lambda i: (0, 0)),
                  pl.BlockSpec((1, w), lambda i: (0, 0))],
        out_specs=pl.BlockSpec((ROW_BLK, w), lambda i: (i, 0)),
        out_shape=jax.ShapeDtypeStruct((n_rows, w), F32),
        compiler_params=_cp("parallel"),
        name="gdn_gate",
    )(ba, alog, dtb)


def _active_rows(x, size, odd):
    n = x.shape[0]
    return jnp.concatenate([x[(2 * b + odd) * size:(2 * b + odd + 1) * size] for b in range(n // (2 * size))], axis=0)


def _weave_rows(rest, active, size, odd):
    pieces = []
    for b in range(active.shape[0] // size):
        if rest is None:
            keep = jnp.zeros((size, active.shape[1]), active.dtype)
        else:
            keep = rest[(2 * b + 1 - odd) * size:(2 * b + 2 - odd) * size]
        act = active[b * size:(b + 1) * size]
        pieces += [keep, act] if odd else [act, keep]
    return jnp.concatenate(pieces, axis=0)


def _unit_tri_inverses(l_mats, dirs, eye, base, ring_ref):
    size = SUBLANE
    xs = [-(l * base) for l in l_mats]
    ts = [eye + x for x in xs]
    for _ in range(size.bit_length() - 2):
        xs = [_dot(x.astype(BF16), x.astype(BF16)) for x in xs]
        ts = [t + _dot(t.astype(BF16), x.astype(BF16)) for t, x in zip(ts, xs)]
    level = 0
    while size < GDN_BLK:
        tbs = [t.astype(BF16) for t in ts]
        offs = [(_active_rows(l, size, 1 - d) * ring_ref[level, d]).astype(BF16) for l, d in zip(l_mats, dirs)]
        mids = [_weave_rows(None, _dot(off, tb), size, 1 - d).astype(BF16) for off, tb, d in zip(offs, tbs, dirs)]
        acts = [_active_rows(t, size, 1 - d) for t, d in zip(ts, dirs)]
        news = [act - _dot(act.astype(BF16), mid) for act, mid in zip(acts, mids)]
        ts = [_weave_rows(t, new, size, 1 - d) for t, new, d in zip(ts, news, dirs)]
        size *= 2
        level += 1
    return ts


def _gdn_masks():
    n, size = GDN_BLK, SUBLANE
    i, j = np.arange(n)[:, None], np.arange(n)[None, :]
    tri = np.stack([j <= i, j >= i, j == i, i // size == j // size]).astype(np.float32)
    rings = []
    while size < n:
        ra = np.arange(n // 2)[:, None]
        per_dir = []
        for d in range(2):
            ia = (ra // size) * 2 * size + (1 - d) * size + ra % size
            per_dir.append((ia // (2 * size) == j // (2 * size)) & (ia // size != j // size))
        rings.append(np.stack(per_dir))
        size *= 2
    return jnp.asarray(tri), jnp.asarray(np.stack(rings).astype(np.float32))


def _gdn_prep_kernel(q_ref, qp_ref, qn_ref, k_ref, kp_ref, kn_ref, v_ref, vp_ref, vn_ref, cwq_ref, cwk_ref, cwv_ref,
                     bg_ref, tri_ref, tri16_ref, ring_ref, u_ref, wq_ref, kd_ref, in_ref, cd_ref, *, rep, hpb):
    n = GDN_BLK
    row0 = pl.program_id(0) * n
    q_all = _conv_silu(q_ref, qp_ref, qn_ref, cwq_ref, row0)
    k_all = _conv_silu(k_ref, kp_ref, kn_ref, cwk_ref, row0)
    v_all = _conv_silu(v_ref, vp_ref, vn_ref, cwv_ref, row0)
    eye, base = tri_ref[2], tri_ref[3]
    not_eye = 1.0 - eye
    chains, l_mats, rhss = [], [], []
    for hh in range(hpb):
        q = _l2norm(q_all[:, hh * GDN_DK:(hh + 1) * GDN_DK]) * GDN_DK ** -0.5
        k = _l2norm(k_all[:, hh * GDN_DK:(hh + 1) * GDN_DK])
        q16, k16 = q.astype(BF16), k.astype(BF16)
        qk = _dot_nt(q16, k16)
        kk = _dot_nt(k16, k16) * not_eye
        bg = bg_ref[hh]
        bg_cols = jnp.concatenate([bg, jnp.zeros((LANE - bg.shape[0], n), F32)], axis=0).T
        hi = bg_cols.astype(BF16)
        rest = bg_cols - hi.astype(F32)
        mid = rest.astype(BF16)
        pieces = jnp.concatenate([hi, mid, (rest - mid.astype(F32)).astype(BF16)], axis=1)
        cums = [_dot(tri16_ref[d], pieces) for d in range(2)]
        gc_cols = [c[:, :LANE] + c[:, LANE:2 * LANE] + c[:, 2 * LANE:] for c in cums]
        gc_rows = [g.T for g in gc_cols]
        for e in range(rep):
            for d in range(2):
                vh = hh * rep + e
                cols = slice(vh * GDN_DV, (vh + 1) * GDN_DV)
                incl = tri_ref[d]
                b_idx, g_idx = d * rep + e, (2 + d) * rep + e
                b_col = bg_cols[:, b_idx:b_idx + 1]
                gc_col, gc_row = gc_cols[d][:, g_idx:g_idx + 1], gc_rows[d][g_idx:g_idx + 1, :]
                g_tot = jnp.sum(bg[g_idx:g_idx + 1, :], axis=1, keepdims=True)
                decay = jnp.exp(jnp.minimum(gc_col - gc_row, 0.0)) * incl
                e_col = jnp.exp(gc_col)
                chains.append((vh, d, cols))
                l_mats.append((kk * b_col) * decay)
                rhss.append(jnp.concatenate([v_all[:, cols] * b_col, (k * b_col) * e_col], axis=1).astype(BF16))
                wq_ref[d, n:, cols] = (q * e_col).astype(wq_ref.dtype)
                kd_ref[d, :, cols] = (k * jnp.exp(g_tot - gc_col)).astype(kd_ref.dtype)
                in_ref[d, vh] = (qk * decay).astype(in_ref.dtype)
                cd_ref[d, vh] = jnp.broadcast_to(jnp.exp(g_tot), (SUBLANE, LANE))
    invs = _unit_tri_inverses(l_mats, [d for _, d, _ in chains], eye, base, ring_ref)
    sols = [_dot(t.astype(BF16), rhs) for t, rhs in zip(invs, rhss)]
    for (vh, d, cols), sol in zip(chains, sols):
        u_ref[d, :, cols] = sol[:, :GDN_DV]
        wq_ref[d, :n, cols] = sol[:, GDN_DV:].astype(wq_ref.dtype)


def _gdn_prep(y_in, conv_w, bg_t, n_rows):
    assert GDN_DK == LANE and GDN_DV == LANE and GDN_BLK == 2 * LANE
    nb, hv, hq, rep = n_rows // GDN_BLK, GDN_V_HEADS, GDN_QK_HEADS, GDN_V_HEADS // GDN_QK_HEADS
    hpb = _pick(hq, (2, 1))
    key_w, wide = hpb * GDN_DK, hpb * rep * GDN_DV
    parts = ((key_w, 0), (key_w, hq // hpb), (wide, 2 * hq * GDN_DK // wide))
    in_specs = []
    for tn, blk0 in parts:
        in_specs += [pl.BlockSpec((GDN_BLK, tn), lambda i, h, blk0=blk0: (i, blk0 + h))]
        in_specs += _halo_specs(GDN_BLK, tn, n_rows, blk0)
    in_specs += [pl.BlockSpec((3, tn), lambda i, h, blk0=blk0: (0, blk0 + h)) for tn, blk0 in parts]
    tri, rings = _gdn_masks()
    in_specs += [pl.BlockSpec((hpb, bg_t.shape[1], GDN_BLK), lambda i, h: (h, 0, i)),
                 pl.BlockSpec(tri.shape, lambda i, h: (0, 0, 0)),
                 pl.BlockSpec((2,) + tri.shape[1:], lambda i, h: (0, 0, 0)),
                 pl.BlockSpec(rings.shape, lambda i, h: (0, 0, 0, 0))]
    return pl.pallas_call(
        functools.partial(_gdn_prep_kernel, rep=rep, hpb=hpb),
        grid=(nb, hq // hpb),
        in_specs=in_specs,
        out_specs=[pl.BlockSpec((2, GDN_BLK, wide), lambda i, h: (0, i, h)),
                   pl.BlockSpec((2, None, 2 * GDN_BLK, wide), lambda i, h: (0, i, 0, h)),
                   pl.BlockSpec((2, GDN_BLK, wide), lambda i, h: (0, i, h)),
                   pl.BlockSpec((2, hpb * rep, GDN_BLK, GDN_BLK), lambda i, h: (0, h, i, 0)),
                   pl.BlockSpec((2, hpb * rep, SUBLANE, LANE), lambda i, h: (0, h, i, 0))],
        out_shape=[jax.ShapeDtypeStruct((2, n_rows, hv * GDN_DV), F32),
                   jax.ShapeDtypeStruct((2, nb, 2 * GDN_BLK, hv * GDN_DV), BF16),
                   jax.ShapeDtypeStruct((2, n_rows, hv * GDN_DV), BF16),
                   jax.ShapeDtypeStruct((2, hv, n_rows, GDN_BLK), BF16),
                   jax.ShapeDtypeStruct((2, hv, nb * SUBLANE, LANE), F32)],
        compiler_params=_cp("parallel", "parallel"),
        name="gdn_prep",
    )(*([y_in] * 9), conv_w, conv_w, conv_w, bg_t, tri, tri[:2].astype(BF16), rings)


def _gdn_scan_kernel(*refs, heads):
    ins, (of_ref, ob_ref, s_ref) = refs[:10], refs[10:]
    n = GDN_BLK

    @pl.when(pl.program_id(2) == 0)
    def _():
        s_ref[...] = jnp.zeros_like(s_ref)

    o_refs = (of_ref, ob_ref)
    chains = [(d, hh, slice(hh * GDN_DV, (hh + 1) * GDN_DV)) for d in range(2) for hh in range(heads)]
    u_refs, wq_refs, kd_refs, in_refs, cd_refs = (ins[0::5], ins[1::5], ins[2::5], ins[3::5], ins[4::5])
    states = [s_ref[d, hh] for d, hh, _ in chains]
    wqs = [_dot(wq_refs[d][:, cols], s.astype(BF16)) for (d, hh, cols), s in zip(chains, states)]
    vbs = [(u_refs[d][:, cols] - wq[:n]).astype(BF16) for (d, hh, cols), wq in zip(chains, wqs)]
    for (d, hh, cols), s, wq, vb in zip(chains, states, wqs, vbs):
        o_refs[d][:, cols] = wq[n:] + _dot(in_refs[d][hh], vb)
        s_ref[d, hh] = s * cd_refs[d][hh, 0:1, :] + _dot_tn(kd_refs[d][:, cols], vb)


def _gdn_scan(prep, n_rows):
    hv = GDN_V_HEADS
    heads = _pick(hv, (4, 2, 1))
    lat_blk, ctx_blk = SEQ // GDN_BLK, CTX_LEN // GDN_BLK
    steps = ctx_blk + lat_blk
    ctx0 = _n_lat() // GDN_BLK
    wide = heads * GDN_DV

    def blk_f(b, n):
        return jnp.where(n < ctx_blk, ctx0 + b * ctx_blk + n, b * lat_blk + n - ctx_blk)

    def blk_b(b, n):
        return jnp.where(n < ctx_blk, ctx0 + b * ctx_blk + ctx_blk - 1 - n, b * lat_blk + steps - 1 - n)

    in_specs = []
    for d, blk in enumerate((blk_f, blk_b)):
        rows = pl.BlockSpec((None, GDN_BLK, wide), lambda b, h, n, d=d, blk=blk: (d, blk(b, n), h))
        in_specs += [rows,
                     pl.BlockSpec((None, None, 2 * GDN_BLK, wide), lambda b, h, n, d=d, blk=blk: (d, blk(b, n), 0, h)),
                     rows,
                     pl.BlockSpec((None, heads, GDN_BLK, GDN_BLK), lambda b, h, n, d=d, blk=blk: (d, h, blk(b, n), 0)),
                     pl.BlockSpec((None, heads, SUBLANE, LANE), lambda b, h, n, d=d, blk=blk: (d, h, blk(b, n), 0))]
    out = jax.ShapeDtypeStruct((n_rows, hv * GDN_DV), F32)
    return pl.pallas_call(
        functools.partial(_gdn_scan_kernel, heads=heads),
        grid=(BATCH, hv // heads, steps),
        in_specs=in_specs,
        out_specs=[pl.BlockSpec((GDN_BLK, wide), lambda b, h, n: (blk_f(b, n), h)),
                   pl.BlockSpec((GDN_BLK, wide), lambda b, h, n: (blk_b(b, n), h))],
        out_shape=[out, out],
        scratch_shapes=[pltpu.VMEM((2, heads, GDN_DK, GDN_DV), F32)],
        compiler_params=_cp("parallel", "parallel", "arbitrary"),
        name="gdn_scan",
    )(*prep, *prep)


def _gdn_readout_kernel(of_ref, ob_ref, z_ref, g_ref, o_ref):
    o = of_ref[...] + ob_ref[...]
    z = z_ref[...]
    for hh in range(o.shape[1] // GDN_DV):
        cols = slice(hh * GDN_DV, (hh + 1) * GDN_DV)
        oh = o[:, cols]
        y = (oh * _rms(oh, GDN_DV)) * g_ref[...]
        o_ref[:, cols] = (y * _silu(z[:, cols])).astype(o_ref.dtype)


def _gdn_readout(o_f, o_b, y_in, z_col0, gain, n_rows):
    width = GDN_V_HEADS * GDN_DV
    tn = _pick(width, (1024, 512, 256, 128))
    spec = pl.BlockSpec((ROW_BLK, tn), lambda i, j: (i, j))
    return pl.pallas_call(
        _gdn_readout_kernel,
        grid=(n_rows // ROW_BLK, width // tn),
        in_specs=[spec, spec, pl.BlockSpec((ROW_BLK, tn), lambda i, j: (i, z_col0 // tn + j)),
                  pl.BlockSpec((1, GDN_DV), lambda i, j: (0, 0))],
        out_specs=spec,
        out_shape=jax.ShapeDtypeStruct((n_rows, width), BF16),
        compiler_params=_cp("parallel", "parallel"),
        name="gdn_readout",
    )(o_f, o_b, y_in, gain.reshape(1, GDN_DV))


def _proj_rows():
    return _pick(math.gcd(SEQ, BATCH * CTX_LEN), (512, 256))


def _rope_spec():
    tm = _proj_rows()
    return pl.BlockSpec((tm, 3 * LANE), lambda i, j: (jnp.where(i < _n_lat() // tm, i % (SEQ // tm), SEQ // tm), 0))


def _rope_table(n_ident):
    n_freq = MLA_ROPE // 4
    inv_freq = ROPE_THETA ** (-jnp.arange(n_freq, dtype=F32) / n_freq)
    t = jnp.arange(SEQ)
    ang = jnp.concatenate([(t // GRID_W).astype(F32)[:, None] * inv_freq,
                           (t % GRID_W).astype(F32)[:, None] * inv_freq], axis=1)
    cos, sin = jnp.cos(ang), jnp.sin(ang)
    zero = jnp.zeros_like(sin[:, :n_freq])
    pad = jnp.zeros((SEQ, LANE - MLA_ROPE), F32)
    c = jnp.concatenate([cos[:, :n_freq], cos[:, :n_freq], cos[:, n_freq:], cos[:, n_freq:], pad + 1.0], axis=1)
    sa = jnp.concatenate([-sin[:, :n_freq], zero, -sin[:, n_freq:], zero, pad], axis=1)
    sb = jnp.concatenate([zero, sin[:, :n_freq], zero, sin[:, n_freq:], pad], axis=1)
    ident = jnp.concatenate([jnp.ones((n_ident, LANE), F32), jnp.zeros((n_ident, 2 * LANE), F32)], axis=1)
    return jnp.concatenate([jnp.concatenate([c, sa, sb], axis=1), ident], axis=0)


def _pad_cols(w, n):
    return jnp.pad(w, ((0, 0), (0, n - w.shape[1])))


def _even_layer(h, n_rows, need_ctx, mod, mod_rows, layer, gain, p, idx, ffn):
    w_in, q_a_g, kv_a_g, w_qb, w_kvb, q_g, k_g, sc_w, w_out_stack = p
    ffn_w_gate, ffn_w_up, d_ff_p = ffn
    d = h.shape[1]
    sc_width = d - MLA_HEADS * MLA_VDIM
    assert sc_width == MLA_HEADS * MLA_VDIM
    head_w = MLA_NOPE + MLA_ROPE
    c0 = MLA_Q_RANK + MLA_KV_RANK
    conv0 = _round_up(c0 + LANE, _pick(sc_width, (1024, 512, 256, 128)))
    n_in = _round_up(conv0 + 3 * sc_width, 1024)
    w_in_p = _split_cols(w_in, c0 + MLA_ROPE, conv0, n_in)
    w_q = w_qb.reshape(MLA_Q_RANK, MLA_HEADS, head_w)
    w_q = jnp.pad(w_q, ((0, 0), (0, 0), (0, QK_PAD - head_w))).reshape(MLA_Q_RANK, MLA_HEADS * QK_PAD).astype(BF16)
    rope = _rope_table(_proj_rows())

    u = _modulate(h, n_rows, gain, mod_rows, layer, 0, 1)
    y = _mm([u], w_in_p, n_rows, name="even_in")
    q = _qproj(y, n_rows, q_a_g.reshape(1, -1), w_q, jnp.pad(q_g, (0, QK_PAD - head_w)).reshape(1, QK_PAD), rope)
    k, v = _kvproj(y, n_rows, kv_a_g.reshape(1, -1), w_kvb.astype(BF16), k_g[:MLA_NOPE].reshape(1, LANE),
                   jnp.pad(k_g[MLA_NOPE:], (0, LANE - MLA_ROPE)).reshape(1, LANE), rope)
    steps = _attn_lat_steps()
    jobs = [_CastJob(ffn_w_gate, layer, d, d_ff_p, 0, steps // 2),
            _CastJob(ffn_w_up, layer, d, d_ff_p, steps // 2, steps - steps // 2),
            _CastJob(w_out_stack, idx, w_out_stack.shape[1], d, 0, steps)]
    n_out = n_rows if need_ctx else _n_lat()
    o, (wg, wu, w_out_b) = _attention(q, k, v, True, n_out, jobs)
    if need_ctx:
        o = _attention(q, k, v, False, n_out, o_prev=o)
    conv = _sconv(y, n_out, sc_w, conv0, sc_width)
    h = _mm([o, conv], w_out_b, n_out, res=h, gate=(mod, layer, 2),
            tn=_pick(d, (512, 256, 128)), name="even_out")
    return h, (wg, wu)


def _odd_layer(h, n_rows, need_ctx, mod, mod_rows, layer, gain, p, idx, ffn, w_in_b):
    w_in_stack, conv_w, a_log, dt_bias, o_g, w_out_stack = p
    ffn_w_gate, ffn_w_up, d_ff_p = ffn
    d = h.shape[1]
    hv, hq = GDN_V_HEADS, GDN_QK_HEADS
    qkv_w = 2 * hq * GDN_DK + hv * GDN_DV
    main_w = qkv_w + hv * GDN_DV
    gate_w = _round_up(4 * hv, LANE)
    half = gate_w // 2
    if w_in_b is None:
        w_in_b = w_in_stack[idx].astype(BF16)
    w_ba = jnp.concatenate([_pad_cols(w_in_b[:, main_w:main_w + 2 * hv], half),
                            _pad_cols(w_in_b[:, main_w + 2 * hv:], half)], axis=1)
    zeros = jnp.zeros((1, half), F32)
    alog = jnp.concatenate([zeros, _pad_cols(a_log.reshape(1, -1), half)], axis=1)
    dtb = jnp.concatenate([zeros, _pad_cols(dt_bias.reshape(1, -1), half)], axis=1)

    u = _modulate(h, n_rows, gain, mod_rows, layer, 0, 1)
    tn = _pick(main_w, (512, 256, 128))
    steps = _mm_steps(n_rows, main_w, tn)
    y, (wg, w_out_b) = _mm([u], w_in_b, n_rows, n_cols=main_w, tn=tn,
                           jobs=[_CastJob(ffn_w_gate, layer, d, d_ff_p, 0, steps),
                                 _CastJob(w_out_stack, idx, w_out_stack.shape[1], d, 0, steps)], name="odd_in")
    ba = _mm([u], w_ba, n_rows, name="odd_in_gates")
    bg = _gdn_gate(ba, n_rows, alog, dtb)
    rep = hv // hq
    bg_t = jnp.stack([bg[:, :2 * hv], bg[:, half:half + 2 * hv]], axis=0)
    bg_t = bg_t.reshape(2, n_rows, 2, hq, rep).transpose(3, 0, 2, 4, 1).reshape(hq, 4 * rep, n_rows)
    bg_t = jnp.pad(bg_t, ((0, 0), (0, _round_up(4 * rep, SUBLANE) - 4 * rep), (0, 0)))
    o_f, o_b = _gdn_scan(_gdn_prep(y, conv_w, bg_t, n_rows), n_rows)
    n_out = n_rows if need_ctx else _n_lat()
    yo = _gdn_readout(o_f, o_b, y, qkv_w, o_g, n_out)
    tn = _pick(d, (512, 256, 128))
    h, (wu,) = _mm([yo], w_out_b, n_out, res=h, gate=(mod, layer, 2), tn=tn,
                   jobs=[_CastJob(ffn_w_up, layer, d, d_ff_p, 0, _mm_steps(n_out, d, tn))], name="odd_out")
    return h, (wg, wu)


def kernel(x, c, ctx, c_ctx, ada_w, ada_b, norm_mix, norm_ffn, ffn_w_gate, ffn_w_up, ffn_conv_w, ffn_conv_b,
           ffn_w_down, a_w_in, a_q_a_norm, a_kv_a_norm, a_w_qb, a_w_kvb, a_q_norm, a_k_norm, a_sc_conv, a_w_out,
           c_w_in, c_conv_w, c_a_log, c_dt_bias, c_o_norm, c_w_out):
    bn, t, d = x.shape
    depth = ada_w.shape[0]
    assert (bn, t, ctx.shape[1], ffn_w_gate.shape[2]) == (BATCH, SEQ, CTX_LEN, D_FF)
    assert CTX_LEN % GDN_BLK == 0 and SEQ % GDN_BLK == 0
    h = jnp.concatenate([x.reshape(bn * t, d), ctx.reshape(bn * CTX_LEN, d)], axis=0)
    cond8 = jnp.concatenate([c, c_ctx[None], jnp.zeros((SUBLANE - bn - 1, d), F32)], axis=0)
    mod = _ada(cond8, ada_w, ada_b)
    mod_rows = mod.reshape(depth * SUBLANE * 6, 1, d)
    d_ff_p = _round_up(D_FF, 512)
    ff_pad = ((0, 0), (0, 0), (0, d_ff_p - D_FF))
    conv_w = jnp.pad(ffn_conv_w, ff_pad)
    conv_b = jnp.pad(ffn_conv_b.reshape(depth, 1, D_FF), ff_pad)
    ffn = (ffn_w_gate, ffn_w_up, d_ff_p)
    w_in_b = None
    for l in range(depth):
        last = l == depth - 1
        n_rows = h.shape[0]
        i = l // 2
        if l % 2 == 0:
            h, (w_gate, w_up) = _even_layer(
                h, n_rows, not last, mod, mod_rows, l, norm_mix[l],
                (a_w_in[i], a_q_a_norm[i], a_kv_a_norm[i], a_w_qb[i], a_w_kvb[i], a_q_norm[i], a_k_norm[i],
                 a_sc_conv[i], a_w_out), i, ffn)
        else:
            h, (w_gate, w_up) = _odd_layer(
                h, n_rows, not last, mod, mod_rows, l, norm_mix[l],
                (c_w_in, c_conv_w[i], c_a_log[i], c_dt_bias[i], c_o_norm[i], c_w_out), i, ffn, w_in_b)
        next_w = (c_w_in, (l + 1) // 2) if (l + 1 < depth and (l + 1) % 2 == 1) else None
        h, w_in_b = _conv_ffn(h, h.shape[0], norm_ffn[l], mod_rows, mod, l,
                              w_gate, w_up, conv_w, conv_b, ffn_w_down, next_w)
    return h[:bn * t].reshape(bn, t, d)
```

```python
import functools
import math
from typing import NamedTuple

import jax
import jax.numpy as jnp
import numpy as np
from jax import lax
from jax.experimental import pallas as pl
from jax.experimental.pallas import tpu as pltpu

F32 = jnp.float32
BF16 = jnp.bfloat16

BATCH = 2
SEQ = 4096
GRID_W = 64
CTX_LEN = 256
EPS = 1e-6
D_FF = 11008

MLA_HEADS = 16
MLA_NOPE = 128
MLA_ROPE = 64
MLA_VDIM = 128
MLA_Q_RANK = 1024
MLA_KV_RANK = 512
ROPE_THETA = 10000.0

GDN_QK_HEADS = 16
GDN_V_HEADS = 32
GDN_DK = 128
GDN_DV = 128

LANE = 128
SUBLANE = 8
QK_PAD = 256
V_PAD = 256
ROW_BLK = 256
GDN_BLK = 256
VMEM_LIMIT = 56 * 2**20


def _cp(*sem, vmem=VMEM_LIMIT):
    return pltpu.CompilerParams(dimension_semantics=sem, vmem_limit_bytes=vmem)


def _pick(n, prefs):
    for p in prefs:
        if n % p == 0:
            return p
    raise ValueError(f"no tile for {n} in {prefs}")


def _round_up(n, m):
    return (n + m - 1) // m * m


def _sigmoid(x):
    return 1.0 / (1.0 + jnp.exp(-x))


def _silu(x):
    return x * _sigmoid(x)


def _dot(a, b):
    return jnp.dot(a, b, preferred_element_type=F32)


def _dot_nt(a, b):
    return lax.dot_general(a, b, (((1,), (1,)), ((), ())), preferred_element_type=F32)


def _dot_tn(a, b):
    return lax.dot_general(a, b, (((0,), (0,)), ((), ())), preferred_element_type=F32)


def _n_lat():
    return BATCH * SEQ


def _seq_edges(row0, tm):
    assert SEQ & (SEQ - 1) == 0 and CTX_LEN & (CTX_LEN - 1) == 0 and SEQ % CTX_LEN == 0
    r = row0 + lax.broadcasted_iota(jnp.int32, (tm, 1), 0)
    is_ctx = r >= _n_lat()
    first = (jnp.bitwise_and(r, CTX_LEN - 1) == 0) & ((jnp.bitwise_and(r, SEQ - 1) == 0) | is_ctx)
    r1 = r + 1
    last = (jnp.bitwise_and(r1, CTX_LEN - 1) == 0) & ((jnp.bitwise_and(r1, SEQ - 1) == 0) | is_ctx)
    return first, last


def _shift_rows(x, prev_row, next_row, row0):
    tm = x.shape[0]
    ridx = lax.broadcasted_iota(jnp.int32, (tm, 1), 0)
    first, last = _seq_edges(row0, tm)
    dn = jnp.where(ridx == 0, prev_row, pltpu.roll(x, 1, 0))
    dn = jnp.where(first, 0.0, dn)
    up = jnp.where(ridx == tm - 1, next_row, pltpu.roll(x, tm - 1, 0))
    up = jnp.where(last, 0.0, up)
    return dn, up


def _row_select(row0, tm, table):
    r = row0 + lax.broadcasted_iota(jnp.int32, (tm, 1), 0)
    out = table[BATCH:BATCH + 1]
    for b in reversed(range(BATCH)):
        out = jnp.where(r < (b + 1) * SEQ, table[b:b + 1], out)
    return out


def _halo_specs(tm, tn, n_rows, col_blk0):
    tmb, last = tm // SUBLANE, n_rows // SUBLANE - 1
    prev = pl.BlockSpec((SUBLANE, tn), lambda i, j: (jnp.maximum(i * tmb - 1, 0), col_blk0 + j))
    nxt = pl.BlockSpec((SUBLANE, tn), lambda i, j: (jnp.minimum((i + 1) * tmb, last), col_blk0 + j))
    return prev, nxt


def _ada_kernel(cond_ref, w_ref, b_ref, o_ref):
    a = _silu(cond_ref[...]).astype(BF16)
    o_ref[...] = _dot(a, w_ref[...].astype(BF16)) + b_ref[...]


def _ada(cond8, ada_w, ada_b):
    n_layer, d, n = ada_w.shape
    tn = _pick(n, (512, 256, 128))
    return pl.pallas_call(
        _ada_kernel,
        grid=(n_layer, n // tn),
        in_specs=[pl.BlockSpec((SUBLANE, d), lambda l, j: (0, 0)),
                  pl.BlockSpec((None, d, tn), lambda l, j: (l, 0, j)),
                  pl.BlockSpec((None, 1, tn), lambda l, j: (l, 0, j))],
        out_specs=pl.BlockSpec((None, SUBLANE, tn), lambda l, j: (l, 0, j)),
        out_shape=jax.ShapeDtypeStruct((n_layer, SUBLANE, n), F32),
        compiler_params=_cp("parallel", "parallel"),
        name="ada",
    )(cond8, ada_w, ada_b.reshape(n_layer, 1, n))


def _modulate_kernel(h_ref, g_ref, sh_ref, sc_ref, o_ref):
    x = h_ref[...]
    y = x * lax.rsqrt(jnp.mean(x * x, axis=-1, keepdims=True) + EPS)
    o_ref[...] = ((y * g_ref[...]) * (1.0 + sc_ref[...]) + sh_ref[...]).astype(o_ref.dtype)


def _modulate_join_kernel(lat_ref, ctx_ref, g_ref, sh_ref, sc_ref, o_ref, h_ref, *, lat_blocks):
    def run(src_ref):
        h_ref[...] = src_ref[...]
        _modulate_kernel(src_ref, g_ref, sh_ref, sc_ref, o_ref)

    pl.when(pl.program_id(0) < lat_blocks)(lambda: run(lat_ref))
    pl.when(pl.program_id(0) >= lat_blocks)(lambda: run(ctx_ref))


def _modulate(h, n_rows, gain, mod_rows, layer, k_shift, k_scale):
    joined = isinstance(h, tuple)
    d = h[0].shape[1] if joined else h.shape[1]
    blk_per_seq = SEQ // ROW_BLK

    def mod_spec(k):
        return pl.BlockSpec(
            (None, 1, d),
            lambda i: ((layer * SUBLANE + jnp.minimum(i // blk_per_seq, BATCH)) * 6 + k, 0, 0))

    if joined:
        lat_blocks, ctx_blocks = h[0].shape[0] // ROW_BLK, h[1].shape[0] // ROW_BLK
        row_spec = pl.BlockSpec((ROW_BLK, d), lambda i: (i, 0))
        return pl.pallas_call(
            functools.partial(_modulate_join_kernel, lat_blocks=lat_blocks),
            grid=(lat_blocks + ctx_blocks,),
            in_specs=[pl.BlockSpec((ROW_BLK, d), lambda i: (jnp.minimum(i, lat_blocks - 1), 0)),
                      pl.BlockSpec((ROW_BLK, d), lambda i: (jnp.maximum(i - lat_blocks, 0), 0)),
                      pl.BlockSpec((1, d), lambda i: (0, 0)),
                      mod_spec(k_shift), mod_spec(k_scale)],
            out_specs=[row_spec, row_spec],
            out_shape=[jax.ShapeDtypeStruct((n_rows, d), BF16), jax.ShapeDtypeStruct((n_rows, d), F32)],
            compiler_params=_cp("parallel"),
            name="modulate_join",
        )(h[0], h[1], gain.reshape(1, d), mod_rows, mod_rows)
    return pl.pallas_call(
        _modulate_kernel,
        grid=(n_rows // ROW_BLK,),
        in_specs=[pl.BlockSpec((ROW_BLK, d), lambda i: (i, 0)),
                  pl.BlockSpec((1, d), lambda i: (0, 0)),
                  mod_spec(k_shift), mod_spec(k_scale)],
        out_specs=pl.BlockSpec((ROW_BLK, d), lambda i: (i, 0)),
        out_shape=jax.ShapeDtypeStruct((n_rows, d), BF16),
        compiler_params=_cp("parallel"),
        name="modulate",
    )(h, gain.reshape(1, d), mod_rows, mod_rows)


class _CastJob(NamedTuple):
    src: jax.Array
    layer: int
    rows_out: int
    cols_out: int
    start: int
    steps: int

    @property
    def tr(self):
        for tr in (16, 32, 64, 128, 256, 512, 1024):
            if self.rows_out % tr == 0 and self.src.shape[1] % tr == 0 and self.rows_out // tr <= self.steps:
                return tr
        raise ValueError("host kernel has too few steps for this cast")


def _hosted_call(kernel_fn, jobs, *, grid, in_specs, out_specs, out_shape, args, scratch_shapes=(), sem, name):
    out_specs, out_shape = list(out_specs), list(out_shape)
    n_in, n_out, n_jobs = len(in_specs), len(out_specs), len(jobs)
    strides = [math.prod(grid[a + 1:]) for a in range(len(grid))]

    def step_of(ids):
        return sum(i * s for i, s in zip(ids, strides))

    cast_in, cast_out, cast_shape = [], [], []
    for job in jobs:
        tr, n_blk, n_src_blk = job.tr, job.rows_out // job.tr, job.src.shape[1] // job.tr

        def blk(*ids, job=job, n_blk=n_blk):
            return jnp.clip(step_of(ids) - job.start, 0, n_blk - 1)

        cast_in.append(pl.BlockSpec((None, tr, job.src.shape[2]),
                                    lambda *ids, job=job, blk=blk, last=n_src_blk - 1:
                                    (job.layer, jnp.minimum(blk(*ids), last), 0)))
        cast_out.append(pl.BlockSpec((tr, job.cols_out), lambda *ids, blk=blk: (blk(*ids), 0)))
        cast_shape.append(jax.ShapeDtypeStruct((job.rows_out, job.cols_out), BF16))

    def kernel(*refs):
        host_in, src_refs = refs[:n_in], refs[n_in:n_in + n_jobs]
        host_out = refs[n_in + n_jobs:n_in + n_jobs + n_out]
        dst_refs = refs[n_in + n_jobs + n_out:n_in + 2 * n_jobs + n_out]
        kernel_fn(*host_in, *host_out, *refs[n_in + 2 * n_jobs + n_out:])
        step = step_of([pl.program_id(a) for a in range(len(grid))])
        for job, src_ref, dst_ref in zip(jobs, src_refs, dst_refs):
            rel, n_src = step - job.start, src_ref.shape[1]
            n_src_blk = job.src.shape[1] // job.tr

            @pl.when((rel >= 0) & (rel < n_src_blk))
            def _(src_ref=src_ref, dst_ref=dst_ref, n_src=n_src):
                dst_ref[:, :n_src] = src_ref[...].astype(dst_ref.dtype)
                if dst_ref.shape[1] > n_src:
                    dst_ref[:, n_src:] = jnp.zeros((dst_ref.shape[0], dst_ref.shape[1] - n_src), dst_ref.dtype)

            @pl.when((rel >= n_src_blk) & (rel < job.rows_out // job.tr))
            def _(dst_ref=dst_ref):
                dst_ref[...] = jnp.zeros_like(dst_ref)

    outs = pl.pallas_call(
        kernel,
        grid=grid,
        in_specs=list(in_specs) + cast_in,
        out_specs=out_specs + cast_out,
        out_shape=out_shape + cast_shape,
        scratch_shapes=list(scratch_shapes),
        compiler_params=_cp(*(["arbitrary"] * len(grid) if jobs else sem)),
        name=name,
    )(*args, *[job.src for job in jobs])
    return list(outs[:n_out]), list(outs[n_out:])


def _split_cols_kernel(w_ref, o_ref, *, split, second):
    n_in = w_ref.shape[1]
    o_ref[:, :split] = w_ref[:, :split].astype(o_ref.dtype)
    o_ref[:, split:second] = jnp.zeros((o_ref.shape[0], second - split), o_ref.dtype)
    o_ref[:, second:second + n_in - split] = w_ref[:, split:].astype(o_ref.dtype)
    tail = o_ref.shape[1] - (second + n_in - split)
    if tail:
        o_ref[:, second + n_in - split:] = jnp.zeros((o_ref.shape[0], tail), o_ref.dtype)


def _split_cols(w_stack, idx, split, second, n_out):
    _, k, n = w_stack.shape
    tr = _pick(k, (256, 128, 64, 32, 16))
    return pl.pallas_call(
        functools.partial(_split_cols_kernel, split=split, second=second),
        grid=(k // tr,),
        in_specs=[pl.BlockSpec((None, tr, n), lambda i: (idx, i, 0))],
        out_specs=pl.BlockSpec((tr, n_out), lambda i: (i, 0)),
        out_shape=jax.ShapeDtypeStruct((k, n_out), BF16),
        compiler_params=_cp("parallel"),
        name="split_cols",
    )(w_stack)


def _mm_kernel(*refs, n_pairs, gated, tm):
    a_refs, w_refs = refs[:n_pairs], refs[n_pairs:2 * n_pairs]
    acc = _dot(a_refs[0][...], w_refs[0][...])
    for a_ref, w_ref in zip(a_refs[1:], w_refs[1:]):
        acc += _dot(a_ref[...], w_ref[...])
    if gated:
        res_ref, gate_ref, o_ref = refs[2 * n_pairs:]
        gate = _row_select(pl.program_id(0) * tm, tm, gate_ref[...])
        acc = res_ref[...] + gate * acc
    else:
        o_ref = refs[2 * n_pairs]
    o_ref[...] = acc.astype(o_ref.dtype)


def _mm_steps(n_rows, n, tn=None):
    return (n_rows // _pick(n_rows, (1088, 1024, 512, 256))) * (n // (tn or _pick(n, (1024, 512, 256, 128))))


def _mm(a_list, w, n_rows, *, n_cols=None, res=None, gate=None, tn=None, jobs=(), name="mm"):
    n_pairs = len(a_list)
    n = n_cols or w.shape[1]
    tm = _pick(n_rows, (1088, 1024, 512, 256))
    tn = tn or _pick(n, (1024, 512, 256, 128))
    in_specs = [pl.BlockSpec((tm, a.shape[1]), lambda i, j: (i, 0)) for a in a_list]
    in_specs += [pl.BlockSpec((a.shape[1], tn), lambda i, j, p=p: (p, j)) for p, a in enumerate(a_list)]
    assert all(a.shape[1] == a_list[0].shape[1] for a in a_list) and w.shape[0] == n_pairs * a_list[0].shape[1]
    args = list(a_list) + [w] * n_pairs
    if res is not None:
        table, layer, chunk = gate
        d = table.shape[2] // 6
        in_specs += [pl.BlockSpec((tm, tn), lambda i, j: (i, j)),
                     pl.BlockSpec((None, SUBLANE, tn), lambda i, j: (layer, 0, chunk * (d // tn) + j))]
        args += [res, table]
    (out,), casts = _hosted_call(
        functools.partial(_mm_kernel, n_pairs=n_pairs, gated=res is not None, tm=tm), jobs,
        grid=(n_rows // tm, n // tn),
        in_specs=in_specs,
        out_specs=[pl.BlockSpec((tm, tn), lambda i, j: (i, j))],
        out_shape=[jax.ShapeDtypeStruct((n_rows, n), F32)],
        args=args, sem=("parallel", "parallel"), name=name)
    return (out, casts) if jobs else out


def _mmk_kernel(a_ref, w_ref, res_ref, gate_ref, o_ref, acc_ref, *, tm):
    k = pl.program_id(2)

    @pl.when(k == 0)
    def _():
        acc_ref[...] = jnp.zeros_like(acc_ref)

    acc_ref[...] += _dot(a_ref[...], w_ref[...])

    @pl.when(k == pl.num_programs(2) - 1)
    def _():
        gate = _row_select(pl.program_id(0) * tm, tm, gate_ref[...])
        o_ref[...] = res_ref[...] + gate * acc_ref[...]


def _mm_ktiled_steps(kdim, n, n_rows):
    return ((n_rows // _pick(n_rows, (1088, 1024, 512, 256))) * (n // _pick(n, (1024, 512, 256, 128)))
            * (kdim // _pick(kdim, (2816, 2048, 1024, 512, 256))))


def _mm_ktiled(a, w, n_rows, res, gate, jobs=()):
    kdim, n = w.shape
    table, gate_layer, chunk = gate
    d = table.shape[2] // 6
    tm = _pick(n_rows, (1088, 1024, 512, 256))
    tn = _pick(n, (1024, 512, 256, 128))
    tk = _pick(kdim, (2816, 2048, 1024, 512, 256))
    (out,), casts = _hosted_call(
        functools.partial(_mmk_kernel, tm=tm), jobs,
        grid=(n_rows // tm, n // tn, kdim // tk),
        in_specs=[pl.BlockSpec((tm, tk), lambda i, j, k: (i, k)),
                  pl.BlockSpec((tk, tn), lambda i, j, k: (k, j)),
                  pl.BlockSpec((tm, tn), lambda i, j, k: (i, j)),
                  pl.BlockSpec((None, SUBLANE, tn), lambda i, j, k: (gate_layer, 0, chunk * (d // tn) + j))],
        out_specs=[pl.BlockSpec((tm, tn), lambda i, j, k: (i, j))],
        out_shape=[jax.ShapeDtypeStruct((n_rows, n), F32)],
        scratch_shapes=[pltpu.VMEM((tm, tn), F32)],
        args=(a, w, res, table), sem=("parallel", "parallel", "arbitrary"), name="ffn_down")
    return out, casts


def _ffn_up_kernel(u_ref, uh_ref, wg_ref, wu_ref, cw_ref, cb_ref, o_ref, halo_ref, *, tm):
    i = pl.program_id(1)

    @pl.when(i == 0)
    def _():
        halo_ref[...] = _dot(uh_ref[...], wg_ref[...])

    u = u_ref[...]
    g = _dot(u, wg_ref[...])
    dn, up = _shift_rows(g, halo_ref[pl.ds(2 * i, 1), :], halo_ref[pl.ds(2 * i + 1, 1), :], i * tm)
    a = dn * cw_ref[0:1, :] + g * cw_ref[1:2, :] + up * cw_ref[2:3, :] + cb_ref[...]
    o_ref[...] = (_silu(a) * _dot(u, wu_ref[...])).astype(o_ref.dtype)


def _ffn_up_steps(n_rows, n):
    return (n // _pick(n, (512, 256, 128))) * (n_rows // _pick(n_rows, (1088, 1024, 512, 256)))


def _ffn_up(u, n_rows, layer, wg, wu, cw_stack, cb_stack, jobs):
    d, n = wg.shape
    tm = _pick(n_rows, (1088, 1024, 512, 256))
    tn = _pick(n, (512, 256, 128))
    gm = n_rows // tm
    n_halo = _round_up(2 * gm, 2 * SUBLANE)
    rows = []
    for i in range(gm):
        rows += [max(i * tm - 1, 0), min((i + 1) * tm, n_rows - 1)]
    rows += [0] * (n_halo - len(rows))
    u_halo = jnp.concatenate([u[r:r + 1] for r in rows], axis=0)
    col = lambda j, i: (layer, 0, j)
    (hid,), casts = _hosted_call(
        functools.partial(_ffn_up_kernel, tm=tm), jobs,
        grid=(n // tn, gm),
        in_specs=[pl.BlockSpec((tm, d), lambda j, i: (i, 0)),
                  pl.BlockSpec((n_halo, d), lambda j, i: (0, 0)),
                  pl.BlockSpec((d, tn), lambda j, i: (0, j)),
                  pl.BlockSpec((d, tn), lambda j, i: (0, j)),
                  pl.BlockSpec((None, 3, tn), col),
                  pl.BlockSpec((None, 1, tn), col)],
        out_specs=[pl.BlockSpec((tm, tn), lambda j, i: (i, j))],
        out_shape=[jax.ShapeDtypeStruct((n_rows, n), BF16)],
        scratch_shapes=[pltpu.VMEM((n_halo, tn), F32)],
        args=(u, u_halo, wg, wu, cw_stack, cb_stack), sem=("parallel", "arbitrary"), name="ffn_up")
    return hid, casts


def _conv_ffn(h, n_rows, gain, mod_rows, mod, layer, wg, wu, cw, cb, wd_stack, next_w):
    d_ff_p = wg.shape[1]
    u = _modulate(h, n_rows, gain, mod_rows, layer, 3, 4)
    job = _CastJob(wd_stack, layer, d_ff_p, wd_stack.shape[2], 0, _ffn_up_steps(n_rows, d_ff_p))
    hid, (wd,) = _ffn_up(u, n_rows, layer, wg, wu, cw, cb, [job])
    jobs = []
    if next_w is not None:
        stack, idx = next_w
        jobs = [_CastJob(stack, idx, stack.shape[1], stack.shape[2], 0, _mm_ktiled_steps(d_ff_p, wd.shape[1], n_rows))]
    h, casts = _mm_ktiled(hid, wd, n_rows, h, (mod, layer, 5), jobs)
    return h, (casts[0] if casts else None)


def _rope(x, rope_ref):
    c, sa, sb = rope_ref[:, 0:LANE], rope_ref[:, LANE:2 * LANE], rope_ref[:, 2 * LANE:3 * LANE]
    quarter = MLA_ROPE // 4
    return x * c + pltpu.roll(x, LANE - quarter, 1) * sa + pltpu.roll(x, quarter, 1) * sb


def _rms(x, width):
    return lax.rsqrt(jnp.sum(x * x, axis=-1, keepdims=True) * (1.0 / width) + EPS)


def _qproj_kernel(cq_ref, ag_ref, w_ref, hg_ref, rope_ref, o_ref, *, heads):
    x = cq_ref[...]
    xn = ((x * _rms(x, x.shape[1])) * ag_ref[...]).astype(BF16)
    y = _dot(xn, w_ref[...])
    for hh in range(heads):
        yh = y[:, hh * QK_PAD:(hh + 1) * QK_PAD]
        yn = (yh * _rms(yh, MLA_NOPE + MLA_ROPE)) * hg_ref[...]
        o_ref[:, hh * QK_PAD:hh * QK_PAD + LANE] = yn[:, :LANE].astype(o_ref.dtype)
        o_ref[:, hh * QK_PAD + LANE:(hh + 1) * QK_PAD] = _rope(yn[:, LANE:], rope_ref).astype(o_ref.dtype)


def _qproj(y_in, n_rows, ag, w, hg, rope):
    heads = _pick(MLA_HEADS, (4, 2, 1))
    tm = _proj_rows()
    tn = heads * QK_PAD
    return pl.pallas_call(
        functools.partial(_qproj_kernel, heads=heads),
        grid=(n_rows // tm, w.shape[1] // tn),
        in_specs=[pl.BlockSpec((tm, MLA_Q_RANK), lambda i, j: (i, 0)),
                  pl.BlockSpec((1, MLA_Q_RANK), lambda i, j: (0, 0)),
                  pl.BlockSpec((MLA_Q_RANK, tn), lambda i, j: (0, j)),
                  pl.BlockSpec((1, QK_PAD), lambda i, j: (0, 0)),
                  _rope_spec()],
        out_specs=pl.BlockSpec((tm, tn), lambda i, j: (i, j)),
        out_shape=jax.ShapeDtypeStruct((n_rows, w.shape[1]), BF16),
        compiler_params=_cp("parallel", "parallel"),
        name="q_proj",
    )(y_in, ag, w, hg, rope)


def _kvproj_kernel(ckv_ref, kr_ref, ag_ref, w_ref, gn_ref, gr_ref, rope_ref, k_ref, v_ref, *, heads):
    x = ckv_ref[...]
    xn = ((x * _rms(x, x.shape[1])) * ag_ref[...]).astype(BF16)
    y = _dot(xn, w_ref[...])
    kr = kr_ref[...]
    kr_ss = jnp.sum(kr * kr, axis=-1, keepdims=True)
    width = MLA_NOPE + MLA_VDIM
    ones = jnp.ones((x.shape[0], V_PAD - MLA_VDIM), v_ref.dtype)
    for hh in range(heads):
        kn = y[:, hh * width:hh * width + MLA_NOPE]
        r = lax.rsqrt((jnp.sum(kn * kn, axis=-1, keepdims=True) + kr_ss) * (1.0 / (MLA_NOPE + MLA_ROPE)) + EPS)
        k_ref[:, hh * QK_PAD:hh * QK_PAD + LANE] = ((kn * r) * gn_ref[...]).astype(k_ref.dtype)
        k_ref[:, hh * QK_PAD + LANE:(hh + 1) * QK_PAD] = _rope((kr * r) * gr_ref[...], rope_ref).astype(k_ref.dtype)
        v_ref[:, hh * V_PAD:hh * V_PAD + MLA_VDIM] = y[:, hh * width + MLA_NOPE:(hh + 1) * width].astype(v_ref.dtype)
        v_ref[:, hh * V_PAD + MLA_VDIM:(hh + 1) * V_PAD] = ones


def _kvproj(y_in, n_rows, ag, w, gn, gr, rope):
    heads = _pick(MLA_HEADS, (4, 2, 1))
    assert MLA_NOPE == LANE and MLA_VDIM == LANE and MLA_Q_RANK % MLA_KV_RANK == 0
    tm = _proj_rows()
    tn = heads * (MLA_NOPE + MLA_VDIM)
    kr_blk = (MLA_Q_RANK + MLA_KV_RANK) // LANE
    return pl.pallas_call(
        functools.partial(_kvproj_kernel, heads=heads),
        grid=(n_rows // tm, w.shape[1] // tn),
        in_specs=[pl.BlockSpec((tm, MLA_KV_RANK), lambda i, j: (i, MLA_Q_RANK // MLA_KV_RANK)),
                  pl.BlockSpec((tm, LANE), lambda i, j: (i, kr_blk)),
                  pl.BlockSpec((1, MLA_KV_RANK), lambda i, j: (0, 0)),
                  pl.BlockSpec((MLA_KV_RANK, tn), lambda i, j: (0, j)),
                  pl.BlockSpec((1, LANE), lambda i, j: (0, 0)),
                  pl.BlockSpec((1, LANE), lambda i, j: (0, 0)),
                  _rope_spec()],
        out_specs=[pl.BlockSpec((tm, heads * QK_PAD), lambda i, j: (i, j)),
                   pl.BlockSpec((tm, heads * V_PAD), lambda i, j: (i, j))],
        out_shape=[jax.ShapeDtypeStruct((n_rows, MLA_HEADS * QK_PAD), BF16),
                   jax.ShapeDtypeStruct((n_rows, MLA_HEADS * V_PAD), BF16)],
        compiler_params=_cp("parallel", "parallel"),
        name="kv_proj",
    )(y_in, y_in, ag, w, gn, gr, rope)


def _attn_kernel(*refs, chunks, scale):
    q = refs[0][...]
    n_kv = (len(refs) - 2) // 2
    k_refs, v_refs, o_ref = refs[1:1 + n_kv], refs[1 + n_kv:1 + 2 * n_kv], refs[1 + 2 * n_kv]
    c = scale * math.log2(math.e)
    m = acc = None
    for idx, start, size in chunks:
        s = _dot_nt(q, k_refs[idx][start:start + size, :])
        m_blk = jnp.max(s, axis=-1, keepdims=True)
        m_new = m_blk if m is None else jnp.maximum(m, m_blk)
        pv = _dot(jnp.exp2((s - m_new) * c).astype(BF16), v_refs[idx][start:start + size, :])
        acc = pv if acc is None else jnp.exp2((m - m_new) * c) * acc + pv
        m = m_new
    o_ref[...] = (acc[:, :MLA_VDIM] / acc[:, MLA_VDIM:2 * MLA_VDIM]).astype(o_ref.dtype)


def _attn_lat_steps():
    return BATCH * MLA_HEADS * (SEQ // _pick(SEQ, (1024, 512, 256)))


def _attention(q, k, v, latent, n_out, jobs=(), o_prev=None):
    assert V_PAD == 2 * MLA_VDIM
    scale = (MLA_NOPE + MLA_ROPE) ** -0.5
    ctx_blk0 = _n_lat() // CTX_LEN
    ctx_k = pl.BlockSpec((CTX_LEN, QK_PAD), lambda b, h, i: (ctx_blk0 + b, h))
    ctx_v = pl.BlockSpec((CTX_LEN, V_PAD), lambda b, h, i: (ctx_blk0 + b, h))
    if latent:
        tq = _pick(SEQ, (1024, 512, 256))
        tk = _pick(SEQ, (1024, 512, 256))
        q_per_b = SEQ // tq
        q_spec = pl.BlockSpec((tq, QK_PAD), lambda b, h, i: (b * q_per_b + i, h))
        o_spec = pl.BlockSpec((tq, MLA_VDIM), lambda b, h, i: (b * q_per_b + i, h))
        k_specs = [ctx_k, pl.BlockSpec((SEQ, QK_PAD), lambda b, h, i: (b, h))]
        v_specs = [ctx_v, pl.BlockSpec((SEQ, V_PAD), lambda b, h, i: (b, h))]
        chunks = [(0, 0, CTX_LEN)] + [(1, s, tk) for s in range(0, SEQ, tk)]
        grid = (BATCH, MLA_HEADS, q_per_b)
    else:
        q_spec = pl.BlockSpec((CTX_LEN, QK_PAD), lambda b, h, i: (ctx_blk0 + b, h))
        o_spec = pl.BlockSpec((CTX_LEN, MLA_VDIM), lambda b, h, i: (ctx_blk0 + b, h))
        k_specs, v_specs, chunks = [ctx_k], [ctx_v], [(0, 0, CTX_LEN)]
        grid = (BATCH, MLA_HEADS, 1)
    n_kv = len(k_specs)
    in_specs = [q_spec] + k_specs + v_specs
    args = (q, *([k] * n_kv), *([v] * n_kv))
    out_shape = jax.ShapeDtypeStruct((n_out, MLA_HEADS * MLA_VDIM), BF16)
    body = functools.partial(_attn_kernel, chunks=tuple(chunks), scale=scale)
    if latent:
        (out,), casts = _hosted_call(body, jobs, grid=grid, in_specs=in_specs, out_specs=[o_spec],
                                     out_shape=[out_shape], args=args,
                                     sem=("parallel", "parallel", "arbitrary"), name="attn_lat")
        return out, casts
    return pl.pallas_call(
        lambda *refs: body(*refs[:len(in_specs)], refs[-1]),
        grid=grid,
        in_specs=in_specs + [pl.BlockSpec(memory_space=pl.ANY)],
        out_specs=o_spec,
        out_shape=out_shape,
        input_output_aliases={len(in_specs): 0},
        compiler_params=_cp("parallel", "parallel", "arbitrary"),
        name="attn_ctx",
    )(*args, o_prev)


def _sconv_kernel(b_ref, c_ref, x_ref, cp_ref, xp_ref, cn_ref, xn_ref, w_ref, o_ref, *, tm):
    p = c_ref[...] * x_ref[...]
    prev = cp_ref[SUBLANE - 1:SUBLANE, :] * xp_ref[SUBLANE - 1:SUBLANE, :]
    nxt = cn_ref[0:1, :] * xn_ref[0:1, :]
    dn, up = _shift_rows(p, prev, nxt, pl.program_id(0) * tm)
    conv = dn * w_ref[0:1, :] + p * w_ref[1:2, :] + up * w_ref[2:3, :]
    o_ref[...] = (b_ref[...] * conv).astype(o_ref.dtype)


def _sconv(y_in, n_rows, w, col0, width):
    tm = ROW_BLK
    tn = _pick(width, (1024, 512, 256, 128))
    blk0 = [(col0 + k * width) // tn for k in range(3)]
    main = [pl.BlockSpec((tm, tn), lambda i, j, o=o: (i, o + j)) for o in blk0]
    cp, cn = _halo_specs(tm, tn, n_rows, blk0[1])
    xp, xn = _halo_specs(tm, tn, n_rows, blk0[2])
    return pl.pallas_call(
        functools.partial(_sconv_kernel, tm=tm),
        grid=(n_rows // tm, width // tn),
        in_specs=main + [cp, xp, cn, xn, pl.BlockSpec((3, tn), lambda i, j: (0, j))],
        out_specs=pl.BlockSpec((tm, tn), lambda i, j: (i, j)),
        out_shape=jax.ShapeDtypeStruct((n_rows, width), BF16),
        compiler_params=_cp("parallel", "parallel"),
        name="short_conv",
    )(*([y_in] * 7), w)


def _conv_silu(x_ref, xp_ref, xn_ref, w_ref, row0):
    x = x_ref[...]
    dn, up = _shift_rows(x, xp_ref[SUBLANE - 1:SUBLANE, :], xn_ref[0:1, :], row0)
    return _silu(dn * w_ref[0:1, :] + x * w_ref[1:2, :] + up * w_ref[2:3, :])


def _l2norm(x):
    return x * lax.rsqrt(jnp.sum(x * x, axis=-1, keepdims=True) + EPS)


def _gdn_gate_kernel(ba_ref, alog_ref, dtb_ref, o_ref):
    x = ba_ref[...]
    lane = lax.broadcasted_iota(jnp.int32, x.shape, 1)
    z = x + dtb_ref[...]
    softplus = jnp.maximum(z, 0.0) + jnp.log(1.0 + jnp.exp(-jnp.abs(z)))
    o_ref[...] = jnp.where(lane < x.shape[1] // 2, _sigmoid(x), -jnp.exp(alog_ref[...]) * softplus)


def _gdn_gate(ba, n_rows, alog, dtb):
    w = ba.shape[1]
    return pl.pallas_call(
        _gdn_gate_kernel,
        grid=(n_rows // ROW_BLK,),
        in_specs=[pl.BlockSpec((ROW_BLK, w), lambda i: (i, 0)),
                  pl.BlockSpec((1, w), lambda i: (0, 0)),
                  pl.BlockSpec((1, w), lambda i: (0, 0))],
        out_specs=pl.BlockSpec((ROW_BLK, w), lambda i: (i, 0)),
        out_shape=jax.ShapeDtypeStruct((n_rows, w), F32),
        compiler_params=_cp("parallel"),
        name="gdn_gate",
    )(ba, alog, dtb)


def _active_rows(x, size, odd):
    n = x.shape[0]
    return jnp.concatenate([x[(2 * b + odd) * size:(2 * b + odd + 1) * size] for b in range(n // (2 * size))], axis=0)


def _weave_rows(rest, active, size, odd):
    pieces = []
    for b in range(active.shape[0] // size):
        if rest is None:
            keep = jnp.zeros((size, active.shape[1]), active.dtype)
        else:
            keep = rest[(2 * b + 1 - odd) * size:(2 * b + 2 - odd) * size]
        act = active[b * size:(b + 1) * size]
        pieces += [keep, act] if odd else [act, keep]
    return jnp.concatenate(pieces, axis=0)


def _unit_tri_inverses(l_mats, dirs, eye, base, ring_ref):
    size = SUBLANE
    xs = [-(l * base) for l in l_mats]
    ts = [eye + x for x in xs]
    for _ in range(size.bit_length() - 2):
        xs = [_dot(x.astype(BF16), x.astype(BF16)) for x in xs]
        ts = [t + _dot(t.astype(BF16), x.astype(BF16)) for t, x in zip(ts, xs)]
    level = 0
    while size < GDN_BLK:
        tbs = [t.astype(BF16) for t in ts]
        offs = [(_active_rows(l, size, 1 - d) * ring_ref[level, d]).astype(BF16) for l, d in zip(l_mats, dirs)]
        mids = [_weave_rows(None, _dot(off, tb), size, 1 - d).astype(BF16) for off, tb, d in zip(offs, tbs, dirs)]
        acts = [_active_rows(t, size, 1 - d) for t, d in zip(ts, dirs)]
        news = [act - _dot(act.astype(BF16), mid) for act, mid in zip(acts, mids)]
        ts = [_weave_rows(t, new, size, 1 - d) for t, new, d in zip(ts, news, dirs)]
        size *= 2
        level += 1
    return ts


def _gdn_masks():
    n, size = GDN_BLK, SUBLANE
    i, j = np.arange(n)[:, None], np.arange(n)[None, :]
    tri = np.stack([j <= i, j >= i, j == i, i // size == j // size]).astype(np.float32)
    rings = []
    while size < n:
        ra = np.arange(n // 2)[:, None]
        per_dir = []
        for d in range(2):
            ia = (ra // size) * 2 * size + (1 - d) * size + ra % size
            per_dir.append((ia // (2 * size) == j // (2 * size)) & (ia // size != j // size))
        rings.append(np.stack(per_dir))
        size *= 2
    return jnp.asarray(tri), jnp.asarray(np.stack(rings).astype(np.float32))


def _gdn_prep_kernel(q_ref, qp_ref, qn_ref, k_ref, kp_ref, kn_ref, v_ref, vp_ref, vn_ref, cwq_ref, cwk_ref, cwv_ref,
                     bg_ref, tri_ref, tri16_ref, ring_ref, u_ref, wq_ref, kd_ref, in_ref, cd_ref, *, rep, hpb):
    n = GDN_BLK
    row0 = pl.program_id(0) * n
    q_all = _conv_silu(q_ref, qp_ref, qn_ref, cwq_ref, row0)
    k_all = _conv_silu(k_ref, kp_ref, kn_ref, cwk_ref, row0)
    v_all = _conv_silu(v_ref, vp_ref, vn_ref, cwv_ref, row0)
    eye, base = tri_ref[2], tri_ref[3]
    not_eye = 1.0 - eye
    chains, l_mats, rhss = [], [], []
    for hh in range(hpb):
        q = _l2norm(q_all[:, hh * GDN_DK:(hh + 1) * GDN_DK]) * GDN_DK ** -0.5
        k = _l2norm(k_all[:, hh * GDN_DK:(hh + 1) * GDN_DK])
        q16, k16 = q.astype(BF16), k.astype(BF16)
        qk = _dot_nt(q16, k16)
        kk = _dot_nt(k16, k16) * not_eye
        bg = bg_ref[hh]
        bg_cols = jnp.concatenate([bg, jnp.zeros((LANE - bg.shape[0], n), F32)], axis=0).T
        hi = bg_cols.astype(BF16)
        rest = bg_cols - hi.astype(F32)
        mid = rest.astype(BF16)
        pieces = jnp.concatenate([hi, mid, (rest - mid.astype(F32)).astype(BF16)], axis=1)
        cums = [_dot(tri16_ref[d], pieces) for d in range(2)]
        gc_cols = [c[:, :LANE] + c[:, LANE:2 * LANE] + c[:, 2 * LANE:] for c in cums]
        gc_rows = [g.T for g in gc_cols]
        for e in range(rep):
            for d in range(2):
                vh = hh * rep + e
                cols = slice(vh * GDN_DV, (vh + 1) * GDN_DV)
                incl = tri_ref[d]
                b_idx, g_idx = d * rep + e, (2 + d) * rep + e
                b_col = bg_cols[:, b_idx:b_idx + 1]
                gc_col, gc_row = gc_cols[d][:, g_idx:g_idx + 1], gc_rows[d][g_idx:g_idx + 1, :]
                g_tot = jnp.sum(bg[g_idx:g_idx + 1, :], axis=1, keepdims=True)
                decay = jnp.exp(jnp.minimum(gc_col - gc_row, 0.0)) * incl
                e_col = jnp.exp(gc_col)
                chains.append((vh, d, cols))
                l_mats.append((kk * b_col) * decay)
                rhss.append(jnp.concatenate([v_all[:, cols] * b_col, (k * b_col) * e_col], axis=1).astype(BF16))
                wq_ref[d, n:, cols] = (q * e_col).astype(wq_ref.dtype)
                kd_ref[d, :, cols] = (k * jnp.exp(g_tot - gc_col)).astype(kd_ref.dtype)
                in_ref[d, vh] = (qk * decay).astype(in_ref.dtype)
                cd_ref[d, vh] = jnp.broadcast_to(jnp.exp(g_tot), (SUBLANE, LANE))
    invs = _unit_tri_inverses(l_mats, [d for _, d, _ in chains], eye, base, ring_ref)
    sols = [_dot(t.astype(BF16), rhs) for t, rhs in zip(invs, rhss)]
    for (vh, d, cols), sol in zip(chains, sols):
        u_ref[d, :, cols] = sol[:, :GDN_DV]
        wq_ref[d, :n, cols] = sol[:, GDN_DV:].astype(wq_ref.dtype)


def _gdn_prep(y_in, conv_w, bg_t, n_rows):
    assert GDN_DK == LANE and GDN_DV == LANE and GDN_BLK == 2 * LANE
    nb, hv, hq, rep = n_rows // GDN_BLK, GDN_V_HEADS, GDN_QK_HEADS, GDN_V_HEADS // GDN_QK_HEADS
    hpb = _pick(hq, (2, 1))
    key_w, wide = hpb * GDN_DK, hpb * rep * GDN_DV
    parts = ((key_w, 0), (key_w, hq // hpb), (wide, 2 * hq * GDN_DK // wide))
    in_specs = []
    for tn, blk0 in parts:
        in_specs += [pl.BlockSpec((GDN_BLK, tn), lambda i, h, blk0=blk0: (i, blk0 + h))]
        in_specs += _halo_specs(GDN_BLK, tn, n_rows, blk0)
    in_specs += [pl.BlockSpec((3, tn), lambda i, h, blk0=blk0: (0, blk0 + h)) for tn, blk0 in parts]
    tri, rings = _gdn_masks()
    in_specs += [pl.BlockSpec((hpb, bg_t.shape[1], GDN_BLK), lambda i, h: (h, 0, i)),
                 pl.BlockSpec(tri.shape, lambda i, h: (0, 0, 0)),
                 pl.BlockSpec((2,) + tri.shape[1:], lambda i, h: (0, 0, 0)),
                 pl.BlockSpec(rings.shape, lambda i, h: (0, 0, 0, 0))]
    return pl.pallas_call(
        functools.partial(_gdn_prep_kernel, rep=rep, hpb=hpb),
        grid=(nb, hq // hpb),
        in_specs=in_specs,
        out_specs=[pl.BlockSpec((2, GDN_BLK, wide), lambda i, h: (0, i, h)),
                   pl.BlockSpec((2, None, 2 * GDN_BLK, wide), lambda i, h: (0, i, 0, h)),
                   pl.BlockSpec((2, GDN_BLK, wide), lambda i, h: (0, i, h)),
                   pl.BlockSpec((2, hpb * rep, GDN_BLK, GDN_BLK), lambda i, h: (0, h, i, 0)),
                   pl.BlockSpec((2, hpb * rep, SUBLANE, LANE), lambda i, h: (0, h, i, 0))],
        out_shape=[jax.ShapeDtypeStruct((2, n_rows, hv * GDN_DV), F32),
                   jax.ShapeDtypeStruct((2, nb, 2 * GDN_BLK, hv * GDN_DV), BF16),
                   jax.ShapeDtypeStruct((2, n_rows, hv * GDN_DV), BF16),
                   jax.ShapeDtypeStruct((2, hv, n_rows, GDN_BLK), BF16),
                   jax.ShapeDtypeStruct((2, hv, nb * SUBLANE, LANE), F32)],
        compiler_params=_cp("parallel", "parallel"),
        name="gdn_prep",
    )(*([y_in] * 9), conv_w, conv_w, conv_w, bg_t, tri, tri[:2].astype(BF16), rings)


def _gdn_scan_kernel(*refs, heads):
    ins, (of_ref, ob_ref, s_ref) = refs[:10], refs[10:]
    n = GDN_BLK

    @pl.when(pl.program_id(2) == 0)
    def _():
        s_ref[...] = jnp.zeros_like(s_ref)

    o_refs = (of_ref, ob_ref)
    chains = [(d, hh, slice(hh * GDN_DV, (hh + 1) * GDN_DV)) for d in range(2) for hh in range(heads)]
    u_refs, wq_refs, kd_refs, in_refs, cd_refs = (ins[0::5], ins[1::5], ins[2::5], ins[3::5], ins[4::5])
    states = [s_ref[d, hh] for d, hh, _ in chains]
    wqs = [_dot(wq_refs[d][:, cols], s.astype(BF16)) for (d, hh, cols), s in zip(chains, states)]
    vbs = [(u_refs[d][:, cols] - wq[:n]).astype(BF16) for (d, hh, cols), wq in zip(chains, wqs)]
    for (d, hh, cols), s, wq, vb in zip(chains, states, wqs, vbs):
        o_refs[d][:, cols] = wq[n:] + _dot(in_refs[d][hh], vb)
        s_ref[d, hh] = s * cd_refs[d][hh, 0:1, :] + _dot_tn(kd_refs[d][:, cols], vb)


def _gdn_scan(prep, n_rows):
    hv = GDN_V_HEADS
    heads = _pick(hv, (4, 2, 1))
    lat_blk, ctx_blk = SEQ // GDN_BLK, CTX_LEN // GDN_BLK
    steps = ctx_blk + lat_blk
    ctx0 = _n_lat() // GDN_BLK
    wide = heads * GDN_DV

    def blk_f(b, n):
        return jnp.where(n < ctx_blk, ctx0 + b * ctx_blk + n, b * lat_blk + n - ctx_blk)

    def blk_b(b, n):
        return jnp.where(n < ctx_blk, ctx0 + b * ctx_blk + ctx_blk - 1 - n, b * lat_blk + steps - 1 - n)

    in_specs = []
    for d, blk in enumerate((blk_f, blk_b)):
        rows = pl.BlockSpec((None, GDN_BLK, wide), lambda b, h, n, d=d, blk=blk: (d, blk(b, n), h))
        in_specs += [rows,
                     pl.BlockSpec((None, None, 2 * GDN_BLK, wide), lambda b, h, n, d=d, blk=blk: (d, blk(b, n), 0, h)),
                     rows,
                     pl.BlockSpec((None, heads, GDN_BLK, GDN_BLK), lambda b, h, n, d=d, blk=blk: (d, h, blk(b, n), 0)),
                     pl.BlockSpec((None, heads, SUBLANE, LANE), lambda b, h, n, d=d, blk=blk: (d, h, blk(b, n), 0))]
    out = jax.ShapeDtypeStruct((n_rows, hv * GDN_DV), F32)
    return pl.pallas_call(
        functools.partial(_gdn_scan_kernel, heads=heads),
        grid=(BATCH, hv // heads, steps),
        in_specs=in_specs,
        out_specs=[pl.BlockSpec((GDN_BLK, wide), lambda b, h, n: (blk_f(b, n), h)),
                   pl.BlockSpec((GDN_BLK, wide), lambda b, h, n: (blk_b(b, n), h))],
        out_shape=[out, out],
        scratch_shapes=[pltpu.VMEM((2, heads, GDN_DK, GDN_DV), F32)],
        compiler_params=_cp("parallel", "parallel", "arbitrary"),
        name="gdn_scan",
    )(*prep, *prep)


def _gdn_readout_kernel(of_ref, ob_ref, z_ref, g_ref, o_ref):
    o = of_ref[...] + ob_ref[...]
    z = z_ref[...]
    for hh in range(o.shape[1] // GDN_DV):
        cols = slice(hh * GDN_DV, (hh + 1) * GDN_DV)
        oh = o[:, cols]
        y = (oh * _rms(oh, GDN_DV)) * g_ref[...]
        o_ref[:, cols] = (y * _silu(z[:, cols])).astype(o_ref.dtype)


def _gdn_readout(o_f, o_b, y_in, z_col0, gain, n_rows):
    width = GDN_V_HEADS * GDN_DV
    tn = _pick(width, (1024, 512, 256, 128))
    spec = pl.BlockSpec((ROW_BLK, tn), lambda i, j: (i, j))
    return pl.pallas_call(
        _gdn_readout_kernel,
        grid=(n_rows // ROW_BLK, width // tn),
        in_specs=[spec, spec, pl.BlockSpec((ROW_BLK, tn), lambda i, j: (i, z_col0 // tn + j)),
                  pl.BlockSpec((1, GDN_DV), lambda i, j: (0, 0))],
        out_specs=spec,
        out_shape=jax.ShapeDtypeStruct((n_rows, width), BF16),
        compiler_params=_cp("parallel", "parallel"),
        name="gdn_readout",
    )(o_f, o_b, y_in, gain.reshape(1, GDN_DV))


def _proj_rows():
    return _pick(math.gcd(SEQ, BATCH * CTX_LEN), (512, 256))


def _rope_spec():
    tm = _proj_rows()
    return pl.BlockSpec((tm, 3 * LANE), lambda i, j: (jnp.where(i < _n_lat() // tm, i % (SEQ // tm), SEQ // tm), 0))


def _rope_table(n_ident):
    n_freq = MLA_ROPE // 4
    inv_freq = ROPE_THETA ** (-jnp.arange(n_freq, dtype=F32) / n_freq)
    t = jnp.arange(SEQ)
    ang = jnp.concatenate([(t // GRID_W).astype(F32)[:, None] * inv_freq,
                           (t % GRID_W).astype(F32)[:, None] * inv_freq], axis=1)
    cos, sin = jnp.cos(ang), jnp.sin(ang)
    zero = jnp.zeros_like(sin[:, :n_freq])
    pad = jnp.zeros((SEQ, LANE - MLA_ROPE), F32)
    c = jnp.concatenate([cos[:, :n_freq], cos[:, :n_freq], cos[:, n_freq:], cos[:, n_freq:], pad + 1.0], axis=1)
    sa = jnp.concatenate([-sin[:, :n_freq], zero, -sin[:, n_freq:], zero, pad], axis=1)
    sb = jnp.concatenate([zero, sin[:, :n_freq], zero, sin[:, n_freq:], pad], axis=1)
    ident = jnp.concatenate([jnp.ones((n_ident, LANE), F32), jnp.zeros((n_ident, 2 * LANE), F32)], axis=1)
    return jnp.concatenate([jnp.concatenate([c, sa, sb], axis=1), ident], axis=0)


def _pad_cols(w, n):
    return jnp.pad(w, ((0, 0), (0, n - w.shape[1])))


def _even_layer(h, n_rows, need_ctx, mod, mod_rows, layer, gain, p, idx, ffn):
    w_in_stack, q_a_g, kv_a_g, w_qb, w_kvb, q_g, k_g, sc_w, w_out_stack = p
    ffn_w_gate, ffn_w_up, d_ff_p = ffn
    d = w_out_stack.shape[2]
    sc_width = d - MLA_HEADS * MLA_VDIM
    assert sc_width == MLA_HEADS * MLA_VDIM
    head_w = MLA_NOPE + MLA_ROPE
    c0 = MLA_Q_RANK + MLA_KV_RANK
    conv0 = _round_up(c0 + LANE, _pick(sc_width, (1024, 512, 256, 128)))
    n_in = _round_up(conv0 + 3 * sc_width, 1024)
    w_in_p = _split_cols(w_in_stack, idx, c0 + MLA_ROPE, conv0, n_in)
    w_q = w_qb.reshape(MLA_Q_RANK, MLA_HEADS, head_w)
    w_q = jnp.pad(w_q, ((0, 0), (0, 0), (0, QK_PAD - head_w))).reshape(MLA_Q_RANK, MLA_HEADS * QK_PAD).astype(BF16)
    rope = _rope_table(_proj_rows())

    if isinstance(h, tuple):
        u, h = _modulate(h, n_rows, gain, mod_rows, layer, 0, 1)
    else:
        u = _modulate(h, n_rows, gain, mod_rows, layer, 0, 1)
    y = _mm([u], w_in_p, n_rows, name="even_in")
    q = _qproj(y, n_rows, q_a_g.reshape(1, -1), w_q, jnp.pad(q_g, (0, QK_PAD - head_w)).reshape(1, QK_PAD), rope)
    k, v = _kvproj(y, n_rows, kv_a_g.reshape(1, -1), w_kvb.astype(BF16), k_g[:MLA_NOPE].reshape(1, LANE),
                   jnp.pad(k_g[MLA_NOPE:], (0, LANE - MLA_ROPE)).reshape(1, LANE), rope)
    steps = _attn_lat_steps()
    jobs = [_CastJob(ffn_w_gate, layer, d, d_ff_p, 0, steps // 2),
            _CastJob(ffn_w_up, layer, d, d_ff_p, steps // 2, steps - steps // 2),
            _CastJob(w_out_stack, idx, w_out_stack.shape[1], d, 0, steps)]
    n_out = n_rows if need_ctx else _n_lat()
    o, (wg, wu, w_out_b) = _attention(q, k, v, True, n_out, jobs)
    if need_ctx:
        o = _attention(q, k, v, False, n_out, o_prev=o)
    conv = _sconv(y, n_out, sc_w, conv0, sc_width)
    h = _mm([o, conv], w_out_b, n_out, res=h, gate=(mod, layer, 2),
            tn=_pick(d, (512, 256, 128)), name="even_out")
    return h, (wg, wu)


def _odd_layer(h, n_rows, need_ctx, mod, mod_rows, layer, gain, p, idx, ffn, w_in_b):
    w_in_stack, conv_w, a_log, dt_bias, o_g, w_out_stack = p
    ffn_w_gate, ffn_w_up, d_ff_p = ffn
    d = h.shape[1]
    hv, hq = GDN_V_HEADS, GDN_QK_HEADS
    qkv_w = 2 * hq * GDN_DK + hv * GDN_DV
    main_w = qkv_w + hv * GDN_DV
    gate_w = _round_up(4 * hv, LANE)
    half = gate_w // 2
    if w_in_b is None:
        w_in_b = w_in_stack[idx].astype(BF16)
    w_ba = jnp.concatenate([_pad_cols(w_in_b[:, main_w:main_w + 2 * hv], half),
                            _pad_cols(w_in_b[:, main_w + 2 * hv:], half)], axis=1)
    zeros = jnp.zeros((1, half), F32)
    alog = jnp.concatenate([zeros, _pad_cols(a_log.reshape(1, -1), half)], axis=1)
    dtb = jnp.concatenate([zeros, _pad_cols(dt_bias.reshape(1, -1), half)], axis=1)

    u = _modulate(h, n_rows, gain, mod_rows, layer, 0, 1)
    tn = _pick(main_w, (512, 256, 128))
    steps = _mm_steps(n_rows, main_w, tn)
    y, (wg, w_out_b) = _mm([u], w_in_b, n_rows, n_cols=main_w, tn=tn,
                           jobs=[_CastJob(ffn_w_gate, layer, d, d_ff_p, 0, steps),
                                 _CastJob(w_out_stack, idx, w_out_stack.shape[1], d, 0, steps)], name="odd_in")
    ba = _mm([u], w_ba, n_rows, name="odd_in_gates")
    bg = _gdn_gate(ba, n_rows, alog, dtb)
    rep = hv // hq
    bg_t = jnp.stack([bg[:, :2 * hv], bg[:, half:half + 2 * hv]], axis=0)
    bg_t = bg_t.reshape(2, n_rows, 2, hq, rep).transpose(3, 0, 2, 4, 1).reshape(hq, 4 * rep, n_rows)
    bg_t = jnp.pad(bg_t, ((0, 0), (0, _round_up(4 * rep, SUBLANE) - 4 * rep), (0, 0)))
    o_f, o_b = _gdn_scan(_gdn_prep(y, conv_w, bg_t, n_rows), n_rows)
    n_out = n_rows if need_ctx else _n_lat()
    yo = _gdn_readout(o_f, o_b, y, qkv_w, o_g, n_out)
    tn = _pick(d, (512, 256, 128))
    h, (wu,) = _mm([yo], w_out_b, n_out, res=h, gate=(mod, layer, 2), tn=tn,
                   jobs=[_CastJob(ffn_w_up, layer, d, d_ff_p, 0, _mm_steps(n_out, d, tn))], name="odd_out")
    return h, (wg, wu)


def kernel(x, c, ctx, c_ctx, ada_w, ada_b, norm_mix, norm_ffn, ffn_w_gate, ffn_w_up, ffn_conv_w, ffn_conv_b,
           ffn_w_down, a_w_in, a_q_a_norm, a_kv_a_norm, a_w_qb, a_w_kvb, a_q_norm, a_k_norm, a_sc_conv, a_w_out,
           c_w_in, c_conv_w, c_a_log, c_dt_bias, c_o_norm, c_w_out):
    bn, t, d = x.shape
    depth = ada_w.shape[0]
    assert (bn, t, ctx.shape[1], ffn_w_gate.shape[2]) == (BATCH, SEQ, CTX_LEN, D_FF)
    assert CTX_LEN % GDN_BLK == 0 and SEQ % GDN_BLK == 0
    h = (x.reshape(bn * t, d), ctx.reshape(bn * CTX_LEN, d))
    cond8 = jnp.concatenate([c, c_ctx[None], jnp.zeros((SUBLANE - bn - 1, d), F32)], axis=0)
    mod = _ada(cond8, ada_w, ada_b)
    mod_rows = mod.reshape(depth * SUBLANE * 6, 1, d)
    d_ff_p = _round_up(D_FF, 512)
    ff_pad = ((0, 0), (0, 0), (0, d_ff_p - D_FF))
    conv_w = jnp.pad(ffn_conv_w, ff_pad)
    conv_b = jnp.pad(ffn_conv_b.reshape(depth, 1, D_FF), ff_pad)
    ffn = (ffn_w_gate, ffn_w_up, d_ff_p)
    w_in_b = None
    for l in range(depth):
        last = l == depth - 1
        if isinstance(h, tuple) and l % 2 == 1:
            h = jnp.concatenate(h, axis=0)
        n_rows = h[0].shape[0] + h[1].shape[0] if isinstance(h, tuple) else h.shape[0]
        i = l // 2
        if l % 2 == 0:
            h, (w_gate, w_up) = _even_layer(
                h, n_rows, not last, mod, mod_rows, l, norm_mix[l],
                (a_w_in, a_q_a_norm[i], a_kv_a_norm[i], a_w_qb[i], a_w_kvb[i], a_q_norm[i], a_k_norm[i],
                 a_sc_conv[i], a_w_out), i, ffn)
        else:
            h, (w_gate, w_up) = _odd_layer(
                h, n_rows, not last, mod, mod_rows, l, norm_mix[l],
                (c_w_in, c_conv_w[i], c_a_log[i], c_dt_bias[i], c_o_norm[i], c_w_out), i, ffn, w_in_b)
        next_w = (c_w_in, (l + 1) // 2) if (l + 1 < depth and (l + 1) % 2 == 1) else None
        h, w_in_b = _conv_ffn(h, h.shape[0], norm_ffn[l], mod_rows, mod, l,
                              w_gate, w_up, conv_w, conv_b, ffn_w_down, next_w)
    return h[:bn * t].reshape(bn, t, d)
```

```python
import functools
import math
from typing import NamedTuple

import jax
import jax.numpy as jnp
import numpy as np
from jax import lax
from jax.experimental import pallas as pl
from jax.experimental.pallas import tpu as pltpu

F32 = jnp.float32
BF16 = jnp.bfloat16

BATCH = 2
SEQ = 4096
GRID_W = 64
CTX_LEN = 256
EPS = 1e-6
D_FF = 11008

MLA_HEADS = 16
MLA_NOPE = 128
MLA_ROPE = 64
MLA_VDIM = 128
MLA_Q_RANK = 1024
MLA_KV_RANK = 512
ROPE_THETA = 10000.0

GDN_QK_HEADS = 16
GDN_V_HEADS = 32
GDN_DK = 128
GDN_DV = 128

LANE = 128
SUBLANE = 8
QK_PAD = 256
V_PAD = 256
ROW_BLK = 256
GDN_BLK = 256
VMEM_LIMIT = 56 * 2**20


def _cp(*sem, vmem=VMEM_LIMIT):
    return pltpu.CompilerParams(dimension_semantics=sem, vmem_limit_bytes=vmem)


def _pick(n, prefs):
    for p in prefs:
        if n % p == 0:
            return p
    raise ValueError(f"no tile for {n} in {prefs}")


def _round_up(n, m):
    return (n + m - 1) // m * m


def _sigmoid(x):
    return 1.0 / (1.0 + jnp.exp(-x))


def _silu(x):
    return x * _sigmoid(x)


def _dot(a, b):
    return jnp.dot(a, b, preferred_element_type=F32)


def _dot_nt(a, b):
    return lax.dot_general(a, b, (((1,), (1,)), ((), ())), preferred_element_type=F32)


def _dot_tn(a, b):
    return lax.dot_general(a, b, (((0,), (0,)), ((), ())), preferred_element_type=F32)


def _n_lat():
    return BATCH * SEQ


def _seq_edges(row0, tm):
    assert SEQ & (SEQ - 1) == 0 and CTX_LEN & (CTX_LEN - 1) == 0 and SEQ % CTX_LEN == 0
    r = row0 + lax.broadcasted_iota(jnp.int32, (tm, 1), 0)
    is_ctx = r >= _n_lat()
    first = (jnp.bitwise_and(r, CTX_LEN - 1) == 0) & ((jnp.bitwise_and(r, SEQ - 1) == 0) | is_ctx)
    r1 = r + 1
    last = (jnp.bitwise_and(r1, CTX_LEN - 1) == 0) & ((jnp.bitwise_and(r1, SEQ - 1) == 0) | is_ctx)
    return first, last


def _shift_rows(x, prev_row, next_row, row0):
    tm = x.shape[0]
    ridx = lax.broadcasted_iota(jnp.int32, (tm, 1), 0)
    first, last = _seq_edges(row0, tm)
    dn = jnp.where(ridx == 0, prev_row, pltpu.roll(x, 1, 0))
    dn = jnp.where(first, 0.0, dn)
    up = jnp.where(ridx == tm - 1, next_row, pltpu.roll(x, tm - 1, 0))
    up = jnp.where(last, 0.0, up)
    return dn, up


def _row_select(row0, tm, table):
    r = row0 + lax.broadcasted_iota(jnp.int32, (tm, 1), 0)
    out = table[BATCH:BATCH + 1]
    for b in reversed(range(BATCH)):
        out = jnp.where(r < (b + 1) * SEQ, table[b:b + 1], out)
    return out


def _halo_specs(tm, tn, n_rows, col_blk0):
    tmb, last = tm // SUBLANE, n_rows // SUBLANE - 1
    prev = pl.BlockSpec((SUBLANE, tn), lambda i, j: (jnp.maximum(i * tmb - 1, 0), col_blk0 + j))
    nxt = pl.BlockSpec((SUBLANE, tn), lambda i, j: (jnp.minimum((i + 1) * tmb, last), col_blk0 + j))
    return prev, nxt


def _ada_kernel(cond_ref, w_ref, b_ref, o_ref):
    a = _silu(cond_ref[...]).astype(BF16)
    o_ref[...] = _dot(a, w_ref[...].astype(BF16)) + b_ref[...]


def _ada(cond8, ada_w, ada_b):
    n_layer, d, n = ada_w.shape
    tn = _pick(n, (512, 256, 128))
    return pl.pallas_call(
        _ada_kernel,
        grid=(n_layer, n // tn),
        in_specs=[pl.BlockSpec((SUBLANE, d), lambda l, j: (0, 0)),
                  pl.BlockSpec((None, d, tn), lambda l, j: (l, 0, j)),
                  pl.BlockSpec((None, 1, tn), lambda l, j: (l, 0, j))],
        out_specs=pl.BlockSpec((None, SUBLANE, tn), lambda l, j: (l, 0, j)),
        out_shape=jax.ShapeDtypeStruct((n_layer, SUBLANE, n), F32),
        compiler_params=_cp("parallel", "parallel"),
        name="ada",
    )(cond8, ada_w, ada_b.reshape(n_layer, 1, n))


def _modulate_kernel(h_ref, g_ref, sh_ref, sc_ref, o_ref):
    x = h_ref[...]
    y = x * lax.rsqrt(jnp.mean(x * x, axis=-1, keepdims=True) + EPS)
    o_ref[...] = ((y * g_ref[...]) * (1.0 + sc_ref[...]) + sh_ref[...]).astype(o_ref.dtype)


def _modulate_join_kernel(lat_ref, ctx_ref, g_ref, sh_ref, sc_ref, o_ref, h_ref, *, lat_blocks):
    def run(src_ref):
        h_ref[...] = src_ref[...]
        _modulate_kernel(src_ref, g_ref, sh_ref, sc_ref, o_ref)

    pl.when(pl.program_id(0) < lat_blocks)(lambda: run(lat_ref))
    pl.when(pl.program_id(0) >= lat_blocks)(lambda: run(ctx_ref))


def _modulate(h, n_rows, gain, mod_rows, layer, k_shift, k_scale):
    joined = isinstance(h, tuple)
    d = h[0].shape[1] if joined else h.shape[1]
    blk_per_seq = SEQ // ROW_BLK

    def mod_spec(k):
        return pl.BlockSpec(
            (None, 1, d),
            lambda i: ((layer * SUBLANE + jnp.minimum(i // blk_per_seq, BATCH)) * 6 + k, 0, 0))

    if joined:
        lat_blocks, ctx_blocks = h[0].shape[0] // ROW_BLK, h[1].shape[0] // ROW_BLK
        row_spec = pl.BlockSpec((ROW_BLK, d), lambda i: (i, 0))
        return pl.pallas_call(
            functools.partial(_modulate_join_kernel, lat_blocks=lat_blocks),
            grid=(lat_blocks + ctx_blocks,),
            in_specs=[pl.BlockSpec((ROW_BLK, d), lambda i: (jnp.minimum(i, lat_blocks - 1), 0)),
                      pl.BlockSpec((ROW_BLK, d), lambda i: (jnp.maximum(i - lat_blocks, 0), 0)),
                      pl.BlockSpec((1, d), lambda i: (0, 0)),
                      mod_spec(k_shift), mod_spec(k_scale)],
            out_specs=[row_spec, row_spec],
            out_shape=[jax.ShapeDtypeStruct((n_rows, d), BF16), jax.ShapeDtypeStruct((n_rows, d), F32)],
            compiler_params=_cp("parallel"),
            name="modulate_join",
        )(h[0], h[1], gain.reshape(1, d), mod_rows, mod_rows)
    return pl.pallas_call(
        _modulate_kernel,
        grid=(n_rows // ROW_BLK,),
        in_specs=[pl.BlockSpec((ROW_BLK, d), lambda i: (i, 0)),
                  pl.BlockSpec((1, d), lambda i: (0, 0)),
                  mod_spec(k_shift), mod_spec(k_scale)],
        out_specs=pl.BlockSpec((ROW_BLK, d), lambda i: (i, 0)),
        out_shape=jax.ShapeDtypeStruct((n_rows, d), BF16),
        compiler_params=_cp("parallel"),
        name="modulate",
    )(h, gain.reshape(1, d), mod_rows, mod_rows)


class _CastJob(NamedTuple):
    src: jax.Array
    layer: int
    rows_out: int
    cols_out: int
    start: int
    steps: int

    @property
    def tr(self):
        for tr in (16, 32, 64, 128, 256, 512, 1024):
            if self.rows_out % tr == 0 and self.src.shape[1] % tr == 0 and self.rows_out // tr <= self.steps:
                return tr
        raise ValueError("host kernel has too few steps for this cast")


def _hosted_call(kernel_fn, jobs, *, grid, in_specs, out_specs, out_shape, args, scratch_shapes=(), sem, name):
    out_specs, out_shape = list(out_specs), list(out_shape)
    n_in, n_out, n_jobs = len(in_specs), len(out_specs), len(jobs)
    strides = [math.prod(grid[a + 1:]) for a in range(len(grid))]

    def step_of(ids):
        return sum(i * s for i, s in zip(ids, strides))

    cast_in, cast_out, cast_shape = [], [], []
    for job in jobs:
        tr, n_blk, n_src_blk = job.tr, job.rows_out // job.tr, job.src.shape[1] // job.tr

        def blk(*ids, job=job, n_blk=n_blk):
            return jnp.clip(step_of(ids) - job.start, 0, n_blk - 1)

        cast_in.append(pl.BlockSpec((None, tr, job.src.shape[2]),
                                    lambda *ids, job=job, blk=blk, last=n_src_blk - 1:
                                    (job.layer, jnp.minimum(blk(*ids), last), 0)))
        cast_out.append(pl.BlockSpec((tr, job.cols_out), lambda *ids, blk=blk: (blk(*ids), 0)))
        cast_shape.append(jax.ShapeDtypeStruct((job.rows_out, job.cols_out), BF16))

    def kernel(*refs):
        host_in, src_refs = refs[:n_in], refs[n_in:n_in + n_jobs]
        host_out = refs[n_in + n_jobs:n_in + n_jobs + n_out]
        dst_refs = refs[n_in + n_jobs + n_out:n_in + 2 * n_jobs + n_out]
        kernel_fn(*host_in, *host_out, *refs[n_in + 2 * n_jobs + n_out:])
        step = step_of([pl.program_id(a) for a in range(len(grid))])
        for job, src_ref, dst_ref in zip(jobs, src_refs, dst_refs):
            rel, n_src = step - job.start, src_ref.shape[1]
            n_src_blk = job.src.shape[1] // job.tr

            @pl.when((rel >= 0) & (rel < n_src_blk))
            def _(src_ref=src_ref, dst_ref=dst_ref, n_src=n_src):
                dst_ref[:, :n_src] = src_ref[...].astype(dst_ref.dtype)
                if dst_ref.shape[1] > n_src:
                    dst_ref[:, n_src:] = jnp.zeros((dst_ref.shape[0], dst_ref.shape[1] - n_src), dst_ref.dtype)

            @pl.when((rel >= n_src_blk) & (rel < job.rows_out // job.tr))
            def _(dst_ref=dst_ref):
                dst_ref[...] = jnp.zeros_like(dst_ref)

    outs = pl.pallas_call(
        kernel,
        grid=grid,
        in_specs=list(in_specs) + cast_in,
        out_specs=out_specs + cast_out,
        out_shape=out_shape + cast_shape,
        scratch_shapes=list(scratch_shapes),
        compiler_params=_cp(*(["arbitrary"] * len(grid) if jobs else sem)),
        name=name,
    )(*args, *[job.src for job in jobs])
    return list(outs[:n_out]), list(outs[n_out:])


def _split_rows_kernel(w_ref, o_ref, *, gap0, gap1, end):
    i = pl.program_id(0)
    zero = ((i >= gap0) & (i < gap1)) | (i >= end)

    @pl.when(zero)
    def _():
        o_ref[...] = jnp.zeros_like(o_ref)

    @pl.when(jnp.logical_not(zero))
    def _():
        o_ref[...] = w_ref[...].astype(o_ref.dtype)


def _split_rows(w_stack, idx, split, second, n_out):
    _, n_in, k = w_stack.shape
    tr = math.gcd(split, second, n_out, n_in)
    assert tr % (2 * SUBLANE) == 0
    gap0, gap1, end, last = split // tr, second // tr, (second + n_in - split) // tr, n_in // tr - 1
    return pl.pallas_call(
        functools.partial(_split_rows_kernel, gap0=gap0, gap1=gap1, end=end),
        grid=(n_out // tr,),
        in_specs=[pl.BlockSpec((None, tr, k),
                               lambda i: (idx, jnp.where(i < gap0, i, jnp.clip(i - (gap1 - gap0), 0, last)), 0))],
        out_specs=pl.BlockSpec((tr, k), lambda i: (i, 0)),
        out_shape=jax.ShapeDtypeStruct((n_out, k), BF16),
        compiler_params=_cp("parallel"),
        name="split_rows",
    )(w_stack)


def _mm_kernel(*refs, n_pairs, gated, tm, w_t):
    a_refs, w_refs = refs[:n_pairs], refs[n_pairs:2 * n_pairs]
    dot = _dot_nt if w_t else _dot
    acc = dot(a_refs[0][...], w_refs[0][...])
    for a_ref, w_ref in zip(a_refs[1:], w_refs[1:]):
        acc += dot(a_ref[...], w_ref[...])
    if gated:
        res_ref, gate_ref, o_ref = refs[2 * n_pairs:]
        gate = _row_select(pl.program_id(0) * tm, tm, gate_ref[...])
        acc = res_ref[...] + gate * acc
    else:
        o_ref = refs[2 * n_pairs]
    o_ref[...] = acc.astype(o_ref.dtype)


def _mm_steps(n_rows, n, tn=None):
    return (n_rows // _pick(n_rows, (1088, 1024, 512, 256))) * (n // (tn or _pick(n, (1024, 512, 256, 128))))


def _mm(a_list, w, n_rows, *, n_cols=None, res=None, gate=None, tn=None, jobs=(), w_t=False, name="mm"):
    n_pairs = len(a_list)
    kdim = a_list[0].shape[1]
    n = n_cols or w.shape[0 if w_t else 1]
    tm = _pick(n_rows, (1088, 1024, 512, 256))
    tn = tn or _pick(n, (1024, 512, 256, 128))
    in_specs = [pl.BlockSpec((tm, kdim), lambda i, j: (i, 0)) for a in a_list]
    if w_t:
        assert n_pairs == 1 and w.shape[1] == kdim
        in_specs += [pl.BlockSpec((tn, kdim), lambda i, j: (j, 0))]
    else:
        assert all(a.shape[1] == kdim for a in a_list) and w.shape[0] == n_pairs * kdim
        in_specs += [pl.BlockSpec((kdim, tn), lambda i, j, p=p: (p, j)) for p in range(n_pairs)]
    args = list(a_list) + [w] * n_pairs
    if res is not None:
        table, layer, chunk = gate
        d = table.shape[2] // 6
        in_specs += [pl.BlockSpec((tm, tn), lambda i, j: (i, j)),
                     pl.BlockSpec((None, SUBLANE, tn), lambda i, j: (layer, 0, chunk * (d // tn) + j))]
        args += [res, table]
    (out,), casts = _hosted_call(
        functools.partial(_mm_kernel, n_pairs=n_pairs, gated=res is not None, tm=tm, w_t=w_t), jobs,
        grid=(n_rows // tm, n // tn),
        in_specs=in_specs,
        out_specs=[pl.BlockSpec((tm, tn), lambda i, j: (i, j))],
        out_shape=[jax.ShapeDtypeStruct((n_rows, n), F32)],
        args=args, sem=("parallel", "parallel"), name=name)
    return (out, casts) if jobs else out


def _mmk_kernel(a_ref, w_ref, res_ref, gate_ref, o_ref, acc_ref, *, tm):
    k = pl.program_id(2)

    @pl.when(k == 0)
    def _():
        acc_ref[...] = jnp.zeros_like(acc_ref)

    acc_ref[...] += _dot(a_ref[...], w_ref[...])

    @pl.when(k == pl.num_programs(2) - 1)
    def _():
        gate = _row_select(pl.program_id(0) * tm, tm, gate_ref[...])
        o_ref[...] = res_ref[...] + gate * acc_ref[...]


def _mm_ktiled_steps(kdim, n, n_rows):
    return ((n_rows // _pick(n_rows, (1088, 1024, 512, 256))) * (n // _pick(n, (1024, 512, 256, 128)))
            * (kdim // _pick(kdim, (2816, 2048, 1024, 512, 256))))


def _mm_ktiled(a, w, n_rows, res, gate, jobs=()):
    kdim, n = w.shape
    table, gate_layer, chunk = gate
    d = table.shape[2] // 6
    tm = _pick(n_rows, (1088, 1024, 512, 256))
    tn = _pick(n, (1024, 512, 256, 128))
    tk = _pick(kdim, (2816, 2048, 1024, 512, 256))
    (out,), casts = _hosted_call(
        functools.partial(_mmk_kernel, tm=tm), jobs,
        grid=(n_rows // tm, n // tn, kdim // tk),
        in_specs=[pl.BlockSpec((tm, tk), lambda i, j, k: (i, k)),
                  pl.BlockSpec((tk, tn), lambda i, j, k: (k, j)),
                  pl.BlockSpec((tm, tn), lambda i, j, k: (i, j)),
                  pl.BlockSpec((None, SUBLANE, tn), lambda i, j, k: (gate_layer, 0, chunk * (d // tn) + j))],
        out_specs=[pl.BlockSpec((tm, tn), lambda i, j, k: (i, j))],
        out_shape=[jax.ShapeDtypeStruct((n_rows, n), F32)],
        scratch_shapes=[pltpu.VMEM((tm, tn), F32)],
        args=(a, w, res, table), sem=("parallel", "parallel", "arbitrary"), name="ffn_down")
    return out, casts


def _ffn_up_kernel(u_ref, uh_ref, wg_ref, wu_ref, cw_ref, cb_ref, o_ref, halo_ref, *, tm):
    i = pl.program_id(1)

    @pl.when(i == 0)
    def _():
        halo_ref[...] = _dot(uh_ref[...], wg_ref[...])

    u = u_ref[...]
    g = _dot(u, wg_ref[...])
    dn, up = _shift_rows(g, halo_ref[pl.ds(2 * i, 1), :], halo_ref[pl.ds(2 * i + 1, 1), :], i * tm)
    a = dn * cw_ref[0:1, :] + g * cw_ref[1:2, :] + up * cw_ref[2:3, :] + cb_ref[...]
    o_ref[...] = (_silu(a) * _dot(u, wu_ref[...])).astype(o_ref.dtype)


def _ffn_up_steps(n_rows, n):
    return (n // _pick(n, (512, 256, 128))) * (n_rows // _pick(n_rows, (1088, 1024, 512, 256)))


def _ffn_up(u, n_rows, layer, wg, wu, cw_stack, cb_stack, jobs):
    d, n = wg.shape
    tm = _pick(n_rows, (1088, 1024, 512, 256))
    tn = _pick(n, (512, 256, 128))
    gm = n_rows // tm
    n_halo = _round_up(2 * gm, 2 * SUBLANE)
    rows = []
    for i in range(gm):
        rows += [max(i * tm - 1, 0), min((i + 1) * tm, n_rows - 1)]
    rows += [0] * (n_halo - len(rows))
    u_halo = jnp.concatenate([u[r:r + 1] for r in rows], axis=0)
    col = lambda j, i: (layer, 0, j)
    (hid,), casts = _hosted_call(
        functools.partial(_ffn_up_kernel, tm=tm), jobs,
        grid=(n // tn, gm),
        in_specs=[pl.BlockSpec((tm, d), lambda j, i: (i, 0)),
                  pl.BlockSpec((n_halo, d), lambda j, i: (0, 0)),
                  pl.BlockSpec((d, tn), lambda j, i: (0, j)),
                  pl.BlockSpec((d, tn), lambda j, i: (0, j)),
                  pl.BlockSpec((None, 3, tn), col),
                  pl.BlockSpec((None, 1, tn), col)],
        out_specs=[pl.BlockSpec((tm, tn), lambda j, i: (i, j))],
        out_shape=[jax.ShapeDtypeStruct((n_rows, n), BF16)],
        scratch_shapes=[pltpu.VMEM((n_halo, tn), F32)],
        args=(u, u_halo, wg, wu, cw_stack, cb_stack), sem=("parallel", "arbitrary"), name="ffn_up")
    return hid, casts


def _conv_ffn(h, n_rows, gain, mod_rows, mod, layer, wg, wu, cw, cb, wd_stack, next_w):
    d_ff_p = wg.shape[1]
    u = _modulate(h, n_rows, gain, mod_rows, layer, 3, 4)
    job = _CastJob(wd_stack, layer, d_ff_p, wd_stack.shape[2], 0, _ffn_up_steps(n_rows, d_ff_p))
    hid, (wd,) = _ffn_up(u, n_rows, layer, wg, wu, cw, cb, [job])
    jobs = []
    if next_w is not None:
        stack, idx = next_w
        jobs = [_CastJob(stack, idx, stack.shape[1], stack.shape[2], 0, _mm_ktiled_steps(d_ff_p, wd.shape[1], n_rows))]
    h, casts = _mm_ktiled(hid, wd, n_rows, h, (mod, layer, 5), jobs)
    return h, (casts[0] if casts else None)


def _rope(x, rope_ref):
    c, sa, sb = rope_ref[:, 0:LANE], rope_ref[:, LANE:2 * LANE], rope_ref[:, 2 * LANE:3 * LANE]
    quarter = MLA_ROPE // 4
    return x * c + pltpu.roll(x, LANE - quarter, 1) * sa + pltpu.roll(x, quarter, 1) * sb


def _rms(x, width):
    return lax.rsqrt(jnp.sum(x * x, axis=-1, keepdims=True) * (1.0 / width) + EPS)


def _qproj_kernel(cq_ref, ag_ref, w_ref, hg_ref, rope_ref, o_ref, *, heads):
    x = cq_ref[...]
    xn = ((x * _rms(x, x.shape[1])) * ag_ref[...]).astype(BF16)
    y = _dot(xn, w_ref[...])
    for hh in range(heads):
        yh = y[:, hh * QK_PAD:(hh + 1) * QK_PAD]
        yn = (yh * _rms(yh, MLA_NOPE + MLA_ROPE)) * hg_ref[...]
        o_ref[:, hh * QK_PAD:hh * QK_PAD + LANE] = yn[:, :LANE].astype(o_ref.dtype)
        o_ref[:, hh * QK_PAD + LANE:(hh + 1) * QK_PAD] = _rope(yn[:, LANE:], rope_ref).astype(o_ref.dtype)


def _qproj(y_in, n_rows, ag, w, hg, rope):
    heads = _pick(MLA_HEADS, (4, 2, 1))
    tm = _proj_rows()
    tn = heads * QK_PAD
    return pl.pallas_call(
        functools.partial(_qproj_kernel, heads=heads),
        grid=(n_rows // tm, w.shape[1] // tn),
        in_specs=[pl.BlockSpec((tm, MLA_Q_RANK), lambda i, j: (i, 0)),
                  pl.BlockSpec((1, MLA_Q_RANK), lambda i, j: (0, 0)),
                  pl.BlockSpec((MLA_Q_RANK, tn), lambda i, j: (0, j)),
                  pl.BlockSpec((1, QK_PAD), lambda i, j: (0, 0)),
                  _rope_spec()],
        out_specs=pl.BlockSpec((tm, tn), lambda i, j: (i, j)),
        out_shape=jax.ShapeDtypeStruct((n_rows, w.shape[1]), BF16),
        compiler_params=_cp("parallel", "parallel"),
        name="q_proj",
    )(y_in, ag, w, hg, rope)


def _kvproj_kernel(ckv_ref, kr_ref, ag_ref, w_ref, gn_ref, gr_ref, rope_ref, k_ref, v_ref, *, heads):
    x = ckv_ref[...]
    xn = ((x * _rms(x, x.shape[1])) * ag_ref[...]).astype(BF16)
    y = _dot(xn, w_ref[...])
    kr = kr_ref[...]
    kr_ss = jnp.sum(kr * kr, axis=-1, keepdims=True)
    width = MLA_NOPE + MLA_VDIM
    ones = jnp.ones((x.shape[0], V_PAD - MLA_VDIM), v_ref.dtype)
    for hh in range(heads):
        kn = y[:, hh * width:hh * width + MLA_NOPE]
        r = lax.rsqrt((jnp.sum(kn * kn, axis=-1, keepdims=True) + kr_ss) * (1.0 / (MLA_NOPE + MLA_ROPE)) + EPS)
        k_ref[:, hh * QK_PAD:hh * QK_PAD + LANE] = ((kn * r) * gn_ref[...]).astype(k_ref.dtype)
        k_ref[:, hh * QK_PAD + LANE:(hh + 1) * QK_PAD] = _rope((kr * r) * gr_ref[...], rope_ref).astype(k_ref.dtype)
        v_ref[:, hh * V_PAD:hh * V_PAD + MLA_VDIM] = y[:, hh * width + MLA_NOPE:(hh + 1) * width].astype(v_ref.dtype)
        v_ref[:, hh * V_PAD + MLA_VDIM:(hh + 1) * V_PAD] = ones


def _kvproj(y_in, n_rows, ag, w, gn, gr, rope):
    heads = _pick(MLA_HEADS, (4, 2, 1))
    assert MLA_NOPE == LANE and MLA_VDIM == LANE and MLA_Q_RANK % MLA_KV_RANK == 0
    tm = _proj_rows()
    tn = heads * (MLA_NOPE + MLA_VDIM)
    kr_blk = (MLA_Q_RANK + MLA_KV_RANK) // LANE
    return pl.pallas_call(
        functools.partial(_kvproj_kernel, heads=heads),
        grid=(n_rows // tm, w.shape[1] // tn),
        in_specs=[pl.BlockSpec((tm, MLA_KV_RANK), lambda i, j: (i, MLA_Q_RANK // MLA_KV_RANK)),
                  pl.BlockSpec((tm, LANE), lambda i, j: (i, kr_blk)),
                  pl.BlockSpec((1, MLA_KV_RANK), lambda i, j: (0, 0)),
                  pl.BlockSpec((MLA_KV_RANK, tn), lambda i, j: (0, j)),
                  pl.BlockSpec((1, LANE), lambda i, j: (0, 0)),
                  pl.BlockSpec((1, LANE), lambda i, j: (0, 0)),
                  _rope_spec()],
        out_specs=[pl.BlockSpec((tm, heads * QK_PAD), lambda i, j: (i, j)),
                   pl.BlockSpec((tm, heads * V_PAD), lambda i, j: (i, j))],
        out_shape=[jax.ShapeDtypeStruct((n_rows, MLA_HEADS * QK_PAD), BF16),
                   jax.ShapeDtypeStruct((n_rows, MLA_HEADS * V_PAD), BF16)],
        compiler_params=_cp("parallel", "parallel"),
        name="kv_proj",
    )(y_in, y_in, ag, w, gn, gr, rope)


def _attn_kernel(*refs, chunks, scale):
    q = refs[0][...]
    n_kv = (len(refs) - 2) // 2
    k_refs, v_refs, o_ref = refs[1:1 + n_kv], refs[1 + n_kv:1 + 2 * n_kv], refs[1 + 2 * n_kv]
    c = scale * math.log2(math.e)
    m = acc = None
    for idx, start, size in chunks:
        s = _dot_nt(q, k_refs[idx][start:start + size, :])
        m_blk = jnp.max(s, axis=-1, keepdims=True)
        m_new = m_blk if m is None else jnp.maximum(m, m_blk)
        pv = _dot(jnp.exp2((s - m_new) * c).astype(BF16), v_refs[idx][start:start + size, :])
        acc = pv if acc is None else jnp.exp2((m - m_new) * c) * acc + pv
        m = m_new
    o_ref[...] = (acc[:, :MLA_VDIM] / acc[:, MLA_VDIM:2 * MLA_VDIM]).astype(o_ref.dtype)


def _attn_lat_steps():
    return BATCH * MLA_HEADS * (SEQ // _pick(SEQ, (1024, 512, 256)))


def _attention(q, k, v, latent, n_out, jobs=(), o_prev=None):
    assert V_PAD == 2 * MLA_VDIM
    scale = (MLA_NOPE + MLA_ROPE) ** -0.5
    ctx_blk0 = _n_lat() // CTX_LEN
    ctx_k = pl.BlockSpec((CTX_LEN, QK_PAD), lambda b, h, i: (ctx_blk0 + b, h))
    ctx_v = pl.BlockSpec((CTX_LEN, V_PAD), lambda b, h, i: (ctx_blk0 + b, h))
    if latent:
        tq = _pick(SEQ, (1024, 512, 256))
        tk = _pick(SEQ, (1024, 512, 256))
        q_per_b = SEQ // tq
        q_spec = pl.BlockSpec((tq, QK_PAD), lambda b, h, i: (b * q_per_b + i, h))
        o_spec = pl.BlockSpec((tq, MLA_VDIM), lambda b, h, i: (b * q_per_b + i, h))
        k_specs = [ctx_k, pl.BlockSpec((SEQ, QK_PAD), lambda b, h, i: (b, h))]
        v_specs = [ctx_v, pl.BlockSpec((SEQ, V_PAD), lambda b, h, i: (b, h))]
        chunks = [(0, 0, CTX_LEN)] + [(1, s, tk) for s in range(0, SEQ, tk)]
        grid = (BATCH, MLA_HEADS, q_per_b)
    else:
        q_spec = pl.BlockSpec((CTX_LEN, QK_PAD), lambda b, h, i: (ctx_blk0 + b, h))
        o_spec = pl.BlockSpec((CTX_LEN, MLA_VDIM), lambda b, h, i: (ctx_blk0 + b, h))
        k_specs, v_specs, chunks = [ctx_k], [ctx_v], [(0, 0, CTX_LEN)]
        grid = (BATCH, MLA_HEADS, 1)
    n_kv = len(k_specs)
    in_specs = [q_spec] + k_specs + v_specs
    args = (q, *([k] * n_kv), *([v] * n_kv))
    out_shape = jax.ShapeDtypeStruct((n_out, MLA_HEADS * MLA_VDIM), BF16)
    body = functools.partial(_attn_kernel, chunks=tuple(chunks), scale=scale)
    if latent:
        (out,), casts = _hosted_call(body, jobs, grid=grid, in_specs=in_specs, out_specs=[o_spec],
                                     out_shape=[out_shape], args=args,
                                     sem=("parallel", "parallel", "arbitrary"), name="attn_lat")
        return out, casts
    return pl.pallas_call(
        lambda *refs: body(*refs[:len(in_specs)], refs[-1]),
        grid=grid,
        in_specs=in_specs + [pl.BlockSpec(memory_space=pl.ANY)],
        out_specs=o_spec,
        out_shape=out_shape,
        input_output_aliases={len(in_specs): 0},
        compiler_params=_cp("parallel", "parallel", "arbitrary"),
        name="attn_ctx",
    )(*args, o_prev)


def _sconv_kernel(b_ref, c_ref, x_ref, cp_ref, xp_ref, cn_ref, xn_ref, w_ref, o_ref, *, tm):
    p = c_ref[...] * x_ref[...]
    prev = cp_ref[SUBLANE - 1:SUBLANE, :] * xp_ref[SUBLANE - 1:SUBLANE, :]
    nxt = cn_ref[0:1, :] * xn_ref[0:1, :]
    dn, up = _shift_rows(p, prev, nxt, pl.program_id(0) * tm)
    conv = dn * w_ref[0:1, :] + p * w_ref[1:2, :] + up * w_ref[2:3, :]
    o_ref[...] = (b_ref[...] * conv).astype(o_ref.dtype)


def _sconv(y_in, n_rows, w, col0, width):
    tm = ROW_BLK
    tn = _pick(width, (1024, 512, 256, 128))
    blk0 = [(col0 + k * width) // tn for k in range(3)]
    main = [pl.BlockSpec((tm, tn), lambda i, j, o=o: (i, o + j)) for o in blk0]
    cp, cn = _halo_specs(tm, tn, n_rows, blk0[1])
    xp, xn = _halo_specs(tm, tn, n_rows, blk0[2])
    return pl.pallas_call(
        functools.partial(_sconv_kernel, tm=tm),
        grid=(n_rows // tm, width // tn),
        in_specs=main + [cp, xp, cn, xn, pl.BlockSpec((3, tn), lambda i, j: (0, j))],
        out_specs=pl.BlockSpec((tm, tn), lambda i, j: (i, j)),
        out_shape=jax.ShapeDtypeStruct((n_rows, width), BF16),
        compiler_params=_cp("parallel", "parallel"),
        name="short_conv",
    )(*([y_in] * 7), w)


def _conv_silu(x_ref, xp_ref, xn_ref, w_ref, row0):
    x = x_ref[...]
    dn, up = _shift_rows(x, xp_ref[SUBLANE - 1:SUBLANE, :], xn_ref[0:1, :], row0)
    return _silu(dn * w_ref[0:1, :] + x * w_ref[1:2, :] + up * w_ref[2:3, :])


def _l2norm(x):
    return x * lax.rsqrt(jnp.sum(x * x, axis=-1, keepdims=True) + EPS)


def _gdn_gate_kernel(ba_ref, alog_ref, dtb_ref, o_ref):
    x = ba_ref[...]
    lane = lax.broadcasted_iota(jnp.int32, x.shape, 1)
    z = x + dtb_ref[...]
    softplus = jnp.maximum(z, 0.0) + jnp.log(1.0 + jnp.exp(-jnp.abs(z)))
    o_ref[...] = jnp.where(lane < x.shape[1] // 2, _sigmoid(x), -jnp.exp(alog_ref[...]) * softplus)


def _gdn_gate(ba, n_rows, alog, dtb):
    w = ba.shape[1]
    return pl.pallas_call(
        _gdn_gate_kernel,
        grid=(n_rows // ROW_BLK,),
        in_specs=[pl.BlockSpec((ROW_BLK, w), lambda i: (i, 0)),
                  pl.BlockSpec((1, w), lambda i: (0, 0)),
                  pl.BlockSpec((1, w), lambda i: (0, 0))],
        out_specs=pl.BlockSpec((ROW_BLK, w), lambda i: (i, 0)),
        out_shape=jax.ShapeDtypeStruct((n_rows, w), F32),
        compiler_params=_cp("parallel"),
        name="gdn_gate",
    )(ba, alog, dtb)


def _active_rows(x, size, odd):
    n = x.shape[0]
    return jnp.concatenate([x[(2 * b + odd) * size:(2 * b + odd + 1) * size] for b in range(n // (2 * size))], axis=0)


def _weave_rows(rest, active, size, odd):
    pieces = []
    for b in range(active.shape[0] // size):
        if rest is None:
            keep = jnp.zeros((size, active.shape[1]), active.dtype)
        else:
            keep = rest[(2 * b + 1 - odd) * size:(2 * b + 2 - odd) * size]
        act = active[b * size:(b + 1) * size]
        pieces += [keep, act] if odd else [act, keep]
    return jnp.concatenate(pieces, axis=0)


def _unit_tri_inverses(l_mats, dirs, eye, base, ring_ref):
    size = SUBLANE
    xs = [-(l * base) for l in l_mats]
    ts = [eye + x for x in xs]
    for _ in range(size.bit_length() - 2):
        xs = [_dot(x.astype(BF16), x.astype(BF16)) for x in xs]
        ts = [t + _dot(t.astype(BF16), x.astype(BF16)) for t, x in zip(ts, xs)]
    level = 0
    while size < GDN_BLK:
        tbs = [t.astype(BF16) for t in ts]
        offs = [(_active_rows(l, size, 1 - d) * ring_ref[level, d]).astype(BF16) for l, d in zip(l_mats, dirs)]
        mids = [_weave_rows(None, _dot(off, tb), size, 1 - d).astype(BF16) for off, tb, d in zip(offs, tbs, dirs)]
        acts = [_active_rows(t, size, 1 - d) for t, d in zip(ts, dirs)]
        news = [act - _dot(act.astype(BF16), mid) for act, mid in zip(acts, mids)]
        ts = [_weave_rows(t, new, size, 1 - d) for t, new, d in zip(ts, news, dirs)]
        size *= 2
        level += 1
    return ts


def _gdn_masks():
    n, size = GDN_BLK, SUBLANE
    i, j = np.arange(n)[:, None], np.arange(n)[None, :]
    tri = np.stack([j <= i, j >= i, j == i, i // size == j // size]).astype(np.float32)
    rings = []
    while size < n:
        ra = np.arange(n // 2)[:, None]
        per_dir = []
        for d in range(2):
            ia = (ra // size) * 2 * size + (1 - d) * size + ra % size
            per_dir.append((ia // (2 * size) == j // (2 * size)) & (ia // size != j // size))
        rings.append(np.stack(per_dir))
        size *= 2
    return jnp.asarray(tri), jnp.asarray(np.stack(rings).astype(np.float32))


def _gdn_prep_kernel(q_ref, qp_ref, qn_ref, k_ref, kp_ref, kn_ref, v_ref, vp_ref, vn_ref, cwq_ref, cwk_ref, cwv_ref,
                     bg_ref, tri_ref, tri16_ref, ring_ref, u_ref, wq_ref, kd_ref, in_ref, cd_ref, *, rep, hpb):
    n = GDN_BLK
    row0 = pl.program_id(0) * n
    q_all = _conv_silu(q_ref, qp_ref, qn_ref, cwq_ref, row0)
    k_all = _conv_silu(k_ref, kp_ref, kn_ref, cwk_ref, row0)
    v_all = _conv_silu(v_ref, vp_ref, vn_ref, cwv_ref, row0)
    eye, base = tri_ref[2], tri_ref[3]
    not_eye = 1.0 - eye
    chains, l_mats, rhss = [], [], []
    for hh in range(hpb):
        q = _l2norm(q_all[:, hh * GDN_DK:(hh + 1) * GDN_DK]) * GDN_DK ** -0.5
        k = _l2norm(k_all[:, hh * GDN_DK:(hh + 1) * GDN_DK])
        q16, k16 = q.astype(BF16), k.astype(BF16)
        qk = _dot_nt(q16, k16)
        kk = _dot_nt(k16, k16) * not_eye
        bg = bg_ref[hh]
        bg_cols = jnp.concatenate([bg, jnp.zeros((LANE - bg.shape[0], n), F32)], axis=0).T
        hi = bg_cols.astype(BF16)
        rest = bg_cols - hi.astype(F32)
        mid = rest.astype(BF16)
        pieces = jnp.concatenate([hi, mid, (rest - mid.astype(F32)).astype(BF16)], axis=1)
        cums = [_dot(tri16_ref[d], pieces) for d in range(2)]
        gc_cols = [c[:, :LANE] + c[:, LANE:2 * LANE] + c[:, 2 * LANE:] for c in cums]
        gc_rows = [g.T for g in gc_cols]
        for e in range(rep):
            for d in range(2):
                vh = hh * rep + e
                cols = slice(vh * GDN_DV, (vh + 1) * GDN_DV)
                incl = tri_ref[d]
                b_idx, g_idx = d * rep + e, (2 + d) * rep + e
                b_col = bg_cols[:, b_idx:b_idx + 1]
                gc_col, gc_row = gc_cols[d][:, g_idx:g_idx + 1], gc_rows[d][g_idx:g_idx + 1, :]
                g_tot = jnp.sum(bg[g_idx:g_idx + 1, :], axis=1, keepdims=True)
                decay = jnp.exp(jnp.minimum(gc_col - gc_row, 0.0)) * incl
                e_col = jnp.exp(gc_col)
                chains.append((vh, d, cols))
                l_mats.append((kk * b_col) * decay)
                rhss.append(jnp.concatenate([v_all[:, cols] * b_col, (k * b_col) * e_col], axis=1).astype(BF16))
                wq_ref[d, n:, cols] = (q * e_col).astype(wq_ref.dtype)
                kd_ref[d, :, cols] = (k * jnp.exp(g_tot - gc_col)).astype(kd_ref.dtype)
                in_ref[d, vh] = (qk * decay).astype(in_ref.dtype)
                cd_ref[d, vh] = jnp.broadcast_to(jnp.exp(g_tot), (SUBLANE, LANE))
    invs = _unit_tri_inverses(l_mats, [d for _, d, _ in chains], eye, base, ring_ref)
    sols = [_dot(t.astype(BF16), rhs) for t, rhs in zip(invs, rhss)]
    for (vh, d, cols), sol in zip(chains, sols):
        u_ref[d, :, cols] = sol[:, :GDN_DV]
        wq_ref[d, :n, cols] = sol[:, GDN_DV:].astype(wq_ref.dtype)


def _gdn_prep(y_in, conv_w, bg_t, n_rows):
    assert GDN_DK == LANE and GDN_DV == LANE and GDN_BLK == 2 * LANE
    nb, hv, hq, rep = n_rows // GDN_BLK, GDN_V_HEADS, GDN_QK_HEADS, GDN_V_HEADS // GDN_QK_HEADS
    hpb = _pick(hq, (2, 1))
    key_w, wide = hpb * GDN_DK, hpb * rep * GDN_DV
    parts = ((key_w, 0), (key_w, hq // hpb), (wide, 2 * hq * GDN_DK // wide))
    in_specs = []
    for tn, blk0 in parts:
        in_specs += [pl.BlockSpec((GDN_BLK, tn), lambda i, h, blk0=blk0: (i, blk0 + h))]
        in_specs += _halo_specs(GDN_BLK, tn, n_rows, blk0)
    in_specs += [pl.BlockSpec((3, tn), lambda i, h, blk0=blk0: (0, blk0 + h)) for tn, blk0 in parts]
    tri, rings = _gdn_masks()
    in_specs += [pl.BlockSpec((hpb, bg_t.shape[1], GDN_BLK), lambda i, h: (h, 0, i)),
                 pl.BlockSpec(tri.shape, lambda i, h: (0, 0, 0)),
                 pl.BlockSpec((2,) + tri.shape[1:], lambda i, h: (0, 0, 0)),
                 pl.BlockSpec(rings.shape, lambda i, h: (0, 0, 0, 0))]
    return pl.pallas_call(
        functools.partial(_gdn_prep_kernel, rep=rep, hpb=hpb),
        grid=(nb, hq // hpb),
        in_specs=in_specs,
        out_specs=[pl.BlockSpec((2, GDN_BLK, wide), lambda i, h: (0, i, h)),
                   pl.BlockSpec((2, None, 2 * GDN_BLK, wide), lambda i, h: (0, i, 0, h)),
                   pl.BlockSpec((2, GDN_BLK, wide), lambda i, h: (0, i, h)),
                   pl.BlockSpec((2, hpb * rep, GDN_BLK, GDN_BLK), lambda i, h: (0, h, i, 0)),
                   pl.BlockSpec((2, hpb * rep, SUBLANE, LANE), lambda i, h: (0, h, i, 0))],
        out_shape=[jax.ShapeDtypeStruct((2, n_rows, hv * GDN_DV), F32),
                   jax.ShapeDtypeStruct((2, nb, 2 * GDN_BLK, hv * GDN_DV), BF16),
                   jax.ShapeDtypeStruct((2, n_rows, hv * GDN_DV), BF16),
                   jax.ShapeDtypeStruct((2, hv, n_rows, GDN_BLK), BF16),
                   jax.ShapeDtypeStruct((2, hv, nb * SUBLANE, LANE), F32)],
        compiler_params=_cp("parallel", "parallel"),
        name="gdn_prep",
    )(*([y_in] * 9), conv_w, conv_w, conv_w, bg_t, tri, tri[:2].astype(BF16), rings)


def _gdn_scan_kernel(*refs, heads):
    ins, (of_ref, ob_ref, s_ref) = refs[:10], refs[10:]
    n = GDN_BLK

    @pl.when(pl.program_id(2) == 0)
    def _():
        s_ref[...] = jnp.zeros_like(s_ref)

    o_refs = (of_ref, ob_ref)
    chains = [(d, hh, slice(hh * GDN_DV, (hh + 1) * GDN_DV)) for d in range(2) for hh in range(heads)]
    u_refs, wq_refs, kd_refs, in_refs, cd_refs = (ins[0::5], ins[1::5], ins[2::5], ins[3::5], ins[4::5])
    states = [s_ref[d, hh] for d, hh, _ in chains]
    wqs = [_dot(wq_refs[d][:, cols], s.astype(BF16)) for (d, hh, cols), s in zip(chains, states)]
    vbs = [(u_refs[d][:, cols] - wq[:n]).astype(BF16) for (d, hh, cols), wq in zip(chains, wqs)]
    for (d, hh, cols), s, wq, vb in zip(chains, states, wqs, vbs):
        o_refs[d][:, cols] = wq[n:] + _dot(in_refs[d][hh], vb)
        s_ref[d, hh] = s * cd_refs[d][hh, 0:1, :] + _dot_tn(kd_refs[d][:, cols], vb)


def _gdn_scan(prep, n_rows):
    hv = GDN_V_HEADS
    heads = _pick(hv, (4, 2, 1))
    lat_blk, ctx_blk = SEQ // GDN_BLK, CTX_LEN // GDN_BLK
    steps = ctx_blk + lat_blk
    ctx0 = _n_lat() // GDN_BLK
    wide = heads * GDN_DV

    def blk_f(b, n):
        return jnp.where(n < ctx_blk, ctx0 + b * ctx_blk + n, b * lat_blk + n - ctx_blk)

    def blk_b(b, n):
        return jnp.where(n < ctx_blk, ctx0 + b * ctx_blk + ctx_blk - 1 - n, b * lat_blk + steps - 1 - n)

    in_specs = []
    for d, blk in enumerate((blk_f, blk_b)):
        rows = pl.BlockSpec((None, GDN_BLK, wide), lambda b, h, n, d=d, blk=blk: (d, blk(b, n), h))
        in_specs += [rows,
                     pl.BlockSpec((None, None, 2 * GDN_BLK, wide), lambda b, h, n, d=d, blk=blk: (d, blk(b, n), 0, h)),
                     rows,
                     pl.BlockSpec((None, heads, GDN_BLK, GDN_BLK), lambda b, h, n, d=d, blk=blk: (d, h, blk(b, n), 0)),
                     pl.BlockSpec((None, heads, SUBLANE, LANE), lambda b, h, n, d=d, blk=blk: (d, h, blk(b, n), 0))]
    out = jax.ShapeDtypeStruct((n_rows, hv * GDN_DV), F32)
    return pl.pallas_call(
        functools.partial(_gdn_scan_kernel, heads=heads),
        grid=(BATCH, hv // heads, steps),
        in_specs=in_specs,
        out_specs=[pl.BlockSpec((GDN_BLK, wide), lambda b, h, n: (blk_f(b, n), h)),
                   pl.BlockSpec((GDN_BLK, wide), lambda b, h, n: (blk_b(b, n), h))],
        out_shape=[out, out],
        scratch_shapes=[pltpu.VMEM((2, heads, GDN_DK, GDN_DV), F32)],
        compiler_params=_cp("parallel", "parallel", "arbitrary"),
        name="gdn_scan",
    )(*prep, *prep)


def _gdn_readout_kernel(of_ref, ob_ref, z_ref, g_ref, o_ref):
    o = of_ref[...] + ob_ref[...]
    z = z_ref[...]
    for hh in range(o.shape[1] // GDN_DV):
        cols = slice(hh * GDN_DV, (hh + 1) * GDN_DV)
        oh = o[:, cols]
        y = (oh * _rms(oh, GDN_DV)) * g_ref[...]
        o_ref[:, cols] = (y * _silu(z[:, cols])).astype(o_ref.dtype)


def _gdn_readout(o_f, o_b, y_in, z_col0, gain, n_rows):
    width = GDN_V_HEADS * GDN_DV
    tn = _pick(width, (1024, 512, 256, 128))
    spec = pl.BlockSpec((ROW_BLK, tn), lambda i, j: (i, j))
    return pl.pallas_call(
        _gdn_readout_kernel,
        grid=(n_rows // ROW_BLK, width // tn),
        in_specs=[spec, spec, pl.BlockSpec((ROW_BLK, tn), lambda i, j: (i, z_col0 // tn + j)),
                  pl.BlockSpec((1, GDN_DV), lambda i, j: (0, 0))],
        out_specs=spec,
        out_shape=jax.ShapeDtypeStruct((n_rows, width), BF16),
        compiler_params=_cp("parallel", "parallel"),
        name="gdn_readout",
    )(o_f, o_b, y_in, gain.reshape(1, GDN_DV))


def _proj_rows():
    return _pick(math.gcd(SEQ, BATCH * CTX_LEN), (512, 256))


def _rope_spec():
    tm = _proj_rows()
    return pl.BlockSpec((tm, 3 * LANE), lambda i, j: (jnp.where(i < _n_lat() // tm, i % (SEQ // tm), SEQ // tm), 0))


def _rope_table(n_ident):
    n_freq = MLA_ROPE // 4
    inv_freq = ROPE_THETA ** (-jnp.arange(n_freq, dtype=F32) / n_freq)
    t = jnp.arange(SEQ)
    ang = jnp.concatenate([(t // GRID_W).astype(F32)[:, None] * inv_freq,
                           (t % GRID_W).astype(F32)[:, None] * inv_freq], axis=1)
    cos, sin = jnp.cos(ang), jnp.sin(ang)
    zero = jnp.zeros_like(sin[:, :n_freq])
    pad = jnp.zeros((SEQ, LANE - MLA_ROPE), F32)
    c = jnp.concatenate([cos[:, :n_freq], cos[:, :n_freq], cos[:, n_freq:], cos[:, n_freq:], pad + 1.0], axis=1)
    sa = jnp.concatenate([-sin[:, :n_freq], zero, -sin[:, n_freq:], zero, pad], axis=1)
    sb = jnp.concatenate([zero, sin[:, :n_freq], zero, sin[:, n_freq:], pad], axis=1)
    ident = jnp.concatenate([jnp.ones((n_ident, LANE), F32), jnp.zeros((n_ident, 2 * LANE), F32)], axis=1)
    return jnp.concatenate([jnp.concatenate([c, sa, sb], axis=1), ident], axis=0)


def _pad_cols(w, n):
    return jnp.pad(w, ((0, 0), (0, n - w.shape[1])))


def _even_layer(h, n_rows, need_ctx, mod, mod_rows, layer, gain, p, idx, ffn):
    w_in_stack, q_a_g, kv_a_g, w_qb, w_kvb, q_g, k_g, sc_w, w_out_stack = p
    ffn_w_gate, ffn_w_up, d_ff_p = ffn
    d = w_out_stack.shape[2]
    sc_width = d - MLA_HEADS * MLA_VDIM
    assert sc_width == MLA_HEADS * MLA_VDIM
    head_w = MLA_NOPE + MLA_ROPE
    c0 = MLA_Q_RANK + MLA_KV_RANK
    conv0 = _round_up(c0 + LANE, _pick(sc_width, (1024, 512, 256, 128)))
    n_in = _round_up(conv0 + 3 * sc_width, 1024)
    w_in_t = _split_rows(jnp.swapaxes(w_in_stack, 1, 2), idx, c0 + MLA_ROPE, conv0, n_in)
    w_q = w_qb.reshape(MLA_Q_RANK, MLA_HEADS, head_w)
    w_q = jnp.pad(w_q, ((0, 0), (0, 0), (0, QK_PAD - head_w))).reshape(MLA_Q_RANK, MLA_HEADS * QK_PAD).astype(BF16)
    rope = _rope_table(_proj_rows())

    if isinstance(h, tuple):
        u, h = _modulate(h, n_rows, gain, mod_rows, layer, 0, 1)
    else:
        u = _modulate(h, n_rows, gain, mod_rows, layer, 0, 1)
    y = _mm([u], w_in_t, n_rows, w_t=True, name="even_in")
    q = _qproj(y, n_rows, q_a_g.reshape(1, -1), w_q, jnp.pad(q_g, (0, QK_PAD - head_w)).reshape(1, QK_PAD), rope)
    k, v = _kvproj(y, n_rows, kv_a_g.reshape(1, -1), w_kvb.astype(BF16), k_g[:MLA_NOPE].reshape(1, LANE),
                   jnp.pad(k_g[MLA_NOPE:], (0, LANE - MLA_ROPE)).reshape(1, LANE), rope)
    steps = _attn_lat_steps()
    jobs = [_CastJob(ffn_w_gate, layer, d, d_ff_p, 0, steps // 2),
            _CastJob(ffn_w_up, layer, d, d_ff_p, steps // 2, steps - steps // 2),
            _CastJob(w_out_stack, idx, w_out_stack.shape[1], d, 0, steps)]
    n_out = n_rows if need_ctx else _n_lat()
    o, (wg, wu, w_out_b) = _attention(q, k, v, True, n_out, jobs)
    if need_ctx:
        o = _attention(q, k, v, False, n_out, o_prev=o)
    conv = _sconv(y, n_out, sc_w, conv0, sc_width)
    h = _mm([o, conv], w_out_b, n_out, res=h, gate=(mod, layer, 2),
            tn=_pick(d, (512, 256, 128)), name="even_out")
    return h, (wg, wu)


def _odd_layer(h, n_rows, need_ctx, mod, mod_rows, layer, gain, p, idx, ffn, w_in_b):
    w_in_stack, conv_w, a_log, dt_bias, o_g, w_out_stack = p
    ffn_w_gate, ffn_w_up, d_ff_p = ffn
    d = h.shape[1]
    hv, hq = GDN_V_HEADS, GDN_QK_HEADS
    qkv_w = 2 * hq * GDN_DK + hv * GDN_DV
    main_w = qkv_w + hv * GDN_DV
    gate_w = _round_up(4 * hv, LANE)
    half = gate_w // 2
    if w_in_b is None:
        w_in_b = w_in_stack[idx].astype(BF16)
    w_ba = jnp.concatenate([_pad_cols(w_in_b[:, main_w:main_w + 2 * hv], half),
                            _pad_cols(w_in_b[:, main_w + 2 * hv:], half)], axis=1)
    zeros = jnp.zeros((1, half), F32)
    alog = jnp.concatenate([zeros, _pad_cols(a_log.reshape(1, -1), half)], axis=1)
    dtb = jnp.concatenate([zeros, _pad_cols(dt_bias.reshape(1, -1), half)], axis=1)

    u = _modulate(h, n_rows, gain, mod_rows, layer, 0, 1)
    tn = _pick(main_w, (512, 256, 128))
    steps = _mm_steps(n_rows, main_w, tn)
    y, (wg, w_out_b) = _mm([u], w_in_b, n_rows, n_cols=main_w, tn=tn,
                           jobs=[_CastJob(ffn_w_gate, layer, d, d_ff_p, 0, steps),
                                 _CastJob(w_out_stack, idx, w_out_stack.shape[1], d, 0, steps)], name="odd_in")
    ba = _mm([u], w_ba, n_rows, name="odd_in_gates")
    bg = _gdn_gate(ba, n_rows, alog, dtb)
    rep = hv // hq
    bg_t = jnp.stack([bg[:, :2 * hv], bg[:, half:half + 2 * hv]], axis=0)
    bg_t = bg_t.reshape(2, n_rows, 2, hq, rep).transpose(3, 0, 2, 4, 1).reshape(hq, 4 * rep, n_rows)
    bg_t = jnp.pad(bg_t, ((0, 0), (0, _round_up(4 * rep, SUBLANE) - 4 * rep), (0, 0)))
    o_f, o_b = _gdn_scan(_gdn_prep(y, conv_w, bg_t, n_rows), n_rows)
    n_out = n_rows if need_ctx else _n_lat()
    yo = _gdn_readout(o_f, o_b, y, qkv_w, o_g, n_out)
    tn = _pick(d, (512, 256, 128))
    h, (wu,) = _mm([yo], w_out_b, n_out, res=h, gate=(mod, layer, 2), tn=tn,
                   jobs=[_CastJob(ffn_w_up, layer, d, d_ff_p, 0, _mm_steps(n_out, d, tn))], name="odd_out")
    return h, (wg, wu)


def kernel(x, c, ctx, c_ctx, ada_w, ada_b, norm_mix, norm_ffn, ffn_w_gate, ffn_w_up, ffn_conv_w, ffn_conv_b,
           ffn_w_down, a_w_in, a_q_a_norm, a_kv_a_norm, a_w_qb, a_w_kvb, a_q_norm, a_k_norm, a_sc_conv, a_w_out,
           c_w_in, c_conv_w, c_a_log, c_dt_bias, c_o_norm, c_w_out):
    bn, t, d = x.shape
    depth = ada_w.shape[0]
    assert (bn, t, ctx.shape[1], ffn_w_gate.shape[2]) == (BATCH, SEQ, CTX_LEN, D_FF)
    assert CTX_LEN % GDN_BLK == 0 and SEQ % GDN_BLK == 0
    h = (x.reshape(bn * t, d), ctx.reshape(bn * CTX_LEN, d))
    cond8 = jnp.concatenate([c, c_ctx[None], jnp.zeros((SUBLANE - bn - 1, d), F32)], axis=0)
    mod = _ada(cond8, ada_w, ada_b)
    mod_rows = mod.reshape(depth * SUBLANE * 6, 1, d)
    d_ff_p = _round_up(D_FF, 512)
    ff_pad = ((0, 0), (0, 0), (0, d_ff_p - D_FF))
    conv_w = jnp.pad(ffn_conv_w, ff_pad)
    conv_b = jnp.pad(ffn_conv_b.reshape(depth, 1, D_FF), ff_pad)
    ffn = (ffn_w_gate, ffn_w_up, d_ff_p)
    w_in_b = None
    for l in range(depth):
        last = l == depth - 1
        if isinstance(h, tuple) and l % 2 == 1:
            h = jnp.concatenate(h, axis=0)
        n_rows = h[0].shape[0] + h[1].shape[0] if isinstance(h, tuple) else h.shape[0]
        i = l // 2
        if l % 2 == 0:
            h, (w_gate, w_up) = _even_layer(
                h, n_rows, not last, mod, mod_rows, l, norm_mix[l],
                (a_w_in, a_q_a_norm[i], a_kv_a_norm[i], a_w_qb[i], a_w_kvb[i], a_q_norm[i], a_k_norm[i],
                 a_sc_conv[i], a_w_out), i, ffn)
        else:
            h, (w_gate, w_up) = _odd_layer(
                h, n_rows, not last, mod, mod_rows, l, norm_mix[l],
                (c_w_in, c_conv_w[i], c_a_log[i], c_dt_bias[i], c_o_norm[i], c_w_out), i, ffn, w_in_b)
        next_w = (c_w_in, (l + 1) // 2) if (l + 1 < depth and (l + 1) % 2 == 1) else None
        h, w_in_b = _conv_ffn(h, h.shape[0], norm_ffn[l], mod_rows, mod, l,
                              w_gate, w_up, conv_w, conv_b, ffn_w_down, next_w)
    return h[:bn * t].reshape(bn, t, d)
```

```python
import functools
import math
from typing import NamedTuple

import jax
import jax.numpy as jnp
import numpy as np
from jax import lax
from jax.experimental import pallas as pl
from jax.experimental.pallas import tpu as pltpu

F32 = jnp.float32
BF16 = jnp.bfloat16

BATCH = 2
SEQ = 4096
GRID_W = 64
CTX_LEN = 256
EPS = 1e-6
D_FF = 11008

MLA_HEADS = 16
MLA_NOPE = 128
MLA_ROPE = 64
MLA_VDIM = 128
MLA_Q_RANK = 1024
MLA_KV_RANK = 512
ROPE_THETA = 10000.0

GDN_QK_HEADS = 16
GDN_V_HEADS = 32
GDN_DK = 128
GDN_DV = 128

LANE = 128
SUBLANE = 8
QK_PAD = 256
V_PAD = 256
ROW_BLK = 256
GDN_BLK = 256
VMEM_LIMIT = 56 * 2**20
ROW_TILES = (1088, 1024, 512, 256)
COL_TILES = (1024, 512, 256, 128)
NARROW_COL_TILES = (512, 256, 128)
K_TILES = (2816, 2048, 1024, 512, 256)


def _cp(*sem, vmem=VMEM_LIMIT):
    return pltpu.CompilerParams(dimension_semantics=sem, vmem_limit_bytes=vmem)


def _pick(n, prefs):
    for p in prefs:
        if n % p == 0:
            return p
    raise ValueError(f"no tile for {n} in {prefs}")


def _round_up(n, m):
    return (n + m - 1) // m * m


def _sigmoid(x):
    return 1.0 / (1.0 + jnp.exp(-x))


def _silu(x):
    return x * _sigmoid(x)


def _dot(a, b):
    return jnp.dot(a, b, preferred_element_type=F32)


def _dot_nt(a, b):
    return lax.dot_general(a, b, (((1,), (1,)), ((), ())), preferred_element_type=F32)


def _dot_tn(a, b):
    return lax.dot_general(a, b, (((0,), (0,)), ((), ())), preferred_element_type=F32)


def _n_lat():
    return BATCH * SEQ


def _seq_edges(row0, tm):
    assert SEQ & (SEQ - 1) == 0 and CTX_LEN & (CTX_LEN - 1) == 0 and SEQ % CTX_LEN == 0
    r = row0 + lax.broadcasted_iota(jnp.int32, (tm, 1), 0)
    is_ctx = r >= _n_lat()
    first = (jnp.bitwise_and(r, CTX_LEN - 1) == 0) & ((jnp.bitwise_and(r, SEQ - 1) == 0) | is_ctx)
    r1 = r + 1
    last = (jnp.bitwise_and(r1, CTX_LEN - 1) == 0) & ((jnp.bitwise_and(r1, SEQ - 1) == 0) | is_ctx)
    return first, last


def _shift_rows(x, prev_row, next_row, row0):
    tm = x.shape[0]
    ridx = lax.broadcasted_iota(jnp.int32, (tm, 1), 0)
    first, last = _seq_edges(row0, tm)
    dn = jnp.where(ridx == 0, prev_row, pltpu.roll(x, 1, 0))
    dn = jnp.where(first, 0.0, dn)
    up = jnp.where(ridx == tm - 1, next_row, pltpu.roll(x, tm - 1, 0))
    up = jnp.where(last, 0.0, up)
    return dn, up


def _row_select(row0, tm, table):
    r = row0 + lax.broadcasted_iota(jnp.int32, (tm, 1), 0)
    out = table[BATCH:BATCH + 1]
    for b in reversed(range(BATCH)):
        out = jnp.where(r < (b + 1) * SEQ, table[b:b + 1], out)
    return out


def _halo_specs(tm, tn, n_rows, col_blk0):
    tmb, last = tm // SUBLANE, n_rows // SUBLANE - 1
    prev = pl.BlockSpec((SUBLANE, tn), lambda i, j: (jnp.maximum(i * tmb - 1, 0), col_blk0 + j))
    nxt = pl.BlockSpec((SUBLANE, tn), lambda i, j: (jnp.minimum((i + 1) * tmb, last), col_blk0 + j))
    return prev, nxt


def _ada_kernel(cond_ref, w_ref, b_ref, o_ref):
    a = _silu(cond_ref[...]).astype(BF16)
    o_ref[...] = _dot(a, w_ref[...].astype(BF16)) + b_ref[...]


def _ada(cond8, ada_w, ada_b):
    n_layer, d, n = ada_w.shape
    tn = _pick(n, NARROW_COL_TILES)
    return pl.pallas_call(
        _ada_kernel,
        grid=(n_layer, n // tn),
        in_specs=[pl.BlockSpec((SUBLANE, d), lambda l, j: (0, 0)),
                  pl.BlockSpec((None, d, tn), lambda l, j: (l, 0, j)),
                  pl.BlockSpec((None, 1, tn), lambda l, j: (l, 0, j))],
        out_specs=pl.BlockSpec((None, SUBLANE, tn), lambda l, j: (l, 0, j)),
        out_shape=jax.ShapeDtypeStruct((n_layer, SUBLANE, n), F32),
        compiler_params=_cp("parallel", "parallel"),
        name="ada",
    )(cond8, ada_w, ada_b.reshape(n_layer, 1, n))


def _modulate_kernel(h_ref, g_ref, sh_ref, sc_ref, o_ref):
    x = h_ref[...]
    y = x * lax.rsqrt(jnp.mean(x * x, axis=-1, keepdims=True) + EPS)
    o_ref[...] = ((y * g_ref[...]) * (1.0 + sc_ref[...]) + sh_ref[...]).astype(o_ref.dtype)


def _modulate_join_kernel(lat_ref, ctx_ref, g_ref, sh_ref, sc_ref, o_ref, h_ref, *, lat_blocks):
    def run(src_ref):
        h_ref[...] = src_ref[...]
        _modulate_kernel(src_ref, g_ref, sh_ref, sc_ref, o_ref)

    pl.when(pl.program_id(0) < lat_blocks)(lambda: run(lat_ref))
    pl.when(pl.program_id(0) >= lat_blocks)(lambda: run(ctx_ref))


def _modulate(h, n_rows, gain, mod_rows, layer, k_shift, k_scale):
    joined = isinstance(h, tuple)
    d = h[0].shape[1] if joined else h.shape[1]
    blk_per_seq = SEQ // ROW_BLK

    def mod_spec(k):
        return pl.BlockSpec(
            (None, 1, d),
            lambda i: ((layer * SUBLANE + jnp.minimum(i // blk_per_seq, BATCH)) * 6 + k, 0, 0))

    if joined:
        lat_blocks, ctx_blocks = h[0].shape[0] // ROW_BLK, h[1].shape[0] // ROW_BLK
        row_spec = pl.BlockSpec((ROW_BLK, d), lambda i: (i, 0))
        return pl.pallas_call(
            functools.partial(_modulate_join_kernel, lat_blocks=lat_blocks),
            grid=(lat_blocks + ctx_blocks,),
            in_specs=[pl.BlockSpec((ROW_BLK, d), lambda i: (jnp.minimum(i, lat_blocks - 1), 0)),
                      pl.BlockSpec((ROW_BLK, d), lambda i: (jnp.maximum(i - lat_blocks, 0), 0)),
                      pl.BlockSpec((1, d), lambda i: (0, 0)),
                      mod_spec(k_shift), mod_spec(k_scale)],
            out_specs=[row_spec, row_spec],
            out_shape=[jax.ShapeDtypeStruct((n_rows, d), BF16), jax.ShapeDtypeStruct((n_rows, d), F32)],
            compiler_params=_cp("parallel"),
            name="modulate_join",
        )(h[0], h[1], gain.reshape(1, d), mod_rows, mod_rows)
    return pl.pallas_call(
        _modulate_kernel,
        grid=(n_rows // ROW_BLK,),
        in_specs=[pl.BlockSpec((ROW_BLK, d), lambda i: (i, 0)),
                  pl.BlockSpec((1, d), lambda i: (0, 0)),
                  mod_spec(k_shift), mod_spec(k_scale)],
        out_specs=pl.BlockSpec((ROW_BLK, d), lambda i: (i, 0)),
        out_shape=jax.ShapeDtypeStruct((n_rows, d), BF16),
        compiler_params=_cp("parallel"),
        name="modulate",
    )(h, gain.reshape(1, d), mod_rows, mod_rows)


class _CastJob(NamedTuple):
    src: jax.Array
    layer: int
    rows_out: int
    cols_out: int
    start: int
    steps: int

    @property
    def tr(self):
        for tr in (16, 32, 64, 128, 256, 512, 1024):
            if self.rows_out % tr == 0 and self.src.shape[1] % tr == 0 and self.rows_out // tr <= self.steps:
                return tr
        raise ValueError("host kernel has too few steps for this cast")


def _hosted_call(kernel_fn, jobs, *, grid, in_specs, out_specs, out_shape, args, scratch_shapes=(), sem, name):
    out_specs, out_shape = list(out_specs), list(out_shape)
    n_in, n_out, n_jobs = len(in_specs), len(out_specs), len(jobs)
    strides = [math.prod(grid[a + 1:]) for a in range(len(grid))]

    def step_of(ids):
        return sum(i * s for i, s in zip(ids, strides))

    cast_in, cast_out, cast_shape = [], [], []
    for job in jobs:
        tr, n_blk, n_src_blk = job.tr, job.rows_out // job.tr, job.src.shape[1] // job.tr

        def blk(*ids, job=job, n_blk=n_blk):
            return jnp.clip(step_of(ids) - job.start, 0, n_blk - 1)

        cast_in.append(pl.BlockSpec((None, tr, job.src.shape[2]),
                                    lambda *ids, job=job, blk=blk, last=n_src_blk - 1:
                                    (job.layer, jnp.minimum(blk(*ids), last), 0)))
        cast_out.append(pl.BlockSpec((tr, job.cols_out), lambda *ids, blk=blk: (blk(*ids), 0)))
        cast_shape.append(jax.ShapeDtypeStruct((job.rows_out, job.cols_out), BF16))

    def kernel(*refs):
        host_in, src_refs = refs[:n_in], refs[n_in:n_in + n_jobs]
        host_out = refs[n_in + n_jobs:n_in + n_jobs + n_out]
        dst_refs = refs[n_in + n_jobs + n_out:n_in + 2 * n_jobs + n_out]
        kernel_fn(*host_in, *host_out, *refs[n_in + 2 * n_jobs + n_out:])
        step = step_of([pl.program_id(a) for a in range(len(grid))])
        for job, src_ref, dst_ref in zip(jobs, src_refs, dst_refs):
            rel, n_src = step - job.start, src_ref.shape[1]
            n_src_blk = job.src.shape[1] // job.tr

            @pl.when((rel >= 0) & (rel < n_src_blk))
            def _(src_ref=src_ref, dst_ref=dst_ref, n_src=n_src):
                dst_ref[:, :n_src] = src_ref[...].astype(dst_ref.dtype)
                if dst_ref.shape[1] > n_src:
                    dst_ref[:, n_src:] = jnp.zeros((dst_ref.shape[0], dst_ref.shape[1] - n_src), dst_ref.dtype)

            @pl.when((rel >= n_src_blk) & (rel < job.rows_out // job.tr))
            def _(dst_ref=dst_ref):
                dst_ref[...] = jnp.zeros_like(dst_ref)

    outs = pl.pallas_call(
        kernel,
        grid=grid,
        in_specs=list(in_specs) + cast_in,
        out_specs=out_specs + cast_out,
        out_shape=out_shape + cast_shape,
        scratch_shapes=list(scratch_shapes),
        compiler_params=_cp(*(["arbitrary"] * len(grid) if jobs else sem)),
        name=name,
    )(*args, *[job.src for job in jobs])
    return list(outs[:n_out]), list(outs[n_out:])


def _split_rows_kernel(w_ref, o_ref, *, gap0, gap1, end):
    i = pl.program_id(0)
    zero = ((i >= gap0) & (i < gap1)) | (i >= end)

    @pl.when(zero)
    def _():
        o_ref[...] = jnp.zeros_like(o_ref)

    @pl.when(jnp.logical_not(zero))
    def _():
        o_ref[...] = w_ref[...].astype(o_ref.dtype)


def _split_rows(w_stack, idx, split, second, n_out):
    _, n_in, k = w_stack.shape
    tr = math.gcd(split, second, n_out, n_in)
    assert tr % (2 * SUBLANE) == 0
    gap0, gap1, end, last = split // tr, second // tr, (second + n_in - split) // tr, n_in // tr - 1
    return pl.pallas_call(
        functools.partial(_split_rows_kernel, gap0=gap0, gap1=gap1, end=end),
        grid=(n_out // tr,),
        in_specs=[pl.BlockSpec((None, tr, k),
                               lambda i: (idx, jnp.where(i < gap0, i, jnp.clip(i - (gap1 - gap0), 0, last)), 0))],
        out_specs=pl.BlockSpec((tr, k), lambda i: (i, 0)),
        out_shape=jax.ShapeDtypeStruct((n_out, k), BF16),
        compiler_params=_cp("parallel"),
        name="split_rows",
    )(w_stack)


def _mm_kernel(*refs, n_pairs, gated, tm, w_t):
    a_refs, w_refs = refs[:n_pairs], refs[n_pairs:2 * n_pairs]
    dot = _dot_nt if w_t else _dot
    acc = dot(a_refs[0][...], w_refs[0][...])
    for a_ref, w_ref in zip(a_refs[1:], w_refs[1:]):
        acc += dot(a_ref[...], w_ref[...])
    if gated:
        res_ref, gate_ref, o_ref = refs[2 * n_pairs:]
        gate = _row_select(pl.program_id(0) * tm, tm, gate_ref[...])
        acc = res_ref[...] + gate * acc
    else:
        o_ref = refs[2 * n_pairs]
    o_ref[...] = acc.astype(o_ref.dtype)


def _mm_steps(n_rows, n, tn=None):
    return (n_rows // _pick(n_rows, ROW_TILES)) * (n // (tn or _pick(n, COL_TILES)))


def _mm(a_list, w, n_rows, *, n_cols=None, res=None, gate=None, tn=None, jobs=(), w_t=False, name="mm"):
    n_pairs = len(a_list)
    kdim = a_list[0].shape[1]
    n = n_cols or w.shape[0 if w_t else 1]
    tm = _pick(n_rows, ROW_TILES)
    tn = tn or _pick(n, COL_TILES)
    in_specs = [pl.BlockSpec((tm, kdim), lambda i, j: (i, 0)) for a in a_list]
    if w_t:
        assert n_pairs == 1 and w.shape[1] == kdim
        in_specs += [pl.BlockSpec((tn, kdim), lambda i, j: (j, 0))]
    else:
        assert all(a.shape[1] == kdim for a in a_list) and w.shape[0] == n_pairs * kdim
        in_specs += [pl.BlockSpec((kdim, tn), lambda i, j, p=p: (p, j)) for p in range(n_pairs)]
    args = list(a_list) + [w] * n_pairs
    if res is not None:
        table, layer, chunk = gate
        d = table.shape[2] // 6
        in_specs += [pl.BlockSpec((tm, tn), lambda i, j: (i, j)),
                     pl.BlockSpec((None, SUBLANE, tn), lambda i, j: (layer, 0, chunk * (d // tn) + j))]
        args += [res, table]
    (out,), casts = _hosted_call(
        functools.partial(_mm_kernel, n_pairs=n_pairs, gated=res is not None, tm=tm, w_t=w_t), jobs,
        grid=(n_rows // tm, n // tn),
        in_specs=in_specs,
        out_specs=[pl.BlockSpec((tm, tn), lambda i, j: (i, j))],
        out_shape=[jax.ShapeDtypeStruct((n_rows, n), F32)],
        args=args, sem=("parallel", "parallel"), name=name)
    return (out, casts) if jobs else out


def _mmk_kernel(a_ref, w_ref, res_ref, gate_ref, o_ref, acc_ref, *, tm):
    k = pl.program_id(2)

    @pl.when(k == 0)
    def _():
        acc_ref[...] = jnp.zeros_like(acc_ref)

    acc_ref[...] += _dot(a_ref[...], w_ref[...])

    @pl.when(k == pl.num_programs(2) - 1)
    def _():
        gate = _row_select(pl.program_id(0) * tm, tm, gate_ref[...])
        o_ref[...] = res_ref[...] + gate * acc_ref[...]


def _mm_ktiled_steps(kdim, n, n_rows):
    return ((n_rows // _pick(n_rows, ROW_TILES)) * (n // _pick(n, COL_TILES))
            * (kdim // _pick(kdim, K_TILES)))


def _mm_ktiled(a, w, n_rows, res, gate, jobs=()):
    kdim, n = w.shape
    table, gate_layer, chunk = gate
    d = table.shape[2] // 6
    tm = _pick(n_rows, ROW_TILES)
    tn = _pick(n, COL_TILES)
    tk = _pick(kdim, K_TILES)
    (out,), casts = _hosted_call(
        functools.partial(_mmk_kernel, tm=tm), jobs,
        grid=(n_rows // tm, n // tn, kdim // tk),
        in_specs=[pl.BlockSpec((tm, tk), lambda i, j, k: (i, k)),
                  pl.BlockSpec((tk, tn), lambda i, j, k: (k, j)),
                  pl.BlockSpec((tm, tn), lambda i, j, k: (i, j)),
                  pl.BlockSpec((None, SUBLANE, tn), lambda i, j, k: (gate_layer, 0, chunk * (d // tn) + j))],
        out_specs=[pl.BlockSpec((tm, tn), lambda i, j, k: (i, j))],
        out_shape=[jax.ShapeDtypeStruct((n_rows, n), F32)],
        scratch_shapes=[pltpu.VMEM((tm, tn), F32)],
        args=(a, w, res, table), sem=("parallel", "parallel", "arbitrary"), name="ffn_down")
    return out, casts


def _ffn_up_kernel(u_ref, uh_ref, wg_ref, wu_ref, cw_ref, cb_ref, o_ref, halo_ref, *, tm):
    i = pl.program_id(1)

    @pl.when(i == 0)
    def _():
        halo_ref[...] = _dot(uh_ref[...], wg_ref[...])

    u = u_ref[...]
    g = _dot(u, wg_ref[...])
    dn, up = _shift_rows(g, halo_ref[pl.ds(2 * i, 1), :], halo_ref[pl.ds(2 * i + 1, 1), :], i * tm)
    a = dn * cw_ref[0:1, :] + g * cw_ref[1:2, :] + up * cw_ref[2:3, :] + cb_ref[...]
    o_ref[...] = (_silu(a) * _dot(u, wu_ref[...])).astype(o_ref.dtype)


def _ffn_up_steps(n_rows, n):
    return (n // _pick(n, NARROW_COL_TILES)) * (n_rows // _pick(n_rows, ROW_TILES))


def _ffn_up(u, n_rows, layer, wg, wu, cw_stack, cb_stack, jobs):
    d, n = wg.shape
    tm = _pick(n_rows, ROW_TILES)
    tn = _pick(n, NARROW_COL_TILES)
    gm = n_rows // tm
    n_halo = _round_up(2 * gm, 2 * SUBLANE)
    rows = []
    for i in range(gm):
        rows += [max(i * tm - 1, 0), min((i + 1) * tm, n_rows - 1)]
    rows += [0] * (n_halo - len(rows))
    u_halo = jnp.concatenate([u[r:r + 1] for r in rows], axis=0)
    col = lambda j, i: (layer, 0, j)
    (hid,), casts = _hosted_call(
        functools.partial(_ffn_up_kernel, tm=tm), jobs,
        grid=(n // tn, gm),
        in_specs=[pl.BlockSpec((tm, d), lambda j, i: (i, 0)),
                  pl.BlockSpec((n_halo, d), lambda j, i: (0, 0)),
                  pl.BlockSpec((d, tn), lambda j, i: (0, j)),
                  pl.BlockSpec((d, tn), lambda j, i: (0, j)),
                  pl.BlockSpec((None, 3, tn), col),
                  pl.BlockSpec((None, 1, tn), col)],
        out_specs=[pl.BlockSpec((tm, tn), lambda j, i: (i, j))],
        out_shape=[jax.ShapeDtypeStruct((n_rows, n), BF16)],
        scratch_shapes=[pltpu.VMEM((n_halo, tn), F32)],
        args=(u, u_halo, wg, wu, cw_stack, cb_stack), sem=("parallel", "arbitrary"), name="ffn_up")
    return hid, casts


def _conv_ffn(h, n_rows, gain, mod_rows, mod, layer, wg, wu, cw, cb, wd_stack, next_w):
    d_ff_p = wg.shape[1]
    u = _modulate(h, n_rows, gain, mod_rows, layer, 3, 4)
    job = _CastJob(wd_stack, layer, d_ff_p, wd_stack.shape[2], 0, _ffn_up_steps(n_rows, d_ff_p))
    hid, (wd,) = _ffn_up(u, n_rows, layer, wg, wu, cw, cb, [job])
    jobs = []
    if next_w is not None:
        stack, idx = next_w
        jobs = [_CastJob(stack, idx, stack.shape[1], stack.shape[2], 0, _mm_ktiled_steps(d_ff_p, wd.shape[1], n_rows))]
    h, casts = _mm_ktiled(hid, wd, n_rows, h, (mod, layer, 5), jobs)
    return h, (casts[0] if casts else None)


def _rope(x, rope_ref):
    c, sa, sb = rope_ref[:, 0:LANE], rope_ref[:, LANE:2 * LANE], rope_ref[:, 2 * LANE:3 * LANE]
    quarter = MLA_ROPE // 4
    return x * c + pltpu.roll(x, LANE - quarter, 1) * sa + pltpu.roll(x, quarter, 1) * sb


def _rms(x, width):
    return lax.rsqrt(jnp.sum(x * x, axis=-1, keepdims=True) * (1.0 / width) + EPS)


def _qproj_kernel(cq_ref, ag_ref, w_ref, hg_ref, rope_ref, o_ref, *, heads):
    x = cq_ref[...]
    xn = ((x * _rms(x, x.shape[1])) * ag_ref[...]).astype(BF16)
    y = _dot(xn, w_ref[...])
    for hh in range(heads):
        yh = y[:, hh * QK_PAD:(hh + 1) * QK_PAD]
        yn = (yh * _rms(yh, MLA_NOPE + MLA_ROPE)) * hg_ref[...]
        o_ref[:, hh * QK_PAD:hh * QK_PAD + LANE] = yn[:, :LANE].astype(o_ref.dtype)
        o_ref[:, hh * QK_PAD + LANE:(hh + 1) * QK_PAD] = _rope(yn[:, LANE:], rope_ref).astype(o_ref.dtype)


def _qproj(y_in, n_rows, ag, w, hg, rope):
    heads = _pick(MLA_HEADS, (4, 2, 1))
    tm = _proj_rows()
    tn = heads * QK_PAD
    return pl.pallas_call(
        functools.partial(_qproj_kernel, heads=heads),
        grid=(n_rows // tm, w.shape[1] // tn),
        in_specs=[pl.BlockSpec((tm, MLA_Q_RANK), lambda i, j: (i, 0)),
                  pl.BlockSpec((1, MLA_Q_RANK), lambda i, j: (0, 0)),
                  pl.BlockSpec((MLA_Q_RANK, tn), lambda i, j: (0, j)),
                  pl.BlockSpec((1, QK_PAD), lambda i, j: (0, 0)),
                  _rope_spec()],
        out_specs=pl.BlockSpec((tm, tn), lambda i, j: (i, j)),
        out_shape=jax.ShapeDtypeStruct((n_rows, w.shape[1]), BF16),
        compiler_params=_cp("parallel", "parallel"),
        name="q_proj",
    )(y_in, ag, w, hg, rope)


def _kvproj_kernel(ckv_ref, kr_ref, ag_ref, w_ref, gn_ref, gr_ref, rope_ref, k_ref, v_ref, *, heads):
    x = ckv_ref[...]
    xn = ((x * _rms(x, x.shape[1])) * ag_ref[...]).astype(BF16)
    y = _dot(xn, w_ref[...])
    kr = kr_ref[...]
    kr_ss = jnp.sum(kr * kr, axis=-1, keepdims=True)
    width = MLA_NOPE + MLA_VDIM
    ones = jnp.ones((x.shape[0], V_PAD - MLA_VDIM), v_ref.dtype)
    for hh in range(heads):
        kn = y[:, hh * width:hh * width + MLA_NOPE]
        r = lax.rsqrt((jnp.sum(kn * kn, axis=-1, keepdims=True) + kr_ss) * (1.0 / (MLA_NOPE + MLA_ROPE)) + EPS)
        k_ref[:, hh * QK_PAD:hh * QK_PAD + LANE] = ((kn * r) * gn_ref[...]).astype(k_ref.dtype)
        k_ref[:, hh * QK_PAD + LANE:(hh + 1) * QK_PAD] = _rope((kr * r) * gr_ref[...], rope_ref).astype(k_ref.dtype)
        v_ref[:, hh * V_PAD:hh * V_PAD + MLA_VDIM] = y[:, hh * width + MLA_NOPE:(hh + 1) * width].astype(v_ref.dtype)
        v_ref[:, hh * V_PAD + MLA_VDIM:(hh + 1) * V_PAD] = ones


def _kvproj(y_in, n_rows, ag, w, gn, gr, rope):
    heads = _pick(MLA_HEADS, (4, 2, 1))
    assert MLA_NOPE == LANE and MLA_VDIM == LANE and MLA_Q_RANK % MLA_KV_RANK == 0
    tm = _proj_rows()
    tn = heads * (MLA_NOPE + MLA_VDIM)
    kr_blk = (MLA_Q_RANK + MLA_KV_RANK) // LANE
    return pl.pallas_call(
        functools.partial(_kvproj_kernel, heads=heads),
        grid=(n_rows // tm, w.shape[1] // tn),
        in_specs=[pl.BlockSpec((tm, MLA_KV_RANK), lambda i, j: (i, MLA_Q_RANK // MLA_KV_RANK)),
                  pl.BlockSpec((tm, LANE), lambda i, j: (i, kr_blk)),
                  pl.BlockSpec((1, MLA_KV_RANK), lambda i, j: (0, 0)),
                  pl.BlockSpec((MLA_KV_RANK, tn), lambda i, j: (0, j)),
                  pl.BlockSpec((1, LANE), lambda i, j: (0, 0)),
                  pl.BlockSpec((1, LANE), lambda i, j: (0, 0)),
                  _rope_spec()],
        out_specs=[pl.BlockSpec((tm, heads * QK_PAD), lambda i, j: (i, j)),
                   pl.BlockSpec((tm, heads * V_PAD), lambda i, j: (i, j))],
        out_shape=[jax.ShapeDtypeStruct((n_rows, MLA_HEADS * QK_PAD), BF16),
                   jax.ShapeDtypeStruct((n_rows, MLA_HEADS * V_PAD), BF16)],
        compiler_params=_cp("parallel", "parallel"),
        name="kv_proj",
    )(y_in, y_in, ag, w, gn, gr, rope)


def _attn_kernel(*refs, chunks, scale):
    q = refs[0][...]
    n_kv = (len(refs) - 2) // 2
    k_refs, v_refs, o_ref = refs[1:1 + n_kv], refs[1 + n_kv:1 + 2 * n_kv], refs[1 + 2 * n_kv]
    c = scale * math.log2(math.e)
    m = acc = None
    for idx, start, size in chunks:
        s = _dot_nt(q, k_refs[idx][start:start + size, :])
        m_blk = jnp.max(s, axis=-1, keepdims=True)
        m_new = m_blk if m is None else jnp.maximum(m, m_blk)
        pv = _dot(jnp.exp2((s - m_new) * c).astype(BF16), v_refs[idx][start:start + size, :])
        acc = pv if acc is None else jnp.exp2((m - m_new) * c) * acc + pv
        m = m_new
    o_ref[...] = (acc[:, :MLA_VDIM] / acc[:, MLA_VDIM:2 * MLA_VDIM]).astype(o_ref.dtype)


def _attn_lat_steps():
    return BATCH * MLA_HEADS * (SEQ // _pick(SEQ, (1024, 512, 256)))


def _attention(q, k, v, latent, n_out, jobs=(), o_prev=None):
    assert V_PAD == 2 * MLA_VDIM
    scale = (MLA_NOPE + MLA_ROPE) ** -0.5
    ctx_blk0 = _n_lat() // CTX_LEN
    ctx_k = pl.BlockSpec((CTX_LEN, QK_PAD), lambda b, h, i: (ctx_blk0 + b, h))
    ctx_v = pl.BlockSpec((CTX_LEN, V_PAD), lambda b, h, i: (ctx_blk0 + b, h))
    if latent:
        tq = _pick(SEQ, (1024, 512, 256))
        tk = _pick(SEQ, (1024, 512, 256))
        q_per_b = SEQ // tq
        q_spec = pl.BlockSpec((tq, QK_PAD), lambda b, h, i: (b * q_per_b + i, h))
        o_spec = pl.BlockSpec((tq, MLA_VDIM), lambda b, h, i: (b * q_per_b + i, h))
        k_specs = [ctx_k, pl.BlockSpec((SEQ, QK_PAD), lambda b, h, i: (b, h))]
        v_specs = [ctx_v, pl.BlockSpec((SEQ, V_PAD), lambda b, h, i: (b, h))]
        chunks = [(0, 0, CTX_LEN)] + [(1, s, tk) for s in range(0, SEQ, tk)]
        grid = (BATCH, MLA_HEADS, q_per_b)
    else:
        q_spec = pl.BlockSpec((CTX_LEN, QK_PAD), lambda b, h, i: (ctx_blk0 + b, h))
        o_spec = pl.BlockSpec((CTX_LEN, MLA_VDIM), lambda b, h, i: (ctx_blk0 + b, h))
        k_specs, v_specs, chunks = [ctx_k], [ctx_v], [(0, 0, CTX_LEN)]
        grid = (BATCH, MLA_HEADS, 1)
    n_kv = len(k_specs)
    in_specs = [q_spec] + k_specs + v_specs
    args = (q, *([k] * n_kv), *([v] * n_kv))
    out_shape = jax.ShapeDtypeStruct((n_out, MLA_HEADS * MLA_VDIM), BF16)
    body = functools.partial(_attn_kernel, chunks=tuple(chunks), scale=scale)
    if latent:
        (out,), casts = _hosted_call(body, jobs, grid=grid, in_specs=in_specs, out_specs=[o_spec],
                                     out_shape=[out_shape], args=args,
                                     sem=("parallel", "parallel", "arbitrary"), name="attn_lat")
        return out, casts
    return pl.pallas_call(
        lambda *refs: body(*refs[:len(in_specs)], refs[-1]),
        grid=grid,
        in_specs=in_specs + [pl.BlockSpec(memory_space=pl.ANY)],
        out_specs=o_spec,
        out_shape=out_shape,
        input_output_aliases={len(in_specs): 0},
        compiler_params=_cp("parallel", "parallel", "arbitrary"),
        name="attn_ctx",
    )(*args, o_prev)


def _sconv_kernel(b_ref, c_ref, x_ref, cp_ref, xp_ref, cn_ref, xn_ref, w_ref, o_ref, *, tm):
    p = c_ref[...] * x_ref[...]
    prev = cp_ref[SUBLANE - 1:SUBLANE, :] * xp_ref[SUBLANE - 1:SUBLANE, :]
    nxt = cn_ref[0:1, :] * xn_ref[0:1, :]
    dn, up = _shift_rows(p, prev, nxt, pl.program_id(0) * tm)
    conv = dn * w_ref[0:1, :] + p * w_ref[1:2, :] + up * w_ref[2:3, :]
    o_ref[...] = (b_ref[...] * conv).astype(o_ref.dtype)


def _sconv(y_in, n_rows, w, col0, width):
    tm = ROW_BLK
    tn = _pick(width, COL_TILES)
    blk0 = [(col0 + k * width) // tn for k in range(3)]
    main = [pl.BlockSpec((tm, tn), lambda i, j, o=o: (i, o + j)) for o in blk0]
    cp, cn = _halo_specs(tm, tn, n_rows, blk0[1])
    xp, xn = _halo_specs(tm, tn, n_rows, blk0[2])
    return pl.pallas_call(
        functools.partial(_sconv_kernel, tm=tm),
        grid=(n_rows // tm, width // tn),
        in_specs=main + [cp, xp, cn, xn, pl.BlockSpec((3, tn), lambda i, j: (0, j))],
        out_specs=pl.BlockSpec((tm, tn), lambda i, j: (i, j)),
        out_shape=jax.ShapeDtypeStruct((n_rows, width), BF16),
        compiler_params=_cp("parallel", "parallel"),
        name="short_conv",
    )(*([y_in] * 7), w)


def _conv_silu(x_ref, xp_ref, xn_ref, w_ref, row0):
    x = x_ref[...]
    dn, up = _shift_rows(x, xp_ref[SUBLANE - 1:SUBLANE, :], xn_ref[0:1, :], row0)
    return _silu(dn * w_ref[0:1, :] + x * w_ref[1:2, :] + up * w_ref[2:3, :])


def _l2norm(x):
    return x * lax.rsqrt(jnp.sum(x * x, axis=-1, keepdims=True) + EPS)


def _gdn_gate_kernel(ba_ref, alog_ref, dtb_ref, o_ref):
    x = ba_ref[...]
    lane = lax.broadcasted_iota(jnp.int32, x.shape, 1)
    z = x + dtb_ref[...]
    softplus = jnp.maximum(z, 0.0) + jnp.log(1.0 + jnp.exp(-jnp.abs(z)))
    o_ref[...] = jnp.where(lane < x.shape[1] // 2, _sigmoid(x), -jnp.exp(alog_ref[...]) * softplus)


def _gdn_gate(ba, n_rows, alog, dtb):
    w = ba.shape[1]
    return pl.pallas_call(
        _gdn_gate_kernel,
        grid=(n_rows // ROW_BLK,),
        in_specs=[pl.BlockSpec((ROW_BLK, w), lambda i: (i, 0)),
                  pl.BlockSpec((1, w), lambda i: (0, 0)),
                  pl.BlockSpec((1, w), lambda i: (0, 0))],
        out_specs=pl.BlockSpec((ROW_BLK, w), lambda i: (i, 0)),
        out_shape=jax.ShapeDtypeStruct((n_rows, w), F32),
        compiler_params=_cp("parallel"),
        name="gdn_gate",
    )(ba, alog, dtb)


def _active_rows(x, size, odd):
    n = x.shape[0]
    return jnp.concatenate([x[(2 * b + odd) * size:(2 * b + odd + 1) * size] for b in range(n // (2 * size))], axis=0)


def _weave_rows(rest, active, size, odd):
    pieces = []
    for b in range(active.shape[0] // size):
        if rest is None:
            keep = jnp.zeros((size, active.shape[1]), active.dtype)
        else:
            keep = rest[(2 * b + 1 - odd) * size:(2 * b + 2 - odd) * size]
        act = active[b * size:(b + 1) * size]
        pieces += [keep, act] if odd else [act, keep]
    return jnp.concatenate(pieces, axis=0)


def _unit_tri_inverses(l_mats, dirs, eye, base, ring_ref):
    size = SUBLANE
    xs = [-(l * base) for l in l_mats]
    ts = [eye + x for x in xs]
    for _ in range(size.bit_length() - 2):
        xs = [_dot(x.astype(BF16), x.astype(BF16)) for x in xs]
        ts = [t + _dot(t.astype(BF16), x.astype(BF16)) for t, x in zip(ts, xs)]
    level = 0
    while size < GDN_BLK:
        tbs = [t.astype(BF16) for t in ts]
        offs = [(_active_rows(l, size, 1 - d) * ring_ref[level, d]).astype(BF16) for l, d in zip(l_mats, dirs)]
        mids = [_weave_rows(None, _dot(off, tb), size, 1 - d).astype(BF16) for off, tb, d in zip(offs, tbs, dirs)]
        acts = [_active_rows(t, size, 1 - d) for t, d in zip(ts, dirs)]
        news = [act - _dot(act.astype(BF16), mid) for act, mid in zip(acts, mids)]
        ts = [_weave_rows(t, new, size, 1 - d) for t, new, d in zip(ts, news, dirs)]
        size *= 2
        level += 1
    return ts


def _gdn_masks():
    n, size = GDN_BLK, SUBLANE
    i, j = np.arange(n)[:, None], np.arange(n)[None, :]
    tri = np.stack([j <= i, j >= i, j == i, i // size == j // size]).astype(np.float32)
    rings = []
    while size < n:
        ra = np.arange(n // 2)[:, None]
        per_dir = []
        for d in range(2):
            ia = (ra // size) * 2 * size + (1 - d) * size + ra % size
            per_dir.append((ia // (2 * size) == j // (2 * size)) & (ia // size != j // size))
        rings.append(np.stack(per_dir))
        size *= 2
    return jnp.asarray(tri), jnp.asarray(np.stack(rings).astype(np.float32))


def _gdn_prep_kernel(q_ref, qp_ref, qn_ref, k_ref, kp_ref, kn_ref, v_ref, vp_ref, vn_ref, cwq_ref, cwk_ref, cwv_ref,
                     bg_ref, tri_ref, tri16_ref, ring_ref, u_ref, wq_ref, kd_ref, in_ref, cd_ref, *, rep, hpb):
    n = GDN_BLK
    row0 = pl.program_id(0) * n
    q_all = _conv_silu(q_ref, qp_ref, qn_ref, cwq_ref, row0)
    k_all = _conv_silu(k_ref, kp_ref, kn_ref, cwk_ref, row0)
    v_all = _conv_silu(v_ref, vp_ref, vn_ref, cwv_ref, row0)
    eye, base = tri_ref[2], tri_ref[3]
    not_eye = 1.0 - eye
    chains, l_mats, rhss = [], [], []
    for hh in range(hpb):
        q = _l2norm(q_all[:, hh * GDN_DK:(hh + 1) * GDN_DK]) * GDN_DK ** -0.5
        k = _l2norm(k_all[:, hh * GDN_DK:(hh + 1) * GDN_DK])
        q16, k16 = q.astype(BF16), k.astype(BF16)
        qk = _dot_nt(q16, k16)
        kk = _dot_nt(k16, k16) * not_eye
        bg = bg_ref[hh]
        bg_cols = jnp.concatenate([bg, jnp.zeros((LANE - bg.shape[0], n), F32)], axis=0).T
        hi = bg_cols.astype(BF16)
        rest = bg_cols - hi.astype(F32)
        mid = rest.astype(BF16)
        pieces = jnp.concatenate([hi, mid, (rest - mid.astype(F32)).astype(BF16)], axis=1)
        cums = [_dot(tri16_ref[d], pieces) for d in range(2)]
        gc_cols = [c[:, :LANE] + c[:, LANE:2 * LANE] + c[:, 2 * LANE:] for c in cums]
        gc_rows = [g.T for g in gc_cols]
        for e in range(rep):
            for d in range(2):
                vh = hh * rep + e
                cols = slice(vh * GDN_DV, (vh + 1) * GDN_DV)
                incl = tri_ref[d]
                b_idx, g_idx = d * rep + e, (2 + d) * rep + e
                b_col = bg_cols[:, b_idx:b_idx + 1]
                gc_col, gc_row = gc_cols[d][:, g_idx:g_idx + 1], gc_rows[d][g_idx:g_idx + 1, :]
                g_tot = jnp.sum(bg[g_idx:g_idx + 1, :], axis=1, keepdims=True)
                decay = jnp.exp(jnp.minimum(gc_col - gc_row, 0.0)) * incl
                e_col = jnp.exp(gc_col)
                chains.append((vh, d, cols))
                l_mats.append((kk * b_col) * decay)
                rhss.append(jnp.concatenate([v_all[:, cols] * b_col, (k * b_col) * e_col], axis=1).astype(BF16))
                wq_ref[d, n:, cols] = (q * e_col).astype(wq_ref.dtype)
                kd_ref[d, :, cols] = (k * jnp.exp(g_tot - gc_col)).astype(kd_ref.dtype)
                in_ref[d, vh] = (qk * decay).astype(in_ref.dtype)
                cd_ref[d, vh] = jnp.broadcast_to(jnp.exp(g_tot), (SUBLANE, LANE))
    invs = _unit_tri_inverses(l_mats, [d for _, d, _ in chains], eye, base, ring_ref)
    sols = [_dot(t.astype(BF16), rhs) for t, rhs in zip(invs, rhss)]
    for (vh, d, cols), sol in zip(chains, sols):
        u_ref[d, :, cols] = sol[:, :GDN_DV]
        wq_ref[d, :n, cols] = sol[:, GDN_DV:].astype(wq_ref.dtype)


def _gdn_prep(y_in, conv_w, bg_t, n_rows):
    assert GDN_DK == LANE and GDN_DV == LANE and GDN_BLK == 2 * LANE
    nb, hv, hq, rep = n_rows // GDN_BLK, GDN_V_HEADS, GDN_QK_HEADS, GDN_V_HEADS // GDN_QK_HEADS
    hpb = _pick(hq, (2, 1))
    key_w, wide = hpb * GDN_DK, hpb * rep * GDN_DV
    parts = ((key_w, 0), (key_w, hq // hpb), (wide, 2 * hq * GDN_DK // wide))
    in_specs = []
    for tn, blk0 in parts:
        in_specs += [pl.BlockSpec((GDN_BLK, tn), lambda i, h, blk0=blk0: (i, blk0 + h))]
        in_specs += _halo_specs(GDN_BLK, tn, n_rows, blk0)
    in_specs += [pl.BlockSpec((3, tn), lambda i, h, blk0=blk0: (0, blk0 + h)) for tn, blk0 in parts]
    tri, rings = _gdn_masks()
    in_specs += [pl.BlockSpec((hpb, bg_t.shape[1], GDN_BLK), lambda i, h: (h, 0, i)),
                 pl.BlockSpec(tri.shape, lambda i, h: (0, 0, 0)),
                 pl.BlockSpec((2,) + tri.shape[1:], lambda i, h: (0, 0, 0)),
                 pl.BlockSpec(rings.shape, lambda i, h: (0, 0, 0, 0))]
    return pl.pallas_call(
        functools.partial(_gdn_prep_kernel, rep=rep, hpb=hpb),
        grid=(nb, hq // hpb),
        in_specs=in_specs,
        out_specs=[pl.BlockSpec((2, GDN_BLK, wide), lambda i, h: (0, i, h)),
                   pl.BlockSpec((2, None, 2 * GDN_BLK, wide), lambda i, h: (0, i, 0, h)),
                   pl.BlockSpec((2, GDN_BLK, wide), lambda i, h: (0, i, h)),
                   pl.BlockSpec((2, hpb * rep, GDN_BLK, GDN_BLK), lambda i, h: (0, h, i, 0)),
                   pl.BlockSpec((2, hpb * rep, SUBLANE, LANE), lambda i, h: (0, h, i, 0))],
        out_shape=[jax.ShapeDtypeStruct((2, n_rows, hv * GDN_DV), F32),
                   jax.ShapeDtypeStruct((2, nb, 2 * GDN_BLK, hv * GDN_DV), BF16),
                   jax.ShapeDtypeStruct((2, n_rows, hv * GDN_DV), BF16),
                   jax.ShapeDtypeStruct((2, hv, n_rows, GDN_BLK), BF16),
                   jax.ShapeDtypeStruct((2, hv, nb * SUBLANE, LANE), F32)],
        compiler_params=_cp("parallel", "parallel"),
        name="gdn_prep",
    )(*([y_in] * 9), conv_w, conv_w, conv_w, bg_t, tri, tri[:2].astype(BF16), rings)


def _gdn_scan_kernel(*refs, heads):
    ins, (of_ref, ob_ref, s_ref) = refs[:10], refs[10:]
    n = GDN_BLK

    @pl.when(pl.program_id(2) == 0)
    def _():
        s_ref[...] = jnp.zeros_like(s_ref)

    o_refs = (of_ref, ob_ref)
    chains = [(d, hh, slice(hh * GDN_DV, (hh + 1) * GDN_DV)) for d in range(2) for hh in range(heads)]
    u_refs, wq_refs, kd_refs, in_refs, cd_refs = (ins[0::5], ins[1::5], ins[2::5], ins[3::5], ins[4::5])
    states = [s_ref[d, hh] for d, hh, _ in chains]
    wqs = [_dot(wq_refs[d][:, cols], s.astype(BF16)) for (d, hh, cols), s in zip(chains, states)]
    vbs = [(u_refs[d][:, cols] - wq[:n]).astype(BF16) for (d, hh, cols), wq in zip(chains, wqs)]
    for (d, hh, cols), s, wq, vb in zip(chains, states, wqs, vbs):
        o_refs[d][:, cols] = wq[n:] + _dot(in_refs[d][hh], vb)
        s_ref[d, hh] = s * cd_refs[d][hh, 0:1, :] + _dot_tn(kd_refs[d][:, cols], vb)


def _gdn_scan(prep, n_rows):
    hv = GDN_V_HEADS
    heads = _pick(hv, (8, 4, 2, 1))
    lat_blk, ctx_blk = SEQ // GDN_BLK, CTX_LEN // GDN_BLK
    steps = ctx_blk + lat_blk
    ctx0 = _n_lat() // GDN_BLK
    wide = heads * GDN_DV

    def blk_f(b, n):
        return jnp.where(n < ctx_blk, ctx0 + b * ctx_blk + n, b * lat_blk + n - ctx_blk)

    def blk_b(b, n):
        return jnp.where(n < ctx_blk, ctx0 + b * ctx_blk + ctx_blk - 1 - n, b * lat_blk + steps - 1 - n)

    in_specs = []
    for d, blk in enumerate((blk_f, blk_b)):
        rows = pl.BlockSpec((None, GDN_BLK, wide), lambda b, h, n, d=d, blk=blk: (d, blk(b, n), h))
        in_specs += [rows,
                     pl.BlockSpec((None, None, 2 * GDN_BLK, wide), lambda b, h, n, d=d, blk=blk: (d, blk(b, n), 0, h)),
                     rows,
                     pl.BlockSpec((None, heads, GDN_BLK, GDN_BLK), lambda b, h, n, d=d, blk=blk: (d, h, blk(b, n), 0)),
                     pl.BlockSpec((None, heads, SUBLANE, LANE), lambda b, h, n, d=d, blk=blk: (d, h, blk(b, n), 0))]
    out = jax.ShapeDtypeStruct((n_rows, hv * GDN_DV), F32)
    return pl.pallas_call(
        functools.partial(_gdn_scan_kernel, heads=heads),
        grid=(BATCH, hv // heads, steps),
        in_specs=in_specs,
        out_specs=[pl.BlockSpec((GDN_BLK, wide), lambda b, h, n: (blk_f(b, n), h)),
                   pl.BlockSpec((GDN_BLK, wide), lambda b, h, n: (blk_b(b, n), h))],
        out_shape=[out, out],
        scratch_shapes=[pltpu.VMEM((2, heads, GDN_DK, GDN_DV), F32)],
        compiler_params=_cp("parallel", "parallel", "arbitrary"),
        name="gdn_scan",
    )(*prep, *prep)


def _gdn_readout_kernel(of_ref, ob_ref, z_ref, g_ref, o_ref):
    o = of_ref[...] + ob_ref[...]
    z = z_ref[...]
    for hh in range(o.shape[1] // GDN_DV):
        cols = slice(hh * GDN_DV, (hh + 1) * GDN_DV)
        oh = o[:, cols]
        y = (oh * _rms(oh, GDN_DV)) * g_ref[...]
        o_ref[:, cols] = (y * _silu(z[:, cols])).astype(o_ref.dtype)


def _gdn_readout(o_f, o_b, y_in, z_col0, gain, n_rows):
    width = GDN_V_HEADS * GDN_DV
    tn = _pick(width, COL_TILES)
    spec = pl.BlockSpec((ROW_BLK, tn), lambda i, j: (i, j))
    return pl.pallas_call(
        _gdn_readout_kernel,
        grid=(n_rows // ROW_BLK, width // tn),
        in_specs=[spec, spec, pl.BlockSpec((ROW_BLK, tn), lambda i, j: (i, z_col0 // tn + j)),
                  pl.BlockSpec((1, GDN_DV), lambda i, j: (0, 0))],
        out_specs=spec,
        out_shape=jax.ShapeDtypeStruct((n_rows, width), BF16),
        compiler_params=_cp("parallel", "parallel"),
        name="gdn_readout",
    )(o_f, o_b, y_in, gain.reshape(1, GDN_DV))


def _proj_rows():
    return _pick(math.gcd(SEQ, BATCH * CTX_LEN), (512, 256))


def _rope_spec():
    tm = _proj_rows()
    return pl.BlockSpec((tm, 3 * LANE), lambda i, j: (jnp.where(i < _n_lat() // tm, i % (SEQ // tm), SEQ // tm), 0))


def _rope_table(n_ident):
    n_freq = MLA_ROPE // 4
    inv_freq = ROPE_THETA ** (-jnp.arange(n_freq, dtype=F32) / n_freq)
    t = jnp.arange(SEQ)
    ang = jnp.concatenate([(t // GRID_W).astype(F32)[:, None] * inv_freq,
                           (t % GRID_W).astype(F32)[:, None] * inv_freq], axis=1)
    cos, sin = jnp.cos(ang), jnp.sin(ang)
    zero = jnp.zeros_like(sin[:, :n_freq])
    pad = jnp.zeros((SEQ, LANE - MLA_ROPE), F32)
    c = jnp.concatenate([cos[:, :n_freq], cos[:, :n_freq], cos[:, n_freq:], cos[:, n_freq:], pad + 1.0], axis=1)
    sa = jnp.concatenate([-sin[:, :n_freq], zero, -sin[:, n_freq:], zero, pad], axis=1)
    sb = jnp.concatenate([zero, sin[:, :n_freq], zero, sin[:, n_freq:], pad], axis=1)
    ident = jnp.concatenate([jnp.ones((n_ident, LANE), F32), jnp.zeros((n_ident, 2 * LANE), F32)], axis=1)
    return jnp.concatenate([jnp.concatenate([c, sa, sb], axis=1), ident], axis=0)


def _pad_cols(w, n):
    return jnp.pad(w, ((0, 0), (0, n - w.shape[1])))


def _even_layer(h, n_rows, need_ctx, mod, mod_rows, layer, gain, p, idx, ffn):
    w_in_stack, q_a_g, kv_a_g, w_qb, w_kvb, q_g, k_g, sc_w, w_out_stack = p
    ffn_w_gate, ffn_w_up, d_ff_p = ffn
    d = w_out_stack.shape[2]
    sc_width = d - MLA_HEADS * MLA_VDIM
    assert sc_width == MLA_HEADS * MLA_VDIM
    head_w = MLA_NOPE + MLA_ROPE
    c0 = MLA_Q_RANK + MLA_KV_RANK
    conv0 = _round_up(c0 + LANE, _pick(sc_width, COL_TILES))
    n_in = _round_up(conv0 + 3 * sc_width, 1024)
    w_in_t = _split_rows(jnp.swapaxes(w_in_stack, 1, 2), idx, c0 + MLA_ROPE, conv0, n_in)
    w_q = w_qb.reshape(MLA_Q_RANK, MLA_HEADS, head_w)
    w_q = jnp.pad(w_q, ((0, 0), (0, 0), (0, QK_PAD - head_w))).reshape(MLA_Q_RANK, MLA_HEADS * QK_PAD).astype(BF16)
    rope = _rope_table(_proj_rows())

    if isinstance(h, tuple):
        u, h = _modulate(h, n_rows, gain, mod_rows, layer, 0, 1)
    else:
        u = _modulate(h, n_rows, gain, mod_rows, layer, 0, 1)
    y = _mm([u], w_in_t, n_rows, w_t=True, name="even_in")
    q = _qproj(y, n_rows, q_a_g.reshape(1, -1), w_q, jnp.pad(q_g, (0, QK_PAD - head_w)).reshape(1, QK_PAD), rope)
    k, v = _kvproj(y, n_rows, kv_a_g.reshape(1, -1), w_kvb.astype(BF16), k_g[:MLA_NOPE].reshape(1, LANE),
                   jnp.pad(k_g[MLA_NOPE:], (0, LANE - MLA_ROPE)).reshape(1, LANE), rope)
    steps = _attn_lat_steps()
    jobs = [_CastJob(ffn_w_gate, layer, d, d_ff_p, 0, steps // 2),
            _CastJob(ffn_w_up, layer, d, d_ff_p, steps // 2, steps - steps // 2),
            _CastJob(w_out_stack, idx, w_out_stack.shape[1], d, 0, steps)]
    n_out = n_rows if need_ctx else _n_lat()
    o, (wg, wu, w_out_b) = _attention(q, k, v, True, n_out, jobs)
    if need_ctx:
        o = _attention(q, k, v, False, n_out, o_prev=o)
    conv = _sconv(y, n_out, sc_w, conv0, sc_width)
    h = _mm([o, conv], w_out_b, n_out, res=h, gate=(mod, layer, 2),
            tn=_pick(d, NARROW_COL_TILES), name="even_out")
    return h, (wg, wu)


def _odd_layer(h, n_rows, need_ctx, mod, mod_rows, layer, gain, p, idx, ffn, w_in_b):
    w_in_stack, conv_w, a_log, dt_bias, o_g, w_out_stack = p
    ffn_w_gate, ffn_w_up, d_ff_p = ffn
    d = h.shape[1]
    hv, hq = GDN_V_HEADS, GDN_QK_HEADS
    qkv_w = 2 * hq * GDN_DK + hv * GDN_DV
    main_w = qkv_w + hv * GDN_DV
    gate_w = _round_up(4 * hv, LANE)
    half = gate_w // 2
    if w_in_b is None:
        w_in_b = w_in_stack[idx].astype(BF16)
    w_ba = jnp.concatenate([_pad_cols(w_in_b[:, main_w:main_w + 2 * hv], half),
                            _pad_cols(w_in_b[:, main_w + 2 * hv:], half)], axis=1)
    zeros = jnp.zeros((1, half), F32)
    alog = jnp.concatenate([zeros, _pad_cols(a_log.reshape(1, -1), half)], axis=1)
    dtb = jnp.concatenate([zeros, _pad_cols(dt_bias.reshape(1, -1), half)], axis=1)

    u = _modulate(h, n_rows, gain, mod_rows, layer, 0, 1)
    tn = _pick(main_w, NARROW_COL_TILES)
    steps = _mm_steps(n_rows, main_w, tn)
    y, (wg, w_out_b) = _mm([u], w_in_b, n_rows, n_cols=main_w, tn=tn,
                           jobs=[_CastJob(ffn_w_gate, layer, d, d_ff_p, 0, steps),
                                 _CastJob(w_out_stack, idx, w_out_stack.shape[1], d, 0, steps)], name="odd_in")
    ba = _mm([u], w_ba, n_rows, name="odd_in_gates")
    bg = _gdn_gate(ba, n_rows, alog, dtb)
    rep = hv // hq
    bg_t = jnp.stack([bg[:, :2 * hv], bg[:, half:half + 2 * hv]], axis=0)
    bg_t = bg_t.reshape(2, n_rows, 2, hq, rep).transpose(3, 0, 2, 4, 1).reshape(hq, 4 * rep, n_rows)
    bg_t = jnp.pad(bg_t, ((0, 0), (0, _round_up(4 * rep, SUBLANE) - 4 * rep), (0, 0)))
    o_f, o_b = _gdn_scan(_gdn_prep(y, conv_w, bg_t, n_rows), n_rows)
    n_out = n_rows if need_ctx else _n_lat()
    yo = _gdn_readout(o_f, o_b, y, qkv_w, o_g, n_out)
    tn = _pick(d, NARROW_COL_TILES)
    h, (wu,) = _mm([yo], w_out_b, n_out, res=h, gate=(mod, layer, 2), tn=tn,
                   jobs=[_CastJob(ffn_w_up, layer, d, d_ff_p, 0, _mm_steps(n_out, d, tn))], name="odd_out")
    return h, (wg, wu)


def kernel(x, c, ctx, c_ctx, ada_w, ada_b, norm_mix, norm_ffn, ffn_w_gate, ffn_w_up, ffn_conv_w, ffn_conv_b,
           ffn_w_down, a_w_in, a_q_a_norm, a_kv_a_norm, a_w_qb, a_w_kvb, a_q_norm, a_k_norm, a_sc_conv, a_w_out,
           c_w_in, c_conv_w, c_a_log, c_dt_bias, c_o_norm, c_w_out):
    bn, t, d = x.shape
    depth = ada_w.shape[0]
    assert (bn, t, ctx.shape[1], ffn_w_gate.shape[2]) == (BATCH, SEQ, CTX_LEN, D_FF)
    assert CTX_LEN % GDN_BLK == 0 and SEQ % GDN_BLK == 0
    h = (x.reshape(bn * t, d), ctx.reshape(bn * CTX_LEN, d))
    cond8 = jnp.concatenate([c, c_ctx[None], jnp.zeros((SUBLANE - bn - 1, d), F32)], axis=0)
    mod = _ada(cond8, ada_w, ada_b)
    mod_rows = mod.reshape(depth * SUBLANE * 6, 1, d)
    d_ff_p = _round_up(D_FF, 512)
    ff_pad = ((0, 0), (0, 0), (0, d_ff_p - D_FF))
    conv_w = jnp.pad(ffn_conv_w, ff_pad)
    conv_b = jnp.pad(ffn_conv_b.reshape(depth, 1, D_FF), ff_pad)
    ffn = (ffn_w_gate, ffn_w_up, d_ff_p)
    w_in_b = None
    for l in range(depth):
        last = l == depth - 1
        if isinstance(h, tuple) and l % 2 == 1:
            h = jnp.concatenate(h, axis=0)
        n_rows = h[0].shape[0] + h[1].shape[0] if isinstance(h, tuple) else h.shape[0]
        i = l // 2
        if l % 2 == 0:
            h, (w_gate, w_up) = _even_layer(
                h, n_rows, not last, mod, mod_rows, l, norm_mix[l],
                (a_w_in, a_q_a_norm[i], a_kv_a_norm[i], a_w_qb[i], a_w_kvb[i], a_q_norm[i], a_k_norm[i],
                 a_sc_conv[i], a_w_out), i, ffn)
        else:
            h, (w_gate, w_up) = _odd_layer(
                h, n_rows, not last, mod, mod_rows, l, norm_mix[l],
                (c_w_in, c_conv_w[i], c_a_log[i], c_dt_bias[i], c_o_norm[i], c_w_out), i, ffn, w_in_b)
        next_w = (c_w_in, (l + 1) // 2) if (l + 1 < depth and (l + 1) % 2 == 1) else None
        h, w_in_b = _conv_ffn(h, h.shape[0], norm_ffn[l], mod_rows, mod, l,
                              w_gate, w_up, conv_w, conv_b, ffn_w_down, next_w)
    return h[:bn * t].reshape(bn, t, d)
```

```python
import functools
import math
from typing import NamedTuple

import jax
import jax.numpy as jnp
import numpy as np
from jax import lax
from jax.experimental import pallas as pl
from jax.experimental.pallas import tpu as pltpu

F32 = jnp.float32
BF16 = jnp.bfloat16

BATCH = 2
SEQ = 4096
GRID_W = 64
CTX_LEN = 256
EPS = 1e-6
D_FF = 11008

MLA_HEADS = 16
MLA_NOPE = 128
MLA_ROPE = 64
MLA_VDIM = 128
MLA_Q_RANK = 1024
MLA_KV_RANK = 512
ROPE_THETA = 10000.0

GDN_QK_HEADS = 16
GDN_V_HEADS = 32
GDN_DK = 128
GDN_DV = 128

LANE = 128
SUBLANE = 8
QK_PAD = 256
V_PAD = 256
ROW_BLK = 256
GDN_BLK = 256
VMEM_LIMIT = 56 * 2**20
ROW_TILES = (1088, 1024, 512, 256)
COL_TILES = (1024, 512, 256, 128)
NARROW_COL_TILES = (512, 256, 128)
K_TILES = (2816, 2048, 1024, 512, 256)


def _cp(*sem, vmem=VMEM_LIMIT):
    return pltpu.CompilerParams(dimension_semantics=sem, vmem_limit_bytes=vmem)


def _pick(n, prefs):
    for p in prefs:
        if n % p == 0:
            return p
    raise ValueError(f"no tile for {n} in {prefs}")


def _round_up(n, m):
    return (n + m - 1) // m * m


def _sigmoid(x):
    return 1.0 / (1.0 + jnp.exp(-x))


def _silu(x):
    return x * _sigmoid(x)


def _dot(a, b):
    return jnp.dot(a, b, preferred_element_type=F32)


def _dot_nt(a, b):
    return lax.dot_general(a, b, (((1,), (1,)), ((), ())), preferred_element_type=F32)


def _dot_tn(a, b):
    return lax.dot_general(a, b, (((0,), (0,)), ((), ())), preferred_element_type=F32)


def _n_lat():
    return BATCH * SEQ


def _seq_edges(row0, tm):
    assert SEQ & (SEQ - 1) == 0 and CTX_LEN & (CTX_LEN - 1) == 0 and SEQ % CTX_LEN == 0
    r = row0 + lax.broadcasted_iota(jnp.int32, (tm, 1), 0)
    is_ctx = r >= _n_lat()
    first = (jnp.bitwise_and(r, CTX_LEN - 1) == 0) & ((jnp.bitwise_and(r, SEQ - 1) == 0) | is_ctx)
    r1 = r + 1
    last = (jnp.bitwise_and(r1, CTX_LEN - 1) == 0) & ((jnp.bitwise_and(r1, SEQ - 1) == 0) | is_ctx)
    return first, last


def _shift_rows(x, prev_row, next_row, row0):
    tm = x.shape[0]
    ridx = lax.broadcasted_iota(jnp.int32, (tm, 1), 0)
    first, last = _seq_edges(row0, tm)
    dn = jnp.where(ridx == 0, prev_row, pltpu.roll(x, 1, 0))
    dn = jnp.where(first, 0.0, dn)
    up = jnp.where(ridx == tm - 1, next_row, pltpu.roll(x, tm - 1, 0))
    up = jnp.where(last, 0.0, up)
    return dn, up


def _row_select(row0, tm, table):
    r = row0 + lax.broadcasted_iota(jnp.int32, (tm, 1), 0)
    out = table[BATCH:BATCH + 1]
    for b in reversed(range(BATCH)):
        out = jnp.where(r < (b + 1) * SEQ, table[b:b + 1], out)
    return out


def _halo_specs(tm, tn, n_rows, col_blk0):
    tmb, last = tm // SUBLANE, n_rows // SUBLANE - 1
    prev = pl.BlockSpec((SUBLANE, tn), lambda i, j: (jnp.maximum(i * tmb - 1, 0), col_blk0 + j))
    nxt = pl.BlockSpec((SUBLANE, tn), lambda i, j: (jnp.minimum((i + 1) * tmb, last), col_blk0 + j))
    return prev, nxt


def _ada_kernel(cond_ref, w_ref, b_ref, o_ref):
    a = _silu(cond_ref[...]).astype(BF16)
    o_ref[...] = _dot(a, w_ref[...].astype(BF16)) + b_ref[...]


def _ada(cond8, ada_w, ada_b):
    n_layer, d, n = ada_w.shape
    tn = _pick(n, NARROW_COL_TILES)
    return pl.pallas_call(
        _ada_kernel,
        grid=(n_layer, n // tn),
        in_specs=[pl.BlockSpec((SUBLANE, d), lambda l, j: (0, 0)),
                  pl.BlockSpec((None, d, tn), lambda l, j: (l, 0, j)),
                  pl.BlockSpec((None, 1, tn), lambda l, j: (l, 0, j))],
        out_specs=pl.BlockSpec((None, SUBLANE, tn), lambda l, j: (l, 0, j)),
        out_shape=jax.ShapeDtypeStruct((n_layer, SUBLANE, n), F32),
        compiler_params=_cp("parallel", "parallel"),
        name="ada",
    )(cond8, ada_w, ada_b.reshape(n_layer, 1, n))


def _modulate_kernel(h_ref, g_ref, sh_ref, sc_ref, o_ref):
    x = h_ref[...]
    y = x * lax.rsqrt(jnp.mean(x * x, axis=-1, keepdims=True) + EPS)
    o_ref[...] = ((y * g_ref[...]) * (1.0 + sc_ref[...]) + sh_ref[...]).astype(o_ref.dtype)


def _modulate_join_kernel(lat_ref, ctx_ref, g_ref, sh_ref, sc_ref, o_ref, h_ref, *, lat_blocks):
    def run(src_ref):
        h_ref[...] = src_ref[...]
        _modulate_kernel(src_ref, g_ref, sh_ref, sc_ref, o_ref)

    pl.when(pl.program_id(0) < lat_blocks)(lambda: run(lat_ref))
    pl.when(pl.program_id(0) >= lat_blocks)(lambda: run(ctx_ref))


def _modulate(h, n_rows, gain, mod_rows, layer, k_shift, k_scale):
    joined = isinstance(h, tuple)
    d = h[0].shape[1] if joined else h.shape[1]
    blk_per_seq = SEQ // ROW_BLK

    def mod_spec(k):
        return pl.BlockSpec(
            (None, 1, d),
            lambda i: ((layer * SUBLANE + jnp.minimum(i // blk_per_seq, BATCH)) * 6 + k, 0, 0))

    if joined:
        lat_blocks, ctx_blocks = h[0].shape[0] // ROW_BLK, h[1].shape[0] // ROW_BLK
        row_spec = pl.BlockSpec((ROW_BLK, d), lambda i: (i, 0))
        return pl.pallas_call(
            functools.partial(_modulate_join_kernel, lat_blocks=lat_blocks),
            grid=(lat_blocks + ctx_blocks,),
            in_specs=[pl.BlockSpec((ROW_BLK, d), lambda i: (jnp.minimum(i, lat_blocks - 1), 0)),
                      pl.BlockSpec((ROW_BLK, d), lambda i: (jnp.maximum(i - lat_blocks, 0), 0)),
                      pl.BlockSpec((1, d), lambda i: (0, 0)),
                      mod_spec(k_shift), mod_spec(k_scale)],
            out_specs=[row_spec, row_spec],
            out_shape=[jax.ShapeDtypeStruct((n_rows, d), BF16), jax.ShapeDtypeStruct((n_rows, d), F32)],
            compiler_params=_cp("parallel"),
            name="modulate_join",
        )(h[0], h[1], gain.reshape(1, d), mod_rows, mod_rows)
    return pl.pallas_call(
        _modulate_kernel,
        grid=(n_rows // ROW_BLK,),
        in_specs=[pl.BlockSpec((ROW_BLK, d), lambda i: (i, 0)),
                  pl.BlockSpec((1, d), lambda i: (0, 0)),
                  mod_spec(k_shift), mod_spec(k_scale)],
        out_specs=pl.BlockSpec((ROW_BLK, d), lambda i: (i, 0)),
        out_shape=jax.ShapeDtypeStruct((n_rows, d), BF16),
        compiler_params=_cp("parallel"),
        name="modulate",
    )(h, gain.reshape(1, d), mod_rows, mod_rows)


class _CastJob(NamedTuple):
    src: jax.Array
    layer: int
    rows_out: int
    cols_out: int
    start: int
    steps: int

    @property
    def tr(self):
        for tr in (16, 32, 64, 128, 256, 512, 1024):
            if self.rows_out % tr == 0 and self.src.shape[1] % tr == 0 and self.rows_out // tr <= self.steps:
                return tr
        raise ValueError("host kernel has too few steps for this cast")


def _hosted_call(kernel_fn, jobs, *, grid, in_specs, out_specs, out_shape, args, scratch_shapes=(), sem, name):
    out_specs, out_shape = list(out_specs), list(out_shape)
    n_in, n_out, n_jobs = len(in_specs), len(out_specs), len(jobs)
    strides = [math.prod(grid[a + 1:]) for a in range(len(grid))]

    def step_of(ids):
        return sum(i * s for i, s in zip(ids, strides))

    cast_in, cast_out, cast_shape = [], [], []
    for job in jobs:
        tr, n_blk, n_src_blk = job.tr, job.rows_out // job.tr, job.src.shape[1] // job.tr

        def blk(*ids, job=job, n_blk=n_blk):
            return jnp.clip(step_of(ids) - job.start, 0, n_blk - 1)

        cast_in.append(pl.BlockSpec((None, tr, job.src.shape[2]),
                                    lambda *ids, job=job, blk=blk, last=n_src_blk - 1:
                                    (job.layer, jnp.minimum(blk(*ids), last), 0)))
        cast_out.append(pl.BlockSpec((tr, job.cols_out), lambda *ids, blk=blk: (blk(*ids), 0)))
        cast_shape.append(jax.ShapeDtypeStruct((job.rows_out, job.cols_out), BF16))

    def kernel(*refs):
        host_in, src_refs = refs[:n_in], refs[n_in:n_in + n_jobs]
        host_out = refs[n_in + n_jobs:n_in + n_jobs + n_out]
        dst_refs = refs[n_in + n_jobs + n_out:n_in + 2 * n_jobs + n_out]
        kernel_fn(*host_in, *host_out, *refs[n_in + 2 * n_jobs + n_out:])
        step = step_of([pl.program_id(a) for a in range(len(grid))])
        for job, src_ref, dst_ref in zip(jobs, src_refs, dst_refs):
            rel, n_src = step - job.start, src_ref.shape[1]
            n_src_blk = job.src.shape[1] // job.tr

            @pl.when((rel >= 0) & (rel < n_src_blk))
            def _(src_ref=src_ref, dst_ref=dst_ref, n_src=n_src):
                dst_ref[:, :n_src] = src_ref[...].astype(dst_ref.dtype)
                if dst_ref.shape[1] > n_src:
                    dst_ref[:, n_src:] = jnp.zeros((dst_ref.shape[0], dst_ref.shape[1] - n_src), dst_ref.dtype)

            @pl.when((rel >= n_src_blk) & (rel < job.rows_out // job.tr))
            def _(dst_ref=dst_ref):
                dst_ref[...] = jnp.zeros_like(dst_ref)

    outs = pl.pallas_call(
        kernel,
        grid=grid,
        in_specs=list(in_specs) + cast_in,
        out_specs=out_specs + cast_out,
        out_shape=out_shape + cast_shape,
        scratch_shapes=list(scratch_shapes),
        compiler_params=_cp(*(["arbitrary"] * len(grid) if jobs else sem)),
        name=name,
    )(*args, *[job.src for job in jobs])
    return list(outs[:n_out]), list(outs[n_out:])


def _split_rows_kernel(w_ref, o_ref, *, gap0, gap1, end):
    i = pl.program_id(0)
    zero = ((i >= gap0) & (i < gap1)) | (i >= end)

    @pl.when(zero)
    def _():
        o_ref[...] = jnp.zeros_like(o_ref)

    @pl.when(jnp.logical_not(zero))
    def _():
        o_ref[...] = w_ref[...].astype(o_ref.dtype)


def _split_rows(w_stack, idx, split, second, n_out):
    _, n_in, k = w_stack.shape
    tr = math.gcd(split, second, n_out, n_in)
    assert tr % (2 * SUBLANE) == 0
    gap0, gap1, end, last = split // tr, second // tr, (second + n_in - split) // tr, n_in // tr - 1
    return pl.pallas_call(
        functools.partial(_split_rows_kernel, gap0=gap0, gap1=gap1, end=end),
        grid=(n_out // tr,),
        in_specs=[pl.BlockSpec((None, tr, k),
                               lambda i: (idx, jnp.where(i < gap0, i, jnp.clip(i - (gap1 - gap0), 0, last)), 0))],
        out_specs=pl.BlockSpec((tr, k), lambda i: (i, 0)),
        out_shape=jax.ShapeDtypeStruct((n_out, k), BF16),
        compiler_params=_cp("parallel"),
        name="split_rows",
    )(w_stack)


def _mm_kernel(*refs, n_pairs, gated, tm, w_t):
    a_refs, w_refs = refs[:n_pairs], refs[n_pairs:2 * n_pairs]
    dot = _dot_nt if w_t else _dot
    acc = dot(a_refs[0][...], w_refs[0][...])
    for a_ref, w_ref in zip(a_refs[1:], w_refs[1:]):
        acc += dot(a_ref[...], w_ref[...])
    if gated:
        res_ref, gate_ref, o_ref = refs[2 * n_pairs:]
        gate = _row_select(pl.program_id(0) * tm, tm, gate_ref[...])
        acc = res_ref[...] + gate * acc
    else:
        o_ref = refs[2 * n_pairs]
    o_ref[...] = acc.astype(o_ref.dtype)


def _mm_steps(n_rows, n, tn=None):
    return (n_rows // _pick(n_rows, ROW_TILES)) * (n // (tn or _pick(n, COL_TILES)))


def _mm(a_list, w, n_rows, *, n_cols=None, res=None, gate=None, tn=None, jobs=(), w_t=False, name="mm"):
    n_pairs = len(a_list)
    kdim = a_list[0].shape[1]
    n = n_cols or w.shape[0 if w_t else 1]
    tm = _pick(n_rows, ROW_TILES)
    tn = tn or _pick(n, COL_TILES)
    in_specs = [pl.BlockSpec((tm, kdim), lambda i, j: (i, 0)) for a in a_list]
    if w_t:
        assert n_pairs == 1 and w.shape[1] == kdim
        in_specs += [pl.BlockSpec((tn, kdim), lambda i, j: (j, 0))]
    else:
        assert all(a.shape[1] == kdim for a in a_list) and w.shape[0] == n_pairs * kdim
        in_specs += [pl.BlockSpec((kdim, tn), lambda i, j, p=p: (p, j)) for p in range(n_pairs)]
    args = list(a_list) + [w] * n_pairs
    if res is not None:
        table, layer, chunk = gate
        d = table.shape[2] // 6
        in_specs += [pl.BlockSpec((tm, tn), lambda i, j: (i, j)),
                     pl.BlockSpec((None, SUBLANE, tn), lambda i, j: (layer, 0, chunk * (d // tn) + j))]
        args += [res, table]
    (out,), casts = _hosted_call(
        functools.partial(_mm_kernel, n_pairs=n_pairs, gated=res is not None, tm=tm, w_t=w_t), jobs,
        grid=(n_rows // tm, n // tn),
        in_specs=in_specs,
        out_specs=[pl.BlockSpec((tm, tn), lambda i, j: (i, j))],
        out_shape=[jax.ShapeDtypeStruct((n_rows, n), F32)],
        args=args, sem=("parallel", "parallel"), name=name)
    return (out, casts) if jobs else out


def _mmk_kernel(a_ref, w_ref, res_ref, gate_ref, o_ref, acc_ref, *, tm):
    k = pl.program_id(2)

    @pl.when(k == 0)
    def _():
        acc_ref[...] = jnp.zeros_like(acc_ref)

    acc_ref[...] += _dot(a_ref[...], w_ref[...])

    @pl.when(k == pl.num_programs(2) - 1)
    def _():
        gate = _row_select(pl.program_id(0) * tm, tm, gate_ref[...])
        o_ref[...] = res_ref[...] + gate * acc_ref[...]


def _mm_ktiled_steps(kdim, n, n_rows):
    return ((n_rows // _pick(n_rows, ROW_TILES)) * (n // _pick(n, COL_TILES))
            * (kdim // _pick(kdim, K_TILES)))


def _mm_ktiled(a, w, n_rows, res, gate, jobs=()):
    kdim, n = w.shape
    table, gate_layer, chunk = gate
    d = table.shape[2] // 6
    tm = _pick(n_rows, ROW_TILES)
    tn = _pick(n, COL_TILES)
    tk = _pick(kdim, K_TILES)
    (out,), casts = _hosted_call(
        functools.partial(_mmk_kernel, tm=tm), jobs,
        grid=(n_rows // tm, n // tn, kdim // tk),
        in_specs=[pl.BlockSpec((tm, tk), lambda i, j, k: (i, k)),
                  pl.BlockSpec((tk, tn), lambda i, j, k: (k, j)),
                  pl.BlockSpec((tm, tn), lambda i, j, k: (i, j)),
                  pl.BlockSpec((None, SUBLANE, tn), lambda i, j, k: (gate_layer, 0, chunk * (d // tn) + j))],
        out_specs=[pl.BlockSpec((tm, tn), lambda i, j, k: (i, j))],
        out_shape=[jax.ShapeDtypeStruct((n_rows, n), F32)],
        scratch_shapes=[pltpu.VMEM((tm, tn), F32)],
        args=(a, w, res, table), sem=("parallel", "parallel", "arbitrary"), name="ffn_down")
    return out, casts


def _ffn_up_kernel(u_ref, uh_ref, wg_ref, wu_ref, cw_ref, cb_ref, o_ref, halo_ref, *, tm):
    i = pl.program_id(1)

    @pl.when(i == 0)
    def _():
        halo_ref[...] = _dot(uh_ref[...], wg_ref[...])

    u = u_ref[...]
    g = _dot(u, wg_ref[...])
    dn, up = _shift_rows(g, halo_ref[pl.ds(2 * i, 1), :], halo_ref[pl.ds(2 * i + 1, 1), :], i * tm)
    a = dn * cw_ref[0:1, :] + g * cw_ref[1:2, :] + up * cw_ref[2:3, :] + cb_ref[...]
    o_ref[...] = (_silu(a) * _dot(u, wu_ref[...])).astype(o_ref.dtype)


def _ffn_up_steps(n_rows, n):
    return (n // _pick(n, NARROW_COL_TILES)) * (n_rows // _pick(n_rows, ROW_TILES))


def _ffn_up(u, n_rows, layer, wg, wu, cw_stack, cb_stack, jobs):
    d, n = wg.shape
    tm = _pick(n_rows, ROW_TILES)
    tn = _pick(n, NARROW_COL_TILES)
    gm = n_rows // tm
    n_halo = _round_up(2 * gm, 2 * SUBLANE)
    rows = []
    for i in range(gm):
        rows += [max(i * tm - 1, 0), min((i + 1) * tm, n_rows - 1)]
    rows += [0] * (n_halo - len(rows))
    u_halo = jnp.concatenate([u[r:r + 1] for r in rows], axis=0)
    col = lambda j, i: (layer, 0, j)
    (hid,), casts = _hosted_call(
        functools.partial(_ffn_up_kernel, tm=tm), jobs,
        grid=(n // tn, gm),
        in_specs=[pl.BlockSpec((tm, d), lambda j, i: (i, 0)),
                  pl.BlockSpec((n_halo, d), lambda j, i: (0, 0)),
                  pl.BlockSpec((d, tn), lambda j, i: (0, j)),
                  pl.BlockSpec((d, tn), lambda j, i: (0, j)),
                  pl.BlockSpec((None, 3, tn), col),
                  pl.BlockSpec((None, 1, tn), col)],
        out_specs=[pl.BlockSpec((tm, tn), lambda j, i: (i, j))],
        out_shape=[jax.ShapeDtypeStruct((n_rows, n), BF16)],
        scratch_shapes=[pltpu.VMEM((n_halo, tn), F32)],
        args=(u, u_halo, wg, wu, cw_stack, cb_stack), sem=("parallel", "arbitrary"), name="ffn_up")
    return hid, casts


def _conv_ffn(h, n_rows, gain, mod_rows, mod, layer, wg, wu, cw, cb, wd_stack, next_w):
    d_ff_p = wg.shape[1]
    u = _modulate(h, n_rows, gain, mod_rows, layer, 3, 4)
    job = _CastJob(wd_stack, layer, d_ff_p, wd_stack.shape[2], 0, _ffn_up_steps(n_rows, d_ff_p))
    hid, (wd,) = _ffn_up(u, n_rows, layer, wg, wu, cw, cb, [job])
    jobs = []
    if next_w is not None:
        stack, idx = next_w
        jobs = [_CastJob(stack, idx, stack.shape[1], stack.shape[2], 0, _mm_ktiled_steps(d_ff_p, wd.shape[1], n_rows))]
    h, casts = _mm_ktiled(hid, wd, n_rows, h, (mod, layer, 5), jobs)
    return h, (casts[0] if casts else None)


def _rope(x, rope_ref):
    c, sa, sb = rope_ref[:, 0:LANE], rope_ref[:, LANE:2 * LANE], rope_ref[:, 2 * LANE:3 * LANE]
    quarter = MLA_ROPE // 4
    return x * c + pltpu.roll(x, LANE - quarter, 1) * sa + pltpu.roll(x, quarter, 1) * sb


def _rms(x, width):
    return lax.rsqrt(jnp.sum(x * x, axis=-1, keepdims=True) * (1.0 / width) + EPS)


def _qproj_kernel(cq_ref, ag_ref, w_ref, hg_ref, rope_ref, o_ref, *, heads):
    x = cq_ref[...]
    xn = ((x * _rms(x, x.shape[1])) * ag_ref[...]).astype(BF16)
    y = _dot(xn, w_ref[...])
    for hh in range(heads):
        yh = y[:, hh * QK_PAD:(hh + 1) * QK_PAD]
        yn = (yh * _rms(yh, MLA_NOPE + MLA_ROPE)) * hg_ref[...]
        o_ref[:, hh * QK_PAD:hh * QK_PAD + LANE] = yn[:, :LANE].astype(o_ref.dtype)
        o_ref[:, hh * QK_PAD + LANE:(hh + 1) * QK_PAD] = _rope(yn[:, LANE:], rope_ref).astype(o_ref.dtype)


def _qproj(y_in, n_rows, ag, w, hg, rope):
    heads = _pick(MLA_HEADS, (4, 2, 1))
    tm = _proj_rows()
    tn = heads * QK_PAD
    return pl.pallas_call(
        functools.partial(_qproj_kernel, heads=heads),
        grid=(n_rows // tm, w.shape[1] // tn),
        in_specs=[pl.BlockSpec((tm, MLA_Q_RANK), lambda i, j: (i, 0)),
                  pl.BlockSpec((1, MLA_Q_RANK), lambda i, j: (0, 0)),
                  pl.BlockSpec((MLA_Q_RANK, tn), lambda i, j: (0, j)),
                  pl.BlockSpec((1, QK_PAD), lambda i, j: (0, 0)),
                  _rope_spec()],
        out_specs=pl.BlockSpec((tm, tn), lambda i, j: (i, j)),
        out_shape=jax.ShapeDtypeStruct((n_rows, w.shape[1]), BF16),
        compiler_params=_cp("parallel", "parallel"),
        name="q_proj",
    )(y_in, ag, w, hg, rope)


def _kvproj_kernel(ckv_ref, kr_ref, ag_ref, w_ref, gn_ref, gr_ref, rope_ref, k_ref, v_ref, *, heads):
    x = ckv_ref[...]
    xn = ((x * _rms(x, x.shape[1])) * ag_ref[...]).astype(BF16)
    y = _dot(xn, w_ref[...])
    kr = kr_ref[...]
    kr_ss = jnp.sum(kr * kr, axis=-1, keepdims=True)
    width = MLA_NOPE + MLA_VDIM
    ones = jnp.ones((x.shape[0], V_PAD - MLA_VDIM), v_ref.dtype)
    for hh in range(heads):
        kn = y[:, hh * width:hh * width + MLA_NOPE]
        r = lax.rsqrt((jnp.sum(kn * kn, axis=-1, keepdims=True) + kr_ss) * (1.0 / (MLA_NOPE + MLA_ROPE)) + EPS)
        k_ref[:, hh * QK_PAD:hh * QK_PAD + LANE] = ((kn * r) * gn_ref[...]).astype(k_ref.dtype)
        k_ref[:, hh * QK_PAD + LANE:(hh + 1) * QK_PAD] = _rope((kr * r) * gr_ref[...], rope_ref).astype(k_ref.dtype)
        v_ref[:, hh * V_PAD:hh * V_PAD + MLA_VDIM] = y[:, hh * width + MLA_NOPE:(hh + 1) * width].astype(v_ref.dtype)
        v_ref[:, hh * V_PAD + MLA_VDIM:(hh + 1) * V_PAD] = ones


def _kvproj(y_in, n_rows, ag, w, gn, gr, rope):
    heads = _pick(MLA_HEADS, (4, 2, 1))
    assert MLA_NOPE == LANE and MLA_VDIM == LANE and MLA_Q_RANK % MLA_KV_RANK == 0
    tm = _proj_rows()
    tn = heads * (MLA_NOPE + MLA_VDIM)
    kr_blk = (MLA_Q_RANK + MLA_KV_RANK) // LANE
    return pl.pallas_call(
        functools.partial(_kvproj_kernel, heads=heads),
        grid=(n_rows // tm, w.shape[1] // tn),
        in_specs=[pl.BlockSpec((tm, MLA_KV_RANK), lambda i, j: (i, MLA_Q_RANK // MLA_KV_RANK)),
                  pl.BlockSpec((tm, LANE), lambda i, j: (i, kr_blk)),
                  pl.BlockSpec((1, MLA_KV_RANK), lambda i, j: (0, 0)),
                  pl.BlockSpec((MLA_KV_RANK, tn), lambda i, j: (0, j)),
                  pl.BlockSpec((1, LANE), lambda i, j: (0, 0)),
                  pl.BlockSpec((1, LANE), lambda i, j: (0, 0)),
                  _rope_spec()],
        out_specs=[pl.BlockSpec((tm, heads * QK_PAD), lambda i, j: (i, j)),
                   pl.BlockSpec((tm, heads * V_PAD), lambda i, j: (i, j))],
        out_shape=[jax.ShapeDtypeStruct((n_rows, MLA_HEADS * QK_PAD), BF16),
                   jax.ShapeDtypeStruct((n_rows, MLA_HEADS * V_PAD), BF16)],
        compiler_params=_cp("parallel", "parallel"),
        name="kv_proj",
    )(y_in, y_in, ag, w, gn, gr, rope)


def _attn_kernel(*refs, chunks, scale):
    q = refs[0][...]
    n_kv = (len(refs) - 2) // 2
    k_refs, v_refs, o_ref = refs[1:1 + n_kv], refs[1 + n_kv:1 + 2 * n_kv], refs[1 + 2 * n_kv]
    c = scale * math.log2(math.e)
    m = acc = None
    for idx, start, size in chunks:
        s = _dot_nt(q, k_refs[idx][start:start + size, :])
        m_blk = jnp.max(s, axis=-1, keepdims=True)
        m_new = m_blk if m is None else jnp.maximum(m, m_blk)
        pv = _dot(jnp.exp2((s - m_new) * c).astype(BF16), v_refs[idx][start:start + size, :])
        acc = pv if acc is None else jnp.exp2((m - m_new) * c) * acc + pv
        m = m_new
    o_ref[...] = (acc[:, :MLA_VDIM] / acc[:, MLA_VDIM:2 * MLA_VDIM]).astype(o_ref.dtype)


def _attn_lat_steps():
    return BATCH * MLA_HEADS * (SEQ // _pick(SEQ, (1024, 512, 256)))


def _attention(q, k, v, latent, jobs=()):
    assert V_PAD == 2 * MLA_VDIM
    scale = (MLA_NOPE + MLA_ROPE) ** -0.5
    ctx_blk0 = _n_lat() // CTX_LEN
    ctx_k = pl.BlockSpec((CTX_LEN, QK_PAD), lambda b, h, i: (ctx_blk0 + b, h))
    ctx_v = pl.BlockSpec((CTX_LEN, V_PAD), lambda b, h, i: (ctx_blk0 + b, h))
    if latent:
        tq = _pick(SEQ, (1024, 512, 256))
        tk = _pick(SEQ, (1024, 512, 256))
        q_per_b = SEQ // tq
        q_spec = pl.BlockSpec((tq, QK_PAD), lambda b, h, i: (b * q_per_b + i, h))
        o_spec = pl.BlockSpec((tq, MLA_VDIM), lambda b, h, i: (b * q_per_b + i, h))
        k_specs = [ctx_k, pl.BlockSpec((SEQ, QK_PAD), lambda b, h, i: (b, h))]
        v_specs = [ctx_v, pl.BlockSpec((SEQ, V_PAD), lambda b, h, i: (b, h))]
        chunks = [(0, 0, CTX_LEN)] + [(1, s, tk) for s in range(0, SEQ, tk)]
        n_out, grid = _n_lat(), (BATCH, MLA_HEADS, q_per_b)
    else:
        q_spec = pl.BlockSpec((CTX_LEN, QK_PAD), lambda b, h, i: (ctx_blk0 + b, h))
        o_spec = pl.BlockSpec((CTX_LEN, MLA_VDIM), lambda b, h, i: (b, h))
        k_specs, v_specs, chunks = [ctx_k], [ctx_v], [(0, 0, CTX_LEN)]
        n_out, grid = BATCH * CTX_LEN, (BATCH, MLA_HEADS, 1)
    n_kv = len(k_specs)
    (out,), casts = _hosted_call(
        functools.partial(_attn_kernel, chunks=tuple(chunks), scale=scale), jobs,
        grid=grid,
        in_specs=[q_spec] + k_specs + v_specs,
        out_specs=[o_spec],
        out_shape=[jax.ShapeDtypeStruct((n_out, MLA_HEADS * MLA_VDIM), BF16)],
        args=(q, *([k] * n_kv), *([v] * n_kv)), sem=("parallel", "parallel", "arbitrary"),
        name="attn_lat" if latent else "attn_ctx")
    return out, casts


def _sconv_kernel(b_ref, c_ref, x_ref, cp_ref, xp_ref, cn_ref, xn_ref, w_ref, o_ref, *, tm):
    p = c_ref[...] * x_ref[...]
    prev = cp_ref[SUBLANE - 1:SUBLANE, :] * xp_ref[SUBLANE - 1:SUBLANE, :]
    nxt = cn_ref[0:1, :] * xn_ref[0:1, :]
    dn, up = _shift_rows(p, prev, nxt, pl.program_id(0) * tm)
    conv = dn * w_ref[0:1, :] + p * w_ref[1:2, :] + up * w_ref[2:3, :]
    o_ref[...] = (b_ref[...] * conv).astype(o_ref.dtype)


def _sconv(y_in, n_rows, w, col0, width):
    tm = ROW_BLK
    tn = _pick(width, COL_TILES)
    blk0 = [(col0 + k * width) // tn for k in range(3)]
    main = [pl.BlockSpec((tm, tn), lambda i, j, o=o: (i, o + j)) for o in blk0]
    cp, cn = _halo_specs(tm, tn, n_rows, blk0[1])
    xp, xn = _halo_specs(tm, tn, n_rows, blk0[2])
    return pl.pallas_call(
        functools.partial(_sconv_kernel, tm=tm),
        grid=(n_rows // tm, width // tn),
        in_specs=main + [cp, xp, cn, xn, pl.BlockSpec((3, tn), lambda i, j: (0, j))],
        out_specs=pl.BlockSpec((tm, tn), lambda i, j: (i, j)),
        out_shape=jax.ShapeDtypeStruct((n_rows, width), BF16),
        compiler_params=_cp("parallel", "parallel"),
        name="short_conv",
    )(*([y_in] * 7), w)


def _conv_silu(x_ref, xp_ref, xn_ref, w_ref, row0):
    x = x_ref[...]
    dn, up = _shift_rows(x, xp_ref[SUBLANE - 1:SUBLANE, :], xn_ref[0:1, :], row0)
    return _silu(dn * w_ref[0:1, :] + x * w_ref[1:2, :] + up * w_ref[2:3, :])


def _l2norm(x):
    return x * lax.rsqrt(jnp.sum(x * x, axis=-1, keepdims=True) + EPS)


def _gdn_gate_kernel(ba_ref, alog_ref, dtb_ref, o_ref):
    x = ba_ref[...]
    lane = lax.broadcasted_iota(jnp.int32, x.shape, 1)
    z = x + dtb_ref[...]
    softplus = jnp.maximum(z, 0.0) + jnp.log(1.0 + jnp.exp(-jnp.abs(z)))
    o_ref[...] = jnp.where(lane < x.shape[1] // 2, _sigmoid(x), -jnp.exp(alog_ref[...]) * softplus)


def _gdn_gate(ba, n_rows, alog, dtb):
    w = ba.shape[1]
    return pl.pallas_call(
        _gdn_gate_kernel,
        grid=(n_rows // ROW_BLK,),
        in_specs=[pl.BlockSpec((ROW_BLK, w), lambda i: (i, 0)),
                  pl.BlockSpec((1, w), lambda i: (0, 0)),
                  pl.BlockSpec((1, w), lambda i: (0, 0))],
        out_specs=pl.BlockSpec((ROW_BLK, w), lambda i: (i, 0)),
        out_shape=jax.ShapeDtypeStruct((n_rows, w), F32),
        compiler_params=_cp("parallel"),
        name="gdn_gate",
    )(ba, alog, dtb)


def _active_rows(x, size, odd):
    n = x.shape[0]
    return jnp.concatenate([x[(2 * b + odd) * size:(2 * b + odd + 1) * size] for b in range(n // (2 * size))], axis=0)


def _weave_rows(rest, active, size, odd):
    pieces = []
    for b in range(active.shape[0] // size):
        if rest is None:
            keep = jnp.zeros((size, active.shape[1]), active.dtype)
        else:
            keep = rest[(2 * b + 1 - odd) * size:(2 * b + 2 - odd) * size]
        act = active[b * size:(b + 1) * size]
        pieces += [keep, act] if odd else [act, keep]
    return jnp.concatenate(pieces, axis=0)


def _unit_tri_inverses(l_mats, dirs, eye, base, ring_ref):
    size = SUBLANE
    xs = [-(l * base) for l in l_mats]
    ts = [eye + x for x in xs]
    for _ in range(size.bit_length() - 2):
        xs = [_dot(x.astype(BF16), x.astype(BF16)) for x in xs]
        ts = [t + _dot(t.astype(BF16), x.astype(BF16)) for t, x in zip(ts, xs)]
    level = 0
    while size < GDN_BLK:
        tbs = [t.astype(BF16) for t in ts]
        offs = [(_active_rows(l, size, 1 - d) * ring_ref[level, d]).astype(BF16) for l, d in zip(l_mats, dirs)]
        mids = [_weave_rows(None, _dot(off, tb), size, 1 - d).astype(BF16) for off, tb, d in zip(offs, tbs, dirs)]
        acts = [_active_rows(t, size, 1 - d) for t, d in zip(ts, dirs)]
        news = [act - _dot(act.astype(BF16), mid) for act, mid in zip(acts, mids)]
        ts = [_weave_rows(t, new, size, 1 - d) for t, new, d in zip(ts, news, dirs)]
        size *= 2
        level += 1
    return ts


def _gdn_masks():
    n, size = GDN_BLK, SUBLANE
    i, j = np.arange(n)[:, None], np.arange(n)[None, :]
    tri = np.stack([j <= i, j >= i, j == i, i // size == j // size]).astype(np.float32)
    rings = []
    while size < n:
        ra = np.arange(n // 2)[:, None]
        per_dir = []
        for d in range(2):
            ia = (ra // size) * 2 * size + (1 - d) * size + ra % size
            per_dir.append((ia // (2 * size) == j // (2 * size)) & (ia // size != j // size))
        rings.append(np.stack(per_dir))
        size *= 2
    return jnp.asarray(tri), jnp.asarray(np.stack(rings).astype(np.float32))


def _gdn_prep_kernel(q_ref, qp_ref, qn_ref, k_ref, kp_ref, kn_ref, v_ref, vp_ref, vn_ref, cwq_ref, cwk_ref, cwv_ref,
                     bg_ref, tri_ref, tri16_ref, ring_ref, u_ref, wq_ref, kd_ref, in_ref, cd_ref, *, rep, hpb):
    n = GDN_BLK
    row0 = pl.program_id(0) * n
    q_all = _conv_silu(q_ref, qp_ref, qn_ref, cwq_ref, row0)
    k_all = _conv_silu(k_ref, kp_ref, kn_ref, cwk_ref, row0)
    v_all = _conv_silu(v_ref, vp_ref, vn_ref, cwv_ref, row0)
    eye, base = tri_ref[2], tri_ref[3]
    not_eye = 1.0 - eye
    chains, l_mats, rhss = [], [], []
    for hh in range(hpb):
        q = _l2norm(q_all[:, hh * GDN_DK:(hh + 1) * GDN_DK]) * GDN_DK ** -0.5
        k = _l2norm(k_all[:, hh * GDN_DK:(hh + 1) * GDN_DK])
        q16, k16 = q.astype(BF16), k.astype(BF16)
        qk = _dot_nt(q16, k16)
        kk = _dot_nt(k16, k16) * not_eye
        bg = bg_ref[hh]
        bg_cols = jnp.concatenate([bg, jnp.zeros((LANE - bg.shape[0], n), F32)], axis=0).T
        hi = bg_cols.astype(BF16)
        rest = bg_cols - hi.astype(F32)
        mid = rest.astype(BF16)
        pieces = jnp.concatenate([hi, mid, (rest - mid.astype(F32)).astype(BF16)], axis=1)
        cums = [_dot(tri16_ref[d], pieces) for d in range(2)]
        gc_cols = [c[:, :LANE] + c[:, LANE:2 * LANE] + c[:, 2 * LANE:] for c in cums]
        gc_rows = [g.T for g in gc_cols]
        for e in range(rep):
            for d in range(2):
                vh = hh * rep + e
                cols = slice(vh * GDN_DV, (vh + 1) * GDN_DV)
                incl = tri_ref[d]
                b_idx, g_idx = d * rep + e, (2 + d) * rep + e
                b_col = bg_cols[:, b_idx:b_idx + 1]
                gc_col, gc_row = gc_cols[d][:, g_idx:g_idx + 1], gc_rows[d][g_idx:g_idx + 1, :]
                g_tot = jnp.sum(bg[g_idx:g_idx + 1, :], axis=1, keepdims=True)
                decay = jnp.exp(jnp.minimum(gc_col - gc_row, 0.0)) * incl
                e_col = jnp.exp(gc_col)
                chains.append((vh, d, cols))
                l_mats.append((kk * b_col) * decay)
                rhss.append(jnp.concatenate([v_all[:, cols] * b_col, (k * b_col) * e_col], axis=1).astype(BF16))
                wq_ref[d, n:, cols] = (q * e_col).astype(wq_ref.dtype)
                kd_ref[d, :, cols] = (k * jnp.exp(g_tot - gc_col)).astype(kd_ref.dtype)
                in_ref[d, vh] = (qk * decay).astype(in_ref.dtype)
                cd_ref[d, vh] = jnp.broadcast_to(jnp.exp(g_tot), (SUBLANE, LANE))
    invs = _unit_tri_inverses(l_mats, [d for _, d, _ in chains], eye, base, ring_ref)
    sols = [_dot(t.astype(BF16), rhs) for t, rhs in zip(invs, rhss)]
    for (vh, d, cols), sol in zip(chains, sols):
        u_ref[d, :, cols] = sol[:, :GDN_DV]
        wq_ref[d, :n, cols] = sol[:, GDN_DV:].astype(wq_ref.dtype)


def _gdn_prep(y_in, conv_w, bg_t, n_rows):
    assert GDN_DK == LANE and GDN_DV == LANE and GDN_BLK == 2 * LANE
    nb, hv, hq, rep = n_rows // GDN_BLK, GDN_V_HEADS, GDN_QK_HEADS, GDN_V_HEADS // GDN_QK_HEADS
    hpb = _pick(hq, (2, 1))
    key_w, wide = hpb * GDN_DK, hpb * rep * GDN_DV
    parts = ((key_w, 0), (key_w, hq // hpb), (wide, 2 * hq * GDN_DK // wide))
    in_specs = []
    for tn, blk0 in parts:
        in_specs += [pl.BlockSpec((GDN_BLK, tn), lambda i, h, blk0=blk0: (i, blk0 + h))]
        in_specs += _halo_specs(GDN_BLK, tn, n_rows, blk0)
    in_specs += [pl.BlockSpec((3, tn), lambda i, h, blk0=blk0: (0, blk0 + h)) for tn, blk0 in parts]
    tri, rings = _gdn_masks()
    in_specs += [pl.BlockSpec((hpb, bg_t.shape[1], GDN_BLK), lambda i, h: (h, 0, i)),
                 pl.BlockSpec(tri.shape, lambda i, h: (0, 0, 0)),
                 pl.BlockSpec((2,) + tri.shape[1:], lambda i, h: (0, 0, 0)),
                 pl.BlockSpec(rings.shape, lambda i, h: (0, 0, 0, 0))]
    return pl.pallas_call(
        functools.partial(_gdn_prep_kernel, rep=rep, hpb=hpb),
        grid=(nb, hq // hpb),
        in_specs=in_specs,
        out_specs=[pl.BlockSpec((2, GDN_BLK, wide), lambda i, h: (0, i, h)),
                   pl.BlockSpec((2, None, 2 * GDN_BLK, wide), lambda i, h: (0, i, 0, h)),
                   pl.BlockSpec((2, GDN_BLK, wide), lambda i, h: (0, i, h)),
                   pl.BlockSpec((2, hpb * rep, GDN_BLK, GDN_BLK), lambda i, h: (0, h, i, 0)),
                   pl.BlockSpec((2, hpb * rep, SUBLANE, LANE), lambda i, h: (0, h, i, 0))],
        out_shape=[jax.ShapeDtypeStruct((2, n_rows, hv * GDN_DV), F32),
                   jax.ShapeDtypeStruct((2, nb, 2 * GDN_BLK, hv * GDN_DV), BF16),
                   jax.ShapeDtypeStruct((2, n_rows, hv * GDN_DV), BF16),
                   jax.ShapeDtypeStruct((2, hv, n_rows, GDN_BLK), BF16),
                   jax.ShapeDtypeStruct((2, hv, nb * SUBLANE, LANE), F32)],
        compiler_params=_cp("parallel", "parallel"),
        name="gdn_prep",
    )(*([y_in] * 9), conv_w, conv_w, conv_w, bg_t, tri, tri[:2].astype(BF16), rings)


def _gdn_scan_kernel(*refs, heads):
    ins, (of_ref, ob_ref, s_ref) = refs[:10], refs[10:]
    n = GDN_BLK

    @pl.when(pl.program_id(2) == 0)
    def _():
        s_ref[...] = jnp.zeros_like(s_ref)

    o_refs = (of_ref, ob_ref)
    chains = [(d, hh, slice(hh * GDN_DV, (hh + 1) * GDN_DV)) for d in range(2) for hh in range(heads)]
    u_refs, wq_refs, kd_refs, in_refs, cd_refs = (ins[0::5], ins[1::5], ins[2::5], ins[3::5], ins[4::5])
    states = [s_ref[d, hh] for d, hh, _ in chains]
    wqs = [_dot(wq_refs[d][:, cols], s.astype(BF16)) for (d, hh, cols), s in zip(chains, states)]
    vbs = [(u_refs[d][:, cols] - wq[:n]).astype(BF16) for (d, hh, cols), wq in zip(chains, wqs)]
    for (d, hh, cols), s, wq, vb in zip(chains, states, wqs, vbs):
        o_refs[d][:, cols] = wq[n:] + _dot(in_refs[d][hh], vb)
        s_ref[d, hh] = s * cd_refs[d][hh, 0:1, :] + _dot_tn(kd_refs[d][:, cols], vb)


def _gdn_scan(prep, n_rows):
    hv = GDN_V_HEADS
    heads = _pick(hv, (8, 4, 2, 1))
    lat_blk, ctx_blk = SEQ // GDN_BLK, CTX_LEN // GDN_BLK
    steps = ctx_blk + lat_blk
    ctx0 = _n_lat() // GDN_BLK
    wide = heads * GDN_DV

    def blk_f(b, n):
        return jnp.where(n < ctx_blk, ctx0 + b * ctx_blk + n, b * lat_blk + n - ctx_blk)

    def blk_b(b, n):
        return jnp.where(n < ctx_blk, ctx0 + b * ctx_blk + ctx_blk - 1 - n, b * lat_blk + steps - 1 - n)

    in_specs = []
    for d, blk in enumerate((blk_f, blk_b)):
        rows = pl.BlockSpec((None, GDN_BLK, wide), lambda b, h, n, d=d, blk=blk: (d, blk(b, n), h))
        in_specs += [rows,
                     pl.BlockSpec((None, None, 2 * GDN_BLK, wide), lambda b, h, n, d=d, blk=blk: (d, blk(b, n), 0, h)),
                     rows,
                     pl.BlockSpec((None, heads, GDN_BLK, GDN_BLK), lambda b, h, n, d=d, blk=blk: (d, h, blk(b, n), 0)),
                     pl.BlockSpec((None, heads, SUBLANE, LANE), lambda b, h, n, d=d, blk=blk: (d, h, blk(b, n), 0))]
    out = jax.ShapeDtypeStruct((n_rows, hv * GDN_DV), F32)
    return pl.pallas_call(
        functools.partial(_gdn_scan_kernel, heads=heads),
        grid=(BATCH, hv // heads, steps),
        in_specs=in_specs,
        out_specs=[pl.BlockSpec((GDN_BLK, wide), lambda b, h, n: (blk_f(b, n), h)),
                   pl.BlockSpec((GDN_BLK, wide), lambda b, h, n: (blk_b(b, n), h))],
        out_shape=[out, out],
        scratch_shapes=[pltpu.VMEM((2, heads, GDN_DK, GDN_DV), F32)],
        compiler_params=_cp("parallel", "parallel", "arbitrary"),
        name="gdn_scan",
    )(*prep, *prep)


def _gdn_readout_kernel(of_ref, ob_ref, z_ref, g_ref, o_ref):
    o = of_ref[...] + ob_ref[...]
    z = z_ref[...]
    for hh in range(o.shape[1] // GDN_DV):
        cols = slice(hh * GDN_DV, (hh + 1) * GDN_DV)
        oh = o[:, cols]
        y = (oh * _rms(oh, GDN_DV)) * g_ref[...]
        o_ref[:, cols] = (y * _silu(z[:, cols])).astype(o_ref.dtype)


def _gdn_readout(o_f, o_b, y_in, z_col0, gain, n_rows):
    width = GDN_V_HEADS * GDN_DV
    tn = _pick(width, COL_TILES)
    spec = pl.BlockSpec((ROW_BLK, tn), lambda i, j: (i, j))
    return pl.pallas_call(
        _gdn_readout_kernel,
        grid=(n_rows // ROW_BLK, width // tn),
        in_specs=[spec, spec, pl.BlockSpec((ROW_BLK, tn), lambda i, j: (i, z_col0 // tn + j)),
                  pl.BlockSpec((1, GDN_DV), lambda i, j: (0, 0))],
        out_specs=spec,
        out_shape=jax.ShapeDtypeStruct((n_rows, width), BF16),
        compiler_params=_cp("parallel", "parallel"),
        name="gdn_readout",
    )(o_f, o_b, y_in, gain.reshape(1, GDN_DV))


def _proj_rows():
    return _pick(math.gcd(SEQ, BATCH * CTX_LEN), (512, 256))


def _rope_spec():
    tm = _proj_rows()
    return pl.BlockSpec((tm, 3 * LANE), lambda i, j: (jnp.where(i < _n_lat() // tm, i % (SEQ // tm), SEQ // tm), 0))


def _rope_table(n_ident):
    n_freq = MLA_ROPE // 4
    inv_freq = ROPE_THETA ** (-jnp.arange(n_freq, dtype=F32) / n_freq)
    t = jnp.arange(SEQ)
    ang = jnp.concatenate([(t // GRID_W).astype(F32)[:, None] * inv_freq,
                           (t % GRID_W).astype(F32)[:, None] * inv_freq], axis=1)
    cos, sin = jnp.cos(ang), jnp.sin(ang)
    zero = jnp.zeros_like(sin[:, :n_freq])
    pad = jnp.zeros((SEQ, LANE - MLA_ROPE), F32)
    c = jnp.concatenate([cos[:, :n_freq], cos[:, :n_freq], cos[:, n_freq:], cos[:, n_freq:], pad + 1.0], axis=1)
    sa = jnp.concatenate([-sin[:, :n_freq], zero, -sin[:, n_freq:], zero, pad], axis=1)
    sb = jnp.concatenate([zero, sin[:, :n_freq], zero, sin[:, n_freq:], pad], axis=1)
    ident = jnp.concatenate([jnp.ones((n_ident, LANE), F32), jnp.zeros((n_ident, 2 * LANE), F32)], axis=1)
    return jnp.concatenate([jnp.concatenate([c, sa, sb], axis=1), ident], axis=0)


def _pad_cols(w, n):
    return jnp.pad(w, ((0, 0), (0, n - w.shape[1])))


def _even_layer(h, n_rows, need_ctx, mod, mod_rows, layer, gain, p, idx, ffn):
    w_in_stack, q_a_g, kv_a_g, w_qb, w_kvb, q_g, k_g, sc_w, w_out_stack = p
    ffn_w_gate, ffn_w_up, d_ff_p = ffn
    d = w_out_stack.shape[2]
    sc_width = d - MLA_HEADS * MLA_VDIM
    assert sc_width == MLA_HEADS * MLA_VDIM
    head_w = MLA_NOPE + MLA_ROPE
    c0 = MLA_Q_RANK + MLA_KV_RANK
    conv0 = _round_up(c0 + LANE, _pick(sc_width, COL_TILES))
    n_in = _round_up(conv0 + 3 * sc_width, 1024)
    w_in_t = _split_rows(jnp.swapaxes(w_in_stack, 1, 2), idx, c0 + MLA_ROPE, conv0, n_in)
    w_q = w_qb.reshape(MLA_Q_RANK, MLA_HEADS, head_w)
    w_q = jnp.pad(w_q, ((0, 0), (0, 0), (0, QK_PAD - head_w))).reshape(MLA_Q_RANK, MLA_HEADS * QK_PAD).astype(BF16)
    rope = _rope_table(_proj_rows())

    if isinstance(h, tuple):
        u, h = _modulate(h, n_rows, gain, mod_rows, layer, 0, 1)
    else:
        u = _modulate(h, n_rows, gain, mod_rows, layer, 0, 1)
    y = _mm([u], w_in_t, n_rows, w_t=True, name="even_in")
    q = _qproj(y, n_rows, q_a_g.reshape(1, -1), w_q, jnp.pad(q_g, (0, QK_PAD - head_w)).reshape(1, QK_PAD), rope)
    k, v = _kvproj(y, n_rows, kv_a_g.reshape(1, -1), w_kvb.astype(BF16), k_g[:MLA_NOPE].reshape(1, LANE),
                   jnp.pad(k_g[MLA_NOPE:], (0, LANE - MLA_ROPE)).reshape(1, LANE), rope)
    steps = _attn_lat_steps()
    jobs = [_CastJob(ffn_w_gate, layer, d, d_ff_p, 0, steps // 2),
            _CastJob(ffn_w_up, layer, d, d_ff_p, steps // 2, steps - steps // 2),
            _CastJob(w_out_stack, idx, w_out_stack.shape[1], d, 0, steps)]
    n_out = n_rows if need_ctx else _n_lat()
    o, (wg, wu, w_out_b) = _attention(q, k, v, True, jobs)
    if need_ctx:
        o = jnp.concatenate([o, _attention(q, k, v, False)[0]], axis=0)
    conv = _sconv(y, n_out, sc_w, conv0, sc_width)
    h = _mm([o, conv], w_out_b, n_out, res=h, gate=(mod, layer, 2),
            tn=_pick(d, NARROW_COL_TILES), name="even_out")
    return h, (wg, wu)


def _odd_layer(h, n_rows, need_ctx, mod, mod_rows, layer, gain, p, idx, ffn, w_in_b):
    w_in_stack, conv_w, a_log, dt_bias, o_g, w_out_stack = p
    ffn_w_gate, ffn_w_up, d_ff_p = ffn
    d = h.shape[1]
    hv, hq = GDN_V_HEADS, GDN_QK_HEADS
    qkv_w = 2 * hq * GDN_DK + hv * GDN_DV
    main_w = qkv_w + hv * GDN_DV
    gate_w = _round_up(4 * hv, LANE)
    half = gate_w // 2
    if w_in_b is None:
        w_in_b = w_in_stack[idx].astype(BF16)
    w_ba = jnp.concatenate([_pad_cols(w_in_b[:, main_w:main_w + 2 * hv], half),
                            _pad_cols(w_in_b[:, main_w + 2 * hv:], half)], axis=1)
    zeros = jnp.zeros((1, half), F32)
    alog = jnp.concatenate([zeros, _pad_cols(a_log.reshape(1, -1), half)], axis=1)
    dtb = jnp.concatenate([zeros, _pad_cols(dt_bias.reshape(1, -1), half)], axis=1)

    u = _modulate(h, n_rows, gain, mod_rows, layer, 0, 1)
    tn = _pick(main_w, NARROW_COL_TILES)
    steps = _mm_steps(n_rows, main_w, tn)
    y, (wg, w_out_b) = _mm([u], w_in_b, n_rows, n_cols=main_w, tn=tn,
                           jobs=[_CastJob(ffn_w_gate, layer, d, d_ff_p, 0, steps),
                                 _CastJob(w_out_stack, idx, w_out_stack.shape[1], d, 0, steps)], name="odd_in")
    ba = _mm([u], w_ba, n_rows, name="odd_in_gates")
    bg = _gdn_gate(ba, n_rows, alog, dtb)
    rep = hv // hq
    bg_t = jnp.stack([bg[:, :2 * hv], bg[:, half:half + 2 * hv]], axis=0)
    bg_t = bg_t.reshape(2, n_rows, 2, hq, rep).transpose(3, 0, 2, 4, 1).reshape(hq, 4 * rep, n_rows)
    bg_t = jnp.pad(bg_t, ((0, 0), (0, _round_up(4 * rep, SUBLANE) - 4 * rep), (0, 0)))
    o_f, o_b = _gdn_scan(_gdn_prep(y, conv_w, bg_t, n_rows), n_rows)
    n_out = n_rows if need_ctx else _n_lat()
    yo = _gdn_readout(o_f, o_b, y, qkv_w, o_g, n_out)
    tn = _pick(d, NARROW_COL_TILES)
    h, (wu,) = _mm([yo], w_out_b, n_out, res=h, gate=(mod, layer, 2), tn=tn,
                   jobs=[_CastJob(ffn_w_up, layer, d, d_ff_p, 0, _mm_steps(n_out, d, tn))], name="odd_out")
    return h, (wg, wu)


def kernel(x, c, ctx, c_ctx, ada_w, ada_b, norm_mix, norm_ffn, ffn_w_gate, ffn_w_up, ffn_conv_w, ffn_conv_b,
           ffn_w_down, a_w_in, a_q_a_norm, a_kv_a_norm, a_w_qb, a_w_kvb, a_q_norm, a_k_norm, a_sc_conv, a_w_out,
           c_w_in, c_conv_w, c_a_log, c_dt_bias, c_o_norm, c_w_out):
    bn, t, d = x.shape
    depth = ada_w.shape[0]
    assert (bn, t, ctx.shape[1], ffn_w_gate.shape[2]) == (BATCH, SEQ, CTX_LEN, D_FF)
    assert CTX_LEN % GDN_BLK == 0 and SEQ % GDN_BLK == 0
    h = (x.reshape(bn * t, d), ctx.reshape(bn * CTX_LEN, d))
    cond8 = jnp.concatenate([c, c_ctx[None], jnp.zeros((SUBLANE - bn - 1, d), F32)], axis=0)
    mod = _ada(cond8, ada_w, ada_b)
    mod_rows = mod.reshape(depth * SUBLANE * 6, 1, d)
    d_ff_p = _round_up(D_FF, 512)
    ff_pad = ((0, 0), (0, 0), (0, d_ff_p - D_FF))
    conv_w = jnp.pad(ffn_conv_w, ff_pad)
    conv_b = jnp.pad(ffn_conv_b.reshape(depth, 1, D_FF), ff_pad)
    ffn = (ffn_w_gate, ffn_w_up, d_ff_p)
    w_in_b = None
    for l in range(depth):
        last = l == depth - 1
        if isinstance(h, tuple) and l % 2 == 1:
            h = jnp.concatenate(h, axis=0)
        n_rows = h[0].shape[0] + h[1].shape[0] if isinstance(h, tuple) else h.shape[0]
        i = l // 2
        if l % 2 == 0:
            h, (w_gate, w_up) = _even_layer(
                h, n_rows, not last, mod, mod_rows, l, norm_mix[l],
                (a_w_in, a_q_a_norm[i], a_kv_a_norm[i], a_w_qb[i], a_w_kvb[i], a_q_norm[i], a_k_norm[i],
                 a_sc_conv[i], a_w_out), i, ffn)
        else:
            h, (w_gate, w_up) = _odd_layer(
                h, n_rows, not last, mod, mod_rows, l, norm_mix[l],
                (c_w_in, c_conv_w[i], c_a_log[i], c_dt_bias[i], c_o_norm[i], c_w_out), i, ffn, w_in_b)
        next_w = (c_w_in, (l + 1) // 2) if (l + 1 < depth and (l + 1) % 2 == 1) else None
        h, w_in_b = _conv_ffn(h, h.shape[0], norm_ffn[l], mod_rows, mod, l,
                              w_gate, w_up, conv_w, conv_b, ffn_w_down, next_w)
    return h[:bn * t].reshape(bn, t, d)
```

```python
import functools
import math
from typing import NamedTuple

import jax
import jax.numpy as jnp
import numpy as np
from jax import lax
from jax.experimental import pallas as pl
from jax.experimental.pallas import tpu as pltpu

F32 = jnp.float32
BF16 = jnp.bfloat16

BATCH = 2
SEQ = 4096
GRID_W = 64
CTX_LEN = 256
EPS = 1e-6
D_FF = 11008

MLA_HEADS = 16
MLA_NOPE = 128
MLA_ROPE = 64
MLA_VDIM = 128
MLA_Q_RANK = 1024
MLA_KV_RANK = 512
ROPE_THETA = 10000.0

GDN_QK_HEADS = 16
GDN_V_HEADS = 32
GDN_DK = 128
GDN_DV = 128

LANE = 128
SUBLANE = 8
QK_PAD = 256
V_PAD = 256
ROW_BLK = 256
GDN_BLK = 256
VMEM_LIMIT = 56 * 2**20
ROW_TILES = (1088, 1024, 512, 256)
COL_TILES = (1024, 512, 256, 128)
NARROW_COL_TILES = (512, 256, 128)
K_TILES = (2816, 2048, 1024, 512, 256)


def _cp(*sem, vmem=VMEM_LIMIT):
    return pltpu.CompilerParams(dimension_semantics=sem, vmem_limit_bytes=vmem)


def _pick(n, prefs):
    for p in prefs:
        if n % p == 0:
            return p
    raise ValueError(f"no tile for {n} in {prefs}")


def _round_up(n, m):
    return (n + m - 1) // m * m


def _sigmoid(x):
    return 1.0 / (1.0 + jnp.exp(-x))


def _silu(x):
    return x * _sigmoid(x)


def _dot(a, b):
    return jnp.dot(a, b, preferred_element_type=F32)


def _dot_nt(a, b):
    return lax.dot_general(a, b, (((1,), (1,)), ((), ())), preferred_element_type=F32)


def _dot_tn(a, b):
    return lax.dot_general(a, b, (((0,), (0,)), ((), ())), preferred_element_type=F32)


def _n_lat():
    return BATCH * SEQ


def _seq_edges(row0, tm):
    assert SEQ & (SEQ - 1) == 0 and CTX_LEN & (CTX_LEN - 1) == 0 and SEQ % CTX_LEN == 0
    r = row0 + lax.broadcasted_iota(jnp.int32, (tm, 1), 0)
    is_ctx = r >= _n_lat()
    first = (jnp.bitwise_and(r, CTX_LEN - 1) == 0) & ((jnp.bitwise_and(r, SEQ - 1) == 0) | is_ctx)
    r1 = r + 1
    last = (jnp.bitwise_and(r1, CTX_LEN - 1) == 0) & ((jnp.bitwise_and(r1, SEQ - 1) == 0) | is_ctx)
    return first, last


def _shift_rows(x, prev_row, next_row, row0):
    tm = x.shape[0]
    ridx = lax.broadcasted_iota(jnp.int32, (tm, 1), 0)
    first, last = _seq_edges(row0, tm)
    dn = jnp.where(ridx == 0, prev_row, pltpu.roll(x, 1, 0))
    dn = jnp.where(first, 0.0, dn)
    up = jnp.where(ridx == tm - 1, next_row, pltpu.roll(x, tm - 1, 0))
    up = jnp.where(last, 0.0, up)
    return dn, up


def _row_select(row0, tm, table):
    r = row0 + lax.broadcasted_iota(jnp.int32, (tm, 1), 0)
    out = table[BATCH:BATCH + 1]
    for b in reversed(range(BATCH)):
        out = jnp.where(r < (b + 1) * SEQ, table[b:b + 1], out)
    return out


def _halo_specs(tm, tn, n_rows, col_blk0):
    tmb, last = tm // SUBLANE, n_rows // SUBLANE - 1
    prev = pl.BlockSpec((SUBLANE, tn), lambda i, j: (jnp.maximum(i * tmb - 1, 0), col_blk0 + j))
    nxt = pl.BlockSpec((SUBLANE, tn), lambda i, j: (jnp.minimum((i + 1) * tmb, last), col_blk0 + j))
    return prev, nxt


def _ada_kernel(cond_ref, w_ref, b_ref, o_ref):
    a = _silu(cond_ref[...]).astype(BF16)
    o_ref[...] = _dot(a, w_ref[...].astype(BF16)) + b_ref[...]


def _ada(cond8, ada_w, ada_b):
    n_layer, d, n = ada_w.shape
    tn = _pick(n, NARROW_COL_TILES)
    return pl.pallas_call(
        _ada_kernel,
        grid=(n_layer, n // tn),
        in_specs=[pl.BlockSpec((SUBLANE, d), lambda l, j: (0, 0)),
                  pl.BlockSpec((None, d, tn), lambda l, j: (l, 0, j)),
                  pl.BlockSpec((None, 1, tn), lambda l, j: (l, 0, j))],
        out_specs=pl.BlockSpec((None, SUBLANE, tn), lambda l, j: (l, 0, j)),
        out_shape=jax.ShapeDtypeStruct((n_layer, SUBLANE, n), F32),
        compiler_params=_cp("parallel", "parallel"),
        name="ada",
    )(cond8, ada_w, ada_b.reshape(n_layer, 1, n))


def _modulate_kernel(h_ref, g_ref, sh_ref, sc_ref, o_ref):
    x = h_ref[...]
    y = x * lax.rsqrt(jnp.mean(x * x, axis=-1, keepdims=True) + EPS)
    o_ref[...] = ((y * g_ref[...]) * (1.0 + sc_ref[...]) + sh_ref[...]).astype(o_ref.dtype)


def _modulate_join_kernel(lat_ref, ctx_ref, g_ref, sh_ref, sc_ref, o_ref, h_ref, *, lat_blocks):
    def run(src_ref):
        h_ref[...] = src_ref[...]
        _modulate_kernel(src_ref, g_ref, sh_ref, sc_ref, o_ref)

    pl.when(pl.program_id(0) < lat_blocks)(lambda: run(lat_ref))
    pl.when(pl.program_id(0) >= lat_blocks)(lambda: run(ctx_ref))


def _modulate(h, n_rows, gain, mod_rows, layer, k_shift, k_scale):
    joined = isinstance(h, tuple)
    d = h[0].shape[1] if joined else h.shape[1]
    blk_per_seq = SEQ // ROW_BLK

    def mod_spec(k):
        return pl.BlockSpec(
            (None, 1, d),
            lambda i: ((layer * SUBLANE + jnp.minimum(i // blk_per_seq, BATCH)) * 6 + k, 0, 0))

    if joined:
        lat_blocks, ctx_blocks = h[0].shape[0] // ROW_BLK, h[1].shape[0] // ROW_BLK
        row_spec = pl.BlockSpec((ROW_BLK, d), lambda i: (i, 0))
        return pl.pallas_call(
            functools.partial(_modulate_join_kernel, lat_blocks=lat_blocks),
            grid=(lat_blocks + ctx_blocks,),
            in_specs=[pl.BlockSpec((ROW_BLK, d), lambda i: (jnp.minimum(i, lat_blocks - 1), 0)),
                      pl.BlockSpec((ROW_BLK, d), lambda i: (jnp.maximum(i - lat_blocks, 0), 0)),
                      pl.BlockSpec((1, d), lambda i: (0, 0)),
                      mod_spec(k_shift), mod_spec(k_scale)],
            out_specs=[row_spec, row_spec],
            out_shape=[jax.ShapeDtypeStruct((n_rows, d), BF16), jax.ShapeDtypeStruct((n_rows, d), F32)],
            compiler_params=_cp("parallel"),
            name="modulate_join",
        )(h[0], h[1], gain.reshape(1, d), mod_rows, mod_rows)
    return pl.pallas_call(
        _modulate_kernel,
        grid=(n_rows // ROW_BLK,),
        in_specs=[pl.BlockSpec((ROW_BLK, d), lambda i: (i, 0)),
                  pl.BlockSpec((1, d), lambda i: (0, 0)),
                  mod_spec(k_shift), mod_spec(k_scale)],
        out_specs=pl.BlockSpec((ROW_BLK, d), lambda i: (i, 0)),
        out_shape=jax.ShapeDtypeStruct((n_rows, d), BF16),
        compiler_params=_cp("parallel"),
        name="modulate",
    )(h, gain.reshape(1, d), mod_rows, mod_rows)


class _CastJob(NamedTuple):
    src: jax.Array
    layer: int
    rows_out: int
    cols_out: int
    start: int
    steps: int

    @property
    def tr(self):
        for tr in (16, 32, 64, 128, 256, 512, 1024):
            if self.rows_out % tr == 0 and self.src.shape[1] % tr == 0 and self.rows_out // tr <= self.steps:
                return tr
        raise ValueError("host kernel has too few steps for this cast")


def _hosted_call(kernel_fn, jobs, *, grid, in_specs, out_specs, out_shape, args, scratch_shapes=(), sem, name):
    out_specs, out_shape = list(out_specs), list(out_shape)
    n_in, n_out, n_jobs = len(in_specs), len(out_specs), len(jobs)
    strides = [math.prod(grid[a + 1:]) for a in range(len(grid))]

    def step_of(ids):
        return sum(i * s for i, s in zip(ids, strides))

    cast_in, cast_out, cast_shape = [], [], []
    for job in jobs:
        tr, n_blk, n_src_blk = job.tr, job.rows_out // job.tr, job.src.shape[1] // job.tr

        def blk(*ids, job=job, n_blk=n_blk):
            return jnp.clip(step_of(ids) - job.start, 0, n_blk - 1)

        cast_in.append(pl.BlockSpec((None, tr, job.src.shape[2]),
                                    lambda *ids, job=job, blk=blk, last=n_src_blk - 1:
                                    (job.layer, jnp.minimum(blk(*ids), last), 0)))
        cast_out.append(pl.BlockSpec((tr, job.cols_out), lambda *ids, blk=blk: (blk(*ids), 0)))
        cast_shape.append(jax.ShapeDtypeStruct((job.rows_out, job.cols_out), BF16))

    def kernel(*refs):
        host_in, src_refs = refs[:n_in], refs[n_in:n_in + n_jobs]
        host_out = refs[n_in + n_jobs:n_in + n_jobs + n_out]
        dst_refs = refs[n_in + n_jobs + n_out:n_in + 2 * n_jobs + n_out]
        kernel_fn(*host_in, *host_out, *refs[n_in + 2 * n_jobs + n_out:])
        step = step_of([pl.program_id(a) for a in range(len(grid))])
        for job, src_ref, dst_ref in zip(jobs, src_refs, dst_refs):
            rel, n_src = step - job.start, src_ref.shape[1]
            n_src_blk = job.src.shape[1] // job.tr

            @pl.when((rel >= 0) & (rel < n_src_blk))
            def _(src_ref=src_ref, dst_ref=dst_ref, n_src=n_src):
                dst_ref[:, :n_src] = src_ref[...].astype(dst_ref.dtype)
                if dst_ref.shape[1] > n_src:
                    dst_ref[:, n_src:] = jnp.zeros((dst_ref.shape[0], dst_ref.shape[1] - n_src), dst_ref.dtype)

            @pl.when((rel >= n_src_blk) & (rel < job.rows_out // job.tr))
            def _(dst_ref=dst_ref):
                dst_ref[...] = jnp.zeros_like(dst_ref)

    outs = pl.pallas_call(
        kernel,
        grid=grid,
        in_specs=list(in_specs) + cast_in,
        out_specs=out_specs + cast_out,
        out_shape=out_shape + cast_shape,
        scratch_shapes=list(scratch_shapes),
        compiler_params=_cp(*(["arbitrary"] * len(grid) if jobs else sem)),
        name=name,
    )(*args, *[job.src for job in jobs])
    return list(outs[:n_out]), list(outs[n_out:])


def _split_rows_kernel(*refs, gap0, gap1, end, tr):
    w_refs, o_ref = refs[:-1], refs[-1]
    for s, w_ref in enumerate(w_refs):
        i = pl.program_id(0) * len(w_refs) + s
        zero = ((i >= gap0) & (i < gap1)) | (i >= end)
        rows = slice(s * tr, (s + 1) * tr)

        @pl.when(zero)
        def _(rows=rows):
            o_ref[rows, :] = jnp.zeros((tr, o_ref.shape[1]), o_ref.dtype)

        @pl.when(jnp.logical_not(zero))
        def _(rows=rows, w_ref=w_ref):
            o_ref[rows, :] = w_ref[...].astype(o_ref.dtype)


def _split_rows(w_stack, idx, split, second, n_out):
    _, n_in, k = w_stack.shape
    tr = math.gcd(split, second, n_out, n_in)
    assert tr % (2 * SUBLANE) == 0
    gap0, gap1, end, last = split // tr, second // tr, (second + n_in - split) // tr, n_in // tr - 1
    pieces = _pick(n_out // tr, (4, 2, 1))

    def src(i):
        return jnp.where(i < gap0, i, jnp.clip(i - (gap1 - gap0), 0, last))

    return pl.pallas_call(
        functools.partial(_split_rows_kernel, gap0=gap0, gap1=gap1, end=end, tr=tr),
        grid=(n_out // (tr * pieces),),
        in_specs=[pl.BlockSpec((None, tr, k), lambda i, s=s: (idx, src(i * pieces + s), 0)) for s in range(pieces)],
        out_specs=pl.BlockSpec((tr * pieces, k), lambda i: (i, 0)),
        out_shape=jax.ShapeDtypeStruct((n_out, k), BF16),
        compiler_params=_cp("parallel"),
        name="split_rows",
    )(*([w_stack] * pieces))


def _mm_kernel(*refs, n_pairs, gated, tm, w_t):
    a_refs, w_refs = refs[:n_pairs], refs[n_pairs:2 * n_pairs]
    dot = _dot_nt if w_t else _dot
    acc = dot(a_refs[0][...], w_refs[0][...])
    for a_ref, w_ref in zip(a_refs[1:], w_refs[1:]):
        acc += dot(a_ref[...], w_ref[...])
    if gated:
        res_ref, gate_ref, o_ref = refs[2 * n_pairs:]
        gate = _row_select(pl.program_id(0) * tm, tm, gate_ref[...])
        acc = res_ref[...] + gate * acc
    else:
        o_ref = refs[2 * n_pairs]
    o_ref[...] = acc.astype(o_ref.dtype)


def _mm_steps(n_rows, n, tn=None):
    return (n_rows // _pick(n_rows, ROW_TILES)) * (n // (tn or _pick(n, COL_TILES)))


def _mm(a_list, w, n_rows, *, n_cols=None, res=None, gate=None, tn=None, jobs=(), w_t=False, name="mm"):
    n_pairs = len(a_list)
    kdim = a_list[0].shape[1]
    n = n_cols or w.shape[0 if w_t else 1]
    tm = _pick(n_rows, ROW_TILES)
    tn = tn or _pick(n, COL_TILES)
    in_specs = [pl.BlockSpec((tm, kdim), lambda i, j: (i, 0)) for a in a_list]
    if w_t:
        assert n_pairs == 1 and w.shape[1] == kdim
        in_specs += [pl.BlockSpec((tn, kdim), lambda i, j: (j, 0))]
    else:
        assert all(a.shape[1] == kdim for a in a_list) and w.shape[0] == n_pairs * kdim
        in_specs += [pl.BlockSpec((kdim, tn), lambda i, j, p=p: (p, j)) for p in range(n_pairs)]
    args = list(a_list) + [w] * n_pairs
    if res is not None:
        table, layer, chunk = gate
        d = table.shape[2] // 6
        in_specs += [pl.BlockSpec((tm, tn), lambda i, j: (i, j)),
                     pl.BlockSpec((None, SUBLANE, tn), lambda i, j: (layer, 0, chunk * (d // tn) + j))]
        args += [res, table]
    (out,), casts = _hosted_call(
        functools.partial(_mm_kernel, n_pairs=n_pairs, gated=res is not None, tm=tm, w_t=w_t), jobs,
        grid=(n_rows // tm, n // tn),
        in_specs=in_specs,
        out_specs=[pl.BlockSpec((tm, tn), lambda i, j: (i, j))],
        out_shape=[jax.ShapeDtypeStruct((n_rows, n), F32)],
        args=args, sem=("parallel", "parallel"), name=name)
    return (out, casts) if jobs else out


def _mmk_kernel(a_ref, w_ref, res_ref, gate_ref, o_ref, acc_ref, *, tm):
    k = pl.program_id(2)

    @pl.when(k == 0)
    def _():
        acc_ref[...] = jnp.zeros_like(acc_ref)

    acc_ref[...] += _dot(a_ref[...], w_ref[...])

    @pl.when(k == pl.num_programs(2) - 1)
    def _():
        gate = _row_select(pl.program_id(0) * tm, tm, gate_ref[...])
        o_ref[...] = res_ref[...] + gate * acc_ref[...]


def _mm_ktiled_steps(kdim, n, n_rows):
    return ((n_rows // _pick(n_rows, ROW_TILES)) * (n // _pick(n, COL_TILES))
            * (kdim // _pick(kdim, K_TILES)))


def _mm_ktiled(a, w, n_rows, res, gate, jobs=()):
    kdim, n = w.shape
    table, gate_layer, chunk = gate
    d = table.shape[2] // 6
    tm = _pick(n_rows, ROW_TILES)
    tn = _pick(n, COL_TILES)
    tk = _pick(kdim, K_TILES)
    (out,), casts = _hosted_call(
        functools.partial(_mmk_kernel, tm=tm), jobs,
        grid=(n_rows // tm, n // tn, kdim // tk),
        in_specs=[pl.BlockSpec((tm, tk), lambda i, j, k: (i, k)),
                  pl.BlockSpec((tk, tn), lambda i, j, k: (k, j)),
                  pl.BlockSpec((tm, tn), lambda i, j, k: (i, j)),
                  pl.BlockSpec((None, SUBLANE, tn), lambda i, j, k: (gate_layer, 0, chunk * (d // tn) + j))],
        out_specs=[pl.BlockSpec((tm, tn), lambda i, j, k: (i, j))],
        out_shape=[jax.ShapeDtypeStruct((n_rows, n), F32)],
        scratch_shapes=[pltpu.VMEM((tm, tn), F32)],
        args=(a, w, res, table), sem=("parallel", "parallel", "arbitrary"), name="ffn_down")
    return out, casts


def _ffn_up_kernel(u_ref, uh_ref, wg_ref, wu_ref, cw_ref, cb_ref, o_ref, halo_ref, *, tm):
    i = pl.program_id(1)

    @pl.when(i == 0)
    def _():
        halo_ref[...] = _dot(uh_ref[...], wg_ref[...])

    u = u_ref[...]
    g = _dot(u, wg_ref[...])
    dn, up = _shift_rows(g, halo_ref[pl.ds(2 * i, 1), :], halo_ref[pl.ds(2 * i + 1, 1), :], i * tm)
    a = dn * cw_ref[0:1, :] + g * cw_ref[1:2, :] + up * cw_ref[2:3, :] + cb_ref[...]
    o_ref[...] = (_silu(a) * _dot(u, wu_ref[...])).astype(o_ref.dtype)


def _ffn_up_steps(n_rows, n):
    return (n // _pick(n, NARROW_COL_TILES)) * (n_rows // _pick(n_rows, ROW_TILES))


def _ffn_up(u, n_rows, layer, wg, wu, cw_stack, cb_stack, jobs):
    d, n = wg.shape
    tm = _pick(n_rows, ROW_TILES)
    tn = _pick(n, NARROW_COL_TILES)
    gm = n_rows // tm
    n_halo = _round_up(2 * gm, 2 * SUBLANE)
    rows = []
    for i in range(gm):
        rows += [max(i * tm - 1, 0), min((i + 1) * tm, n_rows - 1)]
    rows += [0] * (n_halo - len(rows))
    u_halo = jnp.concatenate([u[r:r + 1] for r in rows], axis=0)
    col = lambda j, i: (layer, 0, j)
    (hid,), casts = _hosted_call(
        functools.partial(_ffn_up_kernel, tm=tm), jobs,
        grid=(n // tn, gm),
        in_specs=[pl.BlockSpec((tm, d), lambda j, i: (i, 0)),
                  pl.BlockSpec((n_halo, d), lambda j, i: (0, 0)),
                  pl.BlockSpec((d, tn), lambda j, i: (0, j)),
                  pl.BlockSpec((d, tn), lambda j, i: (0, j)),
                  pl.BlockSpec((None, 3, tn), col),
                  pl.BlockSpec((None, 1, tn), col)],
        out_specs=[pl.BlockSpec((tm, tn), lambda j, i: (i, j))],
        out_shape=[jax.ShapeDtypeStruct((n_rows, n), BF16)],
        scratch_shapes=[pltpu.VMEM((n_halo, tn), F32)],
        args=(u, u_halo, wg, wu, cw_stack, cb_stack), sem=("parallel", "arbitrary"), name="ffn_up")
    return hid, casts


def _conv_ffn(h, n_rows, gain, mod_rows, mod, layer, wg, wu, cw, cb, wd_stack, next_w):
    d_ff_p = wg.shape[1]
    u = _modulate(h, n_rows, gain, mod_rows, layer, 3, 4)
    job = _CastJob(wd_stack, layer, d_ff_p, wd_stack.shape[2], 0, _ffn_up_steps(n_rows, d_ff_p))
    hid, (wd,) = _ffn_up(u, n_rows, layer, wg, wu, cw, cb, [job])
    jobs = []
    if next_w is not None:
        stack, idx = next_w
        jobs = [_CastJob(stack, idx, stack.shape[1], stack.shape[2], 0, _mm_ktiled_steps(d_ff_p, wd.shape[1], n_rows))]
    h, casts = _mm_ktiled(hid, wd, n_rows, h, (mod, layer, 5), jobs)
    return h, (casts[0] if casts else None)


def _rope(x, rope_ref):
    c, sa, sb = rope_ref[:, 0:LANE], rope_ref[:, LANE:2 * LANE], rope_ref[:, 2 * LANE:3 * LANE]
    quarter = MLA_ROPE // 4
    return x * c + pltpu.roll(x, LANE - quarter, 1) * sa + pltpu.roll(x, quarter, 1) * sb


def _rms(x, width):
    return lax.rsqrt(jnp.sum(x * x, axis=-1, keepdims=True) * (1.0 / width) + EPS)


def _qproj_kernel(cq_ref, ag_ref, w_ref, hg_ref, rope_ref, o_ref, *, heads):
    x = cq_ref[...]
    xn = ((x * _rms(x, x.shape[1])) * ag_ref[...]).astype(BF16)
    y = _dot(xn, w_ref[...])
    for hh in range(heads):
        yh = y[:, hh * QK_PAD:(hh + 1) * QK_PAD]
        yn = (yh * _rms(yh, MLA_NOPE + MLA_ROPE)) * hg_ref[...]
        o_ref[:, hh * QK_PAD:hh * QK_PAD + LANE] = yn[:, :LANE].astype(o_ref.dtype)
        o_ref[:, hh * QK_PAD + LANE:(hh + 1) * QK_PAD] = _rope(yn[:, LANE:], rope_ref).astype(o_ref.dtype)


def _qproj(y_in, n_rows, ag, w, hg, rope):
    heads = _pick(MLA_HEADS, (4, 2, 1))
    tm = _proj_rows()
    tn = heads * QK_PAD
    return pl.pallas_call(
        functools.partial(_qproj_kernel, heads=heads),
        grid=(n_rows // tm, w.shape[1] // tn),
        in_specs=[pl.BlockSpec((tm, MLA_Q_RANK), lambda i, j: (i, 0)),
                  pl.BlockSpec((1, MLA_Q_RANK), lambda i, j: (0, 0)),
                  pl.BlockSpec((MLA_Q_RANK, tn), lambda i, j: (0, j)),
                  pl.BlockSpec((1, QK_PAD), lambda i, j: (0, 0)),
                  _rope_spec()],
        out_specs=pl.BlockSpec((tm, tn), lambda i, j: (i, j)),
        out_shape=jax.ShapeDtypeStruct((n_rows, w.shape[1]), BF16),
        compiler_params=_cp("parallel", "parallel"),
        name="q_proj",
    )(y_in, ag, w, hg, rope)


def _kvproj_kernel(ckv_ref, kr_ref, ag_ref, w_ref, gn_ref, gr_ref, rope_ref, k_ref, v_ref, *, heads):
    x = ckv_ref[...]
    xn = ((x * _rms(x, x.shape[1])) * ag_ref[...]).astype(BF16)
    y = _dot(xn, w_ref[...])
    kr = kr_ref[...]
    kr_ss = jnp.sum(kr * kr, axis=-1, keepdims=True)
    width = MLA_NOPE + MLA_VDIM
    ones = jnp.ones((x.shape[0], V_PAD - MLA_VDIM), v_ref.dtype)
    for hh in range(heads):
        kn = y[:, hh * width:hh * width + MLA_NOPE]
        r = lax.rsqrt((jnp.sum(kn * kn, axis=-1, keepdims=True) + kr_ss) * (1.0 / (MLA_NOPE + MLA_ROPE)) + EPS)
        k_ref[:, hh * QK_PAD:hh * QK_PAD + LANE] = ((kn * r) * gn_ref[...]).astype(k_ref.dtype)
        k_ref[:, hh * QK_PAD + LANE:(hh + 1) * QK_PAD] = _rope((kr * r) * gr_ref[...], rope_ref).astype(k_ref.dtype)
        v_ref[:, hh * V_PAD:hh * V_PAD + MLA_VDIM] = y[:, hh * width + MLA_NOPE:(hh + 1) * width].astype(v_ref.dtype)
        v_ref[:, hh * V_PAD + MLA_VDIM:(hh + 1) * V_PAD] = ones


def _kvproj(y_in, n_rows, ag, w, gn, gr, rope):
    heads = _pick(MLA_HEADS, (4, 2, 1))
    assert MLA_NOPE == LANE and MLA_VDIM == LANE and MLA_Q_RANK % MLA_KV_RANK == 0
    tm = _proj_rows()
    tn = heads * (MLA_NOPE + MLA_VDIM)
    kr_blk = (MLA_Q_RANK + MLA_KV_RANK) // LANE
    return pl.pallas_call(
        functools.partial(_kvproj_kernel, heads=heads),
        grid=(n_rows // tm, w.shape[1] // tn),
        in_specs=[pl.BlockSpec((tm, MLA_KV_RANK), lambda i, j: (i, MLA_Q_RANK // MLA_KV_RANK)),
                  pl.BlockSpec((tm, LANE), lambda i, j: (i, kr_blk)),
                  pl.BlockSpec((1, MLA_KV_RANK), lambda i, j: (0, 0)),
                  pl.BlockSpec((MLA_KV_RANK, tn), lambda i, j: (0, j)),
                  pl.BlockSpec((1, LANE), lambda i, j: (0, 0)),
                  pl.BlockSpec((1, LANE), lambda i, j: (0, 0)),
                  _rope_spec()],
        out_specs=[pl.BlockSpec((tm, heads * QK_PAD), lambda i, j: (i, j)),
                   pl.BlockSpec((tm, heads * V_PAD), lambda i, j: (i, j))],
        out_shape=[jax.ShapeDtypeStruct((n_rows, MLA_HEADS * QK_PAD), BF16),
                   jax.ShapeDtypeStruct((n_rows, MLA_HEADS * V_PAD), BF16)],
        compiler_params=_cp("parallel", "parallel"),
        name="kv_proj",
    )(y_in, y_in, ag, w, gn, gr, rope)


def _attn_kernel(*refs, chunks, scale):
    q = refs[0][...]
    n_kv = (len(refs) - 2) // 2
    k_refs, v_refs, o_ref = refs[1:1 + n_kv], refs[1 + n_kv:1 + 2 * n_kv], refs[1 + 2 * n_kv]
    c = scale * math.log2(math.e)
    m = acc = None
    for idx, start, size in chunks:
        s = _dot_nt(q, k_refs[idx][start:start + size, :])
        m_blk = jnp.max(s, axis=-1, keepdims=True)
        m_new = m_blk if m is None else jnp.maximum(m, m_blk)
        pv = _dot(jnp.exp2((s - m_new) * c).astype(BF16), v_refs[idx][start:start + size, :])
        acc = pv if acc is None else jnp.exp2((m - m_new) * c) * acc + pv
        m = m_new
    o_ref[...] = (acc[:, :MLA_VDIM] / acc[:, MLA_VDIM:2 * MLA_VDIM]).astype(o_ref.dtype)


def _attn_lat_steps():
    return BATCH * MLA_HEADS * (SEQ // _pick(SEQ, (1024, 512, 256)))


def _attention(q, k, v, latent, jobs=()):
    assert V_PAD == 2 * MLA_VDIM
    scale = (MLA_NOPE + MLA_ROPE) ** -0.5
    ctx_blk0 = _n_lat() // CTX_LEN
    ctx_k = pl.BlockSpec((CTX_LEN, QK_PAD), lambda b, h, i: (ctx_blk0 + b, h))
    ctx_v = pl.BlockSpec((CTX_LEN, V_PAD), lambda b, h, i: (ctx_blk0 + b, h))
    if latent:
        tq = _pick(SEQ, (1024, 512, 256))
        tk = _pick(SEQ, (1024, 512, 256))
        q_per_b = SEQ // tq
        q_spec = pl.BlockSpec((tq, QK_PAD), lambda b, h, i: (b * q_per_b + i, h))
        o_spec = pl.BlockSpec((tq, MLA_VDIM), lambda b, h, i: (b * q_per_b + i, h))
        k_specs = [ctx_k, pl.BlockSpec((SEQ, QK_PAD), lambda b, h, i: (b, h))]
        v_specs = [ctx_v, pl.BlockSpec((SEQ, V_PAD), lambda b, h, i: (b, h))]
        chunks = [(0, 0, CTX_LEN)] + [(1, s, tk) for s in range(0, SEQ, tk)]
        n_out, grid = _n_lat(), (BATCH, MLA_HEADS, q_per_b)
    else:
        q_spec = pl.BlockSpec((CTX_LEN, QK_PAD), lambda b, h, i: (ctx_blk0 + b, h))
        o_spec = pl.BlockSpec((CTX_LEN, MLA_VDIM), lambda b, h, i: (b, h))
        k_specs, v_specs, chunks = [ctx_k], [ctx_v], [(0, 0, CTX_LEN)]
        n_out, grid = BATCH * CTX_LEN, (BATCH, MLA_HEADS, 1)
    n_kv = len(k_specs)
    (out,), casts = _hosted_call(
        functools.partial(_attn_kernel, chunks=tuple(chunks), scale=scale), jobs,
        grid=grid,
        in_specs=[q_spec] + k_specs + v_specs,
        out_specs=[o_spec],
        out_shape=[jax.ShapeDtypeStruct((n_out, MLA_HEADS * MLA_VDIM), BF16)],
        args=(q, *([k] * n_kv), *([v] * n_kv)), sem=("parallel", "parallel", "arbitrary"),
        name="attn_lat" if latent else "attn_ctx")
    return out, casts


def _sconv_kernel(b_ref, c_ref, x_ref, cp_ref, xp_ref, cn_ref, xn_ref, w_ref, o_ref, *, tm):
    p = c_ref[...] * x_ref[...]
    prev = cp_ref[SUBLANE - 1:SUBLANE, :] * xp_ref[SUBLANE - 1:SUBLANE, :]
    nxt = cn_ref[0:1, :] * xn_ref[0:1, :]
    dn, up = _shift_rows(p, prev, nxt, pl.program_id(0) * tm)
    conv = dn * w_ref[0:1, :] + p * w_ref[1:2, :] + up * w_ref[2:3, :]
    o_ref[...] = (b_ref[...] * conv).astype(o_ref.dtype)


def _sconv(y_in, n_rows, w, col0, width):
    tm = ROW_BLK
    tn = _pick(width, COL_TILES)
    blk0 = [(col0 + k * width) // tn for k in range(3)]
    main = [pl.BlockSpec((tm, tn), lambda i, j, o=o: (i, o + j)) for o in blk0]
    cp, cn = _halo_specs(tm, tn, n_rows, blk0[1])
    xp, xn = _halo_specs(tm, tn, n_rows, blk0[2])
    return pl.pallas_call(
        functools.partial(_sconv_kernel, tm=tm),
        grid=(n_rows // tm, width // tn),
        in_specs=main + [cp, xp, cn, xn, pl.BlockSpec((3, tn), lambda i, j: (0, j))],
        out_specs=pl.BlockSpec((tm, tn), lambda i, j: (i, j)),
        out_shape=jax.ShapeDtypeStruct((n_rows, width), BF16),
        compiler_params=_cp("parallel", "parallel"),
        name="short_conv",
    )(*([y_in] * 7), w)


def _conv_silu(x_ref, xp_ref, xn_ref, w_ref, row0):
    x = x_ref[...]
    dn, up = _shift_rows(x, xp_ref[SUBLANE - 1:SUBLANE, :], xn_ref[0:1, :], row0)
    return _silu(dn * w_ref[0:1, :] + x * w_ref[1:2, :] + up * w_ref[2:3, :])


def _l2norm(x):
    return x * lax.rsqrt(jnp.sum(x * x, axis=-1, keepdims=True) + EPS)


def _gdn_gate_kernel(ba_ref, alog_ref, dtb_ref, o_ref):
    x = ba_ref[...]
    lane = lax.broadcasted_iota(jnp.int32, x.shape, 1)
    z = x + dtb_ref[...]
    softplus = jnp.maximum(z, 0.0) + jnp.log(1.0 + jnp.exp(-jnp.abs(z)))
    o_ref[...] = jnp.where(lane < x.shape[1] // 2, _sigmoid(x), -jnp.exp(alog_ref[...]) * softplus)


def _gdn_gate(ba, n_rows, alog, dtb):
    w = ba.shape[1]
    return pl.pallas_call(
        _gdn_gate_kernel,
        grid=(n_rows // ROW_BLK,),
        in_specs=[pl.BlockSpec((ROW_BLK, w), lambda i: (i, 0)),
                  pl.BlockSpec((1, w), lambda i: (0, 0)),
                  pl.BlockSpec((1, w), lambda i: (0, 0))],
        out_specs=pl.BlockSpec((ROW_BLK, w), lambda i: (i, 0)),
        out_shape=jax.ShapeDtypeStruct((n_rows, w), F32),
        compiler_params=_cp("parallel"),
        name="gdn_gate",
    )(ba, alog, dtb)


def _active_rows(x, size, odd):
    n = x.shape[0]
    return jnp.concatenate([x[(2 * b + odd) * size:(2 * b + odd + 1) * size] for b in range(n // (2 * size))], axis=0)


def _weave_rows(rest, active, size, odd):
    pieces = []
    for b in range(active.shape[0] // size):
        if rest is None:
            keep = jnp.zeros((size, active.shape[1]), active.dtype)
        else:
            keep = rest[(2 * b + 1 - odd) * size:(2 * b + 2 - odd) * size]
        act = active[b * size:(b + 1) * size]
        pieces += [keep, act] if odd else [act, keep]
    return jnp.concatenate(pieces, axis=0)


def _unit_tri_inverses(l_mats, dirs, eye, base, ring_ref):
    size = SUBLANE
    xs = [-(l * base) for l in l_mats]
    ts = [eye + x for x in xs]
    for _ in range(size.bit_length() - 2):
        xs = [_dot(x.astype(BF16), x.astype(BF16)) for x in xs]
        ts = [t + _dot(t.astype(BF16), x.astype(BF16)) for t, x in zip(ts, xs)]
    level = 0
    while size < GDN_BLK:
        tbs = [t.astype(BF16) for t in ts]
        offs = [(_active_rows(l, size, 1 - d) * ring_ref[level, d]).astype(BF16) for l, d in zip(l_mats, dirs)]
        mids = [_weave_rows(None, _dot(off, tb), size, 1 - d).astype(BF16) for off, tb, d in zip(offs, tbs, dirs)]
        acts = [_active_rows(t, size, 1 - d) for t, d in zip(ts, dirs)]
        news = [act - _dot(act.astype(BF16), mid) for act, mid in zip(acts, mids)]
        ts = [_weave_rows(t, new, size, 1 - d) for t, new, d in zip(ts, news, dirs)]
        size *= 2
        level += 1
    return ts


def _gdn_masks():
    n, size = GDN_BLK, SUBLANE
    i, j = np.arange(n)[:, None], np.arange(n)[None, :]
    tri = np.stack([j <= i, j >= i, j == i, i // size == j // size]).astype(np.float32)
    rings = []
    while size < n:
        ra = np.arange(n // 2)[:, None]
        per_dir = []
        for d in range(2):
            ia = (ra // size) * 2 * size + (1 - d) * size + ra % size
            per_dir.append((ia // (2 * size) == j // (2 * size)) & (ia // size != j // size))
        rings.append(np.stack(per_dir))
        size *= 2
    return jnp.asarray(tri), jnp.asarray(np.stack(rings).astype(np.float32))


def _gdn_prep_kernel(q_ref, qp_ref, qn_ref, k_ref, kp_ref, kn_ref, v_ref, vp_ref, vn_ref, cwq_ref, cwk_ref, cwv_ref,
                     bg_ref, tri_ref, tri16_ref, ring_ref, u_ref, wq_ref, kd_ref, in_ref, cd_ref, *, rep, hpb):
    n = GDN_BLK
    row0 = pl.program_id(0) * n
    q_all = _conv_silu(q_ref, qp_ref, qn_ref, cwq_ref, row0)
    k_all = _conv_silu(k_ref, kp_ref, kn_ref, cwk_ref, row0)
    v_all = _conv_silu(v_ref, vp_ref, vn_ref, cwv_ref, row0)
    eye, base = tri_ref[2], tri_ref[3]
    not_eye = 1.0 - eye
    chains, l_mats, rhss = [], [], []
    for hh in range(hpb):
        q = _l2norm(q_all[:, hh * GDN_DK:(hh + 1) * GDN_DK]) * GDN_DK ** -0.5
        k = _l2norm(k_all[:, hh * GDN_DK:(hh + 1) * GDN_DK])
        q16, k16 = q.astype(BF16), k.astype(BF16)
        qk = _dot_nt(q16, k16)
        kk = _dot_nt(k16, k16) * not_eye
        bg = bg_ref[hh]
        bg_cols = jnp.concatenate([bg, jnp.zeros((LANE - bg.shape[0], n), F32)], axis=0).T
        hi = bg_cols.astype(BF16)
        rest = bg_cols - hi.astype(F32)
        mid = rest.astype(BF16)
        pieces = jnp.concatenate([hi, mid, (rest - mid.astype(F32)).astype(BF16)], axis=1)
        cums = [_dot(tri16_ref[d], pieces) for d in range(2)]
        gc_cols = [c[:, :LANE] + c[:, LANE:2 * LANE] + c[:, 2 * LANE:] for c in cums]
        gc_rows = [g.T for g in gc_cols]
        for e in range(rep):
            for d in range(2):
                vh = hh * rep + e
                cols = slice(vh * GDN_DV, (vh + 1) * GDN_DV)
                incl = tri_ref[d]
                b_idx, g_idx = d * rep + e, (2 + d) * rep + e
                b_col = bg_cols[:, b_idx:b_idx + 1]
                gc_col, gc_row = gc_cols[d][:, g_idx:g_idx + 1], gc_rows[d][g_idx:g_idx + 1, :]
                g_tot = jnp.sum(bg[g_idx:g_idx + 1, :], axis=1, keepdims=True)
                decay = jnp.exp(jnp.minimum(gc_col - gc_row, 0.0)) * incl
                e_col = jnp.exp(gc_col)
                chains.append((vh, d, cols))
                l_mats.append((kk * b_col) * decay)
                rhss.append(jnp.concatenate([v_all[:, cols] * b_col, (k * b_col) * e_col], axis=1).astype(BF16))
                wq_ref[d, n:, cols] = (q * e_col).astype(wq_ref.dtype)
                kd_ref[d, :, cols] = (k * jnp.exp(g_tot - gc_col)).astype(kd_ref.dtype)
                in_ref[d, vh] = (qk * decay).astype(in_ref.dtype)
                cd_ref[d, vh] = jnp.broadcast_to(jnp.exp(g_tot), (SUBLANE, LANE))
    invs = _unit_tri_inverses(l_mats, [d for _, d, _ in chains], eye, base, ring_ref)
    sols = [_dot(t.astype(BF16), rhs) for t, rhs in zip(invs, rhss)]
    for (vh, d, cols), sol in zip(chains, sols):
        u_ref[d, :, cols] = sol[:, :GDN_DV]
        wq_ref[d, :n, cols] = sol[:, GDN_DV:].astype(wq_ref.dtype)


def _gdn_prep(y_in, conv_w, bg_t, n_rows):
    assert GDN_DK == LANE and GDN_DV == LANE and GDN_BLK == 2 * LANE
    nb, hv, hq, rep = n_rows // GDN_BLK, GDN_V_HEADS, GDN_QK_HEADS, GDN_V_HEADS // GDN_QK_HEADS
    hpb = _pick(hq, (2, 1))
    key_w, wide = hpb * GDN_DK, hpb * rep * GDN_DV
    parts = ((key_w, 0), (key_w, hq // hpb), (wide, 2 * hq * GDN_DK // wide))
    in_specs = []
    for tn, blk0 in parts:
        in_specs += [pl.BlockSpec((GDN_BLK, tn), lambda i, h, blk0=blk0: (i, blk0 + h))]
        in_specs += _halo_specs(GDN_BLK, tn, n_rows, blk0)
    in_specs += [pl.BlockSpec((3, tn), lambda i, h, blk0=blk0: (0, blk0 + h)) for tn, blk0 in parts]
    tri, rings = _gdn_masks()
    in_specs += [pl.BlockSpec((hpb, bg_t.shape[1], GDN_BLK), lambda i, h: (h, 0, i)),
                 pl.BlockSpec(tri.shape, lambda i, h: (0, 0, 0)),
                 pl.BlockSpec((2,) + tri.shape[1:], lambda i, h: (0, 0, 0)),
                 pl.BlockSpec(rings.shape, lambda i, h: (0, 0, 0, 0))]
    return pl.pallas_call(
        functools.partial(_gdn_prep_kernel, rep=rep, hpb=hpb),
        grid=(nb, hq // hpb),
        in_specs=in_specs,
        out_specs=[pl.BlockSpec((2, GDN_BLK, wide), lambda i, h: (0, i, h)),
                   pl.BlockSpec((2, None, 2 * GDN_BLK, wide), lambda i, h: (0, i, 0, h)),
                   pl.BlockSpec((2, GDN_BLK, wide), lambda i, h: (0, i, h)),
                   pl.BlockSpec((2, hpb * rep, GDN_BLK, GDN_BLK), lambda i, h: (0, h, i, 0)),
                   pl.BlockSpec((2, hpb * rep, SUBLANE, LANE), lambda i, h: (0, h, i, 0))],
        out_shape=[jax.ShapeDtypeStruct((2, n_rows, hv * GDN_DV), F32),
                   jax.ShapeDtypeStruct((2, nb, 2 * GDN_BLK, hv * GDN_DV), BF16),
                   jax.ShapeDtypeStruct((2, n_rows, hv * GDN_DV), BF16),
                   jax.ShapeDtypeStruct((2, hv, n_rows, GDN_BLK), BF16),
                   jax.ShapeDtypeStruct((2, hv, nb * SUBLANE, LANE), F32)],
        compiler_params=_cp("parallel", "parallel"),
        name="gdn_prep",
    )(*([y_in] * 9), conv_w, conv_w, conv_w, bg_t, tri, tri[:2].astype(BF16), rings)


def _gdn_scan_kernel(*refs, heads):
    ins, (of_ref, ob_ref, s_ref) = refs[:10], refs[10:]
    n = GDN_BLK

    @pl.when(pl.program_id(2) == 0)
    def _():
        s_ref[...] = jnp.zeros_like(s_ref)

    o_refs = (of_ref, ob_ref)
    chains = [(d, hh, slice(hh * GDN_DV, (hh + 1) * GDN_DV)) for d in range(2) for hh in range(heads)]
    u_refs, wq_refs, kd_refs, in_refs, cd_refs = (ins[0::5], ins[1::5], ins[2::5], ins[3::5], ins[4::5])
    states = [s_ref[d, hh] for d, hh, _ in chains]
    wqs = [_dot(wq_refs[d][:, cols], s.astype(BF16)) for (d, hh, cols), s in zip(chains, states)]
    vbs = [(u_refs[d][:, cols] - wq[:n]).astype(BF16) for (d, hh, cols), wq in zip(chains, wqs)]
    for (d, hh, cols), s, wq, vb in zip(chains, states, wqs, vbs):
        o_refs[d][:, cols] = wq[n:] + _dot(in_refs[d][hh], vb)
        s_ref[d, hh] = s * cd_refs[d][hh, 0:1, :] + _dot_tn(kd_refs[d][:, cols], vb)


def _gdn_scan(prep, n_rows):
    hv = GDN_V_HEADS
    heads = _pick(hv, (8, 4, 2, 1))
    lat_blk, ctx_blk = SEQ // GDN_BLK, CTX_LEN // GDN_BLK
    steps = ctx_blk + lat_blk
    ctx0 = _n_lat() // GDN_BLK
    wide = heads * GDN_DV

    def blk_f(b, n):
        return jnp.where(n < ctx_blk, ctx0 + b * ctx_blk + n, b * lat_blk + n - ctx_blk)

    def blk_b(b, n):
        return jnp.where(n < ctx_blk, ctx0 + b * ctx_blk + ctx_blk - 1 - n, b * lat_blk + steps - 1 - n)

    in_specs = []
    for d, blk in enumerate((blk_f, blk_b)):
        rows = pl.BlockSpec((None, GDN_BLK, wide), lambda b, h, n, d=d, blk=blk: (d, blk(b, n), h))
        in_specs += [rows,
                     pl.BlockSpec((None, None, 2 * GDN_BLK, wide), lambda b, h, n, d=d, blk=blk: (d, blk(b, n), 0, h)),
                     rows,
                     pl.BlockSpec((None, heads, GDN_BLK, GDN_BLK), lambda b, h, n, d=d, blk=blk: (d, h, blk(b, n), 0)),
                     pl.BlockSpec((None, heads, SUBLANE, LANE), lambda b, h, n, d=d, blk=blk: (d, h, blk(b, n), 0))]
    out = jax.ShapeDtypeStruct((n_rows, hv * GDN_DV), F32)
    return pl.pallas_call(
        functools.partial(_gdn_scan_kernel, heads=heads),
        grid=(BATCH, hv // heads, steps),
        in_specs=in_specs,
        out_specs=[pl.BlockSpec((GDN_BLK, wide), lambda b, h, n: (blk_f(b, n), h)),
                   pl.BlockSpec((GDN_BLK, wide), lambda b, h, n: (blk_b(b, n), h))],
        out_shape=[out, out],
        scratch_shapes=[pltpu.VMEM((2, heads, GDN_DK, GDN_DV), F32)],
        compiler_params=_cp("parallel", "parallel", "arbitrary"),
        name="gdn_scan",
    )(*prep, *prep)


def _gdn_readout_kernel(of_ref, ob_ref, z_ref, g_ref, o_ref):
    o = of_ref[...] + ob_ref[...]
    z = z_ref[...]
    for hh in range(o.shape[1] // GDN_DV):
        cols = slice(hh * GDN_DV, (hh + 1) * GDN_DV)
        oh = o[:, cols]
        y = (oh * _rms(oh, GDN_DV)) * g_ref[...]
        o_ref[:, cols] = (y * _silu(z[:, cols])).astype(o_ref.dtype)


def _gdn_readout(o_f, o_b, y_in, z_col0, gain, n_rows):
    width = GDN_V_HEADS * GDN_DV
    tn = _pick(width, COL_TILES)
    spec = pl.BlockSpec((ROW_BLK, tn), lambda i, j: (i, j))
    return pl.pallas_call(
        _gdn_readout_kernel,
        grid=(n_rows // ROW_BLK, width // tn),
        in_specs=[spec, spec, pl.BlockSpec((ROW_BLK, tn), lambda i, j: (i, z_col0 // tn + j)),
                  pl.BlockSpec((1, GDN_DV), lambda i, j: (0, 0))],
        out_specs=spec,
        out_shape=jax.ShapeDtypeStruct((n_rows, width), BF16),
        compiler_params=_cp("parallel", "parallel"),
        name="gdn_readout",
    )(o_f, o_b, y_in, gain.reshape(1, GDN_DV))


def _proj_rows():
    return _pick(math.gcd(SEQ, BATCH * CTX_LEN), (512, 256))


def _rope_spec():
    tm = _proj_rows()
    return pl.BlockSpec((tm, 3 * LANE), lambda i, j: (jnp.where(i < _n_lat() // tm, i % (SEQ // tm), SEQ // tm), 0))


def _rope_table(n_ident):
    n_freq = MLA_ROPE // 4
    inv_freq = ROPE_THETA ** (-jnp.arange(n_freq, dtype=F32) / n_freq)
    t = jnp.arange(SEQ)
    ang = jnp.concatenate([(t // GRID_W).astype(F32)[:, None] * inv_freq,
                           (t % GRID_W).astype(F32)[:, None] * inv_freq], axis=1)
    cos, sin = jnp.cos(ang), jnp.sin(ang)
    zero = jnp.zeros_like(sin[:, :n_freq])
    pad = jnp.zeros((SEQ, LANE - MLA_ROPE), F32)
    c = jnp.concatenate([cos[:, :n_freq], cos[:, :n_freq], cos[:, n_freq:], cos[:, n_freq:], pad + 1.0], axis=1)
    sa = jnp.concatenate([-sin[:, :n_freq], zero, -sin[:, n_freq:], zero, pad], axis=1)
    sb = jnp.concatenate([zero, sin[:, :n_freq], zero, sin[:, n_freq:], pad], axis=1)
    ident = jnp.concatenate([jnp.ones((n_ident, LANE), F32), jnp.zeros((n_ident, 2 * LANE), F32)], axis=1)
    return jnp.concatenate([jnp.concatenate([c, sa, sb], axis=1), ident], axis=0)


def _pad_cols(w, n):
    return jnp.pad(w, ((0, 0), (0, n - w.shape[1])))


def _even_layer(h, n_rows, need_ctx, mod, mod_rows, layer, gain, p, idx, ffn):
    w_in_stack, q_a_g, kv_a_g, w_qb, w_kvb, q_g, k_g, sc_w, w_out_stack = p
    ffn_w_gate, ffn_w_up, d_ff_p = ffn
    d = w_out_stack.shape[2]
    sc_width = d - MLA_HEADS * MLA_VDIM
    assert sc_width == MLA_HEADS * MLA_VDIM
    head_w = MLA_NOPE + MLA_ROPE
    c0 = MLA_Q_RANK + MLA_KV_RANK
    conv0 = _round_up(c0 + LANE, _pick(sc_width, COL_TILES))
    n_in = _round_up(conv0 + 3 * sc_width, 1024)
    w_in_t = _split_rows(jnp.swapaxes(w_in_stack, 1, 2), idx, c0 + MLA_ROPE, conv0, n_in)
    w_q = w_qb.reshape(MLA_Q_RANK, MLA_HEADS, head_w)
    w_q = jnp.pad(w_q, ((0, 0), (0, 0), (0, QK_PAD - head_w))).reshape(MLA_Q_RANK, MLA_HEADS * QK_PAD).astype(BF16)
    rope = _rope_table(_proj_rows())

    if isinstance(h, tuple):
        u, h = _modulate(h, n_rows, gain, mod_rows, layer, 0, 1)
    else:
        u = _modulate(h, n_rows, gain, mod_rows, layer, 0, 1)
    y = _mm([u], w_in_t, n_rows, w_t=True, name="even_in")
    q = _qproj(y, n_rows, q_a_g.reshape(1, -1), w_q, jnp.pad(q_g, (0, QK_PAD - head_w)).reshape(1, QK_PAD), rope)
    k, v = _kvproj(y, n_rows, kv_a_g.reshape(1, -1), w_kvb.astype(BF16), k_g[:MLA_NOPE].reshape(1, LANE),
                   jnp.pad(k_g[MLA_NOPE:], (0, LANE - MLA_ROPE)).reshape(1, LANE), rope)
    steps = _attn_lat_steps()
    jobs = [_CastJob(ffn_w_gate, layer, d, d_ff_p, 0, steps // 2),
            _CastJob(ffn_w_up, layer, d, d_ff_p, steps // 2, steps - steps // 2),
            _CastJob(w_out_stack, idx, w_out_stack.shape[1], d, 0, steps)]
    n_out = n_rows if need_ctx else _n_lat()
    o, (wg, wu, w_out_b) = _attention(q, k, v, True, jobs)
    if need_ctx:
        o = jnp.concatenate([o, _attention(q, k, v, False)[0]], axis=0)
    conv = _sconv(y, n_out, sc_w, conv0, sc_width)
    h = _mm([o, conv], w_out_b, n_out, res=h, gate=(mod, layer, 2),
            tn=_pick(d, NARROW_COL_TILES), name="even_out")
    return h, (wg, wu)


def _odd_layer(h, n_rows, need_ctx, mod, mod_rows, layer, gain, p, idx, ffn, w_in_b):
    w_in_stack, conv_w, a_log, dt_bias, o_g, w_out_stack = p
    ffn_w_gate, ffn_w_up, d_ff_p = ffn
    d = h.shape[1]
    hv, hq = GDN_V_HEADS, GDN_QK_HEADS
    qkv_w = 2 * hq * GDN_DK + hv * GDN_DV
    main_w = qkv_w + hv * GDN_DV
    gate_w = _round_up(4 * hv, LANE)
    half = gate_w // 2
    if w_in_b is None:
        w_in_b = w_in_stack[idx].astype(BF16)
    w_ba = jnp.concatenate([_pad_cols(w_in_b[:, main_w:main_w + 2 * hv], half),
                            _pad_cols(w_in_b[:, main_w + 2 * hv:], half)], axis=1)
    zeros = jnp.zeros((1, half), F32)
    alog = jnp.concatenate([zeros, _pad_cols(a_log.reshape(1, -1), half)], axis=1)
    dtb = jnp.concatenate([zeros, _pad_cols(dt_bias.reshape(1, -1), half)], axis=1)

    u = _modulate(h, n_rows, gain, mod_rows, layer, 0, 1)
    tn = _pick(main_w, NARROW_COL_TILES)
    steps = _mm_steps(n_rows, main_w, tn)
    y, (wg, w_out_b) = _mm([u], w_in_b, n_rows, n_cols=main_w, tn=tn,
                           jobs=[_CastJob(ffn_w_gate, layer, d, d_ff_p, 0, steps),
                                 _CastJob(w_out_stack, idx, w_out_stack.shape[1], d, 0, steps)], name="odd_in")
    ba = _mm([u], w_ba, n_rows, name="odd_in_gates")
    bg = _gdn_gate(ba, n_rows, alog, dtb)
    rep = hv // hq
    bg_t = jnp.stack([bg[:, :2 * hv], bg[:, half:half + 2 * hv]], axis=0)
    bg_t = bg_t.reshape(2, n_rows, 2, hq, rep).transpose(3, 0, 2, 4, 1).reshape(hq, 4 * rep, n_rows)
    bg_t = jnp.pad(bg_t, ((0, 0), (0, _round_up(4 * rep, SUBLANE) - 4 * rep), (0, 0)))
    o_f, o_b = _gdn_scan(_gdn_prep(y, conv_w, bg_t, n_rows), n_rows)
    n_out = n_rows if need_ctx else _n_lat()
    yo = _gdn_readout(o_f, o_b, y, qkv_w, o_g, n_out)
    tn = _pick(d, NARROW_COL_TILES)
    h, (wu,) = _mm([yo], w_out_b, n_out, res=h, gate=(mod, layer, 2), tn=tn,
                   jobs=[_CastJob(ffn_w_up, layer, d, d_ff_p, 0, _mm_steps(n_out, d, tn))], name="odd_out")
    return h, (wg, wu)


def kernel(x, c, ctx, c_ctx, ada_w, ada_b, norm_mix, norm_ffn, ffn_w_gate, ffn_w_up, ffn_conv_w, ffn_conv_b,
           ffn_w_down, a_w_in, a_q_a_norm, a_kv_a_norm, a_w_qb, a_w_kvb, a_q_norm, a_k_norm, a_sc_conv, a_w_out,
           c_w_in, c_conv_w, c_a_log, c_dt_bias, c_o_norm, c_w_out):
    bn, t, d = x.shape
    depth = ada_w.shape[0]
    assert (bn, t, ctx.shape[1], ffn_w_gate.shape[2]) == (BATCH, SEQ, CTX_LEN, D_FF)
    assert CTX_LEN % GDN_BLK == 0 and SEQ % GDN_BLK == 0
    h = (x.reshape(bn * t, d), ctx.reshape(bn * CTX_LEN, d))
    cond8 = jnp.concatenate([c, c_ctx[None], jnp.zeros((SUBLANE - bn - 1, d), F32)], axis=0)
    mod = _ada(cond8, ada_w, ada_b)
    mod_rows = mod.reshape(depth * SUBLANE * 6, 1, d)
    d_ff_p = _round_up(D_FF, 512)
    ff_pad = ((0, 0), (0, 0), (0, d_ff_p - D_FF))
    conv_w = jnp.pad(ffn_conv_w, ff_pad)
    conv_b = jnp.pad(ffn_conv_b.reshape(depth, 1, D_FF), ff_pad)
    ffn = (ffn_w_gate, ffn_w_up, d_ff_p)
    w_in_b = None
    for l in range(depth):
        last = l == depth - 1
        if isinstance(h, tuple) and l % 2 == 1:
            h = jnp.concatenate(h, axis=0)
        n_rows = h[0].shape[0] + h[1].shape[0] if isinstance(h, tuple) else h.shape[0]
        i = l // 2
        if l % 2 == 0:
            h, (w_gate, w_up) = _even_layer(
                h, n_rows, not last, mod, mod_rows, l, norm_mix[l],
                (a_w_in, a_q_a_norm[i], a_kv_a_norm[i], a_w_qb[i], a_w_kvb[i], a_q_norm[i], a_k_norm[i],
                 a_sc_conv[i], a_w_out), i, ffn)
        else:
            h, (w_gate, w_up) = _odd_layer(
                h, n_rows, not last, mod, mod_rows, l, norm_mix[l],
                (c_w_in, c_conv_w[i], c_a_log[i], c_dt_bias[i], c_o_norm[i], c_w_out), i, ffn, w_in_b)
        next_w = (c_w_in, (l + 1) // 2) if (l + 1 < depth and (l + 1) % 2 == 1) else None
        h, w_in_b = _conv_ffn(h, h.shape[0], norm_ffn[l], mod_rows, mod, l,
                              w_gate, w_up, conv_w, conv_b, ffn_w_down, next_w)
    return h[:bn * t].reshape(bn, t, d)
```
